```python
import jax, jax.numpy as jnp
from jax import lax
import numpy as np

D_MODEL = 1024
BATCH = 4
SEQ = 4096
DEPTH = 1
DEC_BATCH = 32
DEC_SEQ = 8
PAST_LEN = 8192
PAGE_SIZE = 128

HEAD_DIM = 64
POOL_WIDTH = D_MODEL // 4
POOL_WINDOWS = (2, 4, 8, 16)
POOL_GROUPS = len(POOL_WINDOWS)
POOL_GROUP_WIDTH = POOL_WIDTH // POOL_GROUPS
POOL_STATE = max(POOL_WINDOWS) - 1
MOBA_WIDTH = D_MODEL // 2
MOBA_HEADS = MOBA_WIDTH // HEAD_DIM
MOBA_BLOCK = 256
MOBA_TOPK = 3
MOBA_Q_BLOCK = 32
MEM_WIDTH = D_MODEL - POOL_WIDTH - MOBA_WIDTH
MEM_HEADS = 4
MEM_HEAD_DIM = MEM_WIDTH // MEM_HEADS
MEM_TOKENS = 256
IN_WIDTH = POOL_WIDTH + 3 * MOBA_WIDTH + MEM_WIDTH
IN_SPLITS = (POOL_WIDTH, POOL_WIDTH + MOBA_WIDTH, POOL_WIDTH + 2 * MOBA_WIDTH, POOL_WIDTH + 3 * MOBA_WIDTH)
MOE_GROUPS = 4
EXPERTS_PER_GROUP = 8
EXPERT_TOPK = 2
D_EXPERT = D_MODEL // 4
NORM_EPS = 1e-6

kernel_name = "hymba_pool_moba_memxattn_hmoe_step"


def rms_norm(x, gain):
    xf = x.astype(jnp.float32)
    xf = xf * lax.rsqrt(jnp.mean(xf * xf, axis=-1, keepdims=True) + NORM_EPS)
    return xf.astype(x.dtype) * gain


def alibi_slopes(n):
    return jnp.exp2(-8.0 * jnp.arange(1, n + 1, dtype=jnp.float32) / n)


def pool_mix(u, buf, pos0, pool_w, pool_scale):
    B, T, C = u.shape
    ext = jnp.concatenate([buf.astype(u.dtype), u], axis=1)
    cs = jnp.cumsum(ext.astype(jnp.float32), axis=1)
    cs = jnp.concatenate([jnp.zeros((B, 1, C), jnp.float32), cs], axis=1)
    end = cs[:, POOL_STATE + 1:]
    pos = pos0 + jnp.arange(T, dtype=jnp.int32)
    outs = []
    for g, w in enumerate(POOL_WINDOWS):
        sl = slice(g * POOL_GROUP_WIDTH, (g + 1) * POOL_GROUP_WIDTH)
        start = cs[:, POOL_STATE + 1 - w: POOL_STATE + 1 - w + T, sl]
        cnt = jnp.minimum(w, pos + 1).astype(jnp.float32)[None, :, None]
        outs.append((end[..., sl] - start) / cnt)
    pooled = jnp.concatenate(outs, axis=-1) - u.astype(jnp.float32)
    pooled = pooled.reshape(B, T, POOL_GROUPS, POOL_GROUP_WIDTH)
    mixed = jnp.einsum("btgc,gcd->btgd", pooled, pool_w.astype(jnp.float32)).reshape(B, T, C)
    mixed = mixed * pool_scale.astype(jnp.float32)
    return mixed.astype(u.dtype), ext[:, -POOL_STATE:]


def moba_blocks(k, v):
    B, L, H, D = k.shape
    nb = -(-L // MOBA_BLOCK)
    pad = nb * MOBA_BLOCK - L
    kp = jnp.pad(k, ((0, 0), (0, pad), (0, 0), (0, 0)))
    vp = jnp.pad(v, ((0, 0), (0, pad), (0, 0), (0, 0)))
    kb = kp.reshape(B, nb, MOBA_BLOCK, H, D).transpose(0, 3, 1, 2, 4)
    vb = vp.reshape(B, nb, MOBA_BLOCK, H, D).transpose(0, 3, 1, 2, 4)
    kmean = jnp.mean(kb.astype(jnp.float32), axis=3)
    return kb, vb, kmean


def moba_attend(q, qpos, kb, vb, kmean):
    B, Q, H, D = q.shape
    nb = kb.shape[2]
    c = qpos // MOBA_BLOCK
    gate = jnp.einsum("bqhd,bhnd->bhqn", q.astype(jnp.float32), kmean)
    past = jnp.arange(nb, dtype=jnp.int32)[None, :] < c[:, None]
    gate = jnp.where(past[None, None], gate, -jnp.inf)
    n_sel = min(MOBA_TOPK, nb)
    _, top = lax.top_k(gate, n_sel)
    top = top.astype(jnp.int32)
    own = jnp.broadcast_to(c[None, None, :, None], (B, H, Q, 1)).astype(jnp.int32)
    blk = jnp.concatenate([top, own], axis=-1)
    blk_ok = jnp.concatenate([top < c[None, None, :, None], jnp.ones((B, H, Q, 1), bool)], axis=-1)
    bi = jnp.arange(B)[:, None, None, None]
    hi = jnp.arange(H)[None, :, None, None]
    k_g = kb[bi, hi, blk]
    v_g = vb[bi, hi, blk]
    kpos = blk[..., None] * MOBA_BLOCK + jnp.arange(MOBA_BLOCK, dtype=jnp.int32)
    dist = qpos[None, None, :, None, None] - kpos
    mask = blk_ok[..., None] & (dist >= 0)
    slopes = alibi_slopes(H)
    logits = jnp.einsum("bqhd,bhqsjd->bhqsj", q, k_g).astype(jnp.float32) * (D ** -0.5)
    logits = logits - slopes[None, :, None, None, None] * dist.astype(jnp.float32)
    logits = jnp.where(mask, logits, -jnp.inf)
    p = jax.nn.softmax(logits.reshape(B, H, Q, -1), axis=-1).reshape(logits.shape)
    return jnp.einsum("bhqsj,bhqsjd->bqhd", p.astype(v_g.dtype), v_g)


def moba_sweep(q, qpos, kb, vb, kmean, q_block):
    B, T, H, D = q.shape
    n = T // q_block
    qs = q.reshape(B, n, q_block, H, D).transpose(1, 0, 2, 3, 4)
    ps = qpos.reshape(n, q_block)
    out = lax.map(lambda a: moba_attend(a[0], a[1], kb, vb, kmean), (qs, ps))
    return out.transpose(1, 0, 2, 3, 4).reshape(B, T, H, D)


def memory_kv(mem, mem_norm_gain, w_mem_kv, mem_k_gain):
    B, M, _ = mem.shape
    kv = rms_norm(mem, mem_norm_gain) @ w_mem_kv
    k, v = jnp.split(kv, 2, axis=-1)
    k = rms_norm(k.reshape(B, M, MEM_HEADS, MEM_HEAD_DIM), mem_k_gain)
    return k, v.reshape(B, M, MEM_HEADS, MEM_HEAD_DIM)


def memory_attend(q, k, v):
    logits = jnp.einsum("bthd,bmhd->bhtm", q, k).astype(jnp.float32) * (MEM_HEAD_DIM ** -0.5)
    p = jax.nn.softmax(logits, axis=-1)
    return jnp.einsum("bhtm,bmhd->bthd", p.astype(v.dtype), v)


def hier_moe(h, rg_w, rg_b, re_w, re_b, w_gate, w_up, w_down):
    shp = h.shape
    hf = h.reshape(-1, shp[-1])
    N = hf.shape[0]
    pg = jax.nn.softmax((hf @ rg_w + rg_b).astype(jnp.float32), axis=-1)
    g_top = jnp.argmax(pg, axis=-1)
    pg_top = jnp.max(pg, axis=-1)
    le = (hf @ re_w + re_b).astype(jnp.float32).reshape(N, MOE_GROUPS, EXPERTS_PER_GROUP)
    le_sel = jnp.take_along_axis(le, g_top[:, None, None], axis=1)[:, 0]
    pe = jax.nn.softmax(le_sel, axis=-1)
    w2, i2 = lax.top_k(pe, EXPERT_TOPK)
    w2 = w2 / jnp.sum(w2, axis=-1, keepdims=True)
    within = jnp.sum(jax.nn.one_hot(i2, EXPERTS_PER_GROUP, dtype=jnp.float32) * w2[..., None], axis=1)
    comb = jax.nn.one_hot(g_top, MOE_GROUPS, dtype=jnp.float32)[:, :, None] * (pg_top[:, None] * within)[:, None, :]
    comb = comb.astype(hf.dtype)
    y = jnp.zeros_like(hf)
    for g in range(MOE_GROUPS):
        a = jnp.einsum("nd,edf->nef", hf, w_gate[g])
        b = jnp.einsum("nd,edf->nef", hf, w_up[g])
        hid = jax.nn.silu(a) * b * comb[:, g, :, None]
        y = y + jnp.einsum("nef,efd->nd", hid, w_down[g])
    return y.reshape(shp)


def trunk_layer(x, pos0, pool_buf, k_past, v_past, mem_k, mem_v, q_block, weights):
    (norm1_gain, w_in, pool_w, pool_scale, moba_q_gain, moba_k_gain, mem_q_gain, out_gain,
     w_out, norm2_gain, rg_w, rg_b, re_w, re_b, w_gate, w_up, w_down) = weights
    B, T, _ = x.shape
    h = rms_norm(x, norm1_gain)
    u, q, k, v, qm = jnp.split(h @ w_in, IN_SPLITS, axis=-1)
    pool_out, new_buf = pool_mix(u, pool_buf, pos0, pool_w, pool_scale)
    q = rms_norm(q.reshape(B, T, MOBA_HEADS, HEAD_DIM), moba_q_gain)
    k = rms_norm(k.reshape(B, T, MOBA_HEADS, HEAD_DIM), moba_k_gain)
    v = v.reshape(B, T, MOBA_HEADS, HEAD_DIM)
    if k_past is None:
        k_all, v_all = k, v
    else:
        k_all = jnp.concatenate([k_past.astype(k.dtype), k], axis=1)
        v_all = jnp.concatenate([v_past.astype(v.dtype), v], axis=1)
    kb, vb, kmean = moba_blocks(k_all, v_all)
    qpos = pos0 + jnp.arange(T, dtype=jnp.int32)
    attn = moba_sweep(q, qpos, kb, vb, kmean, q_block).reshape(B, T, MOBA_WIDTH)
    qm = rms_norm(qm.reshape(B, T, MEM_HEADS, MEM_HEAD_DIM), mem_q_gain)
    mo = memory_attend(qm, mem_k.astype(qm.dtype), mem_v.astype(qm.dtype)).reshape(B, T, MEM_WIDTH)
    o = jnp.concatenate([
        rms_norm(pool_out, out_gain[:POOL_WIDTH]),
        rms_norm(attn, out_gain[POOL_WIDTH:POOL_WIDTH + MOBA_WIDTH]),
        rms_norm(mo, out_gain[POOL_WIDTH + MOBA_WIDTH:]),
    ], axis=-1)
    x = x + o @ w_out
    x = x + hier_moe(rms_norm(x, norm2_gain), rg_w, rg_b, re_w, re_b, w_gate, w_up, w_down)
    return x, k, v, new_buf


def setup_inputs(seed: int = 0) -> dict:
    key = jax.random.key(seed)
    ks = jax.random.split(key, 32)
    f32 = jnp.float32
    nrm = lambda k, s, scale: jax.random.normal(k, s, f32) * scale
    gain = lambda k, n: 1.0 + 0.02 * jax.random.normal(k, (n,), f32)
    n_pages = PAST_LEN // PAGE_SIZE
    n_used = DEC_BATCH * n_pages
    n_phys = n_used + max(1, n_used // 4)
    perm = jax.random.permutation(ks[0], n_phys)
    page_table = perm[:n_used].reshape(DEC_BATCH, n_pages).astype(jnp.int32)
    return {
        "x_prompt": nrm(ks[1], (BATCH, SEQ, D_MODEL), 1.0),
        "x_sample": nrm(ks[2], (DEC_BATCH, DEC_SEQ, D_MODEL), 1.0),
        "mem_prompt": nrm(ks[3], (BATCH, MEM_TOKENS, D_MODEL), 1.0),
        "cache_k": nrm(ks[4], (n_phys, PAGE_SIZE, MOBA_HEADS, HEAD_DIM), 1.0),
        "cache_v": nrm(ks[5], (n_phys, PAGE_SIZE, MOBA_HEADS, HEAD_DIM), 1.0),
        "state_pool": nrm(ks[6], (DEC_BATCH, POOL_STATE, POOL_WIDTH), 1.0),
        "cache_mem_k": nrm(ks[7], (DEC_BATCH, MEM_TOKENS, MEM_HEADS, MEM_HEAD_DIM), 1.0),
        "cache_mem_v": nrm(ks[8], (DEC_BATCH, MEM_TOKENS, MEM_HEADS, MEM_HEAD_DIM), 1.0),
        "page_table": page_table,
        "norm1_gain": gain(ks[9], D_MODEL),
        "w_in": nrm(ks[10], (D_MODEL, IN_WIDTH), D_MODEL ** -0.5),
        "pool_w": nrm(ks[11], (POOL_GROUPS, POOL_GROUP_WIDTH, POOL_GROUP_WIDTH), POOL_GROUP_WIDTH ** -0.5),
        "pool_scale": gain(ks[12], POOL_WIDTH),
        "moba_q_gain": gain(ks[13], HEAD_DIM),
        "moba_k_gain": gain(ks[14], HEAD_DIM),
        "mem_norm_gain": gain(ks[15], D_MODEL),
        "w_mem_kv": nrm(ks[16], (D_MODEL, 2 * MEM_WIDTH), D_MODEL ** -0.5),
        "mem_q_gain": gain(ks[17], MEM_HEAD_DIM),
        "mem_k_gain": gain(ks[18], MEM_HEAD_DIM),
        "out_gain": gain(ks[19], D_MODEL),
        "w_out": nrm(ks[20], (D_MODEL, D_MODEL), D_MODEL ** -0.5),
        "norm2_gain": gain(ks[21], D_MODEL),
        "router_group_w": nrm(ks[22], (D_MODEL, MOE_GROUPS), D_MODEL ** -0.5),
        "router_group_b": nrm(ks[23], (MOE_GROUPS,), 0.01),
        "router_expert_w": nrm(ks[24], (D_MODEL, MOE_GROUPS * EXPERTS_PER_GROUP), D_MODEL ** -0.5),
        "router_expert_b": nrm(ks[25], (MOE_GROUPS * EXPERTS_PER_GROUP,), 0.01),
        "w_gate": nrm(ks[26], (MOE_GROUPS, EXPERTS_PER_GROUP, D_MODEL, D_EXPERT), D_MODEL ** -0.5),
        "w_up": nrm(ks[27], (MOE_GROUPS, EXPERTS_PER_GROUP, D_MODEL, D_EXPERT), D_MODEL ** -0.5),
        "w_down": nrm(ks[28], (MOE_GROUPS, EXPERTS_PER_GROUP, D_EXPERT, D_MODEL), D_EXPERT ** -0.5),
    }


def reference(x_prompt, x_sample, mem_prompt, cache_k, cache_v, state_pool, cache_mem_k, cache_mem_v,
              page_table, norm1_gain, w_in, pool_w, pool_scale, moba_q_gain, moba_k_gain,
              mem_norm_gain, w_mem_kv, mem_q_gain, mem_k_gain, out_gain, w_out, norm2_gain,
              router_group_w, router_group_b, router_expert_w, router_expert_b, w_gate, w_up, w_down):
    weights = (norm1_gain, w_in, pool_w, pool_scale, moba_q_gain, moba_k_gain, mem_q_gain, out_gain,
               w_out, norm2_gain, router_group_w, router_group_b, router_expert_w, router_expert_b,
               w_gate, w_up, w_down)
    b_s = x_sample.shape[0]
    n_pages = page_table.shape[1]
    past_len = n_pages * PAGE_SIZE
    y_prompt, y_sample = x_prompt, x_sample
    for _ in range(DEPTH):
        mem_k_prompt, mem_v_prompt = memory_kv(mem_prompt, mem_norm_gain, w_mem_kv, mem_k_gain)
        buf0 = jnp.zeros((y_prompt.shape[0], POOL_STATE, POOL_WIDTH), y_prompt.dtype)
        y_prompt, k_prompt, v_prompt, pool_prompt = trunk_layer(
            y_prompt, 0, buf0, None, None, mem_k_prompt, mem_v_prompt, MOBA_Q_BLOCK, weights)
        k_past = cache_k[page_table].reshape(b_s, past_len, MOBA_HEADS, HEAD_DIM)
        v_past = cache_v[page_table].reshape(b_s, past_len, MOBA_HEADS, HEAD_DIM)
        y_sample, k_sample, v_sample, pool_sample = trunk_layer(
            y_sample, past_len, state_pool, k_past, v_past, cache_mem_k, cache_mem_v, 1, weights)
    return (y_prompt, y_sample, k_prompt, v_prompt, pool_prompt, mem_k_prompt, mem_v_prompt,
            k_sample, v_sample, pool_sample)
```

```python
import functools

import jax
import jax.numpy as jnp
from jax import lax
from jax.experimental import pallas as pl
from jax.experimental.pallas import tpu as pltpu

F32 = jnp.float32
BF16 = jnp.bfloat16

D_MODEL = 1024
HEAD_DIM = 64
POOL_WIDTH = 256
POOL_WINDOWS = (2, 4, 8, 16)
POOL_GROUP_WIDTH = 64
POOL_STATE = 15
MOBA_WIDTH = 512
MOBA_HEADS = 8
MOBA_BLOCK = 256
MOBA_TOPK = 3
MEM_WIDTH = 256
MEM_HEADS = 4
MEM_TOKENS = 256
PAGE_SIZE = 128
MOE_GROUPS = 4
EXPERTS_PER_GROUP = 8
N_EXPERTS = MOE_GROUPS * EXPERTS_PER_GROUP
D_EXPERT = 256
NORM_EPS = 1e-6

LANES = 128
HEAD_PAIR = 2 * HEAD_DIM
QK_SCALE = HEAD_DIM ** -0.5
NEG_INF = float("-inf")
MIB = 1024 * 1024
NT_DIMS = (((1,), (1,)), ((), ()))


def _cparams(semantics, vmem_mib):
    return pltpu.CompilerParams(dimension_semantics=semantics, vmem_limit_bytes=vmem_mib * MIB)


def _rms(y, eps=NORM_EPS):
    return y * lax.rsqrt(jnp.mean(y * y, axis=-1, keepdims=True) + eps)


def _norm_proj_body(x_ref, g_ref, w_ref, hg_ref, ones_ref, *outs, segs, tm):
    hb = (_rms(x_ref[...]) * g_ref[...]).astype(BF16)
    oi = 0
    for (c0, wd, headnorm, want_bf16, want_blockmean) in segs:
        y = jnp.dot(hb, w_ref[:, c0:c0 + wd], preferred_element_type=F32)
        if headnorm:
            sq = (y * y).astype(BF16)
            msq = jnp.dot(sq, ones_ref[:wd, :wd], preferred_element_type=F32) * (1.0 / HEAD_DIM)
            y = (y * lax.rsqrt(msq + NORM_EPS)) * hg_ref[:, c0:c0 + wd]
        outs[oi][...] = y
        oi += 1
        if want_bf16:
            outs[oi][...] = y.astype(BF16)
            oi += 1
        if want_blockmean:
            for bi in range(tm // MOBA_BLOCK):
                outs[oi][bi] = jnp.mean(y[bi * MOBA_BLOCK:(bi + 1) * MOBA_BLOCK], axis=0, keepdims=True)
            oi += 1


def _norm_proj(x, gain, w_bf, head_gain, ones_bd, segs, tm):
    n, d = x.shape
    wtot = w_bf.shape[1]
    assert n % tm == 0
    out_shape, out_specs = [], []
    for (c0, wd, headnorm, want_bf16, want_blockmean) in segs:
        out_shape.append(jax.ShapeDtypeStruct((n, wd), F32))
        out_specs.append(pl.BlockSpec((tm, wd), lambda i: (i, 0)))
        if want_bf16:
            out_shape.append(jax.ShapeDtypeStruct((n, wd), BF16))
            out_specs.append(pl.BlockSpec((tm, wd), lambda i: (i, 0)))
        if want_blockmean:
            assert tm % MOBA_BLOCK == 0
            nb = tm // MOBA_BLOCK
            out_shape.append(jax.ShapeDtypeStruct((n // MOBA_BLOCK, 1, wd), F32))
            out_specs.append(pl.BlockSpec((nb, 1, wd), lambda i: (i, 0, 0)))
    return pl.pallas_call(
        functools.partial(_norm_proj_body, segs=segs, tm=tm),
        grid=(n // tm,),
        in_specs=[
            pl.BlockSpec((tm, d), lambda i: (i, 0)),
            pl.BlockSpec((1, d), lambda i: (0, 0)),
            pl.BlockSpec((d, wtot), lambda i: (0, 0)),
            pl.BlockSpec((1, wtot), lambda i: (0, 0)),
            pl.BlockSpec(ones_bd.shape, lambda i: (0, 0)),
        ],
        out_specs=out_specs,
        out_shape=out_shape,
        compiler_params=_cparams(("parallel",), 48),
        name="norm_proj",
    )(x, gain.reshape(1, d), w_bf, head_gain.reshape(1, wtot), ones_bd)


def _pool_windows(win, pos0):
    r = win.shape[0] - 16
    lane = lax.broadcasted_iota(jnp.int32, (r, LANES), 1)
    pos1 = pos0 + lax.broadcasted_iota(jnp.int32, (r, LANES), 0) + 1
    low = lane < POOL_GROUP_WIDTH
    a = win[:, :LANES]
    b = win[:, LANES:]
    a2 = a + pltpu.roll(a, 1, 0)
    a4 = a2 + pltpu.roll(a2, 2, 0)
    b2 = b + pltpu.roll(b, 1, 0)
    b4 = b2 + pltpu.roll(b2, 2, 0)
    b8 = b4 + pltpu.roll(b4, 4, 0)
    b16 = b8 + pltpu.roll(b8, 8, 0)
    cnt_a = jnp.minimum(jnp.where(low, POOL_WINDOWS[0], POOL_WINDOWS[1]), pos1).astype(F32)
    cnt_b = jnp.minimum(jnp.where(low, POOL_WINDOWS[2], POOL_WINDOWS[3]), pos1).astype(F32)
    pa = jnp.where(low, a2[16:], a4[16:]) / cnt_a - a[16:]
    pb = jnp.where(low, b8[16:], b16[16:]) / cnt_b - b[16:]
    return jnp.concatenate([pa, pb], axis=1)


def _pool_finish(pooled, wbd_ref, ps_ref, og_ref):
    mixed = jnp.dot(pooled.astype(BF16), wbd_ref[...], preferred_element_type=F32) * ps_ref[...]
    return _rms(mixed) * og_ref[...]


def _pool_body(u_ref, buf_ref, wbd_ref, ps_ref, og_ref, o_ref, ext_sc, *, bb, t, r, pos0):
    for bi in range(bb):
        ext_sc[bi, 0:16, :] = buf_ref[bi]
        ext_sc[bi, 16:, :] = u_ref[bi]
    if t == r:
        pooled = [_pool_windows(ext_sc[bi], pos0) for bi in range(bb)]
        out = _pool_finish(jnp.concatenate(pooled, axis=0), wbd_ref, ps_ref, og_ref)
        for bi in range(bb):
            o_ref[bi] = out[bi * r:(bi + 1) * r]
    else:
        assert bb == 1

        def chunk(c, carry):
            base = pl.multiple_of(c * r, r)
            pooled = _pool_windows(ext_sc[0, pl.ds(base, r + 16), :], pos0 + base)
            o_ref[0, pl.ds(base, r), :] = _pool_finish(pooled, wbd_ref, ps_ref, og_ref)
            return carry

        lax.fori_loop(0, t // r, chunk, 0)


def _pool(u, buf16, wbd_bf, pool_scale, og, pos0, bb, r):
    b, t, c = u.shape
    assert b % bb == 0 and t % r == 0
    return pl.pallas_call(
        functools.partial(_pool_body, bb=bb, t=t, r=r, pos0=pos0),
        grid=(b // bb,),
        in_specs=[
            pl.BlockSpec((bb, t, c), lambda i: (i, 0, 0)),
            pl.BlockSpec((bb, 16, c), lambda i: (i, 0, 0)),
            pl.BlockSpec((c, c), lambda i: (0, 0)),
            pl.BlockSpec((1, c), lambda i: (0, 0)),
            pl.BlockSpec((1, c), lambda i: (0, 0)),
        ],
        out_specs=pl.BlockSpec((bb, t, c), lambda i: (i, 0, 0)),
        out_shape=jax.ShapeDtypeStruct((b, t, c), F32),
        scratch_shapes=[pltpu.VMEM((bb, t + 16, c), F32)],
        compiler_params=_cparams(("parallel",), 40),
        name="pool",
    )(u, buf16, wbd_bf, pool_scale.reshape(1, c), og.reshape(1, c))


def _alibi_slope(h):
    return 2.0 ** (-8.0 * (h + 1) / MOBA_HEADS)


def _moba_prompt_body(q_ref, k_ref, v_ref, km_ref, og_ref, o_ref, m_sc, l_sc, acc_sc, sel_sc, attn_sc):
    i = pl.program_id(1)
    tq = MOBA_BLOCK
    lane = lax.broadcasted_iota(jnp.int32, (tq, LANES), 1)
    lane_f = lane.astype(F32)
    low = lane < HEAD_DIM
    row2 = lax.broadcasted_iota(jnp.int32, (tq, MOBA_BLOCK), 0)
    col2 = lax.broadcasted_iota(jnp.int32, (tq, MOBA_BLOCK), 1)
    causal = col2 <= row2
    colrow = lax.broadcasted_iota(jnp.int32, (1, MOBA_BLOCK), 1).astype(F32)

    for pr in range(MOBA_HEADS // 2):
        cs = slice(pr * HEAD_PAIR, (pr + 1) * HEAD_PAIR)
        q_pair = q_ref[:, cs]
        km_pair = km_ref[0, :, cs]
        q_heads = [jnp.where(low, q_pair, 0.0), jnp.where(low, 0.0, q_pair)]

        for hh in range(2):
            gate = lax.dot_general(q_heads[hh], km_pair, NT_DIMS, precision=lax.Precision.HIGHEST,
                                   preferred_element_type=F32)
            g = jnp.where(lane < i, gate, NEG_INF)
            sel = jnp.zeros((tq, LANES), F32)
            for _ in range(MOBA_TOPK):
                mx = jnp.max(g, axis=-1, keepdims=True)
                idx = jnp.min(jnp.where(g == mx, lane_f, float(LANES)), axis=-1, keepdims=True)
                pick = (lane_f == idx) & (mx > NEG_INF)
                sel = jnp.where(pick, 1.0, sel)
                g = jnp.where(pick, NEG_INF, g)
            sel_sc[hh] = sel

        qst = (jnp.concatenate(q_heads, axis=0) * QK_SCALE).astype(BF16)

        def scores(j):
            start = pl.multiple_of(j * MOBA_BLOCK, MOBA_BLOCK)
            kj = k_ref[pl.ds(start, MOBA_BLOCK), cs]
            s = lax.dot_general(qst, kj, NT_DIMS, preferred_element_type=F32)
            off = colrow + ((j - i) * MOBA_BLOCK).astype(F32)
            return [s[hh * tq:(hh + 1) * tq] + _alibi_slope(2 * pr + hh) * off for hh in range(2)]

        def pv(j, ps):
            start = pl.multiple_of(j * MOBA_BLOCK, MOBA_BLOCK)
            vj = v_ref[pl.ds(start, MOBA_BLOCK), cs]
            return jnp.dot(jnp.concatenate(ps, axis=0).astype(BF16), vj, preferred_element_type=F32)

        s_own = scores(i)
        ps = []
        for hh in range(2):
            sh = jnp.where(causal, s_own[hh], NEG_INF)
            m = jnp.max(sh, axis=-1, keepdims=True)
            p = jnp.exp(sh - m)
            m_sc[hh] = jnp.broadcast_to(m, (tq, LANES))
            l_sc[hh] = jnp.broadcast_to(jnp.sum(p, axis=-1, keepdims=True), (tq, LANES))
            ps.append(p)
        acc_sc[...] = pv(i, ps)

        def past(j, carry):
            s_j = scores(j)
            ps, alphas = [], []
            for hh in range(2):
                selcol = jnp.sum(jnp.where(lane == j, sel_sc[hh], 0.0), axis=-1, keepdims=True)
                sh = jnp.where(selcol > 0.0, s_j[hh], NEG_INF)
                m_prev = m_sc[hh]
                m_new = jnp.maximum(m_prev, jnp.max(sh, axis=-1, keepdims=True))
                alpha = jnp.exp(m_prev - m_new)
                p = jnp.exp(sh - jnp.concatenate([m_new, m_new], axis=1))
                l_sc[hh] = alpha * l_sc[hh] + jnp.sum(p, axis=-1, keepdims=True)
                m_sc[hh] = m_new
                ps.append(p)
                alphas.append(alpha)
            acc_sc[...] = jnp.concatenate(alphas, axis=0) * acc_sc[...] + pv(j, ps)
            return carry

        lax.fori_loop(0, i, past, 0)
        acc = acc_sc[...]
        attn_sc[:, cs] = jnp.where(low, acc[:tq] / l_sc[0], acc[tq:] / l_sc[1])

    o_ref[...] = _rms(attn_sc[...]) * og_ref[...]


def _moba_prompt(q, k_bf, v_bf, kmean_pad, og, b, t):
    n, w = q.shape
    nblk = t // MOBA_BLOCK
    tq = MOBA_BLOCK
    return pl.pallas_call(
        _moba_prompt_body,
        grid=(b, nblk),
        in_specs=[
            pl.BlockSpec((tq, w), lambda bi, i: (bi * nblk + i, 0)),
            pl.BlockSpec((t, w), lambda bi, i: (bi, 0)),
            pl.BlockSpec((t, w), lambda bi, i: (bi, 0)),
            pl.BlockSpec((1, LANES, w), lambda bi, i: (bi, 0, 0)),
            pl.BlockSpec((1, w), lambda bi, i: (0, 0)),
        ],
        out_specs=pl.BlockSpec((tq, w), lambda bi, i: (bi * nblk + i, 0)),
        out_shape=jax.ShapeDtypeStruct((n, w), F32),
        scratch_shapes=[
            pltpu.VMEM((2, tq, LANES), F32),
            pltpu.VMEM((2, tq, LANES), F32),
            pltpu.VMEM((2 * tq, LANES), F32),
            pltpu.VMEM((2, tq, LANES), F32),
            pltpu.VMEM((tq, w), F32),
        ],
        compiler_params=_cparams(("parallel", "arbitrary"), 40),
        name="moba_prompt",
    )(q, k_bf, v_bf, kmean_pad, og.reshape(1, w))


SAMPLE_BLOCKS_PER_STEP = 8
PAGES_PER_BLOCK = MOBA_BLOCK // PAGE_SIZE
PAGES_PER_STEP = SAMPLE_BLOCKS_PER_STEP * PAGES_PER_BLOCK


def _moba_sample_body(pt_ref, q_ref, kn_ref, vn_ref, slope_ref, og_ref, *rest, n_past_blocks, t_new):
    kp = rest[:PAGES_PER_STEP]
    vp = rest[PAGES_PER_STEP:2 * PAGES_PER_STEP]
    o_ref, o_sc, m_sc, l_sc, g_sc = rest[2 * PAGES_PER_STEP:]
    c = pl.program_id(1)
    rows = MOBA_HEADS * t_new
    row_h = lax.broadcasted_iota(jnp.int32, (rows, MOBA_WIDTH), 0) // t_new
    lane_h = lax.broadcasted_iota(jnp.int32, (rows, MOBA_WIDTH), 1) // HEAD_DIM
    bd = row_h == lane_h
    q = q_ref[...]
    qbd = jnp.where(bd, jnp.concatenate([q] * MOBA_HEADS, axis=0), 0.0)
    qbd_bf = (qbd * QK_SCALE).astype(BF16)
    slope = slope_ref[...]
    slope2 = jnp.concatenate([slope, slope], axis=1)
    colf = lax.broadcasted_iota(jnp.int32, (rows, MOBA_BLOCK), 1).astype(F32)

    for jj in range(SAMPLE_BLOCKS_PER_STEP):
        j = c * SAMPLE_BLOCKS_PER_STEP + jj
        k32 = jnp.concatenate([kp[PAGES_PER_BLOCK * jj + p][0] for p in range(PAGES_PER_BLOCK)], axis=0)
        v32 = jnp.concatenate([vp[PAGES_PER_BLOCK * jj + p][0] for p in range(PAGES_PER_BLOCK)], axis=0)
        kmean = jnp.mean(k32, axis=0, keepdims=True)
        gate = jnp.sum(qbd * kmean, axis=-1, keepdims=True)
        s = lax.dot_general(qbd_bf, k32.astype(BF16), NT_DIMS, preferred_element_type=F32)
        s = s + slope2 * (colf + ((j - n_past_blocks) * MOBA_BLOCK).astype(F32))
        m = jnp.max(s, axis=-1, keepdims=True)
        p = jnp.exp(s - m)
        o = jnp.dot(p.astype(BF16), v32.astype(BF16), preferred_element_type=F32)
        o_sc[j] = jnp.where(bd, o, 0.0)
        m_sc[j] = jnp.broadcast_to(m, (rows, LANES))
        l_sc[j] = jnp.broadcast_to(jnp.sum(p, axis=-1, keepdims=True), (rows, LANES))
        g_sc[j] = jnp.broadcast_to(gate, (rows, LANES))

    @pl.when(c == pl.num_programs(1) - 1)
    def _finish():
        tq = lax.broadcasted_iota(jnp.int32, (rows, LANES), 0) % t_new
        kn = kn_ref[...]
        vn = vn_ref[...]
        qs = qbd * QK_SCALE
        s_own = []
        m_run = jnp.full((rows, LANES), NEG_INF, F32)
        for cc in range(t_new):
            sc = jnp.sum(qs * kn[cc:cc + 1, :], axis=-1, keepdims=True) + slope * float(cc)
            sc = jnp.where(tq >= cc, sc, NEG_INF)
            s_own.append(sc)
            m_run = jnp.maximum(m_run, sc)

        def top_round(excluded):
            best = jnp.full((rows, LANES), NEG_INF, F32)
            bidx = jnp.full((rows, LANES), -1.0, F32)
            for j in range(n_past_blocks):
                ok = g_sc[j] > best
                for e in excluded:
                    ok = ok & (e != float(j))
                best = jnp.where(ok, g_sc[j], best)
                bidx = jnp.where(ok, float(j), bidx)
            return bidx

        picks = []
        for _ in range(min(MOBA_TOPK, n_past_blocks)):
            picks.append(top_round(picks))
        sels = []
        for j in range(n_past_blocks):
            sj = picks[0] == float(j)
            for e in picks[1:]:
                sj = sj | (e == float(j))
            sels.append(sj)
            m_run = jnp.where(sj, jnp.maximum(m_run, m_sc[j]), m_run)

        l_run = jnp.zeros((rows, LANES), F32)
        o_run = jnp.zeros((rows, MOBA_WIDTH), F32)
        for cc in range(t_new):
            p = jnp.exp(s_own[cc] - m_run)
            l_run = l_run + p
            o_run = o_run + jnp.concatenate([p] * (MOBA_WIDTH // LANES), axis=1) * vn[cc:cc + 1, :]
        o_run = jnp.where(bd, o_run, 0.0)
        for j in range(n_past_blocks):
            wj = jnp.where(sels[j], jnp.exp(m_sc[j] - m_run), 0.0)
            l_run = l_run + wj * l_sc[j]
            o_run = o_run + jnp.concatenate([wj] * (MOBA_WIDTH // LANES), axis=1) * o_sc[j]
        o_bd = o_run / jnp.concatenate([l_run] * (MOBA_WIDTH // LANES), axis=1)
        attn = o_bd[0:t_new]
        for h in range(1, MOBA_HEADS):
            attn = attn + o_bd[h * t_new:(h + 1) * t_new]
        o_ref[...] = _rms(attn) * og_ref[...]


def _moba_sample(page_table, q, k_new, v_new, cache_k3, cache_v3, og, b, t_new):
    n_pages = page_table.shape[1]
    n_past_blocks = n_pages // PAGES_PER_BLOCK
    assert n_past_blocks % SAMPLE_BLOCKS_PER_STEP == 0
    n_steps = n_past_blocks // SAMPLE_BLOCKS_PER_STEP
    rows = MOBA_HEADS * t_new
    w = MOBA_WIDTH
    slopes = jnp.exp2(-8.0 * jnp.arange(1, MOBA_HEADS + 1, dtype=F32) / MOBA_HEADS)
    slope_rows = jnp.broadcast_to(jnp.repeat(slopes, t_new)[:, None], (rows, LANES))

    def page_spec(p):
        return pl.BlockSpec((1, PAGE_SIZE, w), lambda bi, c, pt: (pt[bi, c * PAGES_PER_STEP + p], 0, 0))

    row_spec = pl.BlockSpec((t_new, w), lambda bi, c, pt: (bi, 0))
    grid_spec = pltpu.PrefetchScalarGridSpec(
        num_scalar_prefetch=1,
        grid=(b, n_steps),
        in_specs=[row_spec, row_spec, row_spec,
                  pl.BlockSpec((rows, LANES), lambda bi, c, pt: (0, 0)),
                  pl.BlockSpec((1, w), lambda bi, c, pt: (0, 0))]
                 + [page_spec(p) for p in range(PAGES_PER_STEP)]
                 + [page_spec(p) for p in range(PAGES_PER_STEP)],
        out_specs=row_spec,
        scratch_shapes=[
            pltpu.VMEM((n_past_blocks, rows, w), F32),
            pltpu.VMEM((n_past_blocks, rows, LANES), F32),
            pltpu.VMEM((n_past_blocks, rows, LANES), F32),
            pltpu.VMEM((n_past_blocks, rows, LANES), F32),
        ],
    )
    return pl.pallas_call(
        functools.partial(_moba_sample_body, n_past_blocks=n_past_blocks, t_new=t_new),
        grid_spec=grid_spec,
        out_shape=jax.ShapeDtypeStruct((b * t_new, w), F32),
        compiler_params=_cparams(("parallel", "arbitrary"), 48),
        name="moba_sample",
    )(page_table, q, k_new, v_new, slope_rows, og.reshape(1, w),
      *([cache_k3] * PAGES_PER_STEP), *([cache_v3] * PAGES_PER_STEP))


def _mem_attn_body(q_ref, mk_ref, mv_ref, og_ref, o_ref, *, tm):
    lane = lax.broadcasted_iota(jnp.int32, (tm, LANES), 1)
    low = lane < HEAD_DIM
    outs = []
    for pr in range(MEM_HEADS // 2):
        cs = slice(pr * HEAD_PAIR, (pr + 1) * HEAD_PAIR)
        q_pair = q_ref[:, cs]
        qst = (jnp.concatenate([jnp.where(low, q_pair, 0.0), jnp.where(low, 0.0, q_pair)], axis=0)
               * QK_SCALE).astype(BF16)
        s = lax.dot_general(qst, mk_ref[0, :, cs].astype(BF16), NT_DIMS, preferred_element_type=F32)
        p = jnp.exp(s - jnp.max(s, axis=-1, keepdims=True))
        l = jnp.sum(p, axis=-1, keepdims=True)
        o = jnp.dot(p.astype(BF16), mv_ref[0, :, cs].astype(BF16), preferred_element_type=F32) / l
        outs.append(jnp.where(low, o[:tm], o[tm:]))
    o_ref[...] = _rms(jnp.concatenate(outs, axis=1)) * og_ref[...]


def _mem_attn(qm, mem_k, mem_v, og, b, t, tm):
    n, w = qm.shape
    assert t % tm == 0
    steps = t // tm
    return pl.pallas_call(
        functools.partial(_mem_attn_body, tm=tm),
        grid=(b, steps),
        in_specs=[
            pl.BlockSpec((tm, w), lambda bi, i: (bi * steps + i, 0)),
            pl.BlockSpec((1, MEM_TOKENS, w), lambda bi, i: (bi, 0, 0)),
            pl.BlockSpec((1, MEM_TOKENS, w), lambda bi, i: (bi, 0, 0)),
            pl.BlockSpec((1, w), lambda bi, i: (0, 0)),
        ],
        out_specs=pl.BlockSpec((tm, w), lambda bi, i: (bi * steps + i, 0)),
        out_shape=jax.ShapeDtypeStruct((n, w), F32),
        compiler_params=_cparams(("parallel", "arbitrary"), 32),
        name="mem_attn",
    )(qm, mem_k, mem_v, og.reshape(1, w))


ROUTER_GROUP_LANE0 = N_EXPERTS


def _outproj_router_body(op_ref, oa_ref, om_ref, x_ref, wo_ref, g2_ref, wr_ref, br_ref,
                         x1_ref, h2_ref, comb_ref, *, tm):
    a0, a1 = POOL_WIDTH, POOL_WIDTH + MOBA_WIDTH
    y = x_ref[...]
    y = y + jnp.dot(op_ref[...].astype(BF16), wo_ref[0:a0, :], preferred_element_type=F32)
    y = y + jnp.dot(oa_ref[...].astype(BF16), wo_ref[a0:a1, :], preferred_element_type=F32)
    y = y + jnp.dot(om_ref[...].astype(BF16), wo_ref[a1:, :], preferred_element_type=F32)
    x1_ref[...] = y
    h2 = _rms(y) * g2_ref[...]
    h2_ref[...] = h2.astype(BF16)

    logits = jnp.dot(h2, wr_ref[...], precision=lax.Precision.HIGHEST,
                     preferred_element_type=F32) + br_ref[...]
    lane_f = lax.broadcasted_iota(jnp.int32, (tm, LANES), 1).astype(F32)
    big = float(LANES)
    g_lo = float(ROUTER_GROUP_LANE0)
    is_g = (lane_f >= g_lo) & (lane_f < g_lo + MOE_GROUPS)
    lg = jnp.where(is_g, logits, NEG_INF)
    mg = jnp.max(lg, axis=-1, keepdims=True)
    pg_top = 1.0 / jnp.sum(jnp.exp(lg - mg), axis=-1, keepdims=True)
    gidx = jnp.min(jnp.where(lg == mg, lane_f, big), axis=-1, keepdims=True) - g_lo
    e_lo = gidx * EXPERTS_PER_GROUP
    in_grp = (lane_f >= e_lo) & (lane_f < e_lo + EXPERTS_PER_GROUP)
    le = jnp.where(in_grp, logits, NEG_INF)
    m1 = jnp.max(le, axis=-1, keepdims=True)
    se = jnp.sum(jnp.exp(le - m1), axis=-1, keepdims=True)
    i1 = jnp.min(jnp.where(le == m1, lane_f, big), axis=-1, keepdims=True)
    le2 = jnp.where(lane_f == i1, NEG_INF, le)
    m2 = jnp.max(le2, axis=-1, keepdims=True)
    i2 = jnp.min(jnp.where(le2 == m2, lane_f, big), axis=-1, keepdims=True)
    p1 = 1.0 / se
    p2 = jnp.exp(m2 - m1) / se
    den = p1 + p2
    comb_ref[...] = jnp.where(lane_f == i1, pg_top * (p1 / den),
                              jnp.where(lane_f == i2, pg_top * (p2 / den), 0.0))


def _outproj_router(o_pool, o_attn, o_mem, x, wo_bf, g2, wr, br, tm):
    n, d = x.shape
    assert n % tm == 0
    row = lambda wdt: pl.BlockSpec((tm, wdt), lambda i: (i, 0))
    full = lambda shp: pl.BlockSpec(shp, lambda i: (0, 0))
    return pl.pallas_call(
        functools.partial(_outproj_router_body, tm=tm),
        grid=(n // tm,),
        in_specs=[row(POOL_WIDTH), row(MOBA_WIDTH), row(MEM_WIDTH), row(d),
                  full((d, d)), full((1, d)), full((d, LANES)), full((1, LANES))],
        out_specs=[row(d), row(d), row(LANES)],
        out_shape=[jax.ShapeDtypeStruct((n, d), F32), jax.ShapeDtypeStruct((n, d), BF16),
                   jax.ShapeDtypeStruct((n, LANES), F32)],
        compiler_params=_cparams(("parallel",), 40),
        name="outproj_router",
    )(o_pool, o_attn, o_mem, x, wo_bf, g2.reshape(1, d), wr, br)


def _moe_body(h2_ref, comb_ref, x1_ref, wg_ref, wu_ref, wd_ref, y_ref, *, tm):
    e = pl.program_id(1)

    @pl.when(e == 0)
    def _init():
        y_ref[...] = x1_ref[...]

    h = h2_ref[...]
    a = jnp.dot(h, wg_ref[0].astype(BF16), preferred_element_type=F32)
    b = jnp.dot(h, wu_ref[0].astype(BF16), preferred_element_type=F32)
    lane = lax.broadcasted_iota(jnp.int32, (tm, LANES), 1)
    ce = jnp.sum(jnp.where(lane == e, comb_ref[...], 0.0), axis=-1, keepdims=True)
    hid = (a * jax.nn.sigmoid(a)) * b * ce
    y_ref[...] += jnp.dot(hid.astype(BF16), wd_ref[0].astype(BF16), preferred_element_type=F32)


def _moe(h2, comb, x1, w_gate, w_up, w_down, tm):
    n, d = x1.shape
    assert n % tm == 0
    return pl.pallas_call(
        functools.partial(_moe_body, tm=tm),
        grid=(n // tm, N_EXPERTS),
        in_specs=[
            pl.BlockSpec((tm, d), lambda i, e: (i, 0)),
            pl.BlockSpec((tm, LANES), lambda i, e: (i, 0)),
            pl.BlockSpec((tm, d), lambda i, e: (i, 0)),
            pl.BlockSpec((1, d, D_EXPERT), lambda i, e: (e, 0, 0)),
            pl.BlockSpec((1, d, D_EXPERT), lambda i, e: (e, 0, 0)),
            pl.BlockSpec((1, D_EXPERT, d), lambda i, e: (e, 0, 0)),
        ],
        out_specs=pl.BlockSpec((tm, d), lambda i, e: (i, 0)),
        out_shape=jax.ShapeDtypeStruct((n, d), F32),
        compiler_params=_cparams(("parallel", "arbitrary"), 48),
        name="moe",
    )(h2, comb, x1, w_gate, w_up, w_down)


def _block_diag_ones(width, group):
    r = lax.broadcasted_iota(jnp.int32, (width, width), 0) // group
    c = lax.broadcasted_iota(jnp.int32, (width, width), 1) // group
    return (r == c).astype(BF16)


def _pick_tile(n, candidates):
    for c in candidates:
        if n % c == 0:
            return c
    return n


IN_SEGS = (
    (0, POOL_WIDTH, False, False, False),
    (POOL_WIDTH, MOBA_WIDTH, True, False, False),
    (POOL_WIDTH + MOBA_WIDTH, MOBA_WIDTH, True, True, True),
    (POOL_WIDTH + 2 * MOBA_WIDTH, MOBA_WIDTH, False, True, False),
    (POOL_WIDTH + 3 * MOBA_WIDTH, MEM_WIDTH, True, False, False),
)
IN_SEGS_SAMPLE = tuple((c0, wd, hn, False, False) for (c0, wd, hn, _, _) in IN_SEGS)
MEMKV_SEGS = ((0, MEM_WIDTH, True, False, False), (MEM_WIDTH, MEM_WIDTH, False, False, False))


def kernel(x_prompt, x_sample, mem_prompt, cache_k, cache_v, state_pool, cache_mem_k, cache_mem_v,
           page_table, norm1_gain, w_in, pool_w, pool_scale, moba_q_gain, moba_k_gain,
           mem_norm_gain, w_mem_kv, mem_q_gain, mem_k_gain, out_gain, w_out, norm2_gain,
           router_group_w, router_group_b, router_expert_w, router_expert_b, w_gate, w_up, w_down):
    bp, tp, d = x_prompt.shape
    bs, ts, _ = x_sample.shape
    n_p, n_s = bp * tp, bs * ts

    w_in_bf = w_in.astype(BF16)
    w_mem_bf = w_mem_kv.astype(BF16)
    wo_bf = w_out.astype(BF16)
    ones_bd = _block_diag_ones(MOBA_WIDTH, HEAD_DIM)
    head_gain_in = jnp.concatenate([
        jnp.ones((POOL_WIDTH,), F32), jnp.tile(moba_q_gain, MOBA_HEADS), jnp.tile(moba_k_gain, MOBA_HEADS),
        jnp.ones((MOBA_WIDTH,), F32), jnp.tile(mem_q_gain, MEM_HEADS)])
    head_gain_mem = jnp.concatenate([jnp.tile(mem_k_gain, MEM_HEADS), jnp.ones((MEM_WIDTH,), F32)])
    wbd = jnp.zeros((POOL_WIDTH, POOL_WIDTH), F32)
    for g in range(len(POOL_WINDOWS)):
        sl = slice(g * POOL_GROUP_WIDTH, (g + 1) * POOL_GROUP_WIDTH)
        wbd = wbd.at[sl, sl].set(pool_w[g])
    wbd_bf = wbd.astype(BF16)
    og_pool, og_attn, og_mem = (out_gain[:POOL_WIDTH], out_gain[POOL_WIDTH:POOL_WIDTH + MOBA_WIDTH],
                                out_gain[POOL_WIDTH + MOBA_WIDTH:])
    wr = jnp.zeros((d, LANES), F32)
    wr = wr.at[:, :N_EXPERTS].set(router_expert_w).at[:, N_EXPERTS:N_EXPERTS + MOE_GROUPS].set(router_group_w)
    br = jnp.zeros((1, LANES), F32)
    br = br.at[0, :N_EXPERTS].set(router_expert_b).at[0, N_EXPERTS:N_EXPERTS + MOE_GROUPS].set(router_group_b)
    wg = w_gate.reshape(N_EXPERTS, d, D_EXPERT)
    wu = w_up.reshape(N_EXPERTS, d, D_EXPERT)
    wd = w_down.reshape(N_EXPERTS, D_EXPERT, d)

    def tail(o_pool, o_attn, o_mem, x2d):
        n = x2d.shape[0]
        x1, h2, comb = _outproj_router(o_pool, o_attn, o_mem, x2d, wo_bf, norm2_gain, wr, br,
                                       _pick_tile(n, (512, 256)))
        return _moe(h2, comb, x1, wg, wu, wd, _pick_tile(n, (1024, 512, 256)))

    mem_k_p, mem_v_p = _norm_proj(mem_prompt.reshape(bp * MEM_TOKENS, d), mem_norm_gain, w_mem_bf,
                                  head_gain_mem, ones_bd, MEMKV_SEGS, _pick_tile(bp * MEM_TOKENS, (512, 256)))
    xp = x_prompt.reshape(n_p, d)
    u_p, q_p, k_p, k_p_bf, kmean_p, v_p, v_p_bf, qm_p = _norm_proj(
        xp, norm1_gain, w_in_bf, head_gain_in, ones_bd, IN_SEGS, _pick_tile(n_p, (512, 256)))
    o_pool_p = _pool(u_p.reshape(bp, tp, POOL_WIDTH), jnp.zeros((bp, 16, POOL_WIDTH), F32), wbd_bf,
                     pool_scale, og_pool, 0, 1, MOBA_BLOCK).reshape(n_p, POOL_WIDTH)
    nblk = tp // MOBA_BLOCK
    kmean_pad = jnp.pad(kmean_p.reshape(bp, nblk, MOBA_WIDTH), ((0, 0), (0, LANES - nblk), (0, 0)))
    o_attn_p = _moba_prompt(q_p, k_p_bf, v_p_bf, kmean_pad, og_attn, bp, tp)
    o_mem_p = _mem_attn(qm_p, mem_k_p.reshape(bp, MEM_TOKENS, MEM_WIDTH),
                        mem_v_p.reshape(bp, MEM_TOKENS, MEM_WIDTH), og_mem, bp, tp, _pick_tile(tp, (512, 256)))
    y_p = tail(o_pool_p, o_attn_p, o_mem_p, xp)

    past_len = page_table.shape[1] * PAGE_SIZE
    xs = x_sample.reshape(n_s, d)
    u_s, q_s, k_s, v_s, qm_s = _norm_proj(xs, norm1_gain, w_in_bf, head_gain_in, ones_bd, IN_SEGS_SAMPLE, n_s)
    u_s3 = u_s.reshape(bs, ts, POOL_WIDTH)
    buf16 = jnp.concatenate([jnp.zeros((bs, 1, POOL_WIDTH), F32), state_pool], axis=1)
    o_pool_s = _pool(u_s3, buf16, wbd_bf, pool_scale, og_pool, past_len, bs, ts).reshape(n_s, POOL_WIDTH)
    o_attn_s = _moba_sample(page_table, q_s, k_s, v_s,
                            cache_k.reshape(-1, PAGE_SIZE, MOBA_WIDTH), cache_v.reshape(-1, PAGE_SIZE, MOBA_WIDTH),
                            og_attn, bs, ts)
    o_mem_s = _mem_attn(qm_s, cache_mem_k.reshape(bs, MEM_TOKENS, MEM_WIDTH),
                        cache_mem_v.reshape(bs, MEM_TOKENS, MEM_WIDTH), og_mem, bs, ts, ts)
    y_s = tail(o_pool_s, o_attn_s, o_mem_s, xs)

    pool_prompt = u_p.reshape(bp, tp, POOL_WIDTH)[:, tp - POOL_STATE:]
    pool_sample = jnp.concatenate([state_pool, u_s3], axis=1)[:, -POOL_STATE:]
    return (y_p.reshape(bp, tp, d), y_s.reshape(bs, ts, d),
            k_p.reshape(bp, tp, MOBA_HEADS, HEAD_DIM), v_p.reshape(bp, tp, MOBA_HEADS, HEAD_DIM),
            pool_prompt,
            mem_k_p.reshape(bp, MEM_TOKENS, MEM_HEADS, HEAD_DIM), mem_v_p.reshape(bp, MEM_TOKENS, MEM_HEADS, HEAD_DIM),
            k_s.reshape(bs, ts, MOBA_HEADS, HEAD_DIM), v_s.reshape(bs, ts, MOBA_HEADS, HEAD_DIM),
            pool_sample)
```

```python
import functools

import jax
import jax.numpy as jnp
from jax import lax
from jax.experimental import pallas as pl
from jax.experimental.pallas import tpu as pltpu

F32 = jnp.float32
BF16 = jnp.bfloat16

D_MODEL = 1024
HEAD_DIM = 64
POOL_WIDTH = 256
POOL_WINDOWS = (2, 4, 8, 16)
POOL_GROUP_WIDTH = 64
POOL_STATE = 15
MOBA_WIDTH = 512
MOBA_HEADS = 8
MOBA_BLOCK = 256
MOBA_TOPK = 3
MEM_WIDTH = 256
MEM_HEADS = 4
MEM_TOKENS = 256
PAGE_SIZE = 128
MOE_GROUPS = 4
EXPERTS_PER_GROUP = 8
N_EXPERTS = MOE_GROUPS * EXPERTS_PER_GROUP
D_EXPERT = 256
NORM_EPS = 1e-6

LANES = 128
HEAD_PAIR = 2 * HEAD_DIM
QK_SCALE = HEAD_DIM ** -0.5
NEG_INF = float("-inf")
MIB = 1024 * 1024
NT_DIMS = (((1,), (1,)), ((), ()))


def _cparams(semantics, vmem_mib):
    return pltpu.CompilerParams(dimension_semantics=semantics, vmem_limit_bytes=vmem_mib * MIB)


def _rms(y, eps=NORM_EPS):
    return y * lax.rsqrt(jnp.mean(y * y, axis=-1, keepdims=True) + eps)


def _norm_proj_body(x_ref, g_ref, w_ref, hg_ref, ones_ref, *outs, segs, tm):
    hb = (_rms(x_ref[...]) * g_ref[...]).astype(BF16)
    oi = 0
    for (c0, wd, headnorm, want_bf16, want_blockmean) in segs:
        y = jnp.dot(hb, w_ref[:, c0:c0 + wd], preferred_element_type=F32)
        if headnorm:
            sq = (y * y).astype(BF16)
            msq = jnp.dot(sq, ones_ref[:wd, :wd], preferred_element_type=F32) * (1.0 / HEAD_DIM)
            y = (y * lax.rsqrt(msq + NORM_EPS)) * hg_ref[:, c0:c0 + wd]
        outs[oi][...] = y
        oi += 1
        if want_bf16:
            outs[oi][...] = y.astype(BF16)
            oi += 1
        if want_blockmean:
            for bi in range(tm // MOBA_BLOCK):
                outs[oi][bi] = jnp.mean(y[bi * MOBA_BLOCK:(bi + 1) * MOBA_BLOCK], axis=0, keepdims=True)
            oi += 1


def _norm_proj(x, gain, w_bf, head_gain, ones_bd, segs, tm):
    n, d = x.shape
    wtot = w_bf.shape[1]
    assert n % tm == 0
    out_shape, out_specs = [], []
    for (c0, wd, headnorm, want_bf16, want_blockmean) in segs:
        out_shape.append(jax.ShapeDtypeStruct((n, wd), F32))
        out_specs.append(pl.BlockSpec((tm, wd), lambda i: (i, 0)))
        if want_bf16:
            out_shape.append(jax.ShapeDtypeStruct((n, wd), BF16))
            out_specs.append(pl.BlockSpec((tm, wd), lambda i: (i, 0)))
        if want_blockmean:
            assert tm % MOBA_BLOCK == 0
            nb = tm // MOBA_BLOCK
            out_shape.append(jax.ShapeDtypeStruct((n // MOBA_BLOCK, 1, wd), F32))
            out_specs.append(pl.BlockSpec((nb, 1, wd), lambda i: (i, 0, 0)))
    return pl.pallas_call(
        functools.partial(_norm_proj_body, segs=segs, tm=tm),
        grid=(n // tm,),
        in_specs=[
            pl.BlockSpec((tm, d), lambda i: (i, 0)),
            pl.BlockSpec((1, d), lambda i: (0, 0)),
            pl.BlockSpec((d, wtot), lambda i: (0, 0)),
            pl.BlockSpec((1, wtot), lambda i: (0, 0)),
            pl.BlockSpec(ones_bd.shape, lambda i: (0, 0)),
        ],
        out_specs=out_specs,
        out_shape=out_shape,
        compiler_params=_cparams(("parallel",), 48),
        name="norm_proj",
    )(x, gain.reshape(1, d), w_bf, head_gain.reshape(1, wtot), ones_bd)


def _pool_windows(win, pos0):
    r = win.shape[0] - 16
    lane = lax.broadcasted_iota(jnp.int32, (r, LANES), 1)
    pos1 = pos0 + lax.broadcasted_iota(jnp.int32, (r, LANES), 0) + 1
    low = lane < POOL_GROUP_WIDTH
    a = win[:, :LANES]
    b = win[:, LANES:]
    a2 = a + pltpu.roll(a, 1, 0)
    a4 = a2 + pltpu.roll(a2, 2, 0)
    b2 = b + pltpu.roll(b, 1, 0)
    b4 = b2 + pltpu.roll(b2, 2, 0)
    b8 = b4 + pltpu.roll(b4, 4, 0)
    b16 = b8 + pltpu.roll(b8, 8, 0)
    cnt_a = jnp.minimum(jnp.where(low, POOL_WINDOWS[0], POOL_WINDOWS[1]), pos1).astype(F32)
    cnt_b = jnp.minimum(jnp.where(low, POOL_WINDOWS[2], POOL_WINDOWS[3]), pos1).astype(F32)
    pa = jnp.where(low, a2[16:], a4[16:]) / cnt_a - a[16:]
    pb = jnp.where(low, b8[16:], b16[16:]) / cnt_b - b[16:]
    return jnp.concatenate([pa, pb], axis=1)


def _pool_finish(pooled, wbd_ref, ps_ref, og_ref):
    mixed = jnp.dot(pooled.astype(BF16), wbd_ref[...], preferred_element_type=F32) * ps_ref[...]
    return _rms(mixed) * og_ref[...]


def _pool_body(u_ref, buf_ref, wbd_ref, ps_ref, og_ref, o_ref, ext_sc, *, bb, t, r, pos0):
    for bi in range(bb):
        ext_sc[bi, 0:16, :] = buf_ref[bi]
        ext_sc[bi, 16:, :] = u_ref[bi]
    if t == r:
        pooled = [_pool_windows(ext_sc[bi], pos0) for bi in range(bb)]
        out = _pool_finish(jnp.concatenate(pooled, axis=0), wbd_ref, ps_ref, og_ref)
        for bi in range(bb):
            o_ref[bi] = out[bi * r:(bi + 1) * r]
    else:
        assert bb == 1

        def chunk(c, carry):
            base = pl.multiple_of(c * r, r)
            pooled = _pool_windows(ext_sc[0, pl.ds(base, r + 16), :], pos0 + base)
            o_ref[0, pl.ds(base, r), :] = _pool_finish(pooled, wbd_ref, ps_ref, og_ref)
            return carry

        lax.fori_loop(0, t // r, chunk, 0)


def _pool(u, buf16, wbd_bf, pool_scale, og, pos0, bb, r):
    b, t, c = u.shape
    assert b % bb == 0 and t % r == 0
    return pl.pallas_call(
        functools.partial(_pool_body, bb=bb, t=t, r=r, pos0=pos0),
        grid=(b // bb,),
        in_specs=[
            pl.BlockSpec((bb, t, c), lambda i: (i, 0, 0)),
            pl.BlockSpec((bb, 16, c), lambda i: (i, 0, 0)),
            pl.BlockSpec((c, c), lambda i: (0, 0)),
            pl.BlockSpec((1, c), lambda i: (0, 0)),
            pl.BlockSpec((1, c), lambda i: (0, 0)),
        ],
        out_specs=pl.BlockSpec((bb, t, c), lambda i: (i, 0, 0)),
        out_shape=jax.ShapeDtypeStruct((b, t, c), F32),
        scratch_shapes=[pltpu.VMEM((bb, t + 16, c), F32)],
        compiler_params=_cparams(("parallel",), 40),
        name="pool",
    )(u, buf16, wbd_bf, pool_scale.reshape(1, c), og.reshape(1, c))


def _alibi_slope(h):
    return 2.0 ** (-8.0 * (h + 1) / MOBA_HEADS)


def _moba_prompt_body(q_ref, k_ref, v_ref, km_ref, og_ref, o_ref, m_sc, l_sc, acc_sc, sel_sc, qst_sc):
    i = pl.program_id(1)
    tq = MOBA_BLOCK
    n_pairs = MOBA_HEADS // 2
    lane = lax.broadcasted_iota(jnp.int32, (tq, LANES), 1)
    lane_f = lane.astype(F32)
    low = lane < HEAD_DIM
    row2 = lax.broadcasted_iota(jnp.int32, (tq, MOBA_BLOCK), 0)
    col2 = lax.broadcasted_iota(jnp.int32, (tq, MOBA_BLOCK), 1)
    causal = col2 <= row2
    colrow = lax.broadcasted_iota(jnp.int32, (1, MOBA_BLOCK), 1).astype(F32)

    def pair_cols(pr):
        return slice(pr * HEAD_PAIR, (pr + 1) * HEAD_PAIR)

    def scores(pr, j):
        start = pl.multiple_of(j * MOBA_BLOCK, MOBA_BLOCK)
        kj = k_ref[pl.ds(start, MOBA_BLOCK), pair_cols(pr)]
        s = lax.dot_general(qst_sc[pr], kj, NT_DIMS, preferred_element_type=F32)
        off = colrow + ((j - i) * MOBA_BLOCK).astype(F32)
        return [s[hh * tq:(hh + 1) * tq] + _alibi_slope(2 * pr + hh) * off for hh in range(2)]

    def pv(pr, j, ps):
        start = pl.multiple_of(j * MOBA_BLOCK, MOBA_BLOCK)
        vj = v_ref[pl.ds(start, MOBA_BLOCK), pair_cols(pr)]
        return jnp.dot(jnp.concatenate(ps, axis=0).astype(BF16), vj, preferred_element_type=F32)

    for pr in range(n_pairs):
        q_pair = q_ref[:, pair_cols(pr)]
        km_pair = km_ref[0, :, pair_cols(pr)]
        q_heads = [jnp.where(low, q_pair, 0.0), jnp.where(low, 0.0, q_pair)]

        for hh in range(2):
            gate = lax.dot_general(q_heads[hh], km_pair, NT_DIMS, precision=lax.Precision.HIGHEST,
                                   preferred_element_type=F32)
            g = jnp.where(lane < i, gate, NEG_INF)
            sel = jnp.zeros((tq, LANES), F32)
            for _ in range(MOBA_TOPK):
                mx = jnp.max(g, axis=-1, keepdims=True)
                idx = jnp.min(jnp.where(g == mx, lane_f, float(LANES)), axis=-1, keepdims=True)
                pick = (lane_f == idx) & (mx > NEG_INF)
                sel = jnp.where(pick, 1.0, sel)
                g = jnp.where(pick, NEG_INF, g)
            sel_sc[2 * pr + hh] = sel

        qst_sc[pr] = (jnp.concatenate(q_heads, axis=0) * QK_SCALE).astype(BF16)

        s_own = scores(pr, i)
        ps = []
        for hh in range(2):
            sh = jnp.where(causal, s_own[hh], NEG_INF)
            m = jnp.max(sh, axis=-1, keepdims=True)
            p = jnp.exp(sh - m)
            m_sc[2 * pr + hh] = jnp.broadcast_to(m, (tq, LANES))
            l_sc[2 * pr + hh] = jnp.broadcast_to(jnp.sum(p, axis=-1, keepdims=True), (tq, LANES))
            ps.append(p)
        acc_sc[pr] = pv(pr, i, ps)

    def past(j, carry):
        for pr in range(n_pairs):
            s_j = scores(pr, j)
            ps, alphas = [], []
            for hh in range(2):
                h = 2 * pr + hh
                selcol = jnp.sum(jnp.where(lane == j, sel_sc[h], 0.0), axis=-1, keepdims=True)
                sh = jnp.where(selcol > 0.0, s_j[hh], NEG_INF)
                m_prev = m_sc[h]
                m_new = jnp.maximum(m_prev, jnp.max(sh, axis=-1, keepdims=True))
                alpha = jnp.exp(m_prev - m_new)
                p = jnp.exp(sh - jnp.concatenate([m_new, m_new], axis=1))
                l_sc[h] = alpha * l_sc[h] + jnp.sum(p, axis=-1, keepdims=True)
                m_sc[h] = m_new
                ps.append(p)
                alphas.append(alpha)
            acc_sc[pr] = jnp.concatenate(alphas, axis=0) * acc_sc[pr] + pv(pr, j, ps)
        return carry

    lax.fori_loop(0, i, past, 0)

    outs = []
    for pr in range(n_pairs):
        acc = acc_sc[pr]
        outs.append(jnp.where(low, acc[:tq] / l_sc[2 * pr], acc[tq:] / l_sc[2 * pr + 1]))
    o_ref[...] = _rms(jnp.concatenate(outs, axis=1)) * og_ref[...]


def _moba_prompt(q, k_bf, v_bf, kmean_pad, og, b, t):
    n, w = q.shape
    nblk = t // MOBA_BLOCK
    tq = MOBA_BLOCK
    return pl.pallas_call(
        _moba_prompt_body,
        grid=(b, nblk),
        in_specs=[
            pl.BlockSpec((tq, w), lambda bi, i: (bi * nblk + i, 0)),
            pl.BlockSpec((t, w), lambda bi, i: (bi, 0)),
            pl.BlockSpec((t, w), lambda bi, i: (bi, 0)),
            pl.BlockSpec((1, LANES, w), lambda bi, i: (bi, 0, 0)),
            pl.BlockSpec((1, w), lambda bi, i: (0, 0)),
        ],
        out_specs=pl.BlockSpec((tq, w), lambda bi, i: (bi * nblk + i, 0)),
        out_shape=jax.ShapeDtypeStruct((n, w), F32),
        scratch_shapes=[
            pltpu.VMEM((MOBA_HEADS, tq, LANES), F32),
            pltpu.VMEM((MOBA_HEADS, tq, LANES), F32),
            pltpu.VMEM((MOBA_HEADS // 2, 2 * tq, LANES), F32),
            pltpu.VMEM((MOBA_HEADS, tq, LANES), F32),
            pltpu.VMEM((MOBA_HEADS // 2, 2 * tq, LANES), BF16),
        ],
        compiler_params=_cparams(("parallel", "arbitrary"), 40),
        name="moba_prompt",
    )(q, k_bf, v_bf, kmean_pad, og.reshape(1, w))


SAMPLE_BLOCKS_PER_STEP = 8
PAGES_PER_BLOCK = MOBA_BLOCK // PAGE_SIZE
PAGES_PER_STEP = SAMPLE_BLOCKS_PER_STEP * PAGES_PER_BLOCK


def _moba_sample_body(pt_ref, q_ref, kn_ref, vn_ref, slope_ref, og_ref, *rest, n_past_blocks, t_new):
    kp = rest[:PAGES_PER_STEP]
    vp = rest[PAGES_PER_STEP:2 * PAGES_PER_STEP]
    o_ref, o_sc, m_sc, l_sc, kmt_sc = rest[2 * PAGES_PER_STEP:]
    c = pl.program_id(1)
    rows = MOBA_HEADS * t_new
    row_h = lax.broadcasted_iota(jnp.int32, (rows, MOBA_WIDTH), 0) // t_new
    lane_h = lax.broadcasted_iota(jnp.int32, (rows, MOBA_WIDTH), 1) // HEAD_DIM
    bd = row_h == lane_h
    q = q_ref[...]
    qbd = jnp.where(bd, jnp.concatenate([q] * MOBA_HEADS, axis=0), 0.0)
    qbd_bf = (qbd * QK_SCALE).astype(BF16)
    slope = slope_ref[...]
    slope2 = jnp.concatenate([slope, slope], axis=1)
    colf = lax.broadcasted_iota(jnp.int32, (rows, MOBA_BLOCK), 1).astype(F32)

    blk_lane = lax.broadcasted_iota(jnp.int32, (MOBA_WIDTH, LANES), 1)

    @pl.when(c == 0)
    def _init():
        kmt_sc[...] = jnp.zeros((MOBA_WIDTH, LANES), F32)

    for jj in range(SAMPLE_BLOCKS_PER_STEP):
        j = c * SAMPLE_BLOCKS_PER_STEP + jj
        kt_pages = [kp[PAGES_PER_BLOCK * jj + p][0] for p in range(PAGES_PER_BLOCK)]
        vt_pages = [vp[PAGES_PER_BLOCK * jj + p][0] for p in range(PAGES_PER_BLOCK)]
        ksum = kt_pages[0]
        for kt in kt_pages[1:]:
            ksum = ksum + kt
        kmean_col = jnp.sum(ksum, axis=-1, keepdims=True) * (1.0 / MOBA_BLOCK)
        kmt_sc[...] = jnp.where(blk_lane == j, kmean_col, kmt_sc[...])
        kt_bf = jnp.concatenate(kt_pages, axis=1).astype(BF16)
        vt_bf = jnp.concatenate(vt_pages, axis=1).astype(BF16)
        s = jnp.dot(qbd_bf, kt_bf, preferred_element_type=F32)
        s = s + slope2 * (colf + ((j - n_past_blocks) * MOBA_BLOCK).astype(F32))
        m = jnp.max(s, axis=-1, keepdims=True)
        p = jnp.exp(s - m)
        o = lax.dot_general(p.astype(BF16), vt_bf, NT_DIMS, preferred_element_type=F32)
        o_sc[j] = jnp.where(bd, o, 0.0)
        m_sc[j] = jnp.broadcast_to(m, (rows, LANES))
        l_sc[j] = jnp.broadcast_to(jnp.sum(p, axis=-1, keepdims=True), (rows, LANES))

    @pl.when(c == pl.num_programs(1) - 1)
    def _finish():
        tq = lax.broadcasted_iota(jnp.int32, (rows, LANES), 0) % t_new
        kn = kn_ref[...]
        vn = vn_ref[...]
        qs = qbd * QK_SCALE
        s_own = []
        m_run = jnp.full((rows, LANES), NEG_INF, F32)
        for cc in range(t_new):
            sc = jnp.sum(qs * kn[cc:cc + 1, :], axis=-1, keepdims=True) + slope * float(cc)
            sc = jnp.where(tq >= cc, sc, NEG_INF)
            s_own.append(sc)
            m_run = jnp.maximum(m_run, sc)

        gates = jnp.dot(qbd, kmt_sc[...], precision=lax.Precision.HIGHEST, preferred_element_type=F32)
        g_cols = [jnp.broadcast_to(gates[:, j:j + 1], (rows, LANES)) for j in range(n_past_blocks)]

        def top_round(excluded):
            best = jnp.full((rows, LANES), NEG_INF, F32)
            bidx = jnp.full((rows, LANES), -1.0, F32)
            for j in range(n_past_blocks):
                ok = g_cols[j] > best
                for e in excluded:
                    ok = ok & (e != float(j))
                best = jnp.where(ok, g_cols[j], best)
                bidx = jnp.where(ok, float(j), bidx)
            return bidx

        picks = []
        for _ in range(min(MOBA_TOPK, n_past_blocks)):
            picks.append(top_round(picks))
        sels = []
        for j in range(n_past_blocks):
            sj = picks[0] == float(j)
            for e in picks[1:]:
                sj = sj | (e == float(j))
            sels.append(sj)
            m_run = jnp.where(sj, jnp.maximum(m_run, m_sc[j]), m_run)

        l_run = jnp.zeros((rows, LANES), F32)
        o_run = jnp.zeros((rows, MOBA_WIDTH), F32)
        for cc in range(t_new):
            p = jnp.exp(s_own[cc] - m_run)
            l_run = l_run + p
            o_run = o_run + jnp.concatenate([p] * (MOBA_WIDTH // LANES), axis=1) * vn[cc:cc + 1, :]
        o_run = jnp.where(bd, o_run, 0.0)
        for j in range(n_past_blocks):
            wj = jnp.where(sels[j], jnp.exp(m_sc[j] - m_run), 0.0)
            l_run = l_run + wj * l_sc[j]
            o_run = o_run + jnp.concatenate([wj] * (MOBA_WIDTH // LANES), axis=1) * o_sc[j]
        o_bd = o_run / jnp.concatenate([l_run] * (MOBA_WIDTH // LANES), axis=1)
        attn = o_bd[0:t_new]
        for h in range(1, MOBA_HEADS):
            attn = attn + o_bd[h * t_new:(h + 1) * t_new]
        o_ref[...] = _rms(attn) * og_ref[...]


def _moba_sample(page_table, q, k_new, v_new, cache_kt, cache_vt, og, b, t_new):
    n_pages = page_table.shape[1]
    n_past_blocks = n_pages // PAGES_PER_BLOCK
    assert n_past_blocks <= LANES
    assert n_past_blocks % SAMPLE_BLOCKS_PER_STEP == 0
    n_steps = n_past_blocks // SAMPLE_BLOCKS_PER_STEP
    rows = MOBA_HEADS * t_new
    w = MOBA_WIDTH
    slopes = jnp.exp2(-8.0 * jnp.arange(1, MOBA_HEADS + 1, dtype=F32) / MOBA_HEADS)
    slope_rows = jnp.broadcast_to(jnp.repeat(slopes, t_new)[:, None], (rows, LANES))

    def page_spec(p):
        return pl.BlockSpec((1, w, PAGE_SIZE), lambda bi, c, pt: (pt[bi, c * PAGES_PER_STEP + p], 0, 0))

    row_spec = pl.BlockSpec((t_new, w), lambda bi, c, pt: (bi, 0))
    grid_spec = pltpu.PrefetchScalarGridSpec(
        num_scalar_prefetch=1,
        grid=(b, n_steps),
        in_specs=[row_spec, row_spec, row_spec,
                  pl.BlockSpec((rows, LANES), lambda bi, c, pt: (0, 0)),
                  pl.BlockSpec((1, w), lambda bi, c, pt: (0, 0))]
                 + [page_spec(p) for p in range(PAGES_PER_STEP)]
                 + [page_spec(p) for p in range(PAGES_PER_STEP)],
        out_specs=row_spec,
        scratch_shapes=[
            pltpu.VMEM((n_past_blocks, rows, w), F32),
            pltpu.VMEM((n_past_blocks, rows, LANES), F32),
            pltpu.VMEM((n_past_blocks, rows, LANES), F32),
            pltpu.VMEM((w, LANES), F32),
        ],
    )
    return pl.pallas_call(
        functools.partial(_moba_sample_body, n_past_blocks=n_past_blocks, t_new=t_new),
        grid_spec=grid_spec,
        out_shape=jax.ShapeDtypeStruct((b * t_new, w), F32),
        compiler_params=_cparams(("parallel", "arbitrary"), 48),
        name="moba_sample",
    )(page_table, q, k_new, v_new, slope_rows, og.reshape(1, w),
      *([cache_kt] * PAGES_PER_STEP), *([cache_vt] * PAGES_PER_STEP))


def _mem_attn_body(q_ref, mk_ref, mv_ref, og_ref, o_ref, *, tm):
    lane = lax.broadcasted_iota(jnp.int32, (tm, LANES), 1)
    low = lane < HEAD_DIM
    outs = []
    for pr in range(MEM_HEADS // 2):
        cs = slice(pr * HEAD_PAIR, (pr + 1) * HEAD_PAIR)
        q_pair = q_ref[:, cs]
        qst = (jnp.concatenate([jnp.where(low, q_pair, 0.0), jnp.where(low, 0.0, q_pair)], axis=0)
               * QK_SCALE).astype(BF16)
        s = lax.dot_general(qst, mk_ref[0, :, cs].astype(BF16), NT_DIMS, preferred_element_type=F32)
        p = jnp.exp(s - jnp.max(s, axis=-1, keepdims=True))
        l = jnp.sum(p, axis=-1, keepdims=True)
        o = jnp.dot(p.astype(BF16), mv_ref[0, :, cs].astype(BF16), preferred_element_type=F32) / l
        outs.append(jnp.where(low, o[:tm], o[tm:]))
    o_ref[...] = _rms(jnp.concatenate(outs, axis=1)) * og_ref[...]


def _mem_attn(qm, mem_k, mem_v, og, b, t, tm):
    n, w = qm.shape
    assert t % tm == 0
    steps = t // tm
    return pl.pallas_call(
        functools.partial(_mem_attn_body, tm=tm),
        grid=(b, steps),
        in_specs=[
            pl.BlockSpec((tm, w), lambda bi, i: (bi * steps + i, 0)),
            pl.BlockSpec((1, MEM_TOKENS, w), lambda bi, i: (bi, 0, 0)),
            pl.BlockSpec((1, MEM_TOKENS, w), lambda bi, i: (bi, 0, 0)),
            pl.BlockSpec((1, w), lambda bi, i: (0, 0)),
        ],
        out_specs=pl.BlockSpec((tm, w), lambda bi, i: (bi * steps + i, 0)),
        out_shape=jax.ShapeDtypeStruct((n, w), F32),
        compiler_params=_cparams(("parallel", "arbitrary"), 32),
        name="mem_attn",
    )(qm, mem_k, mem_v, og.reshape(1, w))


ROUTER_GROUP_LANE0 = N_EXPERTS


def _outproj_router_body(op_ref, oa_ref, om_ref, x_ref, wo_ref, g2_ref, wr_ref, br_ref,
                         x1_ref, h2_ref, comb_ref, *, tm):
    a0, a1 = POOL_WIDTH, POOL_WIDTH + MOBA_WIDTH
    y = x_ref[...]
    y = y + jnp.dot(op_ref[...].astype(BF16), wo_ref[0:a0, :], preferred_element_type=F32)
    y = y + jnp.dot(oa_ref[...].astype(BF16), wo_ref[a0:a1, :], preferred_element_type=F32)
    y = y + jnp.dot(om_ref[...].astype(BF16), wo_ref[a1:, :], preferred_element_type=F32)
    x1_ref[...] = y
    h2 = _rms(y) * g2_ref[...]
    h2_ref[...] = h2.astype(BF16)

    logits = jnp.dot(h2, wr_ref[...], precision=lax.Precision.HIGHEST,
                     preferred_element_type=F32) + br_ref[...]
    lane_f = lax.broadcasted_iota(jnp.int32, (tm, LANES), 1).astype(F32)
    big = float(LANES)
    g_lo = float(ROUTER_GROUP_LANE0)
    is_g = (lane_f >= g_lo) & (lane_f < g_lo + MOE_GROUPS)
    lg = jnp.where(is_g, logits, NEG_INF)
    mg = jnp.max(lg, axis=-1, keepdims=True)
    pg_top = 1.0 / jnp.sum(jnp.exp(lg - mg), axis=-1, keepdims=True)
    gidx = jnp.min(jnp.where(lg == mg, lane_f, big), axis=-1, keepdims=True) - g_lo
    e_lo = gidx * EXPERTS_PER_GROUP
    in_grp = (lane_f >= e_lo) & (lane_f < e_lo + EXPERTS_PER_GROUP)
    le = jnp.where(in_grp, logits, NEG_INF)
    m1 = jnp.max(le, axis=-1, keepdims=True)
    se = jnp.sum(jnp.exp(le - m1), axis=-1, keepdims=True)
    i1 = jnp.min(jnp.where(le == m1, lane_f, big), axis=-1, keepdims=True)
    le2 = jnp.where(lane_f == i1, NEG_INF, le)
    m2 = jnp.max(le2, axis=-1, keepdims=True)
    i2 = jnp.min(jnp.where(le2 == m2, lane_f, big), axis=-1, keepdims=True)
    p1 = 1.0 / se
    p2 = jnp.exp(m2 - m1) / se
    den = p1 + p2
    comb_ref[...] = jnp.where(lane_f == i1, pg_top * (p1 / den),
                              jnp.where(lane_f == i2, pg_top * (p2 / den), 0.0))


def _outproj_router(o_pool, o_attn, o_mem, x, wo_bf, g2, wr, br, tm):
    n, d = x.shape
    assert n % tm == 0
    row = lambda wdt: pl.BlockSpec((tm, wdt), lambda i: (i, 0))
    full = lambda shp: pl.BlockSpec(shp, lambda i: (0, 0))
    return pl.pallas_call(
        functools.partial(_outproj_router_body, tm=tm),
        grid=(n // tm,),
        in_specs=[row(POOL_WIDTH), row(MOBA_WIDTH), row(MEM_WIDTH), row(d),
                  full((d, d)), full((1, d)), full((d, LANES)), full((1, LANES))],
        out_specs=[row(d), row(d), row(LANES)],
        out_shape=[jax.ShapeDtypeStruct((n, d), F32), jax.ShapeDtypeStruct((n, d), BF16),
                   jax.ShapeDtypeStruct((n, LANES), F32)],
        compiler_params=_cparams(("parallel",), 40),
        name="outproj_router",
    )(o_pool, o_attn, o_mem, x, wo_bf, g2.reshape(1, d), wr, br)


def _moe_body(h2_ref, comb_ref, x1_ref, wg_ref, wu_ref, wd_ref, y_ref, *, tm):
    e = pl.program_id(1)

    @pl.when(e == 0)
    def _init():
        y_ref[...] = x1_ref[...]

    h = h2_ref[...]
    a = jnp.dot(h, wg_ref[0].astype(BF16), preferred_element_type=F32)
    b = jnp.dot(h, wu_ref[0].astype(BF16), preferred_element_type=F32)
    lane = lax.broadcasted_iota(jnp.int32, (tm, LANES), 1)
    ce = jnp.sum(jnp.where(lane == e, comb_ref[...], 0.0), axis=-1, keepdims=True)
    hid = (a * jax.nn.sigmoid(a)) * b * ce
    y_ref[...] += jnp.dot(hid.astype(BF16), wd_ref[0].astype(BF16), preferred_element_type=F32)


def _moe(h2, comb, x1, w_gate, w_up, w_down, tm):
    n, d = x1.shape
    assert n % tm == 0
    return pl.pallas_call(
        functools.partial(_moe_body, tm=tm),
        grid=(n // tm, N_EXPERTS),
        in_specs=[
            pl.BlockSpec((tm, d), lambda i, e: (i, 0)),
            pl.BlockSpec((tm, LANES), lambda i, e: (i, 0)),
            pl.BlockSpec((tm, d), lambda i, e: (i, 0)),
            pl.BlockSpec((1, d, D_EXPERT), lambda i, e: (e, 0, 0)),
            pl.BlockSpec((1, d, D_EXPERT), lambda i, e: (e, 0, 0)),
            pl.BlockSpec((1, D_EXPERT, d), lambda i, e: (e, 0, 0)),
        ],
        out_specs=pl.BlockSpec((tm, d), lambda i, e: (i, 0)),
        out_shape=jax.ShapeDtypeStruct((n, d), F32),
        compiler_params=_cparams(("parallel", "arbitrary"), 48),
        name="moe",
    )(h2, comb, x1, w_gate, w_up, w_down)


def _block_diag_ones(width, group):
    r = lax.broadcasted_iota(jnp.int32, (width, width), 0) // group
    c = lax.broadcasted_iota(jnp.int32, (width, width), 1) // group
    return (r == c).astype(BF16)


def _pick_tile(n, candidates):
    for c in candidates:
        if n % c == 0:
            return c
    return n


IN_SEGS = (
    (0, POOL_WIDTH, False, False, False),
    (POOL_WIDTH, MOBA_WIDTH, True, False, False),
    (POOL_WIDTH + MOBA_WIDTH, MOBA_WIDTH, True, True, True),
    (POOL_WIDTH + 2 * MOBA_WIDTH, MOBA_WIDTH, False, True, False),
    (POOL_WIDTH + 3 * MOBA_WIDTH, MEM_WIDTH, True, False, False),
)
IN_SEGS_SAMPLE = tuple((c0, wd, hn, False, False) for (c0, wd, hn, _, _) in IN_SEGS)
MEMKV_SEGS = ((0, MEM_WIDTH, True, False, False), (MEM_WIDTH, MEM_WIDTH, False, False, False))


def kernel(x_prompt, x_sample, mem_prompt, cache_k, cache_v, state_pool, cache_mem_k, cache_mem_v,
           page_table, norm1_gain, w_in, pool_w, pool_scale, moba_q_gain, moba_k_gain,
           mem_norm_gain, w_mem_kv, mem_q_gain, mem_k_gain, out_gain, w_out, norm2_gain,
           router_group_w, router_group_b, router_expert_w, router_expert_b, w_gate, w_up, w_down):
    bp, tp, d = x_prompt.shape
    bs, ts, _ = x_sample.shape
    n_p, n_s = bp * tp, bs * ts

    w_in_bf = w_in.astype(BF16)
    w_mem_bf = w_mem_kv.astype(BF16)
    wo_bf = w_out.astype(BF16)
    ones_bd = _block_diag_ones(MOBA_WIDTH, HEAD_DIM)
    head_gain_in = jnp.concatenate([
        jnp.ones((POOL_WIDTH,), F32), jnp.tile(moba_q_gain, MOBA_HEADS), jnp.tile(moba_k_gain, MOBA_HEADS),
        jnp.ones((MOBA_WIDTH,), F32), jnp.tile(mem_q_gain, MEM_HEADS)])
    head_gain_mem = jnp.concatenate([jnp.tile(mem_k_gain, MEM_HEADS), jnp.ones((MEM_WIDTH,), F32)])
    wbd = jnp.zeros((POOL_WIDTH, POOL_WIDTH), F32)
    for g in range(len(POOL_WINDOWS)):
        sl = slice(g * POOL_GROUP_WIDTH, (g + 1) * POOL_GROUP_WIDTH)
        wbd = wbd.at[sl, sl].set(pool_w[g])
    wbd_bf = wbd.astype(BF16)
    og_pool, og_attn, og_mem = (out_gain[:POOL_WIDTH], out_gain[POOL_WIDTH:POOL_WIDTH + MOBA_WIDTH],
                                out_gain[POOL_WIDTH + MOBA_WIDTH:])
    wr = jnp.zeros((d, LANES), F32)
    wr = wr.at[:, :N_EXPERTS].set(router_expert_w).at[:, N_EXPERTS:N_EXPERTS + MOE_GROUPS].set(router_group_w)
    br = jnp.zeros((1, LANES), F32)
    br = br.at[0, :N_EXPERTS].set(router_expert_b).at[0, N_EXPERTS:N_EXPERTS + MOE_GROUPS].set(router_group_b)
    wg = w_gate.reshape(N_EXPERTS, d, D_EXPERT)
    wu = w_up.reshape(N_EXPERTS, d, D_EXPERT)
    wd = w_down.reshape(N_EXPERTS, D_EXPERT, d)

    def tail(o_pool, o_attn, o_mem, x2d):
        n = x2d.shape[0]
        x1, h2, comb = _outproj_router(o_pool, o_attn, o_mem, x2d, wo_bf, norm2_gain, wr, br,
                                       _pick_tile(n, (512, 256)))
        return _moe(h2, comb, x1, wg, wu, wd, _pick_tile(n, (1024, 512, 256)))

    mem_k_p, mem_v_p = _norm_proj(mem_prompt.reshape(bp * MEM_TOKENS, d), mem_norm_gain, w_mem_bf,
                                  head_gain_mem, ones_bd, MEMKV_SEGS, _pick_tile(bp * MEM_TOKENS, (512, 256)))
    xp = x_prompt.reshape(n_p, d)
    u_p, q_p, k_p, k_p_bf, kmean_p, v_p, v_p_bf, qm_p = _norm_proj(
        xp, norm1_gain, w_in_bf, head_gain_in, ones_bd, IN_SEGS, _pick_tile(n_p, (512, 256)))
    o_pool_p = _pool(u_p.reshape(bp, tp, POOL_WIDTH), jnp.zeros((bp, 16, POOL_WIDTH), F32), wbd_bf,
                     pool_scale, og_pool, 0, 1, MOBA_BLOCK).reshape(n_p, POOL_WIDTH)
    nblk = tp // MOBA_BLOCK
    kmean_pad = jnp.pad(kmean_p.reshape(bp, nblk, MOBA_WIDTH), ((0, 0), (0, LANES - nblk), (0, 0)))
    o_attn_p = _moba_prompt(q_p, k_p_bf, v_p_bf, kmean_pad, og_attn, bp, tp)
    o_mem_p = _mem_attn(qm_p, mem_k_p.reshape(bp, MEM_TOKENS, MEM_WIDTH),
                        mem_v_p.reshape(bp, MEM_TOKENS, MEM_WIDTH), og_mem, bp, tp, _pick_tile(tp, (512, 256)))
    y_p = tail(o_pool_p, o_attn_p, o_mem_p, xp)

    past_len = page_table.shape[1] * PAGE_SIZE
    xs = x_sample.reshape(n_s, d)
    u_s, q_s, k_s, v_s, qm_s = _norm_proj(xs, norm1_gain, w_in_bf, head_gain_in, ones_bd, IN_SEGS_SAMPLE, n_s)
    u_s3 = u_s.reshape(bs, ts, POOL_WIDTH)
    buf16 = jnp.concatenate([jnp.zeros((bs, 1, POOL_WIDTH), F32), state_pool], axis=1)
    o_pool_s = _pool(u_s3, buf16, wbd_bf, pool_scale, og_pool, past_len, bs, ts).reshape(n_s, POOL_WIDTH)
    o_attn_s = _moba_sample(page_table, q_s, k_s, v_s,
                            cache_k.transpose(0, 2, 3, 1).reshape(-1, MOBA_WIDTH, PAGE_SIZE),
                            cache_v.transpose(0, 2, 3, 1).reshape(-1, MOBA_WIDTH, PAGE_SIZE),
                            og_attn, bs, ts)
    o_mem_s = _mem_attn(qm_s, cache_mem_k.reshape(bs, MEM_TOKENS, MEM_WIDTH),
                        cache_mem_v.reshape(bs, MEM_TOKENS, MEM_WIDTH), og_mem, bs, ts, ts)
    y_s = tail(o_pool_s, o_attn_s, o_mem_s, xs)

    pool_prompt = u_p.reshape(bp, tp, POOL_WIDTH)[:, tp - POOL_STATE:]
    pool_sample = jnp.concatenate([state_pool, u_s3], axis=1)[:, -POOL_STATE:]
    return (y_p.reshape(bp, tp, d), y_s.reshape(bs, ts, d),
            k_p.reshape(bp, tp, MOBA_HEADS, HEAD_DIM), v_p.reshape(bp, tp, MOBA_HEADS, HEAD_DIM),
            pool_prompt,
            mem_k_p.reshape(bp, MEM_TOKENS, MEM_HEADS, HEAD_DIM), mem_v_p.reshape(bp, MEM_TOKENS, MEM_HEADS, HEAD_DIM),
            k_s.reshape(bs, ts, MOBA_HEADS, HEAD_DIM), v_s.reshape(bs, ts, MOBA_HEADS, HEAD_DIM),
            pool_sample)
```

```python
import functools

import jax
import jax.numpy as jnp
from jax import lax
from jax.experimental import pallas as pl
from jax.experimental.pallas import tpu as pltpu

F32 = jnp.float32
BF16 = jnp.bfloat16

D_MODEL = 1024
HEAD_DIM = 64
POOL_WIDTH = 256
POOL_WINDOWS = (2, 4, 8, 16)
POOL_GROUP_WIDTH = 64
POOL_STATE = 15
MOBA_WIDTH = 512
MOBA_HEADS = 8
MOBA_BLOCK = 256
MOBA_TOPK = 3
MEM_WIDTH = 256
MEM_HEADS = 4
MEM_TOKENS = 256
PAGE_SIZE = 128
MOE_GROUPS = 4
EXPERTS_PER_GROUP = 8
N_EXPERTS = MOE_GROUPS * EXPERTS_PER_GROUP
D_EXPERT = 256
NORM_EPS = 1e-6

LANES = 128
HEAD_PAIR = 2 * HEAD_DIM
QK_SCALE = HEAD_DIM ** -0.5
NEG_INF = float("-inf")
MIB = 1024 * 1024
NT_DIMS = (((1,), (1,)), ((), ()))


def _cparams(semantics, vmem_mib):
    return pltpu.CompilerParams(dimension_semantics=semantics, vmem_limit_bytes=vmem_mib * MIB)


def _rms(y, eps=NORM_EPS):
    return y * lax.rsqrt(jnp.mean(y * y, axis=-1, keepdims=True) + eps)


def _norm_proj_body(x_ref, g_ref, w_ref, hg_ref, ones_ref, *outs, segs, tm):
    hb = (_rms(x_ref[...]) * g_ref[...]).astype(BF16)
    oi = 0
    for (c0, wd, headnorm, transposed, want_bf16, want_blockmean) in segs:
        y = jnp.dot(hb, w_ref[:, c0:c0 + wd], preferred_element_type=F32)
        if headnorm:
            sq = (y * y).astype(BF16)
            msq = jnp.dot(sq, ones_ref[:wd, :wd], preferred_element_type=F32) * (1.0 / HEAD_DIM)
            y = (y * lax.rsqrt(msq + NORM_EPS)) * hg_ref[:, c0:c0 + wd]
        if transposed:
            yt = y.T
            outs[oi][0] = yt
        else:
            outs[oi][...] = y
        oi += 1
        if want_bf16 == "row":
            outs[oi][...] = y.astype(BF16)
            oi += 1
        elif want_bf16 == "transposed":
            outs[oi][0] = yt.astype(BF16)
            oi += 1
        if want_blockmean:
            for bi in range(tm // MOBA_BLOCK):
                outs[oi][bi] = jnp.mean(y[bi * MOBA_BLOCK:(bi + 1) * MOBA_BLOCK], axis=0, keepdims=True)
            oi += 1


def _norm_proj(x, gain, w_bf, head_gain, ones_bd, segs, tm, rows_per_batch=None):
    n, d = x.shape
    wtot = w_bf.shape[1]
    assert n % tm == 0
    out_shape, out_specs = [], []
    for (c0, wd, headnorm, transposed, want_bf16, want_blockmean) in segs:
        if transposed:
            assert rows_per_batch % tm == 0 and n % rows_per_batch == 0
            steps = rows_per_batch // tm
            out_shape.append(jax.ShapeDtypeStruct((n // rows_per_batch, wd, rows_per_batch), F32))
            out_specs.append(pl.BlockSpec((1, wd, tm), lambda i, steps=steps: (i // steps, 0, i % steps)))
        else:
            out_shape.append(jax.ShapeDtypeStruct((n, wd), F32))
            out_specs.append(pl.BlockSpec((tm, wd), lambda i: (i, 0)))
        if want_bf16 == "row":
            out_shape.append(jax.ShapeDtypeStruct((n, wd), BF16))
            out_specs.append(pl.BlockSpec((tm, wd), lambda i: (i, 0)))
        elif want_bf16 == "transposed":
            assert transposed
            out_shape.append(jax.ShapeDtypeStruct((n // rows_per_batch, wd, rows_per_batch), BF16))
            out_specs.append(pl.BlockSpec((1, wd, tm), lambda i, steps=steps: (i // steps, 0, i % steps)))
        if want_blockmean:
            assert tm % MOBA_BLOCK == 0
            nb = tm // MOBA_BLOCK
            out_shape.append(jax.ShapeDtypeStruct((n // MOBA_BLOCK, 1, wd), F32))
            out_specs.append(pl.BlockSpec((nb, 1, wd), lambda i: (i, 0, 0)))
    return pl.pallas_call(
        functools.partial(_norm_proj_body, segs=segs, tm=tm),
        grid=(n // tm,),
        in_specs=[
            pl.BlockSpec((tm, d), lambda i: (i, 0)),
            pl.BlockSpec((1, d), lambda i: (0, 0)),
            pl.BlockSpec((d, wtot), lambda i: (0, 0)),
            pl.BlockSpec((1, wtot), lambda i: (0, 0)),
            pl.BlockSpec(ones_bd.shape, lambda i: (0, 0)),
        ],
        out_specs=out_specs,
        out_shape=out_shape,
        compiler_params=_cparams(("parallel",), 48),
        name="norm_proj",
    )(x, gain.reshape(1, d), w_bf, head_gain.reshape(1, wtot), ones_bd)


def _pool_windows(win, pos0):
    r = win.shape[0] - 16
    lane = lax.broadcasted_iota(jnp.int32, (r, LANES), 1)
    pos1 = pos0 + lax.broadcasted_iota(jnp.int32, (r, LANES), 0) + 1
    low = lane < POOL_GROUP_WIDTH
    a = win[:, :LANES]
    b = win[:, LANES:]
    a2 = a + pltpu.roll(a, 1, 0)
    a4 = a2 + pltpu.roll(a2, 2, 0)
    b2 = b + pltpu.roll(b, 1, 0)
    b4 = b2 + pltpu.roll(b2, 2, 0)
    b8 = b4 + pltpu.roll(b4, 4, 0)
    b16 = b8 + pltpu.roll(b8, 8, 0)
    cnt_a = jnp.minimum(jnp.where(low, POOL_WINDOWS[0], POOL_WINDOWS[1]), pos1).astype(F32)
    cnt_b = jnp.minimum(jnp.where(low, POOL_WINDOWS[2], POOL_WINDOWS[3]), pos1).astype(F32)
    pa = jnp.where(low, a2[16:], a4[16:]) / cnt_a - a[16:]
    pb = jnp.where(low, b8[16:], b16[16:]) / cnt_b - b[16:]
    return jnp.concatenate([pa, pb], axis=1)


def _pool_finish(pooled, wbd_ref, ps_ref, og_ref):
    mixed = jnp.dot(pooled.astype(BF16), wbd_ref[...], preferred_element_type=F32) * ps_ref[...]
    return _rms(mixed) * og_ref[...]


def _pool_body(u_ref, buf_ref, wbd_ref, ps_ref, og_ref, o_ref, ext_sc, *, bb, t, r, pos0):
    for bi in range(bb):
        ext_sc[bi, 0:16, :] = buf_ref[bi]
        ext_sc[bi, 16:, :] = u_ref[bi]
    if t == r:
        pooled = [_pool_windows(ext_sc[bi], pos0) for bi in range(bb)]
        out = _pool_finish(jnp.concatenate(pooled, axis=0), wbd_ref, ps_ref, og_ref)
        for bi in range(bb):
            o_ref[bi] = out[bi * r:(bi + 1) * r]
    else:
        assert bb == 1

        def chunk(c, carry):
            base = pl.multiple_of(c * r, r)
            pooled = _pool_windows(ext_sc[0, pl.ds(base, r + 16), :], pos0 + base)
            o_ref[0, pl.ds(base, r), :] = _pool_finish(pooled, wbd_ref, ps_ref, og_ref)
            return carry

        lax.fori_loop(0, t // r, chunk, 0)


def _pool(u, buf16, wbd_bf, pool_scale, og, pos0, bb, r):
    b, t, c = u.shape
    assert b % bb == 0 and t % r == 0
    return pl.pallas_call(
        functools.partial(_pool_body, bb=bb, t=t, r=r, pos0=pos0),
        grid=(b // bb,),
        in_specs=[
            pl.BlockSpec((bb, t, c), lambda i: (i, 0, 0)),
            pl.BlockSpec((bb, 16, c), lambda i: (i, 0, 0)),
            pl.BlockSpec((c, c), lambda i: (0, 0)),
            pl.BlockSpec((1, c), lambda i: (0, 0)),
            pl.BlockSpec((1, c), lambda i: (0, 0)),
        ],
        out_specs=pl.BlockSpec((bb, t, c), lambda i: (i, 0, 0)),
        out_shape=jax.ShapeDtypeStruct((b, t, c), F32),
        scratch_shapes=[pltpu.VMEM((bb, t + 16, c), F32)],
        compiler_params=_cparams(("parallel",), 40),
        name="pool",
    )(u, buf16, wbd_bf, pool_scale.reshape(1, c), og.reshape(1, c))


def _alibi_slope(h):
    return 2.0 ** (-8.0 * (h + 1) / MOBA_HEADS)


def _moba_prompt_body(q_ref, k_ref, v_ref, km_ref, og_ref, o_ref, m_sc, l_sc, acc_sc, sel_sc, qst_sc):
    i = pl.program_id(1)
    tq = MOBA_BLOCK
    n_pairs = MOBA_HEADS // 2
    lane = lax.broadcasted_iota(jnp.int32, (tq, LANES), 1)
    lane_f = lane.astype(F32)
    low = lane < HEAD_DIM
    row2 = lax.broadcasted_iota(jnp.int32, (tq, MOBA_BLOCK), 0)
    col2 = lax.broadcasted_iota(jnp.int32, (tq, MOBA_BLOCK), 1)
    causal = col2 <= row2
    colrow = lax.broadcasted_iota(jnp.int32, (1, MOBA_BLOCK), 1).astype(F32)

    def pair_cols(pr):
        return slice(pr * HEAD_PAIR, (pr + 1) * HEAD_PAIR)

    def scores(pr, j):
        start = pl.multiple_of(j * MOBA_BLOCK, MOBA_BLOCK)
        kj = k_ref[pl.ds(start, MOBA_BLOCK), pair_cols(pr)]
        s = lax.dot_general(qst_sc[pr], kj, NT_DIMS, preferred_element_type=F32)
        off = colrow + ((j - i) * MOBA_BLOCK).astype(F32)
        return [s[hh * tq:(hh + 1) * tq] + _alibi_slope(2 * pr + hh) * off for hh in range(2)]

    def pv(pr, j, ps):
        start = pl.multiple_of(j * MOBA_BLOCK, MOBA_BLOCK)
        vj = v_ref[pl.ds(start, MOBA_BLOCK), pair_cols(pr)]
        return jnp.dot(jnp.concatenate(ps, axis=0).astype(BF16), vj, preferred_element_type=F32)

    for pr in range(n_pairs):
        q_pair = q_ref[:, pair_cols(pr)]
        km_pair = km_ref[0, :, pair_cols(pr)]
        q_heads = [jnp.where(low, q_pair, 0.0), jnp.where(low, 0.0, q_pair)]

        for hh in range(2):
            gate = lax.dot_general(q_heads[hh], km_pair, NT_DIMS, precision=lax.Precision.HIGHEST,
                                   preferred_element_type=F32)
            g = jnp.where(lane < i, gate, NEG_INF)
            sel = jnp.zeros((tq, LANES), F32)
            for _ in range(MOBA_TOPK):
                mx = jnp.max(g, axis=-1, keepdims=True)
                idx = jnp.min(jnp.where(g == mx, lane_f, float(LANES)), axis=-1, keepdims=True)
                pick = (lane_f == idx) & (mx > NEG_INF)
                sel = jnp.where(pick, 1.0, sel)
                g = jnp.where(pick, NEG_INF, g)
            sel_sc[2 * pr + hh] = sel

        qst_sc[pr] = (jnp.concatenate(q_heads, axis=0) * QK_SCALE).astype(BF16)

        s_own = scores(pr, i)
        ps = []
        for hh in range(2):
            sh = jnp.where(causal, s_own[hh], NEG_INF)
            m = jnp.max(sh, axis=-1, keepdims=True)
            p = jnp.exp(sh - m)
            m_sc[2 * pr + hh] = jnp.broadcast_to(m, (tq, LANES))
            l_sc[2 * pr + hh] = jnp.broadcast_to(jnp.sum(p, axis=-1, keepdims=True), (tq, LANES))
            ps.append(p)
        acc_sc[pr] = pv(pr, i, ps)

    def past(j, carry):
        for pr in range(n_pairs):
            s_j = scores(pr, j)
            ps, alphas = [], []
            for hh in range(2):
                h = 2 * pr + hh
                selcol = jnp.sum(jnp.where(lane == j, sel_sc[h], 0.0), axis=-1, keepdims=True)
                sh = jnp.where(selcol > 0.0, s_j[hh], NEG_INF)
                m_prev = m_sc[h]
                m_new = jnp.maximum(m_prev, jnp.max(sh, axis=-1, keepdims=True))
                alpha = jnp.exp(m_prev - m_new)
                p = jnp.exp(sh - jnp.concatenate([m_new, m_new], axis=1))
                l_sc[h] = alpha * l_sc[h] + jnp.sum(p, axis=-1, keepdims=True)
                m_sc[h] = m_new
                ps.append(p)
                alphas.append(alpha)
            acc_sc[pr] = jnp.concatenate(alphas, axis=0) * acc_sc[pr] + pv(pr, j, ps)
        return carry

    lax.fori_loop(0, i, past, 0)

    outs = []
    for pr in range(n_pairs):
        acc = acc_sc[pr]
        outs.append(jnp.where(low, acc[:tq] / l_sc[2 * pr], acc[tq:] / l_sc[2 * pr + 1]))
    o_ref[...] = _rms(jnp.concatenate(outs, axis=1)) * og_ref[...]


def _moba_prompt(q, k_bf, v_bf, kmean_pad, og, b, t):
    n, w = q.shape
    nblk = t // MOBA_BLOCK
    tq = MOBA_BLOCK
    return pl.pallas_call(
        _moba_prompt_body,
        grid=(b, nblk),
        in_specs=[
            pl.BlockSpec((tq, w), lambda bi, i: (bi * nblk + i, 0)),
            pl.BlockSpec((t, w), lambda bi, i: (bi, 0)),
            pl.BlockSpec((t, w), lambda bi, i: (bi, 0)),
            pl.BlockSpec((1, LANES, w), lambda bi, i: (bi, 0, 0)),
            pl.BlockSpec((1, w), lambda bi, i: (0, 0)),
        ],
        out_specs=pl.BlockSpec((tq, w), lambda bi, i: (bi * nblk + i, 0)),
        out_shape=jax.ShapeDtypeStruct((n, w), F32),
        scratch_shapes=[
            pltpu.VMEM((MOBA_HEADS, tq, LANES), F32),
            pltpu.VMEM((MOBA_HEADS, tq, LANES), F32),
            pltpu.VMEM((MOBA_HEADS // 2, 2 * tq, LANES), F32),
            pltpu.VMEM((MOBA_HEADS, tq, LANES), F32),
            pltpu.VMEM((MOBA_HEADS // 2, 2 * tq, LANES), BF16),
        ],
        compiler_params=_cparams(("parallel", "arbitrary"), 40),
        name="moba_prompt",
    )(q, k_bf, v_bf, kmean_pad, og.reshape(1, w))


SAMPLE_BLOCKS_PER_STEP = 8
PAGES_PER_BLOCK = MOBA_BLOCK // PAGE_SIZE
PAGES_PER_STEP = SAMPLE_BLOCKS_PER_STEP * PAGES_PER_BLOCK


def _moba_sample_body(pt_ref, q_ref, kn_ref, vn_ref, slope_ref, og_ref, *rest, n_past_blocks, t_new):
    kp = rest[:PAGES_PER_STEP]
    vp = rest[PAGES_PER_STEP:2 * PAGES_PER_STEP]
    o_ref, o_sc, m_sc, l_sc, kmt_sc = rest[2 * PAGES_PER_STEP:]
    c = pl.program_id(1)
    rows = MOBA_HEADS * t_new
    row_h = lax.broadcasted_iota(jnp.int32, (rows, MOBA_WIDTH), 0) // t_new
    lane_h = lax.broadcasted_iota(jnp.int32, (rows, MOBA_WIDTH), 1) // HEAD_DIM
    bd = row_h == lane_h
    q = q_ref[...]
    qbd = jnp.where(bd, jnp.concatenate([q] * MOBA_HEADS, axis=0), 0.0)
    qbd_bf = (qbd * QK_SCALE).astype(BF16)
    slope = slope_ref[...]
    slope2 = jnp.concatenate([slope, slope], axis=1)
    colf = lax.broadcasted_iota(jnp.int32, (rows, MOBA_BLOCK), 1).astype(F32)

    blk_lane = lax.broadcasted_iota(jnp.int32, (MOBA_WIDTH, LANES), 1)

    @pl.when(c == 0)
    def _init():
        kmt_sc[...] = jnp.zeros((MOBA_WIDTH, LANES), F32)

    for jj in range(SAMPLE_BLOCKS_PER_STEP):
        j = c * SAMPLE_BLOCKS_PER_STEP + jj
        kt_pages = [kp[PAGES_PER_BLOCK * jj + p][0] for p in range(PAGES_PER_BLOCK)]
        vt_pages = [vp[PAGES_PER_BLOCK * jj + p][0] for p in range(PAGES_PER_BLOCK)]
        ksum = kt_pages[0]
        for kt in kt_pages[1:]:
            ksum = ksum + kt
        kmean_col = jnp.sum(ksum, axis=-1, keepdims=True) * (1.0 / MOBA_BLOCK)
        kmt_sc[...] = jnp.where(blk_lane == j, kmean_col, kmt_sc[...])
        kt_bf = jnp.concatenate(kt_pages, axis=1).astype(BF16)
        vt_bf = jnp.concatenate(vt_pages, axis=1).astype(BF16)
        s = jnp.dot(qbd_bf, kt_bf, preferred_element_type=F32)
        s = s + slope2 * (colf + ((j - n_past_blocks) * MOBA_BLOCK).astype(F32))
        m = jnp.max(s, axis=-1, keepdims=True)
        p = jnp.exp(s - m)
        o = lax.dot_general(p.astype(BF16), vt_bf, NT_DIMS, preferred_element_type=F32)
        o_sc[j] = jnp.where(bd, o, 0.0)
        m_sc[j] = jnp.broadcast_to(m, (rows, LANES))
        l_sc[j] = jnp.broadcast_to(jnp.sum(p, axis=-1, keepdims=True), (rows, LANES))

    @pl.when(c == pl.num_programs(1) - 1)
    def _finish():
        tq = lax.broadcasted_iota(jnp.int32, (rows, LANES), 0) % t_new
        kn = kn_ref[...]
        vn = vn_ref[...]
        qs = qbd * QK_SCALE
        s_own = []
        m_run = jnp.full((rows, LANES), NEG_INF, F32)
        for cc in range(t_new):
            sc = jnp.sum(qs * kn[cc:cc + 1, :], axis=-1, keepdims=True) + slope * float(cc)
            sc = jnp.where(tq >= cc, sc, NEG_INF)
            s_own.append(sc)
            m_run = jnp.maximum(m_run, sc)

        gates = jnp.dot(qbd, kmt_sc[...], precision=lax.Precision.HIGHEST, preferred_element_type=F32)
        g_cols = [jnp.broadcast_to(gates[:, j:j + 1], (rows, LANES)) for j in range(n_past_blocks)]

        def top_round(excluded):
            best = jnp.full((rows, LANES), NEG_INF, F32)
            bidx = jnp.full((rows, LANES), -1.0, F32)
            for j in range(n_past_blocks):
                ok = g_cols[j] > best
                for e in excluded:
                    ok = ok & (e != float(j))
                best = jnp.where(ok, g_cols[j], best)
                bidx = jnp.where(ok, float(j), bidx)
            return bidx

        picks = []
        for _ in range(min(MOBA_TOPK, n_past_blocks)):
            picks.append(top_round(picks))
        sels = []
        for j in range(n_past_blocks):
            sj = picks[0] == float(j)
            for e in picks[1:]:
                sj = sj | (e == float(j))
            sels.append(sj)
            m_run = jnp.where(sj, jnp.maximum(m_run, m_sc[j]), m_run)

        l_run = jnp.zeros((rows, LANES), F32)
        o_run = jnp.zeros((rows, MOBA_WIDTH), F32)
        for cc in range(t_new):
            p = jnp.exp(s_own[cc] - m_run)
            l_run = l_run + p
            o_run = o_run + jnp.concatenate([p] * (MOBA_WIDTH // LANES), axis=1) * vn[cc:cc + 1, :]
        o_run = jnp.where(bd, o_run, 0.0)
        for j in range(n_past_blocks):
            wj = jnp.where(sels[j], jnp.exp(m_sc[j] - m_run), 0.0)
            l_run = l_run + wj * l_sc[j]
            o_run = o_run + jnp.concatenate([wj] * (MOBA_WIDTH // LANES), axis=1) * o_sc[j]
        o_bd = o_run / jnp.concatenate([l_run] * (MOBA_WIDTH // LANES), axis=1)
        attn = o_bd[0:t_new]
        for h in range(1, MOBA_HEADS):
            attn = attn + o_bd[h * t_new:(h + 1) * t_new]
        o_ref[...] = _rms(attn) * og_ref[...]


def _moba_sample(page_table, q, k_new, v_new, cache_kt, cache_vt, og, b, t_new):
    n_pages = page_table.shape[1]
    n_past_blocks = n_pages // PAGES_PER_BLOCK
    assert n_past_blocks <= LANES
    assert n_past_blocks % SAMPLE_BLOCKS_PER_STEP == 0
    n_steps = n_past_blocks // SAMPLE_BLOCKS_PER_STEP
    rows = MOBA_HEADS * t_new
    w = MOBA_WIDTH
    slopes = jnp.exp2(-8.0 * jnp.arange(1, MOBA_HEADS + 1, dtype=F32) / MOBA_HEADS)
    slope_rows = jnp.broadcast_to(jnp.repeat(slopes, t_new)[:, None], (rows, LANES))

    def page_spec(p):
        return pl.BlockSpec((1, w, PAGE_SIZE), lambda bi, c, pt: (pt[bi, c * PAGES_PER_STEP + p], 0, 0))

    row_spec = pl.BlockSpec((t_new, w), lambda bi, c, pt: (bi, 0))
    grid_spec = pltpu.PrefetchScalarGridSpec(
        num_scalar_prefetch=1,
        grid=(b, n_steps),
        in_specs=[row_spec, row_spec, row_spec,
                  pl.BlockSpec((rows, LANES), lambda bi, c, pt: (0, 0)),
                  pl.BlockSpec((1, w), lambda bi, c, pt: (0, 0))]
                 + [page_spec(p) for p in range(PAGES_PER_STEP)]
                 + [page_spec(p) for p in range(PAGES_PER_STEP)],
        out_specs=row_spec,
        scratch_shapes=[
            pltpu.VMEM((n_past_blocks, rows, w), F32),
            pltpu.VMEM((n_past_blocks, rows, LANES), F32),
            pltpu.VMEM((n_past_blocks, rows, LANES), F32),
            pltpu.VMEM((w, LANES), F32),
        ],
    )
    return pl.pallas_call(
        functools.partial(_moba_sample_body, n_past_blocks=n_past_blocks, t_new=t_new),
        grid_spec=grid_spec,
        out_shape=jax.ShapeDtypeStruct((b * t_new, w), F32),
        compiler_params=_cparams(("parallel", "arbitrary"), 48),
        name="moba_sample",
    )(page_table, q, k_new, v_new, slope_rows, og.reshape(1, w),
      *([cache_kt] * PAGES_PER_STEP), *([cache_vt] * PAGES_PER_STEP))


def _mem_attn_body(q_ref, mk_ref, mv_ref, og_ref, o_ref, *, tm):
    lane = lax.broadcasted_iota(jnp.int32, (tm, LANES), 1)
    low = lane < HEAD_DIM
    outs = []
    for pr in range(MEM_HEADS // 2):
        cs = slice(pr * HEAD_PAIR, (pr + 1) * HEAD_PAIR)
        q_pair = q_ref[:, cs]
        qst = (jnp.concatenate([jnp.where(low, q_pair, 0.0), jnp.where(low, 0.0, q_pair)], axis=0)
               * QK_SCALE).astype(BF16)
        s = lax.dot_general(qst, mk_ref[0, :, cs].astype(BF16), NT_DIMS, preferred_element_type=F32)
        p = jnp.exp(s - jnp.max(s, axis=-1, keepdims=True))
        l = jnp.sum(p, axis=-1, keepdims=True)
        o = jnp.dot(p.astype(BF16), mv_ref[0, :, cs].astype(BF16), preferred_element_type=F32) / l
        outs.append(jnp.where(low, o[:tm], o[tm:]))
    o_ref[...] = _rms(jnp.concatenate(outs, axis=1)) * og_ref[...]


def _mem_attn(qm, mem_k, mem_v, og, b, t, tm):
    n, w = qm.shape
    assert t % tm == 0
    steps = t // tm
    return pl.pallas_call(
        functools.partial(_mem_attn_body, tm=tm),
        grid=(b, steps),
        in_specs=[
            pl.BlockSpec((tm, w), lambda bi, i: (bi * steps + i, 0)),
            pl.BlockSpec((1, MEM_TOKENS, w), lambda bi, i: (bi, 0, 0)),
            pl.BlockSpec((1, MEM_TOKENS, w), lambda bi, i: (bi, 0, 0)),
            pl.BlockSpec((1, w), lambda bi, i: (0, 0)),
        ],
        out_specs=pl.BlockSpec((tm, w), lambda bi, i: (bi * steps + i, 0)),
        out_shape=jax.ShapeDtypeStruct((n, w), F32),
        compiler_params=_cparams(("parallel", "arbitrary"), 32),
        name="mem_attn",
    )(qm, mem_k, mem_v, og.reshape(1, w))


ROUTER_GROUP_LANE0 = N_EXPERTS


def _outproj_router_body(op_ref, oa_ref, om_ref, x_ref, wo_ref, g2_ref, wr_ref, br_ref,
                         x1_ref, h2_ref, comb_ref, *, tm):
    a0, a1 = POOL_WIDTH, POOL_WIDTH + MOBA_WIDTH
    y = x_ref[...]
    y = y + jnp.dot(op_ref[...].astype(BF16), wo_ref[0:a0, :], preferred_element_type=F32)
    y = y + jnp.dot(oa_ref[...].astype(BF16), wo_ref[a0:a1, :], preferred_element_type=F32)
    y = y + jnp.dot(om_ref[...].astype(BF16), wo_ref[a1:, :], preferred_element_type=F32)
    x1_ref[...] = y
    h2 = _rms(y) * g2_ref[...]
    h2_hi = h2.astype(BF16)
    h2_lo = (h2 - h2_hi.astype(F32)).astype(BF16)
    h2_ref[...] = h2_hi
    hw = jnp.dot(h2_hi, wr_ref[...], preferred_element_type=F32)
    lw = jnp.dot(h2_lo, wr_ref[:, :LANES], preferred_element_type=F32)
    logits = (hw[:, :LANES] + (hw[:, LANES:] + lw)) + br_ref[...]
    lane_f = lax.broadcasted_iota(jnp.int32, (tm, LANES), 1).astype(F32)
    big = float(LANES)
    g_lo = float(ROUTER_GROUP_LANE0)
    is_g = (lane_f >= g_lo) & (lane_f < g_lo + MOE_GROUPS)
    lg = jnp.where(is_g, logits, NEG_INF)
    mg = jnp.max(lg, axis=-1, keepdims=True)
    pg_top = 1.0 / jnp.sum(jnp.exp(lg - mg), axis=-1, keepdims=True)
    gidx = jnp.min(jnp.where(lg == mg, lane_f, big), axis=-1, keepdims=True) - g_lo
    e_lo = gidx * EXPERTS_PER_GROUP
    in_grp = (lane_f >= e_lo) & (lane_f < e_lo + EXPERTS_PER_GROUP)
    le = jnp.where(in_grp, logits, NEG_INF)
    m1 = jnp.max(le, axis=-1, keepdims=True)
    se = jnp.sum(jnp.exp(le - m1), axis=-1, keepdims=True)
    i1 = jnp.min(jnp.where(le == m1, lane_f, big), axis=-1, keepdims=True)
    le2 = jnp.where(lane_f == i1, NEG_INF, le)
    m2 = jnp.max(le2, axis=-1, keepdims=True)
    i2 = jnp.min(jnp.where(le2 == m2, lane_f, big), axis=-1, keepdims=True)
    p1 = 1.0 / se
    p2 = jnp.exp(m2 - m1) / se
    den = p1 + p2
    comb_ref[...] = jnp.where(lane_f == i1, pg_top * (p1 / den),
                              jnp.where(lane_f == i2, pg_top * (p2 / den), 0.0))


def _outproj_router(o_pool, o_attn, o_mem, x, wo_bf, g2, wr, br, tm):
    n, d = x.shape
    assert n % tm == 0
    row = lambda wdt: pl.BlockSpec((tm, wdt), lambda i: (i, 0))
    full = lambda shp: pl.BlockSpec(shp, lambda i: (0, 0))
    return pl.pallas_call(
        functools.partial(_outproj_router_body, tm=tm),
        grid=(n // tm,),
        in_specs=[row(POOL_WIDTH), row(MOBA_WIDTH), row(MEM_WIDTH), row(d),
                  full((d, d)), full((1, d)), full((d, 2 * LANES)), full((1, LANES))],
        out_specs=[row(d), row(d), row(LANES)],
        out_shape=[jax.ShapeDtypeStruct((n, d), F32), jax.ShapeDtypeStruct((n, d), BF16),
                   jax.ShapeDtypeStruct((n, LANES), F32)],
        compiler_params=_cparams(("parallel",), 40),
        name="outproj_router",
    )(o_pool, o_attn, o_mem, x, wo_bf, g2.reshape(1, d), wr, br)


def _moe_body(h2_ref, comb_ref, x1_ref, wg_ref, wu_ref, wd_ref, y_ref, *, tm):
    e = pl.program_id(1)

    @pl.when(e == 0)
    def _init():
        y_ref[...] = x1_ref[...]

    h = h2_ref[...]
    a = jnp.dot(h, wg_ref[0].astype(BF16), preferred_element_type=F32)
    b = jnp.dot(h, wu_ref[0].astype(BF16), preferred_element_type=F32)
    lane = lax.broadcasted_iota(jnp.int32, (tm, LANES), 1)
    ce = jnp.sum(jnp.where(lane == e, comb_ref[...], 0.0), axis=-1, keepdims=True)
    hid = (a * jax.nn.sigmoid(a)) * b * ce
    y_ref[...] += jnp.dot(hid.astype(BF16), wd_ref[0].astype(BF16), preferred_element_type=F32)


def _moe(h2, comb, x1, w_gate, w_up, w_down, tm):
    n, d = x1.shape
    assert n % tm == 0
    return pl.pallas_call(
        functools.partial(_moe_body, tm=tm),
        grid=(n // tm, N_EXPERTS),
        in_specs=[
            pl.BlockSpec((tm, d), lambda i, e: (i, 0)),
            pl.BlockSpec((tm, LANES), lambda i, e: (i, 0)),
            pl.BlockSpec((tm, d), lambda i, e: (i, 0)),
            pl.BlockSpec((1, d, D_EXPERT), lambda i, e: (e, 0, 0)),
            pl.BlockSpec((1, d, D_EXPERT), lambda i, e: (e, 0, 0)),
            pl.BlockSpec((1, D_EXPERT, d), lambda i, e: (e, 0, 0)),
        ],
        out_specs=pl.BlockSpec((tm, d), lambda i, e: (i, 0)),
        out_shape=jax.ShapeDtypeStruct((n, d), F32),
        compiler_params=_cparams(("parallel", "arbitrary"), 48),
        name="moe",
    )(h2, comb, x1, w_gate, w_up, w_down)


def _block_diag_ones(width, group):
    r = lax.broadcasted_iota(jnp.int32, (width, width), 0) // group
    c = lax.broadcasted_iota(jnp.int32, (width, width), 1) // group
    return (r == c).astype(BF16)


def _pick_tile(n, candidates):
    for c in candidates:
        if n % c == 0:
            return c
    return n


IN_SEGS = (
    (0, POOL_WIDTH, False, False, False, False),
    (POOL_WIDTH, MOBA_WIDTH, True, False, False, False),
    (POOL_WIDTH + MOBA_WIDTH, MOBA_WIDTH, True, True, "row", True),
    (POOL_WIDTH + 2 * MOBA_WIDTH, MOBA_WIDTH, False, True, "row", False),
    (POOL_WIDTH + 3 * MOBA_WIDTH, MEM_WIDTH, True, False, False, False),
)
IN_SEGS_SAMPLE = tuple((c0, wd, hn, False, False, False) for (c0, wd, hn, _, _, _) in IN_SEGS)
MEMKV_SEGS = ((0, MEM_WIDTH, True, False, False, False), (MEM_WIDTH, MEM_WIDTH, False, False, False, False))


def kernel(x_prompt, x_sample, mem_prompt, cache_k, cache_v, state_pool, cache_mem_k, cache_mem_v,
           page_table, norm1_gain, w_in, pool_w, pool_scale, moba_q_gain, moba_k_gain,
           mem_norm_gain, w_mem_kv, mem_q_gain, mem_k_gain, out_gain, w_out, norm2_gain,
           router_group_w, router_group_b, router_expert_w, router_expert_b, w_gate, w_up, w_down):
    bp, tp, d = x_prompt.shape
    bs, ts, _ = x_sample.shape
    n_p, n_s = bp * tp, bs * ts

    w_in_bf = w_in.astype(BF16)
    w_mem_bf = w_mem_kv.astype(BF16)
    wo_bf = w_out.astype(BF16)
    ones_bd = _block_diag_ones(MOBA_WIDTH, HEAD_DIM)
    head_gain_in = jnp.concatenate([
        jnp.ones((POOL_WIDTH,), F32), jnp.tile(moba_q_gain, MOBA_HEADS), jnp.tile(moba_k_gain, MOBA_HEADS),
        jnp.ones((MOBA_WIDTH,), F32), jnp.tile(mem_q_gain, MEM_HEADS)])
    head_gain_mem = jnp.concatenate([jnp.tile(mem_k_gain, MEM_HEADS), jnp.ones((MEM_WIDTH,), F32)])
    wbd = jnp.zeros((POOL_WIDTH, POOL_WIDTH), F32)
    for g in range(len(POOL_WINDOWS)):
        sl = slice(g * POOL_GROUP_WIDTH, (g + 1) * POOL_GROUP_WIDTH)
        wbd = wbd.at[sl, sl].set(pool_w[g])
    wbd_bf = wbd.astype(BF16)
    og_pool, og_attn, og_mem = (out_gain[:POOL_WIDTH], out_gain[POOL_WIDTH:POOL_WIDTH + MOBA_WIDTH],
                                out_gain[POOL_WIDTH + MOBA_WIDTH:])
    wr = jnp.zeros((d, LANES), F32)
    wr = wr.at[:, :N_EXPERTS].set(router_expert_w).at[:, N_EXPERTS:N_EXPERTS + MOE_GROUPS].set(router_group_w)
    br = jnp.zeros((1, LANES), F32)
    br = br.at[0, :N_EXPERTS].set(router_expert_b).at[0, N_EXPERTS:N_EXPERTS + MOE_GROUPS].set(router_group_b)
    wr_hi = wr.astype(BF16)
    wr = jnp.concatenate([wr_hi, (wr - wr_hi.astype(F32)).astype(BF16)], axis=1)
    wg = w_gate.reshape(N_EXPERTS, d, D_EXPERT)
    wu = w_up.reshape(N_EXPERTS, d, D_EXPERT)
    wd = w_down.reshape(N_EXPERTS, D_EXPERT, d)

    def tail(o_pool, o_attn, o_mem, x2d):
        n = x2d.shape[0]
        x1, h2, comb = _outproj_router(o_pool, o_attn, o_mem, x2d, wo_bf, norm2_gain, wr, br,
                                       _pick_tile(n, (512, 256)))
        return _moe(h2, comb, x1, wg, wu, wd, _pick_tile(n, (1024, 512, 256)))

    mem_k_p, mem_v_p = _norm_proj(mem_prompt.reshape(bp * MEM_TOKENS, d), mem_norm_gain, w_mem_bf,
                                  head_gain_mem, ones_bd, MEMKV_SEGS, _pick_tile(bp * MEM_TOKENS, (512, 256)))
    xp = x_prompt.reshape(n_p, d)
    u_p, q_p, kt_p, k_p_bf, kmean_p, vt_p, v_p_bf, qm_p = _norm_proj(
        xp, norm1_gain, w_in_bf, head_gain_in, ones_bd, IN_SEGS, _pick_tile(tp, (512, 256)), rows_per_batch=tp)
    o_pool_p = _pool(u_p.reshape(bp, tp, POOL_WIDTH), jnp.zeros((bp, 16, POOL_WIDTH), F32), wbd_bf,
                     pool_scale, og_pool, 0, 1, MOBA_BLOCK).reshape(n_p, POOL_WIDTH)
    nblk = tp // MOBA_BLOCK
    kmean_pad = jnp.pad(kmean_p.reshape(bp, nblk, MOBA_WIDTH), ((0, 0), (0, LANES - nblk), (0, 0)))
    o_attn_p = _moba_prompt(q_p, k_p_bf, v_p_bf, kmean_pad, og_attn, bp, tp)
    o_mem_p = _mem_attn(qm_p, mem_k_p.reshape(bp, MEM_TOKENS, MEM_WIDTH),
                        mem_v_p.reshape(bp, MEM_TOKENS, MEM_WIDTH), og_mem, bp, tp, _pick_tile(tp, (512, 256)))
    y_p = tail(o_pool_p, o_attn_p, o_mem_p, xp)

    past_len = page_table.shape[1] * PAGE_SIZE
    xs = x_sample.reshape(n_s, d)
    u_s, q_s, k_s, v_s, qm_s = _norm_proj(xs, norm1_gain, w_in_bf, head_gain_in, ones_bd, IN_SEGS_SAMPLE, n_s)
    u_s3 = u_s.reshape(bs, ts, POOL_WIDTH)
    buf16 = jnp.concatenate([jnp.zeros((bs, 1, POOL_WIDTH), F32), state_pool], axis=1)
    o_pool_s = _pool(u_s3, buf16, wbd_bf, pool_scale, og_pool, past_len, bs, ts).reshape(n_s, POOL_WIDTH)
    o_attn_s = _moba_sample(page_table, q_s, k_s, v_s,
                            cache_k.transpose(0, 2, 3, 1).reshape(-1, MOBA_WIDTH, PAGE_SIZE),
                            cache_v.transpose(0, 2, 3, 1).reshape(-1, MOBA_WIDTH, PAGE_SIZE),
                            og_attn, bs, ts)
    o_mem_s = _mem_attn(qm_s, cache_mem_k.reshape(bs, MEM_TOKENS, MEM_WIDTH),
                        cache_mem_v.reshape(bs, MEM_TOKENS, MEM_WIDTH), og_mem, bs, ts, ts)
    y_s = tail(o_pool_s, o_attn_s, o_mem_s, xs)

    pool_prompt = u_p.reshape(bp, tp, POOL_WIDTH)[:, tp - POOL_STATE:]
    pool_sample = jnp.concatenate([state_pool, u_s3], axis=1)[:, -POOL_STATE:]
    k_p = kt_p.reshape(bp, MOBA_HEADS, HEAD_DIM, tp).transpose(0, 3, 1, 2)
    v_p = vt_p.reshape(bp, MOBA_HEADS, HEAD_DIM, tp).transpose(0, 3, 1, 2)
    return (y_p.reshape(bp, tp, d), y_s.reshape(bs, ts, d), k_p, v_p,
            pool_prompt,
            mem_k_p.reshape(bp, MEM_TOKENS, MEM_HEADS, HEAD_DIM), mem_v_p.reshape(bp, MEM_TOKENS, MEM_HEADS, HEAD_DIM),
            k_s.reshape(bs, ts, MOBA_HEADS, HEAD_DIM), v_s.reshape(bs, ts, MOBA_HEADS, HEAD_DIM),
            pool_sample)
```

```python
import functools

import jax
import jax.numpy as jnp
from jax import lax
from jax.experimental import pallas as pl
from jax.experimental.pallas import tpu as pltpu

F32 = jnp.float32
BF16 = jnp.bfloat16

D_MODEL = 1024
HEAD_DIM = 64
POOL_WIDTH = 256
POOL_WINDOWS = (2, 4, 8, 16)
POOL_GROUP_WIDTH = 64
POOL_STATE = 15
MOBA_WIDTH = 512
MOBA_HEADS = 8
MOBA_BLOCK = 256
MOBA_TOPK = 3
MEM_WIDTH = 256
MEM_HEADS = 4
MEM_TOKENS = 256
PAGE_SIZE = 128
MOE_GROUPS = 4
EXPERTS_PER_GROUP = 8
N_EXPERTS = MOE_GROUPS * EXPERTS_PER_GROUP
D_EXPERT = 256
NORM_EPS = 1e-6

LANES = 128
HEAD_PAIR = 2 * HEAD_DIM
QK_SCALE = HEAD_DIM ** -0.5
NEG_INF = float("-inf")
MIB = 1024 * 1024
NT_DIMS = (((1,), (1,)), ((), ()))


def _cparams(semantics, vmem_mib):
    return pltpu.CompilerParams(dimension_semantics=semantics, vmem_limit_bytes=vmem_mib * MIB)


def _rms(y, eps=NORM_EPS):
    return y * lax.rsqrt(jnp.mean(y * y, axis=-1, keepdims=True) + eps)


def _norm_proj_body(x_ref, g_ref, w_ref, hg_ref, ones_ref, *outs, segs, tm):
    hb = (_rms(x_ref[...]) * g_ref[...]).astype(BF16)
    oi = 0
    for (c0, wd, headnorm, transposed, want_bf16, want_blockmean) in segs:
        y = jnp.dot(hb, w_ref[:, c0:c0 + wd], preferred_element_type=F32)
        if headnorm:
            sq = (y * y).astype(BF16)
            msq = jnp.dot(sq, ones_ref[:wd, :wd], preferred_element_type=F32) * (1.0 / HEAD_DIM)
            y = (y * lax.rsqrt(msq + NORM_EPS)) * hg_ref[:, c0:c0 + wd]
        if transposed:
            outs[oi][0] = y.T
        else:
            outs[oi][...] = y
        oi += 1
        if want_bf16:
            outs[oi][...] = y.astype(BF16)
            oi += 1
        if want_blockmean:
            for bi in range(tm // MOBA_BLOCK):
                outs[oi][bi] = jnp.mean(y[bi * MOBA_BLOCK:(bi + 1) * MOBA_BLOCK], axis=0, keepdims=True)
            oi += 1


def _norm_proj(x, gain, w_bf, head_gain, ones_bd, segs, tm, rows_per_batch=None):
    n, d = x.shape
    wtot = w_bf.shape[1]
    assert n % tm == 0
    out_shape, out_specs = [], []
    for (c0, wd, headnorm, transposed, want_bf16, want_blockmean) in segs:
        if transposed:
            assert rows_per_batch % tm == 0 and n % rows_per_batch == 0
            steps = rows_per_batch // tm
            out_shape.append(jax.ShapeDtypeStruct((n // rows_per_batch, wd, rows_per_batch), F32))
            out_specs.append(pl.BlockSpec((1, wd, tm), lambda i, steps=steps: (i // steps, 0, i % steps)))
        else:
            out_shape.append(jax.ShapeDtypeStruct((n, wd), F32))
            out_specs.append(pl.BlockSpec((tm, wd), lambda i: (i, 0)))
        if want_bf16:
            out_shape.append(jax.ShapeDtypeStruct((n, wd), BF16))
            out_specs.append(pl.BlockSpec((tm, wd), lambda i: (i, 0)))
        if want_blockmean:
            assert tm % MOBA_BLOCK == 0
            nb = tm // MOBA_BLOCK
            out_shape.append(jax.ShapeDtypeStruct((n // MOBA_BLOCK, 1, wd), F32))
            out_specs.append(pl.BlockSpec((nb, 1, wd), lambda i: (i, 0, 0)))
    return pl.pallas_call(
        functools.partial(_norm_proj_body, segs=segs, tm=tm),
        grid=(n // tm,),
        in_specs=[
            pl.BlockSpec((tm, d), lambda i: (i, 0)),
            pl.BlockSpec((1, d), lambda i: (0, 0)),
            pl.BlockSpec((d, wtot), lambda i: (0, 0)),
            pl.BlockSpec((1, wtot), lambda i: (0, 0)),
            pl.BlockSpec(ones_bd.shape, lambda i: (0, 0)),
        ],
        out_specs=out_specs,
        out_shape=out_shape,
        compiler_params=_cparams(("parallel",), 48),
        name="norm_proj",
    )(x, gain.reshape(1, d), w_bf, head_gain.reshape(1, wtot), ones_bd)


def _pool_windows(win, pos0):
    r = win.shape[0] - 16
    lane = lax.broadcasted_iota(jnp.int32, (r, LANES), 1)
    pos1 = pos0 + lax.broadcasted_iota(jnp.int32, (r, LANES), 0) + 1
    low = lane < POOL_GROUP_WIDTH
    a = win[:, :LANES]
    b = win[:, LANES:]
    a2 = a + pltpu.roll(a, 1, 0)
    a4 = a2 + pltpu.roll(a2, 2, 0)
    b2 = b + pltpu.roll(b, 1, 0)
    b4 = b2 + pltpu.roll(b2, 2, 0)
    b8 = b4 + pltpu.roll(b4, 4, 0)
    b16 = b8 + pltpu.roll(b8, 8, 0)
    cnt_a = jnp.minimum(jnp.where(low, POOL_WINDOWS[0], POOL_WINDOWS[1]), pos1).astype(F32)
    cnt_b = jnp.minimum(jnp.where(low, POOL_WINDOWS[2], POOL_WINDOWS[3]), pos1).astype(F32)
    pa = jnp.where(low, a2[16:], a4[16:]) / cnt_a - a[16:]
    pb = jnp.where(low, b8[16:], b16[16:]) / cnt_b - b[16:]
    return jnp.concatenate([pa, pb], axis=1)


def _pool_finish(pooled, wbd_ref, ps_ref, og_ref):
    mixed = jnp.dot(pooled.astype(BF16), wbd_ref[...], preferred_element_type=F32) * ps_ref[...]
    return _rms(mixed) * og_ref[...]


def _pool_body(u_ref, buf_ref, wbd_ref, ps_ref, og_ref, o_ref, ext_sc, *, bb, t, r, pos0):
    for bi in range(bb):
        ext_sc[bi, 0:16, :] = buf_ref[bi]
        ext_sc[bi, 16:, :] = u_ref[bi]
    if t == r:
        pooled = [_pool_windows(ext_sc[bi], pos0) for bi in range(bb)]
        out = _pool_finish(jnp.concatenate(pooled, axis=0), wbd_ref, ps_ref, og_ref)
        for bi in range(bb):
            o_ref[bi] = out[bi * r:(bi + 1) * r]
    else:
        assert bb == 1

        def chunk(c, carry):
            base = pl.multiple_of(c * r, r)
            pooled = _pool_windows(ext_sc[0, pl.ds(base, r + 16), :], pos0 + base)
            o_ref[0, pl.ds(base, r), :] = _pool_finish(pooled, wbd_ref, ps_ref, og_ref)
            return carry

        lax.fori_loop(0, t // r, chunk, 0)


def _pool(u, buf16, wbd_bf, pool_scale, og, pos0, bb, r):
    b, t, c = u.shape
    assert b % bb == 0 and t % r == 0
    return pl.pallas_call(
        functools.partial(_pool_body, bb=bb, t=t, r=r, pos0=pos0),
        grid=(b // bb,),
        in_specs=[
            pl.BlockSpec((bb, t, c), lambda i: (i, 0, 0)),
            pl.BlockSpec((bb, 16, c), lambda i: (i, 0, 0)),
            pl.BlockSpec((c, c), lambda i: (0, 0)),
            pl.BlockSpec((1, c), lambda i: (0, 0)),
            pl.BlockSpec((1, c), lambda i: (0, 0)),
        ],
        out_specs=pl.BlockSpec((bb, t, c), lambda i: (i, 0, 0)),
        out_shape=jax.ShapeDtypeStruct((b, t, c), F32),
        scratch_shapes=[pltpu.VMEM((bb, t + 16, c), F32)],
        compiler_params=_cparams(("parallel",), 40),
        name="pool",
    )(u, buf16, wbd_bf, pool_scale.reshape(1, c), og.reshape(1, c))


def _alibi_slope(h):
    return 2.0 ** (-8.0 * (h + 1) / MOBA_HEADS)


def _moba_prompt_body(q_ref, k_ref, v_ref, km_ref, og_ref, o_ref, m_sc, l_sc, acc_sc, sel_sc, qst_sc):
    i = pl.program_id(1)
    tq = MOBA_BLOCK
    n_pairs = MOBA_HEADS // 2
    lane = lax.broadcasted_iota(jnp.int32, (tq, LANES), 1)
    lane_f = lane.astype(F32)
    low = lane < HEAD_DIM
    row2 = lax.broadcasted_iota(jnp.int32, (tq, MOBA_BLOCK), 0)
    col2 = lax.broadcasted_iota(jnp.int32, (tq, MOBA_BLOCK), 1)
    causal = col2 <= row2
    colrow = lax.broadcasted_iota(jnp.int32, (1, MOBA_BLOCK), 1).astype(F32)

    def pair_cols(pr):
        return slice(pr * HEAD_PAIR, (pr + 1) * HEAD_PAIR)

    def scores(pr, j):
        start = pl.multiple_of(j * MOBA_BLOCK, MOBA_BLOCK)
        kj = k_ref[pl.ds(start, MOBA_BLOCK), pair_cols(pr)]
        s = lax.dot_general(qst_sc[pr], kj, NT_DIMS, preferred_element_type=F32)
        off = colrow + ((j - i) * MOBA_BLOCK).astype(F32)
        return [s[hh * tq:(hh + 1) * tq] + _alibi_slope(2 * pr + hh) * off for hh in range(2)]

    def pv(pr, j, ps):
        start = pl.multiple_of(j * MOBA_BLOCK, MOBA_BLOCK)
        vj = v_ref[pl.ds(start, MOBA_BLOCK), pair_cols(pr)]
        return jnp.dot(jnp.concatenate(ps, axis=0).astype(BF16), vj, preferred_element_type=F32)

    for pr in range(n_pairs):
        q_pair = q_ref[:, pair_cols(pr)]
        km_pair = km_ref[0, :, pair_cols(pr)]
        q_heads = [jnp.where(low, q_pair, 0.0), jnp.where(low, 0.0, q_pair)]

        for hh in range(2):
            gate = lax.dot_general(q_heads[hh], km_pair, NT_DIMS, precision=lax.Precision.HIGHEST,
                                   preferred_element_type=F32)
            g = jnp.where(lane < i, gate, NEG_INF)
            sel = jnp.zeros((tq, LANES), F32)
            for _ in range(MOBA_TOPK):
                mx = jnp.max(g, axis=-1, keepdims=True)
                idx = jnp.min(jnp.where(g == mx, lane_f, float(LANES)), axis=-1, keepdims=True)
                pick = (lane_f == idx) & (mx > NEG_INF)
                sel = jnp.where(pick, 1.0, sel)
                g = jnp.where(pick, NEG_INF, g)
            sel_sc[2 * pr + hh] = sel

        qst_sc[pr] = (jnp.concatenate(q_heads, axis=0) * QK_SCALE).astype(BF16)

        s_own = scores(pr, i)
        ps = []
        for hh in range(2):
            sh = jnp.where(causal, s_own[hh], NEG_INF)
            m = jnp.max(sh, axis=-1, keepdims=True)
            p = jnp.exp(sh - m)
            m_sc[2 * pr + hh] = jnp.broadcast_to(m, (tq, LANES))
            l_sc[2 * pr + hh] = jnp.broadcast_to(jnp.sum(p, axis=-1, keepdims=True), (tq, LANES))
            ps.append(p)
        acc_sc[pr] = pv(pr, i, ps)

    def past(j, carry):
        for pr in range(n_pairs):
            s_j = scores(pr, j)
            ps, alphas = [], []
            for hh in range(2):
                h = 2 * pr + hh
                selcol = jnp.sum(jnp.where(lane == j, sel_sc[h], 0.0), axis=-1, keepdims=True)
                sh = jnp.where(selcol > 0.0, s_j[hh], NEG_INF)
                m_prev = m_sc[h]
                m_new = jnp.maximum(m_prev, jnp.max(sh, axis=-1, keepdims=True))
                alpha = jnp.exp(m_prev - m_new)
                p = jnp.exp(sh - jnp.concatenate([m_new, m_new], axis=1))
                l_sc[h] = alpha * l_sc[h] + jnp.sum(p, axis=-1, keepdims=True)
                m_sc[h] = m_new
                ps.append(p)
                alphas.append(alpha)
            acc_sc[pr] = jnp.concatenate(alphas, axis=0) * acc_sc[pr] + pv(pr, j, ps)
        return carry

    lax.fori_loop(0, i, past, 0)

    outs = []
    for pr in range(n_pairs):
        acc = acc_sc[pr]
        outs.append(jnp.where(low, acc[:tq] / l_sc[2 * pr], acc[tq:] / l_sc[2 * pr + 1]))
    o_ref[...] = _rms(jnp.concatenate(outs, axis=1)) * og_ref[...]


def _moba_prompt(q, k_bf, v_bf, kmean_pad, og, b, t):
    n, w = q.shape
    nblk = t // MOBA_BLOCK
    tq = MOBA_BLOCK
    return pl.pallas_call(
        _moba_prompt_body,
        grid=(b, nblk),
        in_specs=[
            pl.BlockSpec((tq, w), lambda bi, i: (bi * nblk + i, 0)),
            pl.BlockSpec((t, w), lambda bi, i: (bi, 0)),
            pl.BlockSpec((t, w), lambda bi, i: (bi, 0)),
            pl.BlockSpec((1, LANES, w), lambda bi, i: (bi, 0, 0)),
            pl.BlockSpec((1, w), lambda bi, i: (0, 0)),
        ],
        out_specs=pl.BlockSpec((tq, w), lambda bi, i: (bi * nblk + i, 0)),
        out_shape=jax.ShapeDtypeStruct((n, w), F32),
        scratch_shapes=[
            pltpu.VMEM((MOBA_HEADS, tq, LANES), F32),
            pltpu.VMEM((MOBA_HEADS, tq, LANES), F32),
            pltpu.VMEM((MOBA_HEADS // 2, 2 * tq, LANES), F32),
            pltpu.VMEM((MOBA_HEADS, tq, LANES), F32),
            pltpu.VMEM((MOBA_HEADS // 2, 2 * tq, LANES), BF16),
        ],
        compiler_params=_cparams(("parallel", "arbitrary"), 40),
        name="moba_prompt",
    )(q, k_bf, v_bf, kmean_pad, og.reshape(1, w))


SAMPLE_BLOCKS_PER_STEP = 8
PAGES_PER_BLOCK = MOBA_BLOCK // PAGE_SIZE
PAGES_PER_STEP = SAMPLE_BLOCKS_PER_STEP * PAGES_PER_BLOCK


def _moba_sample_body(pt_ref, q_ref, kn_ref, vn_ref, slope_ref, og_ref, *rest, n_past_blocks, t_new):
    kp = rest[:PAGES_PER_STEP]
    vp = rest[PAGES_PER_STEP:2 * PAGES_PER_STEP]
    o_ref, o_sc, m_sc, l_sc, kmt_sc = rest[2 * PAGES_PER_STEP:]
    c = pl.program_id(1)
    rows = MOBA_HEADS * t_new
    row_h = lax.broadcasted_iota(jnp.int32, (rows, MOBA_WIDTH), 0) // t_new
    lane_h = lax.broadcasted_iota(jnp.int32, (rows, MOBA_WIDTH), 1) // HEAD_DIM
    bd = row_h == lane_h
    q = q_ref[...]
    qbd = jnp.where(bd, jnp.concatenate([q] * MOBA_HEADS, axis=0), 0.0)
    qbd_bf = (qbd * QK_SCALE).astype(BF16)
    slope = slope_ref[...]
    slope2 = jnp.concatenate([slope, slope], axis=1)
    colf = lax.broadcasted_iota(jnp.int32, (rows, MOBA_BLOCK), 1).astype(F32)

    blk_lane = lax.broadcasted_iota(jnp.int32, (MOBA_WIDTH, LANES), 1)

    @pl.when(c == 0)
    def _init():
        kmt_sc[...] = jnp.zeros((MOBA_WIDTH, LANES), F32)

    for jj in range(SAMPLE_BLOCKS_PER_STEP):
        j = c * SAMPLE_BLOCKS_PER_STEP + jj
        kt_pages = [kp[PAGES_PER_BLOCK * jj + p][0] for p in range(PAGES_PER_BLOCK)]
        vt_pages = [vp[PAGES_PER_BLOCK * jj + p][0] for p in range(PAGES_PER_BLOCK)]
        ksum = kt_pages[0]
        for kt in kt_pages[1:]:
            ksum = ksum + kt
        kmean_col = jnp.sum(ksum, axis=-1, keepdims=True) * (1.0 / MOBA_BLOCK)
        kmt_sc[...] = jnp.where(blk_lane == j, kmean_col, kmt_sc[...])
        kt_bf = jnp.concatenate(kt_pages, axis=1).astype(BF16)
        vt_bf = jnp.concatenate(vt_pages, axis=1).astype(BF16)
        s = jnp.dot(qbd_bf, kt_bf, preferred_element_type=F32)
        s = s + slope2 * (colf + ((j - n_past_blocks) * MOBA_BLOCK).astype(F32))
        m = jnp.max(s, axis=-1, keepdims=True)
        p = jnp.exp(s - m)
        o = lax.dot_general(p.astype(BF16), vt_bf, NT_DIMS, preferred_element_type=F32)
        o_sc[j] = jnp.where(bd, o, 0.0)
        m_sc[j] = jnp.broadcast_to(m, (rows, LANES))
        l_sc[j] = jnp.broadcast_to(jnp.sum(p, axis=-1, keepdims=True), (rows, LANES))

    @pl.when(c == pl.num_programs(1) - 1)
    def _finish():
        tq = lax.broadcasted_iota(jnp.int32, (rows, LANES), 0) % t_new
        kn = kn_ref[...]
        vn = vn_ref[...]
        qs = qbd * QK_SCALE
        s_own = []
        m_run = jnp.full((rows, LANES), NEG_INF, F32)
        for cc in range(t_new):
            sc = jnp.sum(qs * kn[cc:cc + 1, :], axis=-1, keepdims=True) + slope * float(cc)
            sc = jnp.where(tq >= cc, sc, NEG_INF)
            s_own.append(sc)
            m_run = jnp.maximum(m_run, sc)

        gates = jnp.dot(qbd, kmt_sc[...], precision=lax.Precision.HIGHEST, preferred_element_type=F32)
        g_cols = [jnp.broadcast_to(gates[:, j:j + 1], (rows, LANES)) for j in range(n_past_blocks)]

        def top_round(excluded):
            best = jnp.full((rows, LANES), NEG_INF, F32)
            bidx = jnp.full((rows, LANES), -1.0, F32)
            for j in range(n_past_blocks):
                ok = g_cols[j] > best
                for e in excluded:
                    ok = ok & (e != float(j))
                best = jnp.where(ok, g_cols[j], best)
                bidx = jnp.where(ok, float(j), bidx)
            return bidx

        picks = []
        for _ in range(min(MOBA_TOPK, n_past_blocks)):
            picks.append(top_round(picks))
        sels = []
        for j in range(n_past_blocks):
            sj = picks[0] == float(j)
            for e in picks[1:]:
                sj = sj | (e == float(j))
            sels.append(sj)
            m_run = jnp.where(sj, jnp.maximum(m_run, m_sc[j]), m_run)

        l_run = jnp.zeros((rows, LANES), F32)
        o_run = jnp.zeros((rows, MOBA_WIDTH), F32)
        for cc in range(t_new):
            p = jnp.exp(s_own[cc] - m_run)
            l_run = l_run + p
            o_run = o_run + jnp.concatenate([p] * (MOBA_WIDTH // LANES), axis=1) * vn[cc:cc + 1, :]
        o_run = jnp.where(bd, o_run, 0.0)
        for j in range(n_past_blocks):
            wj = jnp.where(sels[j], jnp.exp(m_sc[j] - m_run), 0.0)
            l_run = l_run + wj * l_sc[j]
            o_run = o_run + jnp.concatenate([wj] * (MOBA_WIDTH // LANES), axis=1) * o_sc[j]
        o_bd = o_run / jnp.concatenate([l_run] * (MOBA_WIDTH // LANES), axis=1)
        attn = o_bd[0:t_new]
        for h in range(1, MOBA_HEADS):
            attn = attn + o_bd[h * t_new:(h + 1) * t_new]
        o_ref[...] = _rms(attn) * og_ref[...]


def _moba_sample(page_table, q, k_new, v_new, cache_kt, cache_vt, og, b, t_new):
    n_pages = page_table.shape[1]
    n_past_blocks = n_pages // PAGES_PER_BLOCK
    assert n_past_blocks <= LANES
    assert n_past_blocks % SAMPLE_BLOCKS_PER_STEP == 0
    n_steps = n_past_blocks // SAMPLE_BLOCKS_PER_STEP
    rows = MOBA_HEADS * t_new
    w = MOBA_WIDTH
    slopes = jnp.exp2(-8.0 * jnp.arange(1, MOBA_HEADS + 1, dtype=F32) / MOBA_HEADS)
    slope_rows = jnp.broadcast_to(jnp.repeat(slopes, t_new)[:, None], (rows, LANES))

    def page_spec(p):
        return pl.BlockSpec((1, w, PAGE_SIZE), lambda bi, c, pt: (pt[bi, c * PAGES_PER_STEP + p], 0, 0))

    row_spec = pl.BlockSpec((t_new, w), lambda bi, c, pt: (bi, 0))
    grid_spec = pltpu.PrefetchScalarGridSpec(
        num_scalar_prefetch=1,
        grid=(b, n_steps),
        in_specs=[row_spec, row_spec, row_spec,
                  pl.BlockSpec((rows, LANES), lambda bi, c, pt: (0, 0)),
                  pl.BlockSpec((1, w), lambda bi, c, pt: (0, 0))]
                 + [page_spec(p) for p in range(PAGES_PER_STEP)]
                 + [page_spec(p) for p in range(PAGES_PER_STEP)],
        out_specs=row_spec,
        scratch_shapes=[
            pltpu.VMEM((n_past_blocks, rows, w), F32),
            pltpu.VMEM((n_past_blocks, rows, LANES), F32),
            pltpu.VMEM((n_past_blocks, rows, LANES), F32),
            pltpu.VMEM((w, LANES), F32),
        ],
    )
    return pl.pallas_call(
        functools.partial(_moba_sample_body, n_past_blocks=n_past_blocks, t_new=t_new),
        grid_spec=grid_spec,
        out_shape=jax.ShapeDtypeStruct((b * t_new, w), F32),
        compiler_params=_cparams(("parallel", "arbitrary"), 48),
        name="moba_sample",
    )(page_table, q, k_new, v_new, slope_rows, og.reshape(1, w),
      *([cache_kt] * PAGES_PER_STEP), *([cache_vt] * PAGES_PER_STEP))


def _mem_attn_body(q_ref, mk_ref, mv_ref, og_ref, o_ref, *, tm):
    lane = lax.broadcasted_iota(jnp.int32, (tm, LANES), 1)
    low = lane < HEAD_DIM
    outs = []
    for pr in range(MEM_HEADS // 2):
        cs = slice(pr * HEAD_PAIR, (pr + 1) * HEAD_PAIR)
        q_pair = q_ref[:, cs]
        qst = (jnp.concatenate([jnp.where(low, q_pair, 0.0), jnp.where(low, 0.0, q_pair)], axis=0)
               * QK_SCALE).astype(BF16)
        s = lax.dot_general(qst, mk_ref[0, :, cs].astype(BF16), NT_DIMS, preferred_element_type=F32)
        p = jnp.exp(s - jnp.max(s, axis=-1, keepdims=True))
        l = jnp.sum(p, axis=-1, keepdims=True)
        o = jnp.dot(p.astype(BF16), mv_ref[0, :, cs].astype(BF16), preferred_element_type=F32) / l
        outs.append(jnp.where(low, o[:tm], o[tm:]))
    o_ref[...] = _rms(jnp.concatenate(outs, axis=1)) * og_ref[...]


def _mem_attn(qm, mem_k, mem_v, og, b, t, tm):
    n, w = qm.shape
    assert t % tm == 0
    steps = t // tm
    return pl.pallas_call(
        functools.partial(_mem_attn_body, tm=tm),
        grid=(b, steps),
        in_specs=[
            pl.BlockSpec((tm, w), lambda bi, i: (bi * steps + i, 0)),
            pl.BlockSpec((1, MEM_TOKENS, w), lambda bi, i: (bi, 0, 0)),
            pl.BlockSpec((1, MEM_TOKENS, w), lambda bi, i: (bi, 0, 0)),
            pl.BlockSpec((1, w), lambda bi, i: (0, 0)),
        ],
        out_specs=pl.BlockSpec((tm, w), lambda bi, i: (bi * steps + i, 0)),
        out_shape=jax.ShapeDtypeStruct((n, w), F32),
        compiler_params=_cparams(("parallel", "arbitrary"), 32),
        name="mem_attn",
    )(qm, mem_k, mem_v, og.reshape(1, w))


ROUTER_GROUP_LANE0 = N_EXPERTS


PAYLOAD_X1 = 0
PAYLOAD_H2 = D_MODEL
PAYLOAD_COMB = 2 * D_MODEL
PAYLOAD_WIDTH = 2 * D_MODEL + LANES
MOE_TILE = 1024


def _outproj_router_body(op_ref, oa_ref, om_ref, x_ref, wo_ref, g2_ref, wr_ref, br_ref, base_in_ref,
                         payload_ref, route_ref, base_out_ref, base_sc, *, tm):
    a0, a1 = POOL_WIDTH, POOL_WIDTH + MOBA_WIDTH
    y = x_ref[...]
    y = y + jnp.dot(op_ref[...].astype(BF16), wo_ref[0:a0, :], preferred_element_type=F32)
    y = y + jnp.dot(oa_ref[...].astype(BF16), wo_ref[a0:a1, :], preferred_element_type=F32)
    y = y + jnp.dot(om_ref[...].astype(BF16), wo_ref[a1:, :], preferred_element_type=F32)
    payload_ref[:, PAYLOAD_X1:PAYLOAD_X1 + D_MODEL] = y
    h2 = _rms(y) * g2_ref[...]
    payload_ref[:, PAYLOAD_H2:PAYLOAD_H2 + D_MODEL] = h2
    h2_hi = h2.astype(BF16)
    h2_lo = (h2 - h2_hi.astype(F32)).astype(BF16)
    hw = jnp.dot(h2_hi, wr_ref[...], preferred_element_type=F32)
    lw = jnp.dot(h2_lo, wr_ref[:, :LANES], preferred_element_type=F32)
    logits = (hw[:, :LANES] + (hw[:, LANES:] + lw)) + br_ref[...]
    lane_f = lax.broadcasted_iota(jnp.int32, (tm, LANES), 1).astype(F32)
    big = float(LANES)
    g_lo = float(ROUTER_GROUP_LANE0)
    is_g = (lane_f >= g_lo) & (lane_f < g_lo + MOE_GROUPS)
    lg = jnp.where(is_g, logits, NEG_INF)
    mg = jnp.max(lg, axis=-1, keepdims=True)
    pg_top = 1.0 / jnp.sum(jnp.exp(lg - mg), axis=-1, keepdims=True)
    gidx = jnp.min(jnp.where(lg == mg, lane_f, big), axis=-1, keepdims=True) - g_lo
    e_lo = gidx * EXPERTS_PER_GROUP
    in_grp = (lane_f >= e_lo) & (lane_f < e_lo + EXPERTS_PER_GROUP)
    le = jnp.where(in_grp, logits, NEG_INF)
    m1 = jnp.max(le, axis=-1, keepdims=True)
    se = jnp.sum(jnp.exp(le - m1), axis=-1, keepdims=True)
    i1 = jnp.min(jnp.where(le == m1, lane_f, big), axis=-1, keepdims=True)
    le2 = jnp.where(lane_f == i1, NEG_INF, le)
    m2 = jnp.max(le2, axis=-1, keepdims=True)
    i2 = jnp.min(jnp.where(le2 == m2, lane_f, big), axis=-1, keepdims=True)
    p1 = 1.0 / se
    p2 = jnp.exp(m2 - m1) / se
    den = p1 + p2
    payload_ref[:, PAYLOAD_COMB:] = jnp.where(lane_f == i1, pg_top * (p1 / den),
                                              jnp.where(lane_f == i2, pg_top * (p2 / den), 0.0))

    @pl.when(pl.program_id(0) == 0)
    def _first():
        base_sc[...] = base_in_ref[...]

    g_onehot = jnp.where(lane_f == gidx, 1.0, 0.0)
    earlier_tok = (lax.broadcasted_iota(jnp.int32, (tm, tm), 1) < lax.broadcasted_iota(jnp.int32, (tm, tm), 0))
    earlier = jnp.dot(jnp.where(earlier_tok, 1.0, 0.0).astype(BF16), g_onehot.astype(BF16),
                      preferred_element_type=F32)
    base = base_sc[...]
    rank = jnp.sum(g_onehot * (earlier + base), axis=-1, keepdims=True)
    base = base + jnp.sum(g_onehot, axis=0, keepdims=True)
    base_sc[...] = base
    base_out_ref[...] = base
    route_ref[...] = jnp.where(lane_f == 0.0, gidx, jnp.where(lane_f == 1.0, rank, 0.0))


def _outproj_router(o_pool, o_attn, o_mem, x, wo_bf, g2, wr, br, base_in, tm):
    n, d = x.shape
    assert n % tm == 0
    row = lambda wdt: pl.BlockSpec((tm, wdt), lambda i: (i, 0))
    full = lambda shp: pl.BlockSpec(shp, lambda i: (0, 0))
    return pl.pallas_call(
        functools.partial(_outproj_router_body, tm=tm),
        grid=(n // tm,),
        in_specs=[row(POOL_WIDTH), row(MOBA_WIDTH), row(MEM_WIDTH), row(d),
                  full((d, d)), full((1, d)), full((d, 2 * LANES)), full((1, LANES)), full((1, LANES))],
        out_specs=[row(PAYLOAD_WIDTH), row(LANES), full((1, LANES))],
        out_shape=[jax.ShapeDtypeStruct((n, PAYLOAD_WIDTH), F32), jax.ShapeDtypeStruct((n, LANES), F32),
                   jax.ShapeDtypeStruct((1, LANES), F32)],
        scratch_shapes=[pltpu.VMEM((1, LANES), F32)],
        compiler_params=_cparams(("arbitrary",), 48),
        name="outproj_router",
    )(o_pool, o_attn, o_mem, x, wo_bf, g2.reshape(1, d), wr, br, base_in)


def _row_move_body(idx_ref, src_ref, *rest, n_rows, chunk, scatter):
    dst_ref, sem = rest[-2], rest[-1]

    def row_copy(r):
        i = idx_ref[r]
        if scatter:
            return pltpu.make_async_copy(src_ref.at[pl.ds(r, 1)], dst_ref.at[pl.ds(i, 1)], sem)
        return pltpu.make_async_copy(src_ref.at[pl.ds(i, 1)], dst_ref.at[pl.ds(r, 1)], sem)

    def wait_chunk():
        pltpu.make_async_copy(src_ref.at[pl.ds(0, chunk)], dst_ref.at[pl.ds(0, chunk)], sem).wait()

    def do_chunk(c, carry):
        def issue(r, cc):
            row_copy(c * chunk + r).start()
            return cc

        lax.fori_loop(0, chunk, issue, 0)

        @pl.when(c > 0)
        def _():
            wait_chunk()

        return carry

    lax.fori_loop(0, n_rows // chunk, do_chunk, 0)
    wait_chunk()


def _row_scatter(idx, src, dst):
    n_rows, width = src.shape
    chunk = _pick_tile(n_rows, (512, 256))
    assert n_rows % chunk == 0 and dst.shape[0] >= chunk and dst.shape[1] == width
    any_spec = pl.BlockSpec(memory_space=pl.ANY)
    return pl.pallas_call(
        functools.partial(_row_move_body, n_rows=n_rows, chunk=chunk, scatter=True),
        grid_spec=pltpu.PrefetchScalarGridSpec(
            num_scalar_prefetch=1, grid=(1,), in_specs=[any_spec, any_spec], out_specs=any_spec,
            scratch_shapes=[pltpu.SemaphoreType.DMA(())]),
        out_shape=jax.ShapeDtypeStruct(dst.shape, dst.dtype),
        input_output_aliases={2: 0},
        compiler_params=pltpu.CompilerParams(dimension_semantics=("arbitrary",)),
        name="row_scatter",
    )(idx, src, dst)


def _row_gather(idx, src, n_rows):
    width = src.shape[1]
    chunk = _pick_tile(n_rows, (512, 256))
    assert n_rows % chunk == 0 and src.shape[0] >= chunk
    any_spec = pl.BlockSpec(memory_space=pl.ANY)
    return pl.pallas_call(
        functools.partial(_row_move_body, n_rows=n_rows, chunk=chunk, scatter=False),
        grid_spec=pltpu.PrefetchScalarGridSpec(
            num_scalar_prefetch=1, grid=(1,), in_specs=[any_spec], out_specs=any_spec,
            scratch_shapes=[pltpu.SemaphoreType.DMA(())]),
        out_shape=jax.ShapeDtypeStruct((n_rows, width), src.dtype),
        compiler_params=pltpu.CompilerParams(dimension_semantics=("arbitrary",)),
        name="row_gather",
    )(idx, src)


def _moe_body(widx_ref, xidx_ref, nused_ref, xs_ref, wg_ref, wu_ref, wd_ref, y_ref):
    t = pl.program_id(0)
    k = pl.program_id(1)
    used = t < nused_ref[0]

    @pl.when(used & (k == 0))
    def _init():
        y_ref[...] = xs_ref[:, PAYLOAD_X1:PAYLOAD_X1 + D_MODEL]

    @pl.when(jnp.logical_not(used) & (k == 0))
    def _unused():
        y_ref[...] = jnp.zeros(y_ref.shape, F32)

    @pl.when(used)
    def _expert():
        e = widx_ref[t * EXPERTS_PER_GROUP + k]
        h = xs_ref[:, PAYLOAD_H2:PAYLOAD_H2 + D_MODEL].astype(BF16)
        a = jnp.dot(h, wg_ref[0].astype(BF16), preferred_element_type=F32)
        b = jnp.dot(h, wu_ref[0].astype(BF16), preferred_element_type=F32)
        lane = lax.broadcasted_iota(jnp.int32, (MOE_TILE, LANES), 1)
        ce = jnp.sum(jnp.where(lane == e, xs_ref[:, PAYLOAD_COMB:], 0.0), axis=-1, keepdims=True)
        hid = (a * jax.nn.sigmoid(a)) * b * ce
        y_ref[...] += jnp.dot(hid.astype(BF16), wd_ref[0].astype(BF16), preferred_element_type=F32)


def _moe(widx, xidx, nused, xs, w_gate, w_up, w_down):
    p_pad = xs.shape[0]
    d = D_MODEL
    n_tiles = p_pad // MOE_TILE
    w_map = lambda t, k, widx, xidx, nused: (widx[t * EXPERTS_PER_GROUP + k], 0, 0)
    grid_spec = pltpu.PrefetchScalarGridSpec(
        num_scalar_prefetch=3,
        grid=(n_tiles, EXPERTS_PER_GROUP),
        in_specs=[
            pl.BlockSpec((MOE_TILE, PAYLOAD_WIDTH), lambda t, k, widx, xidx, nused: (xidx[t], 0)),
            pl.BlockSpec((1, d, D_EXPERT), w_map),
            pl.BlockSpec((1, d, D_EXPERT), w_map),
            pl.BlockSpec((1, D_EXPERT, d), w_map),
        ],
        out_specs=pl.BlockSpec((MOE_TILE, d), lambda t, k, widx, xidx, nused: (t, 0)),
    )
    return pl.pallas_call(
        _moe_body,
        grid_spec=grid_spec,
        out_shape=jax.ShapeDtypeStruct((p_pad, d), F32),
        compiler_params=_cparams(("arbitrary", "arbitrary"), 56),
        name="moe",
    )(widx, xidx, nused, xs, w_gate, w_up, w_down)


def _moe_plan(route_list, counts):
    n_total = sum(r.shape[0] for r in route_list)
    n_tiles = -(-(n_total + MOE_GROUPS * (MOE_TILE - 1)) // MOE_TILE)
    cnt = counts[0, :MOE_GROUPS].astype(jnp.int32)
    padded = (cnt + (MOE_TILE - 1)) // MOE_TILE * MOE_TILE
    seg_end = jnp.cumsum(padded)
    seg_start = seg_end - padded
    dests = [seg_start[r[:, 0].astype(jnp.int32)] + r[:, 1].astype(jnp.int32) for r in route_list]
    nused = seg_end[-1] // MOE_TILE
    tile = jnp.minimum(jnp.arange(n_tiles, dtype=jnp.int32), nused - 1)
    tile_group = jnp.minimum(jnp.sum((tile[:, None] * MOE_TILE >= seg_end[None, :]).astype(jnp.int32), axis=1),
                             MOE_GROUPS - 1)
    k = jnp.arange(EXPERTS_PER_GROUP, dtype=jnp.int32)[None, :]
    live = (jnp.arange(n_tiles, dtype=jnp.int32) < nused)[:, None]
    widx = tile_group[:, None] * EXPERTS_PER_GROUP + jnp.where(live, k, EXPERTS_PER_GROUP - 1)
    return dests, widx.reshape(-1), tile, nused.reshape(1), n_tiles


def _block_diag_ones(width, group):
    r = lax.broadcasted_iota(jnp.int32, (width, width), 0) // group
    c = lax.broadcasted_iota(jnp.int32, (width, width), 1) // group
    return (r == c).astype(BF16)


def _pick_tile(n, candidates):
    for c in candidates:
        if n % c == 0:
            return c
    return n


IN_SEGS = (
    (0, POOL_WIDTH, False, False, False, False),
    (POOL_WIDTH, MOBA_WIDTH, True, False, False, False),
    (POOL_WIDTH + MOBA_WIDTH, MOBA_WIDTH, True, True, True, True),
    (POOL_WIDTH + 2 * MOBA_WIDTH, MOBA_WIDTH, False, True, True, False),
    (POOL_WIDTH + 3 * MOBA_WIDTH, MEM_WIDTH, True, False, False, False),
)
IN_SEGS_SAMPLE = tuple((c0, wd, hn, False, False, False) for (c0, wd, hn, _, _, _) in IN_SEGS)
MEMKV_SEGS = ((0, MEM_WIDTH, True, False, False, False), (MEM_WIDTH, MEM_WIDTH, False, False, False, False))


def kernel(x_prompt, x_sample, mem_prompt, cache_k, cache_v, state_pool, cache_mem_k, cache_mem_v,
           page_table, norm1_gain, w_in, pool_w, pool_scale, moba_q_gain, moba_k_gain,
           mem_norm_gain, w_mem_kv, mem_q_gain, mem_k_gain, out_gain, w_out, norm2_gain,
           router_group_w, router_group_b, router_expert_w, router_expert_b, w_gate, w_up, w_down):
    bp, tp, d = x_prompt.shape
    bs, ts, _ = x_sample.shape
    n_p, n_s = bp * tp, bs * ts

    w_in_bf = w_in.astype(BF16)
    w_mem_bf = w_mem_kv.astype(BF16)
    wo_bf = w_out.astype(BF16)
    ones_bd = _block_diag_ones(MOBA_WIDTH, HEAD_DIM)
    head_gain_in = jnp.concatenate([
        jnp.ones((POOL_WIDTH,), F32), jnp.tile(moba_q_gain, MOBA_HEADS), jnp.tile(moba_k_gain, MOBA_HEADS),
        jnp.ones((MOBA_WIDTH,), F32), jnp.tile(mem_q_gain, MEM_HEADS)])
    head_gain_mem = jnp.concatenate([jnp.tile(mem_k_gain, MEM_HEADS), jnp.ones((MEM_WIDTH,), F32)])
    wbd = jnp.zeros((POOL_WIDTH, POOL_WIDTH), F32)
    for g in range(len(POOL_WINDOWS)):
        sl = slice(g * POOL_GROUP_WIDTH, (g + 1) * POOL_GROUP_WIDTH)
        wbd = wbd.at[sl, sl].set(pool_w[g])
    wbd_bf = wbd.astype(BF16)
    og_pool, og_attn, og_mem = (out_gain[:POOL_WIDTH], out_gain[POOL_WIDTH:POOL_WIDTH + MOBA_WIDTH],
                                out_gain[POOL_WIDTH + MOBA_WIDTH:])
    wr = jnp.zeros((d, LANES), F32)
    wr = wr.at[:, :N_EXPERTS].set(router_expert_w).at[:, N_EXPERTS:N_EXPERTS + MOE_GROUPS].set(router_group_w)
    br = jnp.zeros((1, LANES), F32)
    br = br.at[0, :N_EXPERTS].set(router_expert_b).at[0, N_EXPERTS:N_EXPERTS + MOE_GROUPS].set(router_group_b)
    wr_hi = wr.astype(BF16)
    wr = jnp.concatenate([wr_hi, (wr - wr_hi.astype(F32)).astype(BF16)], axis=1)
    wg = w_gate.reshape(N_EXPERTS, d, D_EXPERT)
    wu = w_up.reshape(N_EXPERTS, d, D_EXPERT)
    wd = w_down.reshape(N_EXPERTS, D_EXPERT, d)

    def router(o_pool, o_attn, o_mem, x2d, base):
        return _outproj_router(o_pool, o_attn, o_mem, x2d, wo_bf, norm2_gain, wr, br, base,
                               _pick_tile(x2d.shape[0], (512, 256)))

    mem_k_p, mem_v_p = _norm_proj(mem_prompt.reshape(bp * MEM_TOKENS, d), mem_norm_gain, w_mem_bf,
                                  head_gain_mem, ones_bd, MEMKV_SEGS, _pick_tile(bp * MEM_TOKENS, (512, 256)))
    xp = x_prompt.reshape(n_p, d)
    u_p, q_p, kt_p, k_p_bf, kmean_p, vt_p, v_p_bf, qm_p = _norm_proj(
        xp, norm1_gain, w_in_bf, head_gain_in, ones_bd, IN_SEGS, _pick_tile(tp, (512, 256)), rows_per_batch=tp)
    o_pool_p = _pool(u_p.reshape(bp, tp, POOL_WIDTH), jnp.zeros((bp, 16, POOL_WIDTH), F32), wbd_bf,
                     pool_scale, og_pool, 0, 1, MOBA_BLOCK).reshape(n_p, POOL_WIDTH)
    nblk = tp // MOBA_BLOCK
    kmean_pad = jnp.pad(kmean_p.reshape(bp, nblk, MOBA_WIDTH), ((0, 0), (0, LANES - nblk), (0, 0)))
    o_attn_p = _moba_prompt(q_p, k_p_bf, v_p_bf, kmean_pad, og_attn, bp, tp)
    o_mem_p = _mem_attn(qm_p, mem_k_p.reshape(bp, MEM_TOKENS, MEM_WIDTH),
                        mem_v_p.reshape(bp, MEM_TOKENS, MEM_WIDTH), og_mem, bp, tp, _pick_tile(tp, (512, 256)))
    payload_p, route_p, counts_p = router(o_pool_p, o_attn_p, o_mem_p, xp, jnp.zeros((1, LANES), F32))

    past_len = page_table.shape[1] * PAGE_SIZE
    xs = x_sample.reshape(n_s, d)
    u_s, q_s, k_s, v_s, qm_s = _norm_proj(xs, norm1_gain, w_in_bf, head_gain_in, ones_bd, IN_SEGS_SAMPLE, n_s)
    u_s3 = u_s.reshape(bs, ts, POOL_WIDTH)
    buf16 = jnp.concatenate([jnp.zeros((bs, 1, POOL_WIDTH), F32), state_pool], axis=1)
    o_pool_s = _pool(u_s3, buf16, wbd_bf, pool_scale, og_pool, past_len, bs, ts).reshape(n_s, POOL_WIDTH)
    o_attn_s = _moba_sample(page_table, q_s, k_s, v_s,
                            cache_k.transpose(0, 2, 3, 1).reshape(-1, MOBA_WIDTH, PAGE_SIZE),
                            cache_v.transpose(0, 2, 3, 1).reshape(-1, MOBA_WIDTH, PAGE_SIZE),
                            og_attn, bs, ts)
    o_mem_s = _mem_attn(qm_s, cache_mem_k.reshape(bs, MEM_TOKENS, MEM_WIDTH),
                        cache_mem_v.reshape(bs, MEM_TOKENS, MEM_WIDTH), og_mem, bs, ts, ts)
    payload_s, route_s, counts = router(o_pool_s, o_attn_s, o_mem_s, xs, counts_p)

    (dest_p, dest_s), widx, xidx, nused, n_tiles = _moe_plan([route_p, route_s], counts)
    xs_sorted = jnp.zeros((n_tiles * MOE_TILE, PAYLOAD_WIDTH), F32)
    xs_sorted = _row_scatter(dest_p, payload_p, xs_sorted)
    xs_sorted = _row_scatter(dest_s, payload_s, xs_sorted)
    y_sorted = _moe(widx, xidx, nused, xs_sorted, wg, wu, wd)
    y_p = _row_gather(dest_p, y_sorted, n_p)
    y_s = _row_gather(dest_s, y_sorted, n_s)

    pool_prompt = u_p.reshape(bp, tp, POOL_WIDTH)[:, tp - POOL_STATE:]
    pool_sample = jnp.concatenate([state_pool, u_s3], axis=1)[:, -POOL_STATE:]
    k_p = kt_p.reshape(bp, MOBA_HEADS, HEAD_DIM, tp).transpose(0, 3, 1, 2)
    v_p = vt_p.reshape(bp, MOBA_HEADS, HEAD_DIM, tp).transpose(0, 3, 1, 2)
    return (y_p.reshape(bp, tp, d), y_s.reshape(bs, ts, d), k_p, v_p,
            pool_prompt,
            mem_k_p.reshape(bp, MEM_TOKENS, MEM_HEADS, HEAD_DIM), mem_v_p.reshape(bp, MEM_TOKENS, MEM_HEADS, HEAD_DIM),
            k_s.reshape(bs, ts, MOBA_HEADS, HEAD_DIM), v_s.reshape(bs, ts, MOBA_HEADS, HEAD_DIM),
            pool_sample)
```

```python
import functools

import jax
import jax.numpy as jnp
from jax import lax
from jax.experimental import pallas as pl
from jax.experimental.pallas import tpu as pltpu

F32 = jnp.float32
BF16 = jnp.bfloat16

D_MODEL = 1024
HEAD_DIM = 64
POOL_WIDTH = 256
POOL_WINDOWS = (2, 4, 8, 16)
POOL_GROUP_WIDTH = 64
POOL_STATE = 15
MOBA_WIDTH = 512
MOBA_HEADS = 8
MOBA_BLOCK = 256
MOBA_TOPK = 3
MEM_WIDTH = 256
MEM_HEADS = 4
MEM_TOKENS = 256
PAGE_SIZE = 128
MOE_GROUPS = 4
EXPERTS_PER_GROUP = 8
N_EXPERTS = MOE_GROUPS * EXPERTS_PER_GROUP
D_EXPERT = 256
NORM_EPS = 1e-6

LANES = 128
HEAD_PAIR = 2 * HEAD_DIM
QK_SCALE = HEAD_DIM ** -0.5
NEG_INF = float("-inf")
MIB = 1024 * 1024
NT_DIMS = (((1,), (1,)), ((), ()))


def _cparams(semantics, vmem_mib):
    return pltpu.CompilerParams(dimension_semantics=semantics, vmem_limit_bytes=vmem_mib * MIB)


def _rms(y, eps=NORM_EPS):
    return y * lax.rsqrt(jnp.mean(y * y, axis=-1, keepdims=True) + eps)


def _norm_proj_body(x_ref, g_ref, w_ref, hg_ref, ones_ref, *outs, segs, tm):
    hb = (_rms(x_ref[...]) * g_ref[...]).astype(BF16)
    oi = 0
    for (c0, wd, headnorm, transposed, want_bf16, want_blockmean) in segs:
        y = jnp.dot(hb, w_ref[:, c0:c0 + wd], preferred_element_type=F32)
        if headnorm:
            sq = (y * y).astype(BF16)
            msq = jnp.dot(sq, ones_ref[:wd, :wd], preferred_element_type=F32) * (1.0 / HEAD_DIM)
            y = (y * lax.rsqrt(msq + NORM_EPS)) * hg_ref[:, c0:c0 + wd]
        if transposed:
            outs[oi][0] = y.T
        else:
            outs[oi][...] = y
        oi += 1
        if want_bf16:
            outs[oi][...] = y.astype(BF16)
            oi += 1
        if want_blockmean:
            for bi in range(tm // MOBA_BLOCK):
                outs[oi][bi] = jnp.mean(y[bi * MOBA_BLOCK:(bi + 1) * MOBA_BLOCK], axis=0, keepdims=True)
            oi += 1


def _norm_proj(x, gain, w_bf, head_gain, ones_bd, segs, tm, rows_per_batch=None):
    n, d = x.shape
    wtot = w_bf.shape[1]
    assert n % tm == 0
    out_shape, out_specs = [], []
    for (c0, wd, headnorm, transposed, want_bf16, want_blockmean) in segs:
        if transposed:
            assert rows_per_batch % tm == 0 and n % rows_per_batch == 0
            steps = rows_per_batch // tm
            out_shape.append(jax.ShapeDtypeStruct((n // rows_per_batch, wd, rows_per_batch), F32))
            out_specs.append(pl.BlockSpec((1, wd, tm), lambda i, steps=steps: (i // steps, 0, i % steps)))
        else:
            out_shape.append(jax.ShapeDtypeStruct((n, wd), F32))
            out_specs.append(pl.BlockSpec((tm, wd), lambda i: (i, 0)))
        if want_bf16:
            out_shape.append(jax.ShapeDtypeStruct((n, wd), BF16))
            out_specs.append(pl.BlockSpec((tm, wd), lambda i: (i, 0)))
        if want_blockmean:
            assert tm % MOBA_BLOCK == 0
            nb = tm // MOBA_BLOCK
            out_shape.append(jax.ShapeDtypeStruct((n // MOBA_BLOCK, 1, wd), F32))
            out_specs.append(pl.BlockSpec((nb, 1, wd), lambda i: (i, 0, 0)))
    return pl.pallas_call(
        functools.partial(_norm_proj_body, segs=segs, tm=tm),
        grid=(n // tm,),
        in_specs=[
            pl.BlockSpec((tm, d), lambda i: (i, 0)),
            pl.BlockSpec((1, d), lambda i: (0, 0)),
            pl.BlockSpec((d, wtot), lambda i: (0, 0)),
            pl.BlockSpec((1, wtot), lambda i: (0, 0)),
            pl.BlockSpec(ones_bd.shape, lambda i: (0, 0)),
        ],
        out_specs=out_specs,
        out_shape=out_shape,
        compiler_params=_cparams(("parallel",), 48),
        name="norm_proj",
    )(x, gain.reshape(1, d), w_bf, head_gain.reshape(1, wtot), ones_bd)


def _pool_windows(win, pos0):
    r = win.shape[0] - 16
    lane = lax.broadcasted_iota(jnp.int32, (r, LANES), 1)
    pos1 = pos0 + lax.broadcasted_iota(jnp.int32, (r, LANES), 0) + 1
    low = lane < POOL_GROUP_WIDTH
    a = win[:, :LANES]
    b = win[:, LANES:]
    a2 = a + pltpu.roll(a, 1, 0)
    a4 = a2 + pltpu.roll(a2, 2, 0)
    b2 = b + pltpu.roll(b, 1, 0)
    b4 = b2 + pltpu.roll(b2, 2, 0)
    b8 = b4 + pltpu.roll(b4, 4, 0)
    b16 = b8 + pltpu.roll(b8, 8, 0)
    cnt_a = jnp.minimum(jnp.where(low, POOL_WINDOWS[0], POOL_WINDOWS[1]), pos1).astype(F32)
    cnt_b = jnp.minimum(jnp.where(low, POOL_WINDOWS[2], POOL_WINDOWS[3]), pos1).astype(F32)
    pa = jnp.where(low, a2[16:], a4[16:]) / cnt_a - a[16:]
    pb = jnp.where(low, b8[16:], b16[16:]) / cnt_b - b[16:]
    return jnp.concatenate([pa, pb], axis=1)


def _pool_finish(pooled, wbd_ref, ps_ref, og_ref):
    mixed = jnp.dot(pooled.astype(BF16), wbd_ref[...], preferred_element_type=F32) * ps_ref[...]
    return _rms(mixed) * og_ref[...]


def _pool_body(u_ref, buf_ref, wbd_ref, ps_ref, og_ref, o_ref, ext_sc, *, bb, t, r, pos0):
    for bi in range(bb):
        ext_sc[bi, 0:16, :] = buf_ref[bi]
        ext_sc[bi, 16:, :] = u_ref[bi]
    if t == r:
        pooled = [_pool_windows(ext_sc[bi], pos0) for bi in range(bb)]
        out = _pool_finish(jnp.concatenate(pooled, axis=0), wbd_ref, ps_ref, og_ref)
        for bi in range(bb):
            o_ref[bi] = out[bi * r:(bi + 1) * r]
    else:
        assert bb == 1

        def chunk(c, carry):
            base = pl.multiple_of(c * r, r)
            pooled = _pool_windows(ext_sc[0, pl.ds(base, r + 16), :], pos0 + base)
            o_ref[0, pl.ds(base, r), :] = _pool_finish(pooled, wbd_ref, ps_ref, og_ref)
            return carry

        lax.fori_loop(0, t // r, chunk, 0)


def _pool(u, buf16, wbd_bf, pool_scale, og, pos0, bb, r):
    b, t, c = u.shape
    assert b % bb == 0 and t % r == 0
    return pl.pallas_call(
        functools.partial(_pool_body, bb=bb, t=t, r=r, pos0=pos0),
        grid=(b // bb,),
        in_specs=[
            pl.BlockSpec((bb, t, c), lambda i: (i, 0, 0)),
            pl.BlockSpec((bb, 16, c), lambda i: (i, 0, 0)),
            pl.BlockSpec((c, c), lambda i: (0, 0)),
            pl.BlockSpec((1, c), lambda i: (0, 0)),
            pl.BlockSpec((1, c), lambda i: (0, 0)),
        ],
        out_specs=pl.BlockSpec((bb, t, c), lambda i: (i, 0, 0)),
        out_shape=jax.ShapeDtypeStruct((b, t, c), F32),
        scratch_shapes=[pltpu.VMEM((bb, t + 16, c), F32)],
        compiler_params=_cparams(("parallel",), 40),
        name="pool",
    )(u, buf16, wbd_bf, pool_scale.reshape(1, c), og.reshape(1, c))


def _alibi_slope(h):
    return 2.0 ** (-8.0 * (h + 1) / MOBA_HEADS)


def _moba_prompt_body(q_ref, k_ref, v_ref, km_ref, og_ref, o_ref, m_sc, l_sc, acc_sc, sel_sc, qst_sc):
    i = pl.program_id(1)
    tq = MOBA_BLOCK
    n_pairs = MOBA_HEADS // 2
    lane = lax.broadcasted_iota(jnp.int32, (tq, LANES), 1)
    lane_f = lane.astype(F32)
    low = lane < HEAD_DIM
    row2 = lax.broadcasted_iota(jnp.int32, (tq, MOBA_BLOCK), 0)
    col2 = lax.broadcasted_iota(jnp.int32, (tq, MOBA_BLOCK), 1)
    causal = col2 <= row2
    colrow = lax.broadcasted_iota(jnp.int32, (1, MOBA_BLOCK), 1).astype(F32)

    def pair_cols(pr):
        return slice(pr * HEAD_PAIR, (pr + 1) * HEAD_PAIR)

    def scores(pr, j):
        start = pl.multiple_of(j * MOBA_BLOCK, MOBA_BLOCK)
        kj = k_ref[pl.ds(start, MOBA_BLOCK), pair_cols(pr)]
        s = lax.dot_general(qst_sc[pr], kj, NT_DIMS, preferred_element_type=F32)
        off = colrow + ((j - i) * MOBA_BLOCK).astype(F32)
        return [s[hh * tq:(hh + 1) * tq] + _alibi_slope(2 * pr + hh) * off for hh in range(2)]

    def pv(pr, j, ps):
        start = pl.multiple_of(j * MOBA_BLOCK, MOBA_BLOCK)
        vj = v_ref[pl.ds(start, MOBA_BLOCK), pair_cols(pr)]
        return jnp.dot(jnp.concatenate(ps, axis=0).astype(BF16), vj, preferred_element_type=F32)

    for pr in range(n_pairs):
        q_pair = q_ref[:, pair_cols(pr)]
        km_pair = km_ref[0, :, pair_cols(pr)]
        q_heads = [jnp.where(low, q_pair, 0.0), jnp.where(low, 0.0, q_pair)]

        for hh in range(2):
            gate = lax.dot_general(q_heads[hh], km_pair, NT_DIMS, precision=lax.Precision.HIGHEST,
                                   preferred_element_type=F32)
            g = jnp.where(lane < i, gate, NEG_INF)
            sel = jnp.zeros((tq, LANES), F32)
            for _ in range(MOBA_TOPK):
                mx = jnp.max(g, axis=-1, keepdims=True)
                idx = jnp.min(jnp.where(g == mx, lane_f, float(LANES)), axis=-1, keepdims=True)
                pick = (lane_f == idx) & (mx > NEG_INF)
                sel = jnp.where(pick, 1.0, sel)
                g = jnp.where(pick, NEG_INF, g)
            sel_sc[2 * pr + hh] = sel

        qst_sc[pr] = (jnp.concatenate(q_heads, axis=0) * QK_SCALE).astype(BF16)

        s_own = scores(pr, i)
        ps = []
        for hh in range(2):
            sh = jnp.where(causal, s_own[hh], NEG_INF)
            m = jnp.max(sh, axis=-1, keepdims=True)
            p = jnp.exp(sh - m)
            m_sc[2 * pr + hh] = jnp.broadcast_to(m, (tq, LANES))
            l_sc[2 * pr + hh] = jnp.broadcast_to(jnp.sum(p, axis=-1, keepdims=True), (tq, LANES))
            ps.append(p)
        acc_sc[pr] = pv(pr, i, ps)

    def past(j, carry):
        for pr in range(n_pairs):
            s_j = scores(pr, j)
            ps, alphas = [], []
            for hh in range(2):
                h = 2 * pr + hh
                selcol = jnp.sum(jnp.where(lane == j, sel_sc[h], 0.0), axis=-1, keepdims=True)
                sh = jnp.where(selcol > 0.0, s_j[hh], NEG_INF)
                m_prev = m_sc[h]
                m_new = jnp.maximum(m_prev, jnp.max(sh, axis=-1, keepdims=True))
                alpha = jnp.exp(m_prev - m_new)
                p = jnp.exp(sh - jnp.concatenate([m_new, m_new], axis=1))
                l_sc[h] = alpha * l_sc[h] + jnp.sum(p, axis=-1, keepdims=True)
                m_sc[h] = m_new
                ps.append(p)
                alphas.append(alpha)
            acc_sc[pr] = jnp.concatenate(alphas, axis=0) * acc_sc[pr] + pv(pr, j, ps)
        return carry

    lax.fori_loop(0, i, past, 0)

    outs = []
    for pr in range(n_pairs):
        acc = acc_sc[pr]
        outs.append(jnp.where(low, acc[:tq] / l_sc[2 * pr], acc[tq:] / l_sc[2 * pr + 1]))
    o_ref[...] = _rms(jnp.concatenate(outs, axis=1)) * og_ref[...]


def _moba_prompt(q, k_bf, v_bf, kmean_pad, og, b, t):
    n, w = q.shape
    nblk = t // MOBA_BLOCK
    tq = MOBA_BLOCK
    return pl.pallas_call(
        _moba_prompt_body,
        grid=(b, nblk),
        in_specs=[
            pl.BlockSpec((tq, w), lambda bi, i: (bi * nblk + i, 0)),
            pl.BlockSpec((t, w), lambda bi, i: (bi, 0)),
            pl.BlockSpec((t, w), lambda bi, i: (bi, 0)),
            pl.BlockSpec((1, LANES, w), lambda bi, i: (bi, 0, 0)),
            pl.BlockSpec((1, w), lambda bi, i: (0, 0)),
        ],
        out_specs=pl.BlockSpec((tq, w), lambda bi, i: (bi * nblk + i, 0)),
        out_shape=jax.ShapeDtypeStruct((n, w), F32),
        scratch_shapes=[
            pltpu.VMEM((MOBA_HEADS, tq, LANES), F32),
            pltpu.VMEM((MOBA_HEADS, tq, LANES), F32),
            pltpu.VMEM((MOBA_HEADS // 2, 2 * tq, LANES), F32),
            pltpu.VMEM((MOBA_HEADS, tq, LANES), F32),
            pltpu.VMEM((MOBA_HEADS // 2, 2 * tq, LANES), BF16),
        ],
        compiler_params=_cparams(("parallel", "arbitrary"), 40),
        name="moba_prompt",
    )(q, k_bf, v_bf, kmean_pad, og.reshape(1, w))


SAMPLE_BLOCKS_PER_STEP = 8
PAGES_PER_BLOCK = MOBA_BLOCK // PAGE_SIZE
PAGES_PER_STEP = SAMPLE_BLOCKS_PER_STEP * PAGES_PER_BLOCK


def _moba_sample_body(pt_ref, q_ref, kn_ref, vn_ref, slope_ref, og_ref, *rest, n_past_blocks, t_new):
    kp = rest[:PAGES_PER_STEP]
    vp = rest[PAGES_PER_STEP:2 * PAGES_PER_STEP]
    o_ref, o_sc, m_sc, l_sc, kmt_sc = rest[2 * PAGES_PER_STEP:]
    c = pl.program_id(1)
    rows = MOBA_HEADS * t_new
    row_h = lax.broadcasted_iota(jnp.int32, (rows, MOBA_WIDTH), 0) // t_new
    lane_h = lax.broadcasted_iota(jnp.int32, (rows, MOBA_WIDTH), 1) // HEAD_DIM
    bd = row_h == lane_h
    q = q_ref[...]
    qbd = jnp.where(bd, jnp.concatenate([q] * MOBA_HEADS, axis=0), 0.0)
    qbd_bf = (qbd * QK_SCALE).astype(BF16)
    slope = slope_ref[...]
    slope2 = jnp.concatenate([slope, slope], axis=1)
    colf = lax.broadcasted_iota(jnp.int32, (rows, MOBA_BLOCK), 1).astype(F32)

    blk_lane = lax.broadcasted_iota(jnp.int32, (MOBA_WIDTH, LANES), 1)

    @pl.when(c == 0)
    def _init():
        kmt_sc[...] = jnp.zeros((MOBA_WIDTH, LANES), F32)

    for jj in range(SAMPLE_BLOCKS_PER_STEP):
        j = c * SAMPLE_BLOCKS_PER_STEP + jj
        kt_pages = [kp[PAGES_PER_BLOCK * jj + p][0] for p in range(PAGES_PER_BLOCK)]
        vt_pages = [vp[PAGES_PER_BLOCK * jj + p][0] for p in range(PAGES_PER_BLOCK)]
        ksum = kt_pages[0]
        for kt in kt_pages[1:]:
            ksum = ksum + kt
        kmean_col = jnp.sum(ksum, axis=-1, keepdims=True) * (1.0 / MOBA_BLOCK)
        kmt_sc[...] = jnp.where(blk_lane == j, kmean_col, kmt_sc[...])
        kt_bf = jnp.concatenate(kt_pages, axis=1).astype(BF16)
        vt_bf = jnp.concatenate(vt_pages, axis=1).astype(BF16)
        s = jnp.dot(qbd_bf, kt_bf, preferred_element_type=F32)
        s = s + slope2 * (colf + ((j - n_past_blocks) * MOBA_BLOCK).astype(F32))
        m = jnp.max(s, axis=-1, keepdims=True)
        p = jnp.exp(s - m)
        o = lax.dot_general(p.astype(BF16), vt_bf, NT_DIMS, preferred_element_type=F32)
        o_sc[j] = jnp.where(bd, o, 0.0)
        m_sc[j] = jnp.broadcast_to(m, (rows, LANES))
        l_sc[j] = jnp.broadcast_to(jnp.sum(p, axis=-1, keepdims=True), (rows, LANES))

    @pl.when(c == pl.num_programs(1) - 1)
    def _finish():
        tq = lax.broadcasted_iota(jnp.int32, (rows, LANES), 0) % t_new
        kn = kn_ref[...]
        vn = vn_ref[...]
        qs = qbd * QK_SCALE
        s_own = []
        m_run = jnp.full((rows, LANES), NEG_INF, F32)
        for cc in range(t_new):
            sc = jnp.sum(qs * kn[cc:cc + 1, :], axis=-1, keepdims=True) + slope * float(cc)
            sc = jnp.where(tq >= cc, sc, NEG_INF)
            s_own.append(sc)
            m_run = jnp.maximum(m_run, sc)

        gates = jnp.dot(qbd, kmt_sc[...], precision=lax.Precision.HIGHEST, preferred_element_type=F32)
        blk = lax.broadcasted_iota(jnp.int32, (rows, LANES), 1)
        blk_f = blk.astype(F32)
        g = jnp.where(blk < n_past_blocks, gates, NEG_INF)
        sel = jnp.zeros((rows, LANES), F32)
        for _ in range(MOBA_TOPK):
            mx = jnp.max(g, axis=-1, keepdims=True)
            idx = jnp.min(jnp.where(g == mx, blk_f, float(LANES)), axis=-1, keepdims=True)
            pick = (blk_f == idx) & (mx > NEG_INF)
            sel = jnp.where(pick, 1.0, sel)
            g = jnp.where(pick, NEG_INF, g)

        m_all = jnp.full((rows, LANES), NEG_INF, F32)
        l_all = jnp.zeros((rows, LANES), F32)
        for j in range(n_past_blocks):
            m_all = jnp.where(blk == j, m_sc[j], m_all)
            l_all = jnp.where(blk == j, l_sc[j], l_all)
        m_run = jnp.maximum(m_run, jnp.max(jnp.where(sel > 0.0, m_all, NEG_INF), axis=-1, keepdims=True))
        w_all = jnp.where(sel > 0.0, jnp.exp(m_all - m_run), 0.0)

        l_run = jnp.broadcast_to(jnp.sum(w_all * l_all, axis=-1, keepdims=True), (rows, LANES))
        o_run = jnp.zeros((rows, MOBA_WIDTH), F32)
        for cc in range(t_new):
            p = jnp.exp(s_own[cc] - m_run)
            l_run = l_run + p
            o_run = o_run + jnp.concatenate([p] * (MOBA_WIDTH // LANES), axis=1) * vn[cc:cc + 1, :]
        o_run = jnp.where(bd, o_run, 0.0)
        for j in range(n_past_blocks):
            o_run = o_run + w_all[:, j:j + 1] * o_sc[j]
        o_bd = o_run / jnp.concatenate([l_run] * (MOBA_WIDTH // LANES), axis=1)
        attn = o_bd[0:t_new]
        for h in range(1, MOBA_HEADS):
            attn = attn + o_bd[h * t_new:(h + 1) * t_new]
        o_ref[...] = _rms(attn) * og_ref[...]


def _moba_sample(page_table, q, k_new, v_new, cache_kt, cache_vt, og, b, t_new):
    n_pages = page_table.shape[1]
    n_past_blocks = n_pages // PAGES_PER_BLOCK
    assert n_past_blocks <= LANES
    assert n_past_blocks % SAMPLE_BLOCKS_PER_STEP == 0
    n_steps = n_past_blocks // SAMPLE_BLOCKS_PER_STEP
    rows = MOBA_HEADS * t_new
    w = MOBA_WIDTH
    slopes = jnp.exp2(-8.0 * jnp.arange(1, MOBA_HEADS + 1, dtype=F32) / MOBA_HEADS)
    slope_rows = jnp.broadcast_to(jnp.repeat(slopes, t_new)[:, None], (rows, LANES))

    def page_spec(p):
        return pl.BlockSpec((1, w, PAGE_SIZE), lambda bi, c, pt: (pt[bi, c * PAGES_PER_STEP + p], 0, 0))

    row_spec = pl.BlockSpec((t_new, w), lambda bi, c, pt: (bi, 0))
    grid_spec = pltpu.PrefetchScalarGridSpec(
        num_scalar_prefetch=1,
        grid=(b, n_steps),
        in_specs=[row_spec, row_spec, row_spec,
                  pl.BlockSpec((rows, LANES), lambda bi, c, pt: (0, 0)),
                  pl.BlockSpec((1, w), lambda bi, c, pt: (0, 0))]
                 + [page_spec(p) for p in range(PAGES_PER_STEP)]
                 + [page_spec(p) for p in range(PAGES_PER_STEP)],
        out_specs=row_spec,
        scratch_shapes=[
            pltpu.VMEM((n_past_blocks, rows, w), F32),
            pltpu.VMEM((n_past_blocks, rows, LANES), F32),
            pltpu.VMEM((n_past_blocks, rows, LANES), F32),
            pltpu.VMEM((w, LANES), F32),
        ],
    )
    return pl.pallas_call(
        functools.partial(_moba_sample_body, n_past_blocks=n_past_blocks, t_new=t_new),
        grid_spec=grid_spec,
        out_shape=jax.ShapeDtypeStruct((b * t_new, w), F32),
        compiler_params=_cparams(("parallel", "arbitrary"), 48),
        name="moba_sample",
    )(page_table, q, k_new, v_new, slope_rows, og.reshape(1, w),
      *([cache_kt] * PAGES_PER_STEP), *([cache_vt] * PAGES_PER_STEP))


def _mem_attn_body(q_ref, mk_ref, mv_ref, og_ref, o_ref, *, tm):
    lane = lax.broadcasted_iota(jnp.int32, (tm, LANES), 1)
    low = lane < HEAD_DIM
    outs = []
    for pr in range(MEM_HEADS // 2):
        cs = slice(pr * HEAD_PAIR, (pr + 1) * HEAD_PAIR)
        q_pair = q_ref[:, cs]
        qst = (jnp.concatenate([jnp.where(low, q_pair, 0.0), jnp.where(low, 0.0, q_pair)], axis=0)
               * QK_SCALE).astype(BF16)
        s = lax.dot_general(qst, mk_ref[0, :, cs].astype(BF16), NT_DIMS, preferred_element_type=F32)
        p = jnp.exp(s - jnp.max(s, axis=-1, keepdims=True))
        l = jnp.sum(p, axis=-1, keepdims=True)
        o = jnp.dot(p.astype(BF16), mv_ref[0, :, cs].astype(BF16), preferred_element_type=F32) / l
        outs.append(jnp.where(low, o[:tm], o[tm:]))
    o_ref[...] = _rms(jnp.concatenate(outs, axis=1)) * og_ref[...]


def _mem_attn(qm, mem_k, mem_v, og, b, t, tm):
    n, w = qm.shape
    assert t % tm == 0
    steps = t // tm
    return pl.pallas_call(
        functools.partial(_mem_attn_body, tm=tm),
        grid=(b, steps),
        in_specs=[
            pl.BlockSpec((tm, w), lambda bi, i: (bi * steps + i, 0)),
            pl.BlockSpec((1, MEM_TOKENS, w), lambda bi, i: (bi, 0, 0)),
            pl.BlockSpec((1, MEM_TOKENS, w), lambda bi, i: (bi, 0, 0)),
            pl.BlockSpec((1, w), lambda bi, i: (0, 0)),
        ],
        out_specs=pl.BlockSpec((tm, w), lambda bi, i: (bi * steps + i, 0)),
        out_shape=jax.ShapeDtypeStruct((n, w), F32),
        compiler_params=_cparams(("parallel", "arbitrary"), 32),
        name="mem_attn",
    )(qm, mem_k, mem_v, og.reshape(1, w))


ROUTER_GROUP_LANE0 = N_EXPERTS


def _outproj_router_body(op_ref, oa_ref, om_ref, x_ref, wo_ref, g2_ref, wr_ref, br_ref,
                         x1_ref, h2_ref, comb_ref, *, tm):
    a0, a1 = POOL_WIDTH, POOL_WIDTH + MOBA_WIDTH
    y = x_ref[...]
    y = y + jnp.dot(op_ref[...].astype(BF16), wo_ref[0:a0, :], preferred_element_type=F32)
    y = y + jnp.dot(oa_ref[...].astype(BF16), wo_ref[a0:a1, :], preferred_element_type=F32)
    y = y + jnp.dot(om_ref[...].astype(BF16), wo_ref[a1:, :], preferred_element_type=F32)
    x1_ref[...] = y
    h2 = _rms(y) * g2_ref[...]
    h2_hi = h2.astype(BF16)
    h2_lo = (h2 - h2_hi.astype(F32)).astype(BF16)
    h2_ref[...] = h2_hi
    hw =jnp.dot(h2_hi, wr_ref[...], preferred_element_type=F32)
    lw = jnp.dot(h2_lo, wr_ref[:, :LANES], preferred_element_type=F32)
    logits = (hw[:, :LANES] + (hw[:, LANES:] + lw)) + br_ref[...]
    lane_f = lax.broadcasted_iota(jnp.int32, (tm, LANES), 1).astype(F32)
    big = float(LANES)
    g_lo = float(ROUTER_GROUP_LANE0)
    is_g = (lane_f >= g_lo) & (lane_f < g_lo + MOE_GROUPS)
    lg = jnp.where(is_g, logits, NEG_INF)
    mg = jnp.max(lg, axis=-1, keepdims=True)
    pg_top = 1.0 / jnp.sum(jnp.exp(lg - mg), axis=-1, keepdims=True)
    gidx = jnp.min(jnp.where(lg == mg, lane_f, big), axis=-1, keepdims=True) - g_lo
    e_lo = gidx * EXPERTS_PER_GROUP
    in_grp = (lane_f >= e_lo) & (lane_f < e_lo + EXPERTS_PER_GROUP)
    le = jnp.where(in_grp, logits, NEG_INF)
    m1 = jnp.max(le, axis=-1, keepdims=True)
    se = jnp.sum(jnp.exp(le - m1), axis=-1, keepdims=True)
    i1 = jnp.min(jnp.where(le == m1, lane_f, big), axis=-1, keepdims=True)
    le2 = jnp.where(lane_f == i1, NEG_INF, le)
    m2 = jnp.max(le2, axis=-1, keepdims=True)
    i2 = jnp.min(jnp.where(le2 == m2, lane_f, big), axis=-1, keepdims=True)
    p1 = 1.0 / se
    p2 = jnp.exp(m2 - m1) / se
    den = p1 + p2
    comb_ref[...] = jnp.where(lane_f == i1, pg_top * (p1 / den),
                              jnp.where(lane_f == i2, pg_top * (p2 / den), 0.0))


def _outproj_router(o_pool, o_attn, o_mem, x, wo_bf, g2, wr, br, tm):
    n, d = x.shape
    assert n % tm == 0
    row = lambda wdt: pl.BlockSpec((tm, wdt), lambda i: (i, 0))
    full = lambda shp: pl.BlockSpec(shp, lambda i: (0, 0))
    return pl.pallas_call(
        functools.partial(_outproj_router_body, tm=tm),
        grid=(n // tm,),
        in_specs=[row(POOL_WIDTH), row(MOBA_WIDTH), row(MEM_WIDTH), row(d),
                  full((d, d)), full((1, d)), full((d, 2 * LANES)), full((1, LANES))],
        out_specs=[row(d), row(d), row(LANES)],
        out_shape=[jax.ShapeDtypeStruct((n, d), F32), jax.ShapeDtypeStruct((n, d), BF16),
                   jax.ShapeDtypeStruct((n, LANES), F32)],
        compiler_params=_cparams(("parallel",), 40),
        name="outproj_router",
    )(o_pool, o_attn, o_mem, x, wo_bf, g2.reshape(1, d), wr, br)


MOE_EXPERTS_PER_STEP = 2


def _moe_body(h2_ref, comb_ref, x1_hbm, wg_ref, wu_ref, wd_ref, y_ref, sem, *, tm):
    i = pl.program_id(0)
    s = pl.program_id(1)
    first = s == 0
    x1_copy = pltpu.make_async_copy(x1_hbm.at[pl.ds(pl.multiple_of(i * tm, tm), tm)], y_ref, sem)

    @pl.when(first)
    def _start():
        x1_copy.start()

    h = h2_ref[...]
    lane = lax.broadcasted_iota(jnp.int32, (tm, LANES), 1)
    for j in range(MOE_EXPERTS_PER_STEP):
        e = s * MOE_EXPERTS_PER_STEP + j
        a = jnp.dot(h, wg_ref[j].astype(BF16), preferred_element_type=F32)
        b = jnp.dot(h, wu_ref[j].astype(BF16), preferred_element_type=F32)
        ce = jnp.sum(jnp.where(lane == e, comb_ref[...], 0.0), axis=-1, keepdims=True)
        hid = ((a * jax.nn.sigmoid(a)) * b * ce).astype(BF16)
        if j == 0:
            @pl.when(first)
            def _landed():
                x1_copy.wait()

        y_ref[...] += jnp.dot(hid, wd_ref[j].astype(BF16), preferred_element_type=F32)


def _moe(h2, comb, x1, w_gate, w_up, w_down, tm):
    n, d = x1.shape
    assert n % tm == 0 and N_EXPERTS % MOE_EXPERTS_PER_STEP == 0
    eps = MOE_EXPERTS_PER_STEP
    return pl.pallas_call(
        functools.partial(_moe_body, tm=tm),
        grid=(n // tm, N_EXPERTS // eps),
        in_specs=[
            pl.BlockSpec((tm, d), lambda i, s: (i, 0)),
            pl.BlockSpec((tm, LANES), lambda i, s: (i, 0)),
            pl.BlockSpec(memory_space=pl.ANY),
            pl.BlockSpec((eps, d, D_EXPERT), lambda i, s: (s, 0, 0)),
            pl.BlockSpec((eps, d, D_EXPERT), lambda i, s: (s, 0, 0)),
            pl.BlockSpec((eps, D_EXPERT, d), lambda i, s: (s, 0, 0)),
        ],
        out_specs=pl.BlockSpec((tm, d), lambda i, s: (i, 0)),
        out_shape=jax.ShapeDtypeStruct((n, d), F32),
        scratch_shapes=[pltpu.SemaphoreType.DMA(())],
        compiler_params=_cparams(("parallel", "arbitrary"), 56),
        name="moe",
    )(h2, comb, x1, w_gate, w_up, w_down)


def _block_diag_ones(width, group):
    r = lax.broadcasted_iota(jnp.int32, (width, width), 0) // group
    c = lax.broadcasted_iota(jnp.int32, (width, width), 1) // group
    return (r == c).astype(BF16)


def _pick_tile(n, candidates):
    for c in candidates:
        if n % c == 0:
            return c
    return n


IN_SEGS = (
    (0, POOL_WIDTH, False, False, False, False),
    (POOL_WIDTH, MOBA_WIDTH, True, False, False, False),
    (POOL_WIDTH + MOBA_WIDTH, MOBA_WIDTH, True, True, True, True),
    (POOL_WIDTH + 2 * MOBA_WIDTH, MOBA_WIDTH, False, True, True, False),
    (POOL_WIDTH + 3 * MOBA_WIDTH, MEM_WIDTH, True, False, False, False),
)
IN_SEGS_SAMPLE = tuple((c0, wd, hn, False, False, False) for (c0, wd, hn, _, _, _) in IN_SEGS)
MEMKV_SEGS = ((0, MEM_WIDTH, True, False, False, False), (MEM_WIDTH, MEM_WIDTH, False, False, False, False))


def kernel(x_prompt, x_sample, mem_prompt, cache_k, cache_v, state_pool, cache_mem_k, cache_mem_v,
           page_table, norm1_gain, w_in, pool_w, pool_scale, moba_q_gain, moba_k_gain,
           mem_norm_gain, w_mem_kv, mem_q_gain, mem_k_gain, out_gain, w_out, norm2_gain,
           router_group_w, router_group_b, router_expert_w, router_expert_b, w_gate, w_up, w_down):
    bp, tp, d = x_prompt.shape
    bs, ts, _ = x_sample.shape
    n_p, n_s = bp * tp, bs * ts

    w_in_bf = w_in.astype(BF16)
    w_mem_bf = w_mem_kv.astype(BF16)
    wo_bf = w_out.astype(BF16)
    ones_bd = _block_diag_ones(MOBA_WIDTH, HEAD_DIM)
    head_gain_in = jnp.concatenate([
        jnp.ones((POOL_WIDTH,), F32), jnp.tile(moba_q_gain, MOBA_HEADS), jnp.tile(moba_k_gain, MOBA_HEADS),
        jnp.ones((MOBA_WIDTH,), F32), jnp.tile(mem_q_gain, MEM_HEADS)])
    head_gain_mem = jnp.concatenate([jnp.tile(mem_k_gain, MEM_HEADS), jnp.ones((MEM_WIDTH,), F32)])
    wbd = jnp.zeros((POOL_WIDTH, POOL_WIDTH), F32)
    for g in range(len(POOL_WINDOWS)):
        sl = slice(g * POOL_GROUP_WIDTH, (g + 1) * POOL_GROUP_WIDTH)
        wbd = wbd.at[sl, sl].set(pool_w[g])
    wbd_bf = wbd.astype(BF16)
    og_pool, og_attn, og_mem = (out_gain[:POOL_WIDTH], out_gain[POOL_WIDTH:POOL_WIDTH + MOBA_WIDTH],
                                out_gain[POOL_WIDTH + MOBA_WIDTH:])
    wr = jnp.zeros((d, LANES), F32)
    wr = wr.at[:, :N_EXPERTS].set(router_expert_w).at[:, N_EXPERTS:N_EXPERTS + MOE_GROUPS].set(router_group_w)
    br = jnp.zeros((1, LANES), F32)
    br = br.at[0, :N_EXPERTS].set(router_expert_b).at[0, N_EXPERTS:N_EXPERTS + MOE_GROUPS].set(router_group_b)
    wr_hi = wr.astype(BF16)
    wr = jnp.concatenate([wr_hi, (wr - wr_hi.astype(F32)).astype(BF16)], axis=1)
    wg = w_gate.reshape(N_EXPERTS, d, D_EXPERT)
    wu = w_up.reshape(N_EXPERTS, d, D_EXPERT)
    wd = w_down.reshape(N_EXPERTS, D_EXPERT, d)

    def tail(o_pool, o_attn, o_mem, x2d):
        n = x2d.shape[0]
        x1, h2, comb = _outproj_router(o_pool, o_attn, o_mem, x2d, wo_bf, norm2_gain, wr, br,
                                       _pick_tile(n, (512, 256)))
        return _moe(h2, comb, x1, wg, wu, wd, _pick_tile(n, (2048, 1024, 512, 256)))

    mem_k_p, mem_v_p = _norm_proj(mem_prompt.reshape(bp * MEM_TOKENS, d), mem_norm_gain, w_mem_bf,
                                  head_gain_mem, ones_bd, MEMKV_SEGS, _pick_tile(bp * MEM_TOKENS, (512, 256)))
    xp = x_prompt.reshape(n_p, d)
    u_p, q_p, kt_p, k_p_bf, kmean_p, vt_p, v_p_bf, qm_p = _norm_proj(
        xp, norm1_gain, w_in_bf, head_gain_in, ones_bd, IN_SEGS, _pick_tile(tp, (512, 256)), rows_per_batch=tp)
    o_pool_p = _pool(u_p.reshape(bp, tp, POOL_WIDTH), jnp.zeros((bp, 16, POOL_WIDTH), F32), wbd_bf,
                     pool_scale, og_pool, 0, 1, MOBA_BLOCK).reshape(n_p, POOL_WIDTH)
    nblk = tp // MOBA_BLOCK
    kmean_pad = jnp.pad(kmean_p.reshape(bp, nblk, MOBA_WIDTH), ((0, 0), (0, LANES - nblk), (0, 0)))
    o_attn_p = _moba_prompt(q_p, k_p_bf, v_p_bf, kmean_pad, og_attn, bp, tp)
    o_mem_p = _mem_attn(qm_p, mem_k_p.reshape(bp, MEM_TOKENS, MEM_WIDTH),
                        mem_v_p.reshape(bp, MEM_TOKENS, MEM_WIDTH), og_mem, bp, tp, _pick_tile(tp, (512, 256)))
    y_p = tail(o_pool_p, o_attn_p, o_mem_p, xp)

    past_len = page_table.shape[1] * PAGE_SIZE
    xs = x_sample.reshape(n_s, d)
    u_s, q_s, k_s, v_s, qm_s = _norm_proj(xs, norm1_gain, w_in_bf, head_gain_in, ones_bd, IN_SEGS_SAMPLE, n_s)
    u_s3 = u_s.reshape(bs, ts, POOL_WIDTH)
    buf16 = jnp.concatenate([jnp.zeros((bs, 1, POOL_WIDTH), F32), state_pool], axis=1)
    o_pool_s = _pool(u_s3, buf16, wbd_bf, pool_scale, og_pool, past_len, bs, ts).reshape(n_s, POOL_WIDTH)
    o_attn_s = _moba_sample(page_table, q_s, k_s, v_s,
                            cache_k.transpose(0, 2, 3, 1).reshape(-1, MOBA_WIDTH, PAGE_SIZE),
                            cache_v.transpose(0, 2, 3, 1).reshape(-1, MOBA_WIDTH, PAGE_SIZE),
                            og_attn, bs, ts)
    o_mem_s = _mem_attn(qm_s, cache_mem_k.reshape(bs, MEM_TOKENS, MEM_WIDTH),
                        cache_mem_v.reshape(bs, MEM_TOKENS, MEM_WIDTH), og_mem, bs, ts, ts)
    y_s = tail(o_pool_s, o_attn_s, o_mem_s, xs)

    pool_prompt = u_p.reshape(bp, tp, POOL_WIDTH)[:, tp - POOL_STATE:]
    pool_sample = jnp.concatenate([state_pool, u_s3], axis=1)[:, -POOL_STATE:]
    k_p = kt_p.reshape(bp, MOBA_HEADS, HEAD_DIM, tp).transpose(0, 3, 1, 2)
    v_p = vt_p.reshape(bp, MOBA_HEADS, HEAD_DIM, tp).transpose(0, 3, 1, 2)
    return (y_p.reshape(bp, tp, d), y_s.reshape(bs, ts, d), k_p, v_p,
            pool_prompt,
            mem_k_p.reshape(bp, MEM_TOKENS, MEM_HEADS, HEAD_DIM), mem_v_p.reshape(bp, MEM_TOKENS, MEM_HEADS, HEAD_DIM),
            k_s.reshape(bs, ts, MOBA_HEADS, HEAD_DIM), v_s.reshape(bs, ts, MOBA_HEADS, HEAD_DIM),
            pool_sample)
```

```python
import functools

import jax
import jax.numpy as jnp
from jax import lax
from jax.experimental import pallas as pl
from jax.experimental.pallas import tpu as pltpu

F32 = jnp.float32
BF16 = jnp.bfloat16

D_MODEL = 1024
HEAD_DIM = 64
POOL_WIDTH = 256
POOL_WINDOWS = (2, 4, 8, 16)
POOL_GROUP_WIDTH = 64
POOL_STATE = 15
MOBA_WIDTH = 512
MOBA_HEADS = 8
MOBA_BLOCK = 256
MOBA_TOPK = 3
MEM_WIDTH = 256
MEM_HEADS = 4
MEM_TOKENS = 256
PAGE_SIZE = 128
MOE_GROUPS = 4
EXPERTS_PER_GROUP = 8
N_EXPERTS = MOE_GROUPS * EXPERTS_PER_GROUP
D_EXPERT = 256
NORM_EPS = 1e-6

LANES = 128
HEAD_PAIR = 2 * HEAD_DIM
QK_SCALE = HEAD_DIM ** -0.5
NEG_INF = float("-inf")
MIB = 1024 * 1024
NT_DIMS = (((1,), (1,)), ((), ()))


def _cparams(semantics, vmem_mib):
    return pltpu.CompilerParams(dimension_semantics=semantics, vmem_limit_bytes=vmem_mib * MIB)


def _rms(y, eps=NORM_EPS):
    return y * lax.rsqrt(jnp.mean(y * y, axis=-1, keepdims=True) + eps)


def _norm_proj_body(x_ref, g_ref, w_ref, hg_ref, ones_ref, *outs, segs, tm):
    hb = (_rms(x_ref[...]) * g_ref[...]).astype(BF16)
    oi = 0
    for (c0, wd, headnorm, transposed, want_bf16, want_blockmean) in segs:
        y = jnp.dot(hb, w_ref[:, c0:c0 + wd], preferred_element_type=F32)
        if headnorm:
            sq = (y * y).astype(BF16)
            msq = jnp.dot(sq, ones_ref[:wd, :wd], preferred_element_type=F32) * (1.0 / HEAD_DIM)
            y = (y * lax.rsqrt(msq + NORM_EPS)) * hg_ref[:, c0:c0 + wd]
        if transposed:
            outs[oi][0] = y.T
        else:
            outs[oi][...] = y
        oi += 1
        if want_bf16:
            outs[oi][...] = y.astype(BF16)
            oi += 1
        if want_blockmean:
            for bi in range(tm // MOBA_BLOCK):
                outs[oi][bi] = jnp.mean(y[bi * MOBA_BLOCK:(bi + 1) * MOBA_BLOCK], axis=0, keepdims=True)
            oi += 1


def _norm_proj(x, gain, w_bf, head_gain, ones_bd, segs, tm, rows_per_batch=None):
    n, d = x.shape
    wtot = w_bf.shape[1]
    assert n % tm == 0
    out_shape, out_specs = [], []
    for (c0, wd, headnorm, transposed, want_bf16, want_blockmean) in segs:
        if transposed:
            assert rows_per_batch % tm == 0 and n % rows_per_batch == 0
            steps = rows_per_batch // tm
            out_shape.append(jax.ShapeDtypeStruct((n // rows_per_batch, wd, rows_per_batch), F32))
            out_specs.append(pl.BlockSpec((1, wd, tm), lambda i, steps=steps: (i // steps, 0, i % steps)))
        else:
            out_shape.append(jax.ShapeDtypeStruct((n, wd), F32))
            out_specs.append(pl.BlockSpec((tm, wd), lambda i: (i, 0)))
        if want_bf16:
            out_shape.append(jax.ShapeDtypeStruct((n, wd), BF16))
            out_specs.append(pl.BlockSpec((tm, wd), lambda i: (i, 0)))
        if want_blockmean:
            assert tm % MOBA_BLOCK == 0
            nb = tm // MOBA_BLOCK
            out_shape.append(jax.ShapeDtypeStruct((n // MOBA_BLOCK, 1, wd), F32))
            out_specs.append(pl.BlockSpec((nb, 1, wd), lambda i: (i, 0, 0)))
    return pl.pallas_call(
        functools.partial(_norm_proj_body, segs=segs, tm=tm),
        grid=(n // tm,),
        in_specs=[
            pl.BlockSpec((tm, d), lambda i: (i, 0)),
            pl.BlockSpec((1, d), lambda i: (0, 0)),
            pl.BlockSpec((d, wtot), lambda i: (0, 0)),
            pl.BlockSpec((1, wtot), lambda i: (0, 0)),
            pl.BlockSpec(ones_bd.shape, lambda i: (0, 0)),
        ],
        out_specs=out_specs,
        out_shape=out_shape,
        compiler_params=_cparams(("parallel",), 48),
        name="norm_proj",
    )(x, gain.reshape(1, d), w_bf, head_gain.reshape(1, wtot), ones_bd)


def _pool_windows(win, pos0):
    r = win.shape[0] - 16
    lane = lax.broadcasted_iota(jnp.int32, (r, LANES), 1)
    pos1 = pos0 + lax.broadcasted_iota(jnp.int32, (r, LANES), 0) + 1
    low = lane < POOL_GROUP_WIDTH
    a = win[:, :LANES]
    b = win[:, LANES:]
    a2 = a + pltpu.roll(a, 1, 0)
    a4 = a2 + pltpu.roll(a2, 2, 0)
    b2 = b + pltpu.roll(b, 1, 0)
    b4 = b2 + pltpu.roll(b2, 2, 0)
    b8 = b4 + pltpu.roll(b4, 4, 0)
    b16 = b8 + pltpu.roll(b8, 8, 0)
    cnt_a = jnp.minimum(jnp.where(low, POOL_WINDOWS[0], POOL_WINDOWS[1]), pos1).astype(F32)
    cnt_b = jnp.minimum(jnp.where(low, POOL_WINDOWS[2], POOL_WINDOWS[3]), pos1).astype(F32)
    pa = jnp.where(low, a2[16:], a4[16:]) / cnt_a - a[16:]
    pb = jnp.where(low, b8[16:], b16[16:]) / cnt_b - b[16:]
    return jnp.concatenate([pa, pb], axis=1)


def _pool_finish(pooled, wbd_ref, ps_ref, og_ref):
    mixed = jnp.dot(pooled.astype(BF16), wbd_ref[...], preferred_element_type=F32) * ps_ref[...]
    return _rms(mixed) * og_ref[...]


def _pool_body(u_ref, buf_ref, wbd_ref, ps_ref, og_ref, o_ref, ext_sc, *, bb, t, r, pos0):
    for bi in range(bb):
        ext_sc[bi, 0:16, :] = buf_ref[bi]
        ext_sc[bi, 16:, :] = u_ref[bi]
    if t == r:
        pooled = [_pool_windows(ext_sc[bi], pos0) for bi in range(bb)]
        out = _pool_finish(jnp.concatenate(pooled, axis=0), wbd_ref, ps_ref, og_ref)
        for bi in range(bb):
            o_ref[bi] = out[bi * r:(bi + 1) * r]
    else:
        assert bb == 1

        def chunk(c, carry):
            base = pl.multiple_of(c * r, r)
            pooled = _pool_windows(ext_sc[0, pl.ds(base, r + 16), :], pos0 + base)
            o_ref[0, pl.ds(base, r), :] = _pool_finish(pooled, wbd_ref, ps_ref, og_ref)
            return carry

        lax.fori_loop(0, t // r, chunk, 0)


def _pool(u, buf16, wbd_bf, pool_scale, og, pos0, bb, r):
    b, t, c = u.shape
    assert b % bb == 0 and t % r == 0
    return pl.pallas_call(
        functools.partial(_pool_body, bb=bb, t=t, r=r, pos0=pos0),
        grid=(b // bb,),
        in_specs=[
            pl.BlockSpec((bb, t, c), lambda i: (i, 0, 0)),
            pl.BlockSpec((bb, 16, c), lambda i: (i, 0, 0)),
            pl.BlockSpec((c, c), lambda i: (0, 0)),
            pl.BlockSpec((1, c), lambda i: (0, 0)),
            pl.BlockSpec((1, c), lambda i: (0, 0)),
        ],
        out_specs=pl.BlockSpec((bb, t, c), lambda i: (i, 0, 0)),
        out_shape=jax.ShapeDtypeStruct((b, t, c), F32),
        scratch_shapes=[pltpu.VMEM((bb, t + 16, c), F32)],
        compiler_params=_cparams(("parallel",), 40),
        name="pool",
    )(u, buf16, wbd_bf, pool_scale.reshape(1, c), og.reshape(1, c))


def _alibi_slope(h):
    return 2.0 ** (-8.0 * (h + 1) / MOBA_HEADS)


def _moba_prompt_body(q_ref, k_ref, v_ref, km_ref, og_ref, o_ref, m_sc, l_sc, acc_sc, sel_sc, qst_sc):
    i = pl.program_id(1)
    tq = MOBA_BLOCK
    n_pairs = MOBA_HEADS // 2
    lane = lax.broadcasted_iota(jnp.int32, (tq, LANES), 1)
    lane_f = lane.astype(F32)
    low = lane < HEAD_DIM
    row2 = lax.broadcasted_iota(jnp.int32, (tq, MOBA_BLOCK), 0)
    col2 = lax.broadcasted_iota(jnp.int32, (tq, MOBA_BLOCK), 1)
    causal = col2 <= row2
    colrow = lax.broadcasted_iota(jnp.int32, (1, MOBA_BLOCK), 1).astype(F32)

    def pair_cols(pr):
        return slice(pr * HEAD_PAIR, (pr + 1) * HEAD_PAIR)

    def scores(pr, j):
        start = pl.multiple_of(j * MOBA_BLOCK, MOBA_BLOCK)
        kj = k_ref[pl.ds(start, MOBA_BLOCK), pair_cols(pr)]
        s = lax.dot_general(qst_sc[pr], kj, NT_DIMS, preferred_element_type=F32)
        off = colrow + ((j - i) * MOBA_BLOCK).astype(F32)
        return [s[hh * tq:(hh + 1) * tq] + _alibi_slope(2 * pr + hh) * off for hh in range(2)]

    def pv(pr, j, ps):
        start = pl.multiple_of(j * MOBA_BLOCK, MOBA_BLOCK)
        vj = v_ref[pl.ds(start, MOBA_BLOCK), pair_cols(pr)]
        return jnp.dot(jnp.concatenate(ps, axis=0).astype(BF16), vj, preferred_element_type=F32)

    for pr in range(n_pairs):
        q_pair = q_ref[:, pair_cols(pr)]
        km_pair = km_ref[0, :, pair_cols(pr)]
        q_heads = [jnp.where(low, q_pair, 0.0), jnp.where(low, 0.0, q_pair)]

        for hh in range(2):
            gate = lax.dot_general(q_heads[hh], km_pair, NT_DIMS, precision=lax.Precision.HIGHEST,
                                   preferred_element_type=F32)
            g = jnp.where(lane < i, gate, NEG_INF)
            sel = jnp.zeros((tq, LANES), F32)
            for _ in range(MOBA_TOPK):
                mx = jnp.max(g, axis=-1, keepdims=True)
                idx = jnp.min(jnp.where(g == mx, lane_f, float(LANES)), axis=-1, keepdims=True)
                pick = (lane_f == idx) & (mx > NEG_INF)
                sel = jnp.where(pick, 1.0, sel)
                g = jnp.where(pick, NEG_INF, g)
            sel_sc[2 * pr + hh] = sel

        qst_sc[pr] = (jnp.concatenate(q_heads, axis=0) * QK_SCALE).astype(BF16)

        s_own = scores(pr, i)
        ps = []
        for hh in range(2):
            sh = jnp.where(causal, s_own[hh], NEG_INF)
            m = jnp.max(sh, axis=-1, keepdims=True)
            p = jnp.exp(sh - m)
            m_sc[2 * pr + hh] = jnp.broadcast_to(m, (tq, LANES))
            l_sc[2 * pr + hh] = jnp.broadcast_to(jnp.sum(p, axis=-1, keepdims=True), (tq, LANES))
            ps.append(p)
        acc_sc[pr] = pv(pr, i, ps)

    def past(j, carry):
        for pr in range(n_pairs):
            s_j = scores(pr, j)
            ps, alphas = [], []
            for hh in range(2):
                h = 2 * pr + hh
                selcol = jnp.sum(jnp.where(lane == j, sel_sc[h], 0.0), axis=-1, keepdims=True)
                sh = jnp.where(selcol > 0.0, s_j[hh], NEG_INF)
                m_prev = m_sc[h]
                m_new = jnp.maximum(m_prev, jnp.max(sh, axis=-1, keepdims=True))
                alpha = jnp.exp(m_prev - m_new)
                p = jnp.exp(sh - jnp.concatenate([m_new, m_new], axis=1))
                l_sc[h] = alpha * l_sc[h] + jnp.sum(p, axis=-1, keepdims=True)
                m_sc[h] = m_new
                ps.append(p)
                alphas.append(alpha)
            acc_sc[pr] = jnp.concatenate(alphas, axis=0) * acc_sc[pr] + pv(pr, j, ps)
        return carry

    lax.fori_loop(0, i, past, 0)

    outs = []
    for pr in range(n_pairs):
        acc = acc_sc[pr]
        outs.append(jnp.where(low, acc[:tq] / l_sc[2 * pr], acc[tq:] / l_sc[2 * pr + 1]))
    o_ref[...] = _rms(jnp.concatenate(outs, axis=1)) * og_ref[...]


def _moba_prompt(q, k_bf, v_bf, kmean_pad, og, b, t):
    n, w = q.shape
    nblk = t // MOBA_BLOCK
    tq = MOBA_BLOCK
    return pl.pallas_call(
        _moba_prompt_body,
        grid=(b, nblk),
        in_specs=[
            pl.BlockSpec((tq, w), lambda bi, i: (bi * nblk + i, 0)),
            pl.BlockSpec((t, w), lambda bi, i: (bi, 0)),
            pl.BlockSpec((t, w), lambda bi, i: (bi, 0)),
            pl.BlockSpec((1, LANES, w), lambda bi, i: (bi, 0, 0)),
            pl.BlockSpec((1, w), lambda bi, i: (0, 0)),
        ],
        out_specs=pl.BlockSpec((tq, w), lambda bi, i: (bi * nblk + i, 0)),
        out_shape=jax.ShapeDtypeStruct((n, w), F32),
        scratch_shapes=[
            pltpu.VMEM((MOBA_HEADS, tq, LANES), F32),
            pltpu.VMEM((MOBA_HEADS, tq, LANES), F32),
            pltpu.VMEM((MOBA_HEADS // 2, 2 * tq, LANES), F32),
            pltpu.VMEM((MOBA_HEADS, tq, LANES), F32),
            pltpu.VMEM((MOBA_HEADS // 2, 2 * tq, LANES), BF16),
        ],
        compiler_params=_cparams(("parallel", "arbitrary"), 40),
        name="moba_prompt",
    )(q, k_bf, v_bf, kmean_pad, og.reshape(1, w))


SAMPLE_BLOCKS_PER_STEP = 8
PAGES_PER_BLOCK = MOBA_BLOCK // PAGE_SIZE
PAGES_PER_STEP = SAMPLE_BLOCKS_PER_STEP * PAGES_PER_BLOCK


def _moba_sample_body(pt_ref, q_ref, kn_ref, vn_ref, slope_ref, og_ref, *rest, n_past_blocks, t_new):
    kp = rest[:PAGES_PER_STEP]
    vp = rest[PAGES_PER_STEP:2 * PAGES_PER_STEP]
    o_ref, o_sc, m_sc, l_sc, kmt_sc = rest[2 * PAGES_PER_STEP:]
    c = pl.program_id(1)
    rows = MOBA_HEADS * t_new
    row_h = lax.broadcasted_iota(jnp.int32, (rows, MOBA_WIDTH), 0) // t_new
    lane_h = lax.broadcasted_iota(jnp.int32, (rows, MOBA_WIDTH), 1) // HEAD_DIM
    bd = row_h == lane_h
    q = q_ref[...]
    qbd = jnp.where(bd, jnp.concatenate([q] * MOBA_HEADS, axis=0), 0.0)
    qbd_bf = (qbd * QK_SCALE).astype(BF16)
    slope = slope_ref[...]
    slope2 = jnp.concatenate([slope, slope], axis=1)
    colf = lax.broadcasted_iota(jnp.int32, (rows, MOBA_BLOCK), 1).astype(F32)

    @pl.when(c == 0)
    def _init():
        kmt_sc[...] = jnp.zeros(kmt_sc.shape, F32)

    kt_bf = []
    for jj in range(SAMPLE_BLOCKS_PER_STEP):
        kt_pages = [kp[PAGES_PER_BLOCK * jj + p][0] for p in range(PAGES_PER_BLOCK)]
        ksum = kt_pages[0]
        for kt in kt_pages[1:]:
            ksum = ksum + kt
        kmt_sc[c, :, jj:jj + 1] = jnp.sum(ksum, axis=-1, keepdims=True) * (1.0 / MOBA_BLOCK)
        kt_bf.append(jnp.concatenate(kt_pages, axis=1).astype(BF16))
    s_all = jnp.dot(qbd_bf, jnp.concatenate(kt_bf, axis=1), preferred_element_type=F32)

    for jj in range(SAMPLE_BLOCKS_PER_STEP):
        j = c * SAMPLE_BLOCKS_PER_STEP + jj
        s = s_all[:, jj * MOBA_BLOCK:(jj + 1) * MOBA_BLOCK] + slope2 * (
            colf + ((j - n_past_blocks) * MOBA_BLOCK).astype(F32))
        m = jnp.max(s, axis=-1, keepdims=True)
        p = jnp.exp(s - m)
        vt_bf = jnp.concatenate([vp[PAGES_PER_BLOCK * jj + q][0] for q in range(PAGES_PER_BLOCK)],
                                axis=1).astype(BF16)
        o = lax.dot_general(p.astype(BF16), vt_bf, NT_DIMS, preferred_element_type=F32)
        o_sc[j] = jnp.where(bd, o, 0.0)
        m_sc[j] = jnp.broadcast_to(m, (rows, LANES))
        l_sc[j] = jnp.broadcast_to(jnp.sum(p, axis=-1, keepdims=True), (rows, LANES))

    @pl.when(c == pl.num_programs(1) - 1)
    def _finish():
        tq = lax.broadcasted_iota(jnp.int32, (rows, LANES), 0) % t_new
        kn = kn_ref[...]
        vn = vn_ref[...]
        qs = qbd * QK_SCALE
        s_own = []
        m_run = jnp.full((rows, LANES), NEG_INF, F32)
        for cc in range(t_new):
            sc = jnp.sum(qs * kn[cc:cc + 1, :], axis=-1, keepdims=True) + slope * float(cc)
            sc = jnp.where(tq >= cc, sc, NEG_INF)
            s_own.append(sc)
            m_run = jnp.maximum(m_run, sc)

        blk = lax.broadcasted_iota(jnp.int32, (rows, LANES), 1)
        blk_f = blk.astype(F32)
        gates = jnp.zeros((rows, LANES), F32)
        for cs in range(n_past_blocks // SAMPLE_BLOCKS_PER_STEP):
            g_cs = jnp.dot(qbd, kmt_sc[cs], precision=lax.Precision.HIGHEST, preferred_element_type=F32)
            g_cs = jnp.where(blk < SAMPLE_BLOCKS_PER_STEP, g_cs, 0.0)
            gates = gates + (pltpu.roll(g_cs, cs * SAMPLE_BLOCKS_PER_STEP, 1) if cs else g_cs)
        g = jnp.where(blk < n_past_blocks, gates, NEG_INF)
        sel = jnp.zeros((rows, LANES), F32)
        for _ in range(MOBA_TOPK):
            mx = jnp.max(g, axis=-1, keepdims=True)
            idx = jnp.min(jnp.where(g == mx, blk_f, float(LANES)), axis=-1, keepdims=True)
            pick = (blk_f == idx) & (mx > NEG_INF)
            sel = jnp.where(pick, 1.0, sel)
            g = jnp.where(pick, NEG_INF, g)

        m_all = jnp.full((rows, LANES), NEG_INF, F32)
        l_all = jnp.zeros((rows, LANES), F32)
        for j in range(n_past_blocks):
            m_all = jnp.where(blk == j, m_sc[j], m_all)
            l_all = jnp.where(blk == j, l_sc[j], l_all)
        m_run = jnp.maximum(m_run, jnp.max(jnp.where(sel > 0.0, m_all, NEG_INF), axis=-1, keepdims=True))
        w_all = jnp.where(sel > 0.0, jnp.exp(m_all - m_run), 0.0)

        l_run = jnp.broadcast_to(jnp.sum(w_all * l_all, axis=-1, keepdims=True), (rows, LANES))
        o_run = jnp.zeros((rows, MOBA_WIDTH), F32)
        for cc in range(t_new):
            p = jnp.exp(s_own[cc] - m_run)
            l_run = l_run + p
            o_run = o_run + jnp.concatenate([p] * (MOBA_WIDTH // LANES), axis=1) * vn[cc:cc + 1, :]
        o_run = jnp.where(bd, o_run, 0.0)
        for j in range(n_past_blocks):
            o_run = o_run + w_all[:, j:j + 1] * o_sc[j]
        o_bd = o_run / jnp.concatenate([l_run] * (MOBA_WIDTH // LANES), axis=1)
        attn = o_bd[0:t_new]
        for h in range(1, MOBA_HEADS):
            attn = attn + o_bd[h * t_new:(h + 1) * t_new]
        o_ref[...] = _rms(attn) * og_ref[...]


def _moba_sample(page_table, q, k_new, v_new, cache_kt, cache_vt, og, b, t_new):
    n_pages = page_table.shape[1]
    n_past_blocks = n_pages // PAGES_PER_BLOCK
    assert n_past_blocks <= LANES
    assert n_past_blocks % SAMPLE_BLOCKS_PER_STEP == 0
    n_steps = n_past_blocks // SAMPLE_BLOCKS_PER_STEP
    rows = MOBA_HEADS * t_new
    w = MOBA_WIDTH
    slopes = jnp.exp2(-8.0 * jnp.arange(1, MOBA_HEADS + 1, dtype=F32) / MOBA_HEADS)
    slope_rows = jnp.broadcast_to(jnp.repeat(slopes, t_new)[:, None], (rows, LANES))

    def page_spec(p):
        return pl.BlockSpec((1, w, PAGE_SIZE), lambda bi, c, pt: (pt[bi, c * PAGES_PER_STEP + p], 0, 0))

    row_spec = pl.BlockSpec((t_new, w), lambda bi, c, pt: (bi, 0))
    grid_spec = pltpu.PrefetchScalarGridSpec(
        num_scalar_prefetch=1,
        grid=(b, n_steps),
        in_specs=[row_spec, row_spec, row_spec,
                  pl.BlockSpec((rows, LANES), lambda bi, c, pt: (0, 0)),
                  pl.BlockSpec((1, w), lambda bi, c, pt: (0, 0))]
                 + [page_spec(p) for p in range(PAGES_PER_STEP)]
                 + [page_spec(p) for p in range(PAGES_PER_STEP)],
        out_specs=row_spec,
        scratch_shapes=[
            pltpu.VMEM((n_past_blocks, rows, w), F32),
            pltpu.VMEM((n_past_blocks, rows, LANES), F32),
            pltpu.VMEM((n_past_blocks, rows, LANES), F32),
            pltpu.VMEM((n_steps, w, LANES), F32),
        ],
    )
    return pl.pallas_call(
        functools.partial(_moba_sample_body, n_past_blocks=n_past_blocks, t_new=t_new),
        grid_spec=grid_spec,
        out_shape=jax.ShapeDtypeStruct((b * t_new, w), F32),
        compiler_params=_cparams(("parallel", "arbitrary"), 48),
        name="moba_sample",
    )(page_table, q, k_new, v_new, slope_rows, og.reshape(1, w),
      *([cache_kt] * PAGES_PER_STEP), *([cache_vt] * PAGES_PER_STEP))


def _mem_attn_body(q_ref, mk_ref, mv_ref, og_ref, o_ref, *, tm):
    lane = lax.broadcasted_iota(jnp.int32, (tm, LANES), 1)
    low = lane < HEAD_DIM
    outs = []
    for pr in range(MEM_HEADS // 2):
        cs = slice(pr * HEAD_PAIR, (pr + 1) * HEAD_PAIR)
        q_pair = q_ref[:, cs]
        qst = (jnp.concatenate([jnp.where(low, q_pair, 0.0), jnp.where(low, 0.0, q_pair)], axis=0)
               * QK_SCALE).astype(BF16)
        s = lax.dot_general(qst, mk_ref[0, :, cs].astype(BF16), NT_DIMS, preferred_element_type=F32)
        p = jnp.exp(s - jnp.max(s, axis=-1, keepdims=True))
        l = jnp.sum(p, axis=-1, keepdims=True)
        o = jnp.dot(p.astype(BF16), mv_ref[0, :, cs].astype(BF16), preferred_element_type=F32) / l
        outs.append(jnp.where(low, o[:tm], o[tm:]))
    o_ref[...] = _rms(jnp.concatenate(outs, axis=1)) * og_ref[...]


def _mem_attn(qm, mem_k, mem_v, og, b, t, tm):
    n, w = qm.shape
    assert t % tm == 0
    steps = t // tm
    return pl.pallas_call(
        functools.partial(_mem_attn_body, tm=tm),
        grid=(b, steps),
        in_specs=[
            pl.BlockSpec((tm, w), lambda bi, i: (bi * steps + i, 0)),
            pl.BlockSpec((1, MEM_TOKENS, w), lambda bi, i: (bi, 0, 0)),
            pl.BlockSpec((1, MEM_TOKENS, w), lambda bi, i: (bi, 0, 0)),
            pl.BlockSpec((1, w), lambda bi, i: (0, 0)),
        ],
        out_specs=pl.BlockSpec((tm, w), lambda bi, i: (bi * steps + i, 0)),
        out_shape=jax.ShapeDtypeStruct((n, w), F32),
        compiler_params=_cparams(("parallel", "arbitrary"), 32),
        name="mem_attn",
    )(qm, mem_k, mem_v, og.reshape(1, w))


ROUTER_GROUP_LANE0 = N_EXPERTS


def _outproj_router_body(op_ref, oa_ref, om_ref, x_ref, wo_ref, g2_ref, wr_ref, br_ref,
                         x1_ref, h2_ref, comb_ref, *, tm):
    a0, a1 = POOL_WIDTH, POOL_WIDTH + MOBA_WIDTH
    y = x_ref[...]
    y = y + jnp.dot(op_ref[...].astype(BF16), wo_ref[0:a0, :], preferred_element_type=F32)
    y = y + jnp.dot(oa_ref[...].astype(BF16), wo_ref[a0:a1, :], preferred_element_type=F32)
    y = y + jnp.dot(om_ref[...].astype(BF16), wo_ref[a1:, :], preferred_element_type=F32)
    x1_ref[...] = y
    h2 = _rms(y) * g2_ref[...]
    h2_hi = h2.astype(BF16)
    h2_lo = (h2 - h2_hi.astype(F32)).astype(BF16)
    h2_ref[...] = h2_hi
    hw =jnp.dot(h2_hi, wr_ref[...], preferred_element_type=F32)
    lw = jnp.dot(h2_lo, wr_ref[:, :LANES], preferred_element_type=F32)
    logits = (hw[:, :LANES] + (hw[:, LANES:] + lw)) + br_ref[...]
    lane_f = lax.broadcasted_iota(jnp.int32, (tm, LANES), 1).astype(F32)
    big = float(LANES)
    g_lo = float(ROUTER_GROUP_LANE0)
    is_g = (lane_f >= g_lo) & (lane_f < g_lo + MOE_GROUPS)
    lg = jnp.where(is_g, logits, NEG_INF)
    mg = jnp.max(lg, axis=-1, keepdims=True)
    pg_top = 1.0 / jnp.sum(jnp.exp(lg - mg), axis=-1, keepdims=True)
    gidx = jnp.min(jnp.where(lg == mg, lane_f, big), axis=-1, keepdims=True) - g_lo
    e_lo = gidx * EXPERTS_PER_GROUP
    in_grp = (lane_f >= e_lo) & (lane_f < e_lo + EXPERTS_PER_GROUP)
    le = jnp.where(in_grp, logits, NEG_INF)
    m1 = jnp.max(le, axis=-1, keepdims=True)
    se = jnp.sum(jnp.exp(le - m1), axis=-1, keepdims=True)
    i1 = jnp.min(jnp.where(le == m1, lane_f, big), axis=-1, keepdims=True)
    le2 = jnp.where(lane_f == i1, NEG_INF, le)
    m2 = jnp.max(le2, axis=-1, keepdims=True)
    i2 = jnp.min(jnp.where(le2 == m2, lane_f, big), axis=-1, keepdims=True)
    p1 = 1.0 / se
    p2 = jnp.exp(m2 - m1) / se
    den = p1 + p2
    comb_ref[...] = jnp.where(lane_f == i1, pg_top * (p1 / den),
                              jnp.where(lane_f == i2, pg_top * (p2 / den), 0.0))


def _outproj_router(o_pool, o_attn, o_mem, x, wo_bf, g2, wr, br, tm):
    n, d = x.shape
    assert n % tm == 0
    row = lambda wdt: pl.BlockSpec((tm, wdt), lambda i: (i, 0))
    full = lambda shp: pl.BlockSpec(shp, lambda i: (0, 0))
    return pl.pallas_call(
        functools.partial(_outproj_router_body, tm=tm),
        grid=(n // tm,),
        in_specs=[row(POOL_WIDTH), row(MOBA_WIDTH), row(MEM_WIDTH), row(d),
                  full((d, d)), full((1, d)), full((d, 2 * LANES)), full((1, LANES))],
        out_specs=[row(d), row(d), row(LANES)],
        out_shape=[jax.ShapeDtypeStruct((n, d), F32), jax.ShapeDtypeStruct((n, d), BF16),
                   jax.ShapeDtypeStruct((n, LANES), F32)],
        compiler_params=_cparams(("parallel",), 40),
        name="outproj_router",
    )(o_pool, o_attn, o_mem, x, wo_bf, g2.reshape(1, d), wr, br)


MOE_EXPERTS_PER_STEP = 2


def _moe_body(h2_ref, comb_ref, x1_hbm, wg_ref, wu_ref, wd_ref, y_ref, sem, *, tm):
    i = pl.program_id(0)
    s = pl.program_id(1)
    first = s == 0
    x1_copy = pltpu.make_async_copy(x1_hbm.at[pl.ds(pl.multiple_of(i * tm, tm), tm)], y_ref, sem)

    @pl.when(first)
    def _start():
        x1_copy.start()

    h = h2_ref[...]
    lane = lax.broadcasted_iota(jnp.int32, (tm, LANES), 1)
    for j in range(MOE_EXPERTS_PER_STEP):
        e = s * MOE_EXPERTS_PER_STEP + j
        a = jnp.dot(h, wg_ref[j].astype(BF16), preferred_element_type=F32)
        b = jnp.dot(h, wu_ref[j].astype(BF16), preferred_element_type=F32)
        ce = jnp.sum(jnp.where(lane == e, comb_ref[...], 0.0), axis=-1, keepdims=True)
        hid = ((a * jax.nn.sigmoid(a)) * b * ce).astype(BF16)
        if j == 0:
            @pl.when(first)
            def _landed():
                x1_copy.wait()

        y_ref[...] += jnp.dot(hid, wd_ref[j].astype(BF16), preferred_element_type=F32)


def _moe(h2, comb, x1, w_gate, w_up, w_down, tm):
    n, d = x1.shape
    assert n % tm == 0 and N_EXPERTS % MOE_EXPERTS_PER_STEP == 0
    eps = MOE_EXPERTS_PER_STEP
    return pl.pallas_call(
        functools.partial(_moe_body, tm=tm),
        grid=(n // tm, N_EXPERTS // eps),
        in_specs=[
            pl.BlockSpec((tm, d), lambda i, s: (i, 0)),
            pl.BlockSpec((tm, LANES), lambda i, s: (i, 0)),
            pl.BlockSpec(memory_space=pl.ANY),
            pl.BlockSpec((eps, d, D_EXPERT), lambda i, s: (s, 0, 0)),
            pl.BlockSpec((eps, d, D_EXPERT), lambda i, s: (s, 0, 0)),
            pl.BlockSpec((eps, D_EXPERT, d), lambda i, s: (s, 0, 0)),
        ],
        out_specs=pl.BlockSpec((tm, d), lambda i, s: (i, 0)),
        out_shape=jax.ShapeDtypeStruct((n, d), F32),
        scratch_shapes=[pltpu.SemaphoreType.DMA(())],
        compiler_params=_cparams(("parallel", "arbitrary"), 56),
        name="moe",
    )(h2, comb, x1, w_gate, w_up, w_down)


def _block_diag_ones(width, group):
    r = lax.broadcasted_iota(jnp.int32, (width, width), 0) // group
    c = lax.broadcasted_iota(jnp.int32, (width, width), 1) // group
    return (r == c).astype(BF16)


def _pick_tile(n, candidates):
    for c in candidates:
        if n % c == 0:
            return c
    return n


IN_SEGS = (
    (0, POOL_WIDTH, False, False, False, False),
    (POOL_WIDTH, MOBA_WIDTH, True, False, False, False),
    (POOL_WIDTH + MOBA_WIDTH, MOBA_WIDTH, True, True, True, True),
    (POOL_WIDTH + 2 * MOBA_WIDTH, MOBA_WIDTH, False, True, True, False),
    (POOL_WIDTH + 3 * MOBA_WIDTH, MEM_WIDTH, True, False, False, False),
)
IN_SEGS_SAMPLE = tuple((c0, wd, hn, False, False, False) for (c0, wd, hn, _, _, _) in IN_SEGS)
MEMKV_SEGS = ((0, MEM_WIDTH, True, False, False, False), (MEM_WIDTH, MEM_WIDTH, False, False, False, False))


def kernel(x_prompt, x_sample, mem_prompt, cache_k, cache_v, state_pool, cache_mem_k, cache_mem_v,
           page_table, norm1_gain, w_in, pool_w, pool_scale, moba_q_gain, moba_k_gain,
           mem_norm_gain, w_mem_kv, mem_q_gain, mem_k_gain, out_gain, w_out, norm2_gain,
           router_group_w, router_group_b, router_expert_w, router_expert_b, w_gate, w_up, w_down):
    bp, tp, d = x_prompt.shape
    bs, ts, _ = x_sample.shape
    n_p, n_s = bp * tp, bs * ts

    w_in_bf = w_in.astype(BF16)
    w_mem_bf = w_mem_kv.astype(BF16)
    wo_bf = w_out.astype(BF16)
    ones_bd = _block_diag_ones(MOBA_WIDTH, HEAD_DIM)
    head_gain_in = jnp.concatenate([
        jnp.ones((POOL_WIDTH,), F32), jnp.tile(moba_q_gain, MOBA_HEADS), jnp.tile(moba_k_gain, MOBA_HEADS),
        jnp.ones((MOBA_WIDTH,), F32), jnp.tile(mem_q_gain, MEM_HEADS)])
    head_gain_mem = jnp.concatenate([jnp.tile(mem_k_gain, MEM_HEADS), jnp.ones((MEM_WIDTH,), F32)])
    wbd = jnp.zeros((POOL_WIDTH, POOL_WIDTH), F32)
    for g in range(len(POOL_WINDOWS)):
        sl = slice(g * POOL_GROUP_WIDTH, (g + 1) * POOL_GROUP_WIDTH)
        wbd = wbd.at[sl, sl].set(pool_w[g])
    wbd_bf = wbd.astype(BF16)
    og_pool, og_attn, og_mem = (out_gain[:POOL_WIDTH], out_gain[POOL_WIDTH:POOL_WIDTH + MOBA_WIDTH],
                                out_gain[POOL_WIDTH + MOBA_WIDTH:])
    wr = jnp.zeros((d, LANES), F32)
    wr = wr.at[:, :N_EXPERTS].set(router_expert_w).at[:, N_EXPERTS:N_EXPERTS + MOE_GROUPS].set(router_group_w)
    br = jnp.zeros((1, LANES), F32)
    br = br.at[0, :N_EXPERTS].set(router_expert_b).at[0, N_EXPERTS:N_EXPERTS + MOE_GROUPS].set(router_group_b)
    wr_hi = wr.astype(BF16)
    wr = jnp.concatenate([wr_hi, (wr - wr_hi.astype(F32)).astype(BF16)], axis=1)
    wg = w_gate.reshape(N_EXPERTS, d, D_EXPERT)
    wu = w_up.reshape(N_EXPERTS, d, D_EXPERT)
    wd = w_down.reshape(N_EXPERTS, D_EXPERT, d)

    def tail(o_pool, o_attn, o_mem, x2d):
        n = x2d.shape[0]
        x1, h2, comb = _outproj_router(o_pool, o_attn, o_mem, x2d, wo_bf, norm2_gain, wr, br,
                                       _pick_tile(n, (512, 256)))
        return _moe(h2, comb, x1, wg, wu, wd, _pick_tile(n, (2048, 1024, 512, 256)))

    mem_k_p, mem_v_p = _norm_proj(mem_prompt.reshape(bp * MEM_TOKENS, d), mem_norm_gain, w_mem_bf,
                                  head_gain_mem, ones_bd, MEMKV_SEGS, _pick_tile(bp * MEM_TOKENS, (512, 256)))
    xp = x_prompt.reshape(n_p, d)
    u_p, q_p, kt_p, k_p_bf, kmean_p, vt_p, v_p_bf, qm_p = _norm_proj(
        xp, norm1_gain, w_in_bf, head_gain_in, ones_bd, IN_SEGS, _pick_tile(tp, (512, 256)), rows_per_batch=tp)
    o_pool_p = _pool(u_p.reshape(bp, tp, POOL_WIDTH), jnp.zeros((bp, 16, POOL_WIDTH), F32), wbd_bf,
                     pool_scale, og_pool, 0, 1, MOBA_BLOCK).reshape(n_p, POOL_WIDTH)
    nblk = tp // MOBA_BLOCK
    kmean_pad = jnp.pad(kmean_p.reshape(bp, nblk, MOBA_WIDTH), ((0, 0), (0, LANES - nblk), (0, 0)))
    o_attn_p = _moba_prompt(q_p, k_p_bf, v_p_bf, kmean_pad, og_attn, bp, tp)
    o_mem_p = _mem_attn(qm_p, mem_k_p.reshape(bp, MEM_TOKENS, MEM_WIDTH),
                        mem_v_p.reshape(bp, MEM_TOKENS, MEM_WIDTH), og_mem, bp, tp, _pick_tile(tp, (512, 256)))
    y_p = tail(o_pool_p, o_attn_p, o_mem_p, xp)

    past_len = page_table.shape[1] * PAGE_SIZE
    xs = x_sample.reshape(n_s, d)
    u_s, q_s, k_s, v_s, qm_s = _norm_proj(xs, norm1_gain, w_in_bf, head_gain_in, ones_bd, IN_SEGS_SAMPLE, n_s)
    u_s3 = u_s.reshape(bs, ts, POOL_WIDTH)
    buf16 = jnp.concatenate([jnp.zeros((bs, 1, POOL_WIDTH), F32), state_pool], axis=1)
    o_pool_s = _pool(u_s3, buf16, wbd_bf, pool_scale, og_pool, past_len, bs, ts).reshape(n_s, POOL_WIDTH)
    o_attn_s = _moba_sample(page_table, q_s, k_s, v_s,
                            cache_k.transpose(0, 2, 3, 1).reshape(-1, MOBA_WIDTH, PAGE_SIZE),
                            cache_v.transpose(0, 2, 3, 1).reshape(-1, MOBA_WIDTH, PAGE_SIZE),
                            og_attn, bs, ts)
    o_mem_s = _mem_attn(qm_s, cache_mem_k.reshape(bs, MEM_TOKENS, MEM_WIDTH),
                        cache_mem_v.reshape(bs, MEM_TOKENS, MEM_WIDTH), og_mem, bs, ts, ts)
    y_s = tail(o_pool_s, o_attn_s, o_mem_s, xs)

    pool_prompt = u_p.reshape(bp, tp, POOL_WIDTH)[:, tp - POOL_STATE:]
    pool_sample = jnp.concatenate([state_pool, u_s3], axis=1)[:, -POOL_STATE:]
    k_p = kt_p.reshape(bp, MOBA_HEADS, HEAD_DIM, tp).transpose(0, 3, 1, 2)
    v_p = vt_p.reshape(bp, MOBA_HEADS, HEAD_DIM, tp).transpose(0, 3, 1, 2)
    return (y_p.reshape(bp, tp, d), y_s.reshape(bs, ts, d), k_p, v_p,
            pool_prompt,
            mem_k_p.reshape(bp, MEM_TOKENS, MEM_HEADS, HEAD_DIM), mem_v_p.reshape(bp, MEM_TOKENS, MEM_HEADS, HEAD_DIM),
            k_s.reshape(bs, ts, MOBA_HEADS, HEAD_DIM), v_s.reshape(bs, ts, MOBA_HEADS, HEAD_DIM),
            pool_sample)
```

```python
import functools

import jax
import jax.numpy as jnp
from jax import lax
from jax.experimental import pallas as pl
from jax.experimental.pallas import tpu as pltpu

F32 = jnp.float32
BF16 = jnp.bfloat16

D_MODEL = 1024
HEAD_DIM = 64
POOL_WIDTH = 256
POOL_WINDOWS = (2, 4, 8, 16)
POOL_GROUP_WIDTH = 64
POOL_STATE = 15
MOBA_WIDTH = 512
MOBA_HEADS = 8
MOBA_BLOCK = 256
MOBA_TOPK = 3
MEM_WIDTH = 256
MEM_HEADS = 4
MEM_TOKENS = 256
PAGE_SIZE = 128
MOE_GROUPS = 4
EXPERTS_PER_GROUP = 8
N_EXPERTS = MOE_GROUPS * EXPERTS_PER_GROUP
D_EXPERT = 256
NORM_EPS = 1e-6

LANES = 128
HEAD_PAIR = 2 * HEAD_DIM
QK_SCALE = HEAD_DIM ** -0.5
NEG_INF = float("-inf")
MIB = 1024 * 1024
NT_DIMS = (((1,), (1,)), ((), ()))


def _cparams(semantics, vmem_mib):
    return pltpu.CompilerParams(dimension_semantics=semantics, vmem_limit_bytes=vmem_mib * MIB)


def _rms(y, eps=NORM_EPS):
    return y * lax.rsqrt(jnp.mean(y * y, axis=-1, keepdims=True) + eps)


def _norm_proj_body(x_ref, g_ref, w_ref, hg_ref, ones_ref, *outs, segs, tm):
    hb = (_rms(x_ref[...]) * g_ref[...]).astype(BF16)
    oi = 0
    for (c0, wd, headnorm, transposed, want_bf16, want_blockmean) in segs:
        y = jnp.dot(hb, w_ref[:, c0:c0 + wd], preferred_element_type=F32)
        if headnorm:
            sq = (y * y).astype(BF16)
            msq = jnp.dot(sq, ones_ref[:wd, :wd], preferred_element_type=F32) * (1.0 / HEAD_DIM)
            y = (y * lax.rsqrt(msq + NORM_EPS)) * hg_ref[:, c0:c0 + wd]
        if transposed:
            outs[oi][0] = y.T
        else:
            outs[oi][...] = y
        oi += 1
        if want_bf16:
            outs[oi][...] = y.astype(BF16)
            oi += 1
        if want_blockmean:
            for bi in range(tm // MOBA_BLOCK):
                outs[oi][bi] = jnp.mean(y[bi * MOBA_BLOCK:(bi + 1) * MOBA_BLOCK], axis=0, keepdims=True)
            oi += 1


def _norm_proj(x, gain, w_bf, head_gain, ones_bd, segs, tm, rows_per_batch=None):
    n, d = x.shape
    wtot = w_bf.shape[1]
    assert n % tm == 0
    out_shape, out_specs = [], []
    for (c0, wd, headnorm, transposed, want_bf16, want_blockmean) in segs:
        if transposed:
            assert rows_per_batch % tm == 0 and n % rows_per_batch == 0
            steps = rows_per_batch // tm
            out_shape.append(jax.ShapeDtypeStruct((n // rows_per_batch, wd, rows_per_batch), F32))
            out_specs.append(pl.BlockSpec((1, wd, tm), lambda i, steps=steps: (i // steps, 0, i % steps)))
        else:
            out_shape.append(jax.ShapeDtypeStruct((n, wd), F32))
            out_specs.append(pl.BlockSpec((tm, wd), lambda i: (i, 0)))
        if want_bf16:
            out_shape.append(jax.ShapeDtypeStruct((n, wd), BF16))
            out_specs.append(pl.BlockSpec((tm, wd), lambda i: (i, 0)))
        if want_blockmean:
            assert tm % MOBA_BLOCK == 0
            nb = tm // MOBA_BLOCK
            out_shape.append(jax.ShapeDtypeStruct((n // MOBA_BLOCK, 1, wd), F32))
            out_specs.append(pl.BlockSpec((nb, 1, wd), lambda i: (i, 0, 0)))
    return pl.pallas_call(
        functools.partial(_norm_proj_body, segs=segs, tm=tm),
        grid=(n // tm,),
        in_specs=[
            pl.BlockSpec((tm, d), lambda i: (i, 0)),
            pl.BlockSpec((1, d), lambda i: (0, 0)),
            pl.BlockSpec((d, wtot), lambda i: (0, 0)),
            pl.BlockSpec((1, wtot), lambda i: (0, 0)),
            pl.BlockSpec(ones_bd.shape, lambda i: (0, 0)),
        ],
        out_specs=out_specs,
        out_shape=out_shape,
        compiler_params=_cparams(("parallel",), 48),
        name="norm_proj",
    )(x, gain.reshape(1, d), w_bf, head_gain.reshape(1, wtot), ones_bd)


def _pool_windows(win, pos0):
    r = win.shape[0] - 16
    lane = lax.broadcasted_iota(jnp.int32, (r, LANES), 1)
    pos1 = pos0 + lax.broadcasted_iota(jnp.int32, (r, LANES), 0) + 1
    low = lane < POOL_GROUP_WIDTH
    a = win[:, :LANES]
    b = win[:, LANES:]
    a2 = a + pltpu.roll(a, 1, 0)
    a4 = a2 + pltpu.roll(a2, 2, 0)
    b2 = b + pltpu.roll(b, 1, 0)
    b4 = b2 + pltpu.roll(b2, 2, 0)
    b8 = b4 + pltpu.roll(b4, 4, 0)
    b16 = b8 + pltpu.roll(b8, 8, 0)
    cnt_a = jnp.minimum(jnp.where(low, POOL_WINDOWS[0], POOL_WINDOWS[1]), pos1).astype(F32)
    cnt_b = jnp.minimum(jnp.where(low, POOL_WINDOWS[2], POOL_WINDOWS[3]), pos1).astype(F32)
    pa = jnp.where(low, a2[16:], a4[16:]) / cnt_a - a[16:]
    pb = jnp.where(low, b8[16:], b16[16:]) / cnt_b - b[16:]
    return jnp.concatenate([pa, pb], axis=1)


def _pool_finish(pooled, wbd_ref, ps_ref, og_ref):
    mixed = jnp.dot(pooled.astype(BF16), wbd_ref[...], preferred_element_type=F32) * ps_ref[...]
    return _rms(mixed) * og_ref[...]


def _pool_body(u_ref, buf_ref, wbd_ref, ps_ref, og_ref, o_ref, ext_sc, *, bb, t, r, pos0):
    for bi in range(bb):
        ext_sc[bi, 0:16, :] = buf_ref[bi]
        ext_sc[bi, 16:, :] = u_ref[bi]
    if t == r:
        pooled = [_pool_windows(ext_sc[bi], pos0) for bi in range(bb)]
        out = _pool_finish(jnp.concatenate(pooled, axis=0), wbd_ref, ps_ref, og_ref)
        for bi in range(bb):
            o_ref[bi] = out[bi * r:(bi + 1) * r]
    else:
        assert bb == 1

        def chunk(c, carry):
            base = pl.multiple_of(c * r, r)
            pooled = _pool_windows(ext_sc[0, pl.ds(base, r + 16), :], pos0 + base)
            o_ref[0, pl.ds(base, r), :] = _pool_finish(pooled, wbd_ref, ps_ref, og_ref)
            return carry

        lax.fori_loop(0, t // r, chunk, 0)


def _pool(u, buf16, wbd_bf, pool_scale, og, pos0, bb, r):
    b, t, c = u.shape
    assert b % bb == 0 and t % r == 0
    return pl.pallas_call(
        functools.partial(_pool_body, bb=bb, t=t, r=r, pos0=pos0),
        grid=(b // bb,),
        in_specs=[
            pl.BlockSpec((bb, t, c), lambda i: (i, 0, 0)),
            pl.BlockSpec((bb, 16, c), lambda i: (i, 0, 0)),
            pl.BlockSpec((c, c), lambda i: (0, 0)),
            pl.BlockSpec((1, c), lambda i: (0, 0)),
            pl.BlockSpec((1, c), lambda i: (0, 0)),
        ],
        out_specs=pl.BlockSpec((bb, t, c), lambda i: (i, 0, 0)),
        out_shape=jax.ShapeDtypeStruct((b, t, c), F32),
        scratch_shapes=[pltpu.VMEM((bb, t + 16, c), F32)],
        compiler_params=_cparams(("parallel",), 40),
        name="pool",
    )(u, buf16, wbd_bf, pool_scale.reshape(1, c), og.reshape(1, c))


def _alibi_slope(h):
    return 2.0 ** (-8.0 * (h + 1) / MOBA_HEADS)


def _moba_prompt_body(q_ref, k_ref, v_ref, km_ref, og_ref, o_ref, m_sc, l_sc, acc_sc, sel_sc, qst_sc):
    i = pl.program_id(1)
    tq = MOBA_BLOCK
    n_pairs = MOBA_HEADS // 2
    lane = lax.broadcasted_iota(jnp.int32, (tq, LANES), 1)
    lane_f = lane.astype(F32)
    low = lane < HEAD_DIM
    row2 = lax.broadcasted_iota(jnp.int32, (tq, MOBA_BLOCK), 0)
    col2 = lax.broadcasted_iota(jnp.int32, (tq, MOBA_BLOCK), 1)
    causal = col2 <= row2
    colrow = lax.broadcasted_iota(jnp.int32, (1, MOBA_BLOCK), 1).astype(F32)

    def pair_cols(pr):
        return slice(pr * HEAD_PAIR, (pr + 1) * HEAD_PAIR)

    def scores(pr, j):
        start = pl.multiple_of(j * MOBA_BLOCK, MOBA_BLOCK)
        kj = k_ref[pl.ds(start, MOBA_BLOCK), pair_cols(pr)]
        s = lax.dot_general(qst_sc[pr], kj, NT_DIMS, preferred_element_type=F32)
        off = colrow + ((j - i) * MOBA_BLOCK).astype(F32)
        return [s[hh * tq:(hh + 1) * tq] + _alibi_slope(2 * pr + hh) * off for hh in range(2)]

    def pv(pr, j, ps):
        start = pl.multiple_of(j * MOBA_BLOCK, MOBA_BLOCK)
        vj = v_ref[pl.ds(start, MOBA_BLOCK), pair_cols(pr)]
        return jnp.dot(jnp.concatenate(ps, axis=0).astype(BF16), vj, preferred_element_type=F32)

    for pr in range(n_pairs):
        q_pair = q_ref[:, pair_cols(pr)]
        km_pair = km_ref[0, :, pair_cols(pr)]
        q_heads = [jnp.where(low, q_pair, 0.0), jnp.where(low, 0.0, q_pair)]

        for hh in range(2):
            gate = lax.dot_general(q_heads[hh], km_pair, NT_DIMS, precision=lax.Precision.HIGHEST,
                                   preferred_element_type=F32)
            g = jnp.where(lane < i, gate, NEG_INF)
            sel = jnp.zeros((tq, LANES), F32)
            for _ in range(MOBA_TOPK):
                mx = jnp.max(g, axis=-1, keepdims=True)
                idx = jnp.min(jnp.where(g == mx, lane_f, float(LANES)), axis=-1, keepdims=True)
                pick = (lane_f == idx) & (mx > NEG_INF)
                sel = jnp.where(pick, 1.0, sel)
                g = jnp.where(pick, NEG_INF, g)
            sel_sc[2 * pr + hh] = sel

        qst_sc[pr] = (jnp.concatenate(q_heads, axis=0) * QK_SCALE).astype(BF16)

        s_own = scores(pr, i)
        ps = []
        for hh in range(2):
            sh = jnp.where(causal, s_own[hh], NEG_INF)
            m = jnp.max(sh, axis=-1, keepdims=True)
            p = jnp.exp(sh - m)
            m_sc[2 * pr + hh] = jnp.broadcast_to(m, (tq, LANES))
            l_sc[2 * pr + hh] = jnp.broadcast_to(jnp.sum(p, axis=-1, keepdims=True), (tq, LANES))
            ps.append(p)
        acc_sc[pr] = pv(pr, i, ps)

    def past(j, carry):
        for pr in range(n_pairs):
            s_j = scores(pr, j)
            ps, alphas = [], []
            for hh in range(2):
                h = 2 * pr + hh
                selcol = jnp.sum(jnp.where(lane == j, sel_sc[h], 0.0), axis=-1, keepdims=True)
                sh = jnp.where(selcol > 0.0, s_j[hh], NEG_INF)
                m_prev = m_sc[h]
                m_new = jnp.maximum(m_prev, jnp.max(sh, axis=-1, keepdims=True))
                alpha = jnp.exp(m_prev - m_new)
                p = jnp.exp(sh - jnp.concatenate([m_new, m_new], axis=1))
                l_sc[h] = alpha * l_sc[h] + jnp.sum(p, axis=-1, keepdims=True)
                m_sc[h] = m_new
                ps.append(p)
                alphas.append(alpha)
            acc_sc[pr] = jnp.concatenate(alphas, axis=0) * acc_sc[pr] + pv(pr, j, ps)
        return carry

    lax.fori_loop(0, i, past, 0)

    outs = []
    for pr in range(n_pairs):
        acc = acc_sc[pr]
        outs.append(jnp.where(low, acc[:tq] / l_sc[2 * pr], acc[tq:] / l_sc[2 * pr + 1]))
    o_ref[...] = _rms(jnp.concatenate(outs, axis=1)) * og_ref[...]


def _moba_prompt(q, k_bf, v_bf, kmean_pad, og, b, t):
    n, w = q.shape
    nblk = t // MOBA_BLOCK
    tq = MOBA_BLOCK
    return pl.pallas_call(
        _moba_prompt_body,
        grid=(b, nblk),
        in_specs=[
            pl.BlockSpec((tq, w), lambda bi, i: (bi * nblk + i, 0)),
            pl.BlockSpec((t, w), lambda bi, i: (bi, 0)),
            pl.BlockSpec((t, w), lambda bi, i: (bi, 0)),
            pl.BlockSpec((1, LANES, w), lambda bi, i: (bi, 0, 0)),
            pl.BlockSpec((1, w), lambda bi, i: (0, 0)),
        ],
        out_specs=pl.BlockSpec((tq, w), lambda bi, i: (bi * nblk + i, 0)),
        out_shape=jax.ShapeDtypeStruct((n, w), F32),
        scratch_shapes=[
            pltpu.VMEM((MOBA_HEADS, tq, LANES), F32),
            pltpu.VMEM((MOBA_HEADS, tq, LANES), F32),
            pltpu.VMEM((MOBA_HEADS // 2, 2 * tq, LANES), F32),
            pltpu.VMEM((MOBA_HEADS, tq, LANES), F32),
            pltpu.VMEM((MOBA_HEADS // 2, 2 * tq, LANES), BF16),
        ],
        compiler_params=_cparams(("parallel", "arbitrary"), 40),
        name="moba_prompt",
    )(q, k_bf, v_bf, kmean_pad, og.reshape(1, w))


SAMPLE_BLOCKS_PER_STEP = 8
PAGES_PER_BLOCK = MOBA_BLOCK // PAGE_SIZE
PAGES_PER_STEP = SAMPLE_BLOCKS_PER_STEP * PAGES_PER_BLOCK


def _moba_sample_body(pt_ref, q_ref, kn_ref, vn_ref, slope_ref, og_ref, *rest, n_past_blocks, t_new):
    kp = rest[:PAGES_PER_STEP]
    vp = rest[PAGES_PER_STEP:2 * PAGES_PER_STEP]
    o_ref, o_sc, m_sc, l_sc, kmt_sc = rest[2 * PAGES_PER_STEP:]
    c = pl.program_id(1)
    rows = MOBA_HEADS * t_new
    row_h = lax.broadcasted_iota(jnp.int32, (rows, MOBA_WIDTH), 0) // t_new
    lane_h = lax.broadcasted_iota(jnp.int32, (rows, MOBA_WIDTH), 1) // HEAD_DIM
    bd = row_h == lane_h
    q = q_ref[...]
    qbd = jnp.where(bd, jnp.concatenate([q] * MOBA_HEADS, axis=0), 0.0)
    qbd_bf = (qbd * QK_SCALE).astype(BF16)
    slope = slope_ref[...]
    slope2 = jnp.concatenate([slope, slope], axis=1)
    colf = lax.broadcasted_iota(jnp.int32, (rows, MOBA_BLOCK), 1).astype(F32)

    @pl.when(c == 0)
    def _init():
        kmt_sc[...] = jnp.zeros(kmt_sc.shape, F32)

    kt_bf = []
    for jj in range(SAMPLE_BLOCKS_PER_STEP):
        kt_pages = [kp[PAGES_PER_BLOCK * jj + p][0] for p in range(PAGES_PER_BLOCK)]
        ksum = kt_pages[0]
        for kt in kt_pages[1:]:
            ksum = ksum + kt
        kmt_sc[c, :, jj:jj + 1] = jnp.sum(ksum, axis=-1, keepdims=True) * (1.0 / MOBA_BLOCK)
        kt_bf.append(jnp.concatenate(kt_pages, axis=1).astype(BF16))
    s_all = jnp.dot(qbd_bf, jnp.concatenate(kt_bf, axis=1), preferred_element_type=F32)

    for jj in range(SAMPLE_BLOCKS_PER_STEP):
        j = c * SAMPLE_BLOCKS_PER_STEP + jj
        s = s_all[:, jj * MOBA_BLOCK:(jj + 1) * MOBA_BLOCK] + slope2 * (
            colf + ((j - n_past_blocks) * MOBA_BLOCK).astype(F32))
        m = jnp.max(s, axis=-1, keepdims=True)
        p = jnp.exp(s - m)
        vt_bf = jnp.concatenate([vp[PAGES_PER_BLOCK * jj + q][0] for q in range(PAGES_PER_BLOCK)],
                                axis=1).astype(BF16)
        o = lax.dot_general(p.astype(BF16), vt_bf, NT_DIMS, preferred_element_type=F32)
        o_sc[j] = jnp.where(bd, o, 0.0)
        m_sc[j] = jnp.broadcast_to(m, (rows, LANES))
        l_sc[j] = jnp.broadcast_to(jnp.sum(p, axis=-1, keepdims=True), (rows, LANES))

    @pl.when(c == pl.num_programs(1) - 1)
    def _finish():
        tq = lax.broadcasted_iota(jnp.int32, (rows, LANES), 0) % t_new
        kn = kn_ref[...]
        vn = vn_ref[...]
        qs = qbd * QK_SCALE
        s_own = []
        m_run = jnp.full((rows, LANES), NEG_INF, F32)
        for cc in range(t_new):
            sc = jnp.sum(qs * kn[cc:cc + 1, :], axis=-1, keepdims=True) + slope * float(cc)
            sc = jnp.where(tq >= cc, sc, NEG_INF)
            s_own.append(sc)
            m_run = jnp.maximum(m_run, sc)

        blk = lax.broadcasted_iota(jnp.int32, (rows, LANES), 1)
        blk_f = blk.astype(F32)
        gates = jnp.zeros((rows, LANES), F32)
        for cs in range(n_past_blocks // SAMPLE_BLOCKS_PER_STEP):
            g_cs = jnp.dot(qbd, kmt_sc[cs], precision=lax.Precision.HIGHEST, preferred_element_type=F32)
            g_cs = jnp.where(blk < SAMPLE_BLOCKS_PER_STEP, g_cs, 0.0)
            gates = gates + (pltpu.roll(g_cs, cs * SAMPLE_BLOCKS_PER_STEP, 1) if cs else g_cs)
        g = jnp.where(blk < n_past_blocks, gates, NEG_INF)
        sel = jnp.zeros((rows, LANES), F32)
        for _ in range(MOBA_TOPK):
            mx = jnp.max(g, axis=-1, keepdims=True)
            idx = jnp.min(jnp.where(g == mx, blk_f, float(LANES)), axis=-1, keepdims=True)
            pick = (blk_f == idx) & (mx > NEG_INF)
            sel = jnp.where(pick, 1.0, sel)
            g = jnp.where(pick, NEG_INF, g)

        m_all = jnp.full((rows, LANES), NEG_INF, F32)
        l_all = jnp.zeros((rows, LANES), F32)
        for j in range(n_past_blocks):
            m_all = jnp.where(blk == j, m_sc[j], m_all)
            l_all = jnp.where(blk == j, l_sc[j], l_all)
        m_run = jnp.maximum(m_run, jnp.max(jnp.where(sel > 0.0, m_all, NEG_INF), axis=-1, keepdims=True))
        w_all = jnp.where(sel > 0.0, jnp.exp(m_all - m_run), 0.0)

        l_run = jnp.broadcast_to(jnp.sum(w_all * l_all, axis=-1, keepdims=True), (rows, LANES))
        o_run = jnp.zeros((rows, MOBA_WIDTH), F32)
        for cc in range(t_new):
            p = jnp.exp(s_own[cc] - m_run)
            l_run = l_run + p
            o_run = o_run + jnp.concatenate([p] * (MOBA_WIDTH // LANES), axis=1) * vn[cc:cc + 1, :]
        o_run = jnp.where(bd, o_run, 0.0)
        for j in range(n_past_blocks):
            o_run = o_run + w_all[:, j:j + 1] * o_sc[j]
        o_bd = o_run / jnp.concatenate([l_run] * (MOBA_WIDTH // LANES), axis=1)
        attn = o_bd[0:t_new]
        for h in range(1, MOBA_HEADS):
            attn = attn + o_bd[h * t_new:(h + 1) * t_new]
        o_ref[...] = _rms(attn) * og_ref[...]


def _moba_sample(page_table, q, k_new, v_new, cache_kt, cache_vt, og, b, t_new):
    n_pages = page_table.shape[1]
    n_past_blocks = n_pages // PAGES_PER_BLOCK
    assert n_past_blocks <= LANES
    assert n_past_blocks % SAMPLE_BLOCKS_PER_STEP == 0
    n_steps = n_past_blocks // SAMPLE_BLOCKS_PER_STEP
    rows = MOBA_HEADS * t_new
    w = MOBA_WIDTH
    slopes = jnp.exp2(-8.0 * jnp.arange(1, MOBA_HEADS + 1, dtype=F32) / MOBA_HEADS)
    slope_rows = jnp.broadcast_to(jnp.repeat(slopes, t_new)[:, None], (rows, LANES))

    def page_spec(p):
        return pl.BlockSpec((1, w, PAGE_SIZE), lambda bi, c, pt: (pt[bi, c * PAGES_PER_STEP + p], 0, 0))

    row_spec = pl.BlockSpec((t_new, w), lambda bi, c, pt: (bi, 0))
    grid_spec = pltpu.PrefetchScalarGridSpec(
        num_scalar_prefetch=1,
        grid=(b, n_steps),
        in_specs=[row_spec, row_spec, row_spec,
                  pl.BlockSpec((rows, LANES), lambda bi, c, pt: (0, 0)),
                  pl.BlockSpec((1, w), lambda bi, c, pt: (0, 0))]
                 + [page_spec(p) for p in range(PAGES_PER_STEP)]
                 + [page_spec(p) for p in range(PAGES_PER_STEP)],
        out_specs=row_spec,
        scratch_shapes=[
            pltpu.VMEM((n_past_blocks, rows, w), F32),
            pltpu.VMEM((n_past_blocks, rows, LANES), F32),
            pltpu.VMEM((n_past_blocks, rows, LANES), F32),
            pltpu.VMEM((n_steps, w, LANES), F32),
        ],
    )
    return pl.pallas_call(
        functools.partial(_moba_sample_body, n_past_blocks=n_past_blocks, t_new=t_new),
        grid_spec=grid_spec,
        out_shape=jax.ShapeDtypeStruct((b * t_new, w), F32),
        compiler_params=_cparams(("parallel", "arbitrary"), 48),
        name="moba_sample",
    )(page_table, q, k_new, v_new, slope_rows, og.reshape(1, w),
      *([cache_kt] * PAGES_PER_STEP), *([cache_vt] * PAGES_PER_STEP))


def _mem_attn_body(q_ref, mk_ref, mv_ref, og_ref, o_ref, *, tm):
    lane = lax.broadcasted_iota(jnp.int32, (tm, LANES), 1)
    low = lane < HEAD_DIM
    outs = []
    for pr in range(MEM_HEADS // 2):
        cs = slice(pr * HEAD_PAIR, (pr + 1) * HEAD_PAIR)
        q_pair = q_ref[:, cs]
        qst = (jnp.concatenate([jnp.where(low, q_pair, 0.0), jnp.where(low, 0.0, q_pair)], axis=0)
               * QK_SCALE).astype(BF16)
        s = lax.dot_general(qst, mk_ref[0, :, cs].astype(BF16), NT_DIMS, preferred_element_type=F32)
        p = jnp.exp(s - jnp.max(s, axis=-1, keepdims=True))
        l = jnp.sum(p, axis=-1, keepdims=True)
        o = jnp.dot(p.astype(BF16), mv_ref[0, :, cs].astype(BF16), preferred_element_type=F32) / l
        outs.append(jnp.where(low, o[:tm], o[tm:]))
    o_ref[...] = _rms(jnp.concatenate(outs, axis=1)) * og_ref[...]


def _mem_attn(qm, mem_k, mem_v, og, b, t, tm):
    n, w = qm.shape
    assert t % tm == 0
    steps = t // tm
    return pl.pallas_call(
        functools.partial(_mem_attn_body, tm=tm),
        grid=(b, steps),
        in_specs=[
            pl.BlockSpec((tm, w), lambda bi, i: (bi * steps + i, 0)),
            pl.BlockSpec((1, MEM_TOKENS, w), lambda bi, i: (bi, 0, 0)),
            pl.BlockSpec((1, MEM_TOKENS, w), lambda bi, i: (bi, 0, 0)),
            pl.BlockSpec((1, w), lambda bi, i: (0, 0)),
        ],
        out_specs=pl.BlockSpec((tm, w), lambda bi, i: (bi * steps + i, 0)),
        out_shape=jax.ShapeDtypeStruct((n, w), F32),
        compiler_params=_cparams(("parallel", "arbitrary"), 32),
        name="mem_attn",
    )(qm, mem_k, mem_v, og.reshape(1, w))


ROUTER_GROUP_LANE0 = N_EXPERTS
ROUTER_GID_LANE = 64


def _outproj_router_body(op_ref, oa_ref, om_ref, x_ref, wo_ref, g2_ref, wr_ref, br_ref,
                         x1_ref, h2_ref, comb_ref, gcnt_ref, *, tm):
    a0, a1 = POOL_WIDTH, POOL_WIDTH + MOBA_WIDTH
    y = x_ref[...]
    y = y + jnp.dot(op_ref[...].astype(BF16), wo_ref[0:a0, :], preferred_element_type=F32)
    y = y + jnp.dot(oa_ref[...].astype(BF16), wo_ref[a0:a1, :], preferred_element_type=F32)
    y = y + jnp.dot(om_ref[...].astype(BF16), wo_ref[a1:, :], preferred_element_type=F32)
    x1_ref[...] = y
    h2 = _rms(y) * g2_ref[...]
    h2_hi = h2.astype(BF16)
    h2_lo = (h2 - h2_hi.astype(F32)).astype(BF16)
    h2_ref[...] = h2_hi
    hw =jnp.dot(h2_hi, wr_ref[...], preferred_element_type=F32)
    lw = jnp.dot(h2_lo, wr_ref[:, :LANES], preferred_element_type=F32)
    logits = (hw[:, :LANES] + (hw[:, LANES:] + lw)) + br_ref[...]
    lane_f = lax.broadcasted_iota(jnp.int32, (tm, LANES), 1).astype(F32)
    big = float(LANES)
    g_lo = float(ROUTER_GROUP_LANE0)
    is_g = (lane_f >= g_lo) & (lane_f < g_lo + MOE_GROUPS)
    lg = jnp.where(is_g, logits, NEG_INF)
    mg = jnp.max(lg, axis=-1, keepdims=True)
    pg_top = 1.0 / jnp.sum(jnp.exp(lg - mg), axis=-1, keepdims=True)
    gidx = jnp.min(jnp.where(lg == mg, lane_f, big), axis=-1, keepdims=True) - g_lo
    e_lo = gidx * EXPERTS_PER_GROUP
    in_grp = (lane_f >= e_lo) & (lane_f < e_lo + EXPERTS_PER_GROUP)
    le = jnp.where(in_grp, logits, NEG_INF)
    m1 = jnp.max(le, axis=-1, keepdims=True)
    se = jnp.sum(jnp.exp(le - m1), axis=-1, keepdims=True)
    i1 = jnp.min(jnp.where(le == m1, lane_f, big), axis=-1, keepdims=True)
    le2 = jnp.where(lane_f == i1, NEG_INF, le)
    m2 = jnp.max(le2, axis=-1, keepdims=True)
    i2 = jnp.min(jnp.where(le2 == m2, lane_f, big), axis=-1, keepdims=True)
    p1 = 1.0 / se
    p2 = jnp.exp(m2 - m1) / se
    den = p1 + p2
    comb_ref[...] = jnp.where(lane_f == i1, pg_top * (p1 / den),
                              jnp.where(lane_f == i2, pg_top * (p2 / den),
                                        jnp.where(lane_f == float(ROUTER_GID_LANE), gidx, 0.0)))
    gcnt_ref[0] = jnp.sum(jnp.where(lane_f == gidx, 1.0, 0.0), axis=0, keepdims=True)


def _outproj_router(o_pool, o_attn, o_mem, x, wo_bf, g2, wr, br, tm):
    n, d = x.shape
    assert n % tm == 0
    row = lambda wdt: pl.BlockSpec((tm, wdt), lambda i: (i, 0))
    full = lambda shp: pl.BlockSpec(shp, lambda i: (0, 0))
    return pl.pallas_call(
        functools.partial(_outproj_router_body, tm=tm),
        grid=(n // tm,),
        in_specs=[row(POOL_WIDTH), row(MOBA_WIDTH), row(MEM_WIDTH), row(d),
                  full((d, d)), full((1, d)), full((d, 2 * LANES)), full((1, LANES))],
        out_specs=[row(d), row(d), row(LANES), pl.BlockSpec((1, 1, LANES), lambda i: (i, 0, 0))],
        out_shape=[jax.ShapeDtypeStruct((n, d), F32), jax.ShapeDtypeStruct((n, d), BF16),
                   jax.ShapeDtypeStruct((n, LANES), F32), jax.ShapeDtypeStruct((n // tm, 1, LANES), F32)],
        compiler_params=_cparams(("parallel",), 40),
        name="outproj_router",
    )(o_pool, o_attn, o_mem, x, wo_bf, g2.reshape(1, d), wr, br)


MOE_EXPERTS_PER_STEP = 4
MOE_RUN_ALIGN = 16
MOE_SORT_PAD = LANES


def _moe_window(tm):
    return max(LANES, tm // MOE_GROUPS + 64)


def _moe_body(cnt_ref, h2_ref, comb_ref, x1_hbm, wg_ref, wu_ref, wd_ref, y_ref,
              xs_sc, cs_sc, ys_sc, p_sc, pt_sc, sem, *, tm, win):
    i = pl.program_id(0)
    s = pl.program_id(1)
    eps = MOE_EXPERTS_PER_STEP
    span = tm + MOE_SORT_PAD
    g = s // (EXPERTS_PER_GROUP // eps)
    starts, nxt = [], jnp.int32(0)
    for gg in range(MOE_GROUPS):
        starts.append(nxt)
        nxt = nxt + (cnt_ref[i * MOE_GROUPS + gg] + (MOE_RUN_ALIGN - 1)) // MOE_RUN_ALIGN * MOE_RUN_ALIGN
    x1_copy = pltpu.make_async_copy(x1_hbm.at[pl.ds(pl.multiple_of(i * tm, tm), tm)], y_ref, sem)

    @pl.when(s == 0)
    def _sort():
        x1_copy.start()
        comb = comb_ref[...]
        lane = lax.broadcasted_iota(jnp.int32, (tm, LANES), 1)
        lane_f = lane.astype(F32)
        gid = jnp.sum(jnp.where(lane == ROUTER_GID_LANE, comb, 0.0), axis=-1, keepdims=True)
        onehot = jnp.where(lane_f == gid, 1.0, 0.0)
        earlier_tok = lax.broadcasted_iota(jnp.int32, (tm, tm), 1) < lax.broadcasted_iota(jnp.int32, (tm, tm), 0)
        earlier = jnp.dot(jnp.where(earlier_tok, 1.0, 0.0).astype(BF16), onehot.astype(BF16),
                          preferred_element_type=F32)
        start_vec = jnp.zeros((tm, LANES), F32)
        for gg in range(MOE_GROUPS):
            start_vec = jnp.where(lane == gg, starts[gg].astype(F32), start_vec)
        dest = jnp.sum(onehot * (earlier + start_vec), axis=-1, keepdims=True)
        hi = jnp.floor(dest * (1.0 / 32.0))
        lo = dest - 32.0 * hi
        dest_cols = jnp.where(lane == 0, hi, jnp.where(lane == 1, lo, 0.0)).astype(BF16)
        r16 = lax.broadcasted_iota(jnp.int32, (16, LANES), 0)
        l16 = lax.broadcasted_iota(jnp.int32, (16, LANES), 1)
        pick = jnp.where((r16 == 0) & (l16 == 0), 32.0, jnp.where((r16 == 0) & (l16 == 1), 1.0, 0.0)).astype(BF16)
        dest_row = lax.dot_general(pick, dest_cols, NT_DIMS, preferred_element_type=F32)[0:1, :]
        p = jnp.where(lax.broadcasted_iota(jnp.int32, (span, tm), 0).astype(F32) == dest_row, 1.0, 0.0).astype(BF16)
        p_sc[...] = p
        pt_sc[...] = jnp.where(lax.broadcasted_iota(jnp.int32, (tm, span), 1).astype(F32) == dest,
                               1.0, 0.0).astype(BF16)
        xs_sc[0:span, :] = jnp.dot(p, h2_ref[...], preferred_element_type=F32).astype(BF16)
        xs_sc[span:, :] = jnp.zeros((win, D_MODEL), BF16)
        c1 = comb.astype(BF16)
        r1 = comb - c1.astype(F32)
        c2 = r1.astype(BF16)
        c3 = (r1 - c2.astype(F32)).astype(BF16)
        cs_sc[0:span, :] = (jnp.dot(p, c1, preferred_element_type=F32) + jnp.dot(p, c2, preferred_element_type=F32)
                            + jnp.dot(p, c3, preferred_element_type=F32))
        cs_sc[span:, :] = jnp.zeros((win, LANES), F32)
        ys_sc[...] = jnp.zeros(ys_sc.shape, F32)

    run_start = starts[0]
    for gg in range(1, MOE_GROUPS):
        run_start = jnp.where(g == gg, starts[gg], run_start)
    run_len = cnt_ref[i * MOE_GROUPS + g]
    lane_w = lax.broadcasted_iota(jnp.int32, (win, LANES), 1)

    def window(w, carry):
        r0 = pl.multiple_of(run_start + w * win, MOE_RUN_ALIGN)
        rows = xs_sc[pl.ds(r0, win), :]
        comb_rows = cs_sc[pl.ds(r0, win), :]
        acc = None
        for j in range(eps):
            e = s * eps + j
            a = jnp.dot(rows, wg_ref[j], preferred_element_type=F32)
            b = jnp.dot(rows, wu_ref[j], preferred_element_type=F32)
            ce = jnp.sum(jnp.where(lane_w == e, comb_rows, 0.0), axis=-1, keepdims=True)
            hid = ((a * jax.nn.sigmoid(a)) * b * ce).astype(BF16)
            c = jnp.dot(hid, wd_ref[j], preferred_element_type=F32)
            acc = c if acc is None else acc + c
        ys_sc[pl.ds(r0, win), :] += acc
        return carry

    lax.fori_loop(0, (run_len + win - 1) // win, window, 0)

    @pl.when(s == pl.num_programs(1) - 1)
    def _unsort():
        ys = ys_sc[0:span, :]
        ys_hi = ys.astype(BF16)
        ys_lo = (ys - ys_hi.astype(F32)).astype(BF16)
        moe = (jnp.dot(pt_sc[...], ys_hi, preferred_element_type=F32)
               + jnp.dot(pt_sc[...], ys_lo, preferred_element_type=F32))
        x1_copy.wait()
        y_ref[...] = y_ref[...] + moe


def _moe(tile_cnt, h2, comb, x1, wg_bf, wu_bf, wd_bf, tm):
    n, d = x1.shape
    eps = MOE_EXPERTS_PER_STEP
    assert n % tm == 0 and EXPERTS_PER_GROUP % eps == 0 and MOE_GROUPS * (MOE_RUN_ALIGN - 1) <= MOE_SORT_PAD
    win = _moe_window(tm)
    assert win % MOE_RUN_ALIGN == 0
    span = tm + MOE_SORT_PAD
    grid_spec = pltpu.PrefetchScalarGridSpec(
        num_scalar_prefetch=1,
        grid=(n // tm, N_EXPERTS // eps),
        in_specs=[
            pl.BlockSpec((tm, d), lambda i, s, cnt: (i, 0)),
            pl.BlockSpec((tm, LANES), lambda i, s, cnt: (i, 0)),
            pl.BlockSpec(memory_space=pl.ANY),
            pl.BlockSpec((eps, d, D_EXPERT), lambda i, s, cnt: (s, 0, 0)),
            pl.BlockSpec((eps, d, D_EXPERT), lambda i, s, cnt: (s, 0, 0)),
            pl.BlockSpec((eps, D_EXPERT, d), lambda i, s, cnt: (s, 0, 0)),
        ],
        out_specs=pl.BlockSpec((tm, d), lambda i, s, cnt: (i, 0)),
        scratch_shapes=[
            pltpu.VMEM((span + win, d), BF16),
            pltpu.VMEM((span + win, LANES), F32),
            pltpu.VMEM((span + win, d), F32),
            pltpu.VMEM((span, tm), BF16),
            pltpu.VMEM((tm, span), BF16),
            pltpu.SemaphoreType.DMA(()),
        ],
    )
    return pl.pallas_call(
        functools.partial(_moe_body, tm=tm, win=win),
        grid_spec=grid_spec,
        out_shape=jax.ShapeDtypeStruct((n, d), F32),
        compiler_params=_cparams(("parallel", "arbitrary"), 56),
        name="moe",
    )(tile_cnt, h2, comb, x1, wg_bf, wu_bf, wd_bf)


def _block_diag_ones(width, group):
    r = lax.broadcasted_iota(jnp.int32, (width, width), 0) // group
    c = lax.broadcasted_iota(jnp.int32, (width, width), 1) // group
    return (r == c).astype(BF16)


def _pick_tile(n, candidates):
    for c in candidates:
        if n % c == 0:
            return c
    return n


IN_SEGS = (
    (0, POOL_WIDTH, False, False, False, False),
    (POOL_WIDTH, MOBA_WIDTH, True, False, False, False),
    (POOL_WIDTH + MOBA_WIDTH, MOBA_WIDTH, True, True, True, True),
    (POOL_WIDTH + 2 * MOBA_WIDTH, MOBA_WIDTH, False, True, True, False),
    (POOL_WIDTH + 3 * MOBA_WIDTH, MEM_WIDTH, True, False, False, False),
)
IN_SEGS_SAMPLE = tuple((c0, wd, hn, False, False, False) for (c0, wd, hn, _, _, _) in IN_SEGS)
MEMKV_SEGS = ((0, MEM_WIDTH, True, False, False, False), (MEM_WIDTH, MEM_WIDTH, False, False, False, False))


def kernel(x_prompt, x_sample, mem_prompt, cache_k, cache_v, state_pool, cache_mem_k, cache_mem_v,
           page_table, norm1_gain, w_in, pool_w, pool_scale, moba_q_gain, moba_k_gain,
           mem_norm_gain, w_mem_kv, mem_q_gain, mem_k_gain, out_gain, w_out, norm2_gain,
           router_group_w, router_group_b, router_expert_w, router_expert_b, w_gate, w_up, w_down):
    bp, tp, d = x_prompt.shape
    bs, ts, _ = x_sample.shape
    n_p, n_s = bp * tp, bs * ts

    w_in_bf = w_in.astype(BF16)
    w_mem_bf = w_mem_kv.astype(BF16)
    wo_bf = w_out.astype(BF16)
    ones_bd = _block_diag_ones(MOBA_WIDTH, HEAD_DIM)
    head_gain_in = jnp.concatenate([
        jnp.ones((POOL_WIDTH,), F32), jnp.tile(moba_q_gain, MOBA_HEADS), jnp.tile(moba_k_gain, MOBA_HEADS),
        jnp.ones((MOBA_WIDTH,), F32), jnp.tile(mem_q_gain, MEM_HEADS)])
    head_gain_mem = jnp.concatenate([jnp.tile(mem_k_gain, MEM_HEADS), jnp.ones((MEM_WIDTH,), F32)])
    wbd = jnp.zeros((POOL_WIDTH, POOL_WIDTH), F32)
    for g in range(len(POOL_WINDOWS)):
        sl = slice(g * POOL_GROUP_WIDTH, (g + 1) * POOL_GROUP_WIDTH)
        wbd = wbd.at[sl, sl].set(pool_w[g])
    wbd_bf = wbd.astype(BF16)
    og_pool, og_attn, og_mem = (out_gain[:POOL_WIDTH], out_gain[POOL_WIDTH:POOL_WIDTH + MOBA_WIDTH],
                                out_gain[POOL_WIDTH + MOBA_WIDTH:])
    wr = jnp.zeros((d, LANES), F32)
    wr = wr.at[:, :N_EXPERTS].set(router_expert_w).at[:, N_EXPERTS:N_EXPERTS + MOE_GROUPS].set(router_group_w)
    br = jnp.zeros((1, LANES), F32)
    br = br.at[0, :N_EXPERTS].set(router_expert_b).at[0, N_EXPERTS:N_EXPERTS + MOE_GROUPS].set(router_group_b)
    wr_hi = wr.astype(BF16)
    wr = jnp.concatenate([wr_hi, (wr - wr_hi.astype(F32)).astype(BF16)], axis=1)
    wg = w_gate.reshape(N_EXPERTS, d, D_EXPERT).astype(BF16)
    wu = w_up.reshape(N_EXPERTS, d, D_EXPERT).astype(BF16)
    wd = w_down.reshape(N_EXPERTS, D_EXPERT, d).astype(BF16)

    def tail(o_pool, o_attn, o_mem, x2d):
        n = x2d.shape[0]
        tm_r = _pick_tile(n, (512, 256))
        x1, h2, comb, gcnt = _outproj_router(o_pool, o_attn, o_mem, x2d, wo_bf, norm2_gain, wr, br, tm_r)
        tm = _pick_tile(n, (1024, 512, 256))
        tile_cnt = gcnt[:, 0, :MOE_GROUPS].reshape(n // tm, tm // tm_r, MOE_GROUPS).sum(axis=1)
        return _moe(tile_cnt.astype(jnp.int32).reshape(-1), h2, comb, x1, wg, wu, wd, tm)

    mem_k_p, mem_v_p = _norm_proj(mem_prompt.reshape(bp * MEM_TOKENS, d), mem_norm_gain, w_mem_bf,
                                  head_gain_mem, ones_bd, MEMKV_SEGS, _pick_tile(bp * MEM_TOKENS, (512, 256)))
    xp = x_prompt.reshape(n_p, d)
    u_p, q_p, kt_p, k_p_bf, kmean_p, vt_p, v_p_bf, qm_p = _norm_proj(
        xp, norm1_gain, w_in_bf, head_gain_in, ones_bd, IN_SEGS, _pick_tile(tp, (512, 256)), rows_per_batch=tp)
    o_pool_p = _pool(u_p.reshape(bp, tp, POOL_WIDTH), jnp.zeros((bp, 16, POOL_WIDTH), F32), wbd_bf,
                     pool_scale, og_pool, 0, 1, MOBA_BLOCK).reshape(n_p, POOL_WIDTH)
    nblk = tp // MOBA_BLOCK
    kmean_pad = jnp.pad(kmean_p.reshape(bp, nblk, MOBA_WIDTH), ((0, 0), (0, LANES - nblk), (0, 0)))
    o_attn_p = _moba_prompt(q_p, k_p_bf, v_p_bf, kmean_pad, og_attn, bp, tp)
    o_mem_p = _mem_attn(qm_p, mem_k_p.reshape(bp, MEM_TOKENS, MEM_WIDTH),
                        mem_v_p.reshape(bp, MEM_TOKENS, MEM_WIDTH), og_mem, bp, tp, _pick_tile(tp, (512, 256)))
    y_p = tail(o_pool_p, o_attn_p, o_mem_p, xp)

    past_len = page_table.shape[1] * PAGE_SIZE
    xs = x_sample.reshape(n_s, d)
    u_s, q_s, k_s, v_s, qm_s = _norm_proj(xs, norm1_gain, w_in_bf, head_gain_in, ones_bd, IN_SEGS_SAMPLE, n_s)
    u_s3 = u_s.reshape(bs, ts, POOL_WIDTH)
    buf16 = jnp.concatenate([jnp.zeros((bs, 1, POOL_WIDTH), F32), state_pool], axis=1)
    o_pool_s = _pool(u_s3, buf16, wbd_bf, pool_scale, og_pool, past_len, bs, ts).reshape(n_s, POOL_WIDTH)
    o_attn_s = _moba_sample(page_table, q_s, k_s, v_s,
                            cache_k.transpose(0, 2, 3, 1).reshape(-1, MOBA_WIDTH, PAGE_SIZE),
                            cache_v.transpose(0, 2, 3, 1).reshape(-1, MOBA_WIDTH, PAGE_SIZE),
                            og_attn, bs, ts)
    o_mem_s = _mem_attn(qm_s, cache_mem_k.reshape(bs, MEM_TOKENS, MEM_WIDTH),
                        cache_mem_v.reshape(bs, MEM_TOKENS, MEM_WIDTH), og_mem, bs, ts, ts)
    y_s = tail(o_pool_s, o_attn_s, o_mem_s, xs)

    pool_prompt = u_p.reshape(bp, tp, POOL_WIDTH)[:, tp - POOL_STATE:]
    pool_sample = jnp.concatenate([state_pool, u_s3], axis=1)[:, -POOL_STATE:]
    k_p = kt_p.reshape(bp, MOBA_HEADS, HEAD_DIM, tp).transpose(0, 3, 1, 2)
    v_p = vt_p.reshape(bp, MOBA_HEADS, HEAD_DIM, tp).transpose(0, 3, 1, 2)
    return (y_p.reshape(bp, tp, d), y_s.reshape(bs, ts, d), k_p, v_p,
            pool_prompt,
            mem_k_p.reshape(bp, MEM_TOKENS, MEM_HEADS, HEAD_DIM), mem_v_p.reshape(bp, MEM_TOKENS, MEM_HEADS, HEAD_DIM),
            k_s.reshape(bs, ts, MOBA_HEADS, HEAD_DIM), v_s.reshape(bs, ts, MOBA_HEADS, HEAD_DIM),
            pool_sample)
```

```python
import functools

import jax
import jax.numpy as jnp
from jax import lax
from jax.experimental import pallas as pl
from jax.experimental.pallas import tpu as pltpu

F32 = jnp.float32
BF16 = jnp.bfloat16

D_MODEL = 1024
HEAD_DIM = 64
POOL_WIDTH = 256
POOL_WINDOWS = (2, 4, 8, 16)
POOL_GROUP_WIDTH = 64
POOL_STATE = 15
MOBA_WIDTH = 512
MOBA_HEADS = 8
MOBA_BLOCK = 256
MOBA_TOPK = 3
MEM_WIDTH = 256
MEM_HEADS = 4
MEM_TOKENS = 256
PAGE_SIZE = 128
MOE_GROUPS = 4
EXPERTS_PER_GROUP = 8
N_EXPERTS = MOE_GROUPS * EXPERTS_PER_GROUP
D_EXPERT = 256
NORM_EPS = 1e-6

LANES = 128
HEAD_PAIR = 2 * HEAD_DIM
QK_SCALE = HEAD_DIM ** -0.5
NEG_INF = float("-inf")
MIB = 1024 * 1024
NT_DIMS = (((1,), (1,)), ((), ()))


def _cparams(semantics, vmem_mib):
    return pltpu.CompilerParams(dimension_semantics=semantics, vmem_limit_bytes=vmem_mib * MIB)


def _rms(y, eps=NORM_EPS):
    return y * lax.rsqrt(jnp.mean(y * y, axis=-1, keepdims=True) + eps)


def _norm_proj_body(x_ref, g_ref, w_ref, hg_ref, ones_ref, *outs, segs, tm):
    hb = (_rms(x_ref[...]) * g_ref[...]).astype(BF16)
    oi = 0
    for (c0, wd, headnorm, transposed, want_bf16, want_blockmean) in segs:
        y = jnp.dot(hb, w_ref[:, c0:c0 + wd], preferred_element_type=F32)
        if headnorm:
            sq = (y * y).astype(BF16)
            msq = jnp.dot(sq, ones_ref[:wd, :wd], preferred_element_type=F32) * (1.0 / HEAD_DIM)
            y = (y * lax.rsqrt(msq + NORM_EPS)) * hg_ref[:, c0:c0 + wd]
        if transposed:
            outs[oi][0] = y.T
        else:
            outs[oi][...] = y
        oi += 1
        if want_bf16:
            outs[oi][...] = y.astype(BF16)
            oi += 1
        if want_blockmean:
            for bi in range(tm // MOBA_BLOCK):
                outs[oi][bi] = jnp.mean(y[bi * MOBA_BLOCK:(bi + 1) * MOBA_BLOCK], axis=0, keepdims=True)
            oi += 1


def _norm_proj(x, gain, w_bf, head_gain, ones_bd, segs, tm, rows_per_batch=None):
    n, d = x.shape
    wtot = w_bf.shape[1]
    assert n % tm == 0
    out_shape, out_specs = [], []
    for (c0, wd, headnorm, transposed, want_bf16, want_blockmean) in segs:
        if transposed:
            assert rows_per_batch % tm == 0 and n % rows_per_batch == 0
            steps = rows_per_batch // tm
            out_shape.append(jax.ShapeDtypeStruct((n // rows_per_batch, wd, rows_per_batch), F32))
            out_specs.append(pl.BlockSpec((1, wd, tm), lambda i, steps=steps: (i // steps, 0, i % steps)))
        else:
            out_shape.append(jax.ShapeDtypeStruct((n, wd), F32))
            out_specs.append(pl.BlockSpec((tm, wd), lambda i: (i, 0)))
        if want_bf16:
            out_shape.append(jax.ShapeDtypeStruct((n, wd), BF16))
            out_specs.append(pl.BlockSpec((tm, wd), lambda i: (i, 0)))
        if want_blockmean:
            assert tm % MOBA_BLOCK == 0
            nb = tm // MOBA_BLOCK
            out_shape.append(jax.ShapeDtypeStruct((n // MOBA_BLOCK, 1, wd), F32))
            out_specs.append(pl.BlockSpec((nb, 1, wd), lambda i: (i, 0, 0)))
    return pl.pallas_call(
        functools.partial(_norm_proj_body, segs=segs, tm=tm),
        grid=(n // tm,),
        in_specs=[
            pl.BlockSpec((tm, d), lambda i: (i, 0)),
            pl.BlockSpec((1, d), lambda i: (0, 0)),
            pl.BlockSpec((d, wtot), lambda i: (0, 0)),
            pl.BlockSpec((1, wtot), lambda i: (0, 0)),
            pl.BlockSpec(ones_bd.shape, lambda i: (0, 0)),
        ],
        out_specs=out_specs,
        out_shape=out_shape,
        compiler_params=_cparams(("parallel",), 48),
        name="norm_proj",
    )(x, gain.reshape(1, d), w_bf, head_gain.reshape(1, wtot), ones_bd)


def _pool_windows(win, pos0):
    r = win.shape[0] - 16
    lane = lax.broadcasted_iota(jnp.int32, (r, LANES), 1)
    pos1 = pos0 + lax.broadcasted_iota(jnp.int32, (r, LANES), 0) + 1
    low = lane < POOL_GROUP_WIDTH
    a = win[:, :LANES]
    b = win[:, LANES:]
    a2 = a + pltpu.roll(a, 1, 0)
    a4 = a2 + pltpu.roll(a2, 2, 0)
    b2 = b + pltpu.roll(b, 1, 0)
    b4 = b2 + pltpu.roll(b2, 2, 0)
    b8 = b4 + pltpu.roll(b4, 4, 0)
    b16 = b8 + pltpu.roll(b8, 8, 0)
    cnt_a = jnp.minimum(jnp.where(low, POOL_WINDOWS[0], POOL_WINDOWS[1]), pos1).astype(F32)
    cnt_b = jnp.minimum(jnp.where(low, POOL_WINDOWS[2], POOL_WINDOWS[3]), pos1).astype(F32)
    pa = jnp.where(low, a2[16:], a4[16:]) / cnt_a - a[16:]
    pb = jnp.where(low, b8[16:], b16[16:]) / cnt_b - b[16:]
    return jnp.concatenate([pa, pb], axis=1)


def _pool_finish(pooled, wbd_ref, ps_ref, og_ref):
    mixed = jnp.dot(pooled.astype(BF16), wbd_ref[...], preferred_element_type=F32) * ps_ref[...]
    return _rms(mixed) * og_ref[...]


def _pool_body(u_ref, buf_ref, wbd_ref, ps_ref, og_ref, o_ref, ext_sc, *, bb, t, r, pos0):
    for bi in range(bb):
        ext_sc[bi, 0:16, :] = buf_ref[bi]
        ext_sc[bi, 16:, :] = u_ref[bi]
    if t == r:
        pooled = [_pool_windows(ext_sc[bi], pos0) for bi in range(bb)]
        out = _pool_finish(jnp.concatenate(pooled, axis=0), wbd_ref, ps_ref, og_ref)
        for bi in range(bb):
            o_ref[bi] = out[bi * r:(bi + 1) * r]
    else:
        assert bb == 1

        def chunk(c, carry):
            base = pl.multiple_of(c * r, r)
            pooled = _pool_windows(ext_sc[0, pl.ds(base, r + 16), :], pos0 + base)
            o_ref[0, pl.ds(base, r), :] = _pool_finish(pooled, wbd_ref, ps_ref, og_ref)
            return carry

        lax.fori_loop(0, t // r, chunk, 0)


def _pool(u, buf16, wbd_bf, pool_scale, og, pos0, bb, r):
    b, t, c = u.shape
    assert b % bb == 0 and t % r == 0
    return pl.pallas_call(
        functools.partial(_pool_body, bb=bb, t=t, r=r, pos0=pos0),
        grid=(b // bb,),
        in_specs=[
            pl.BlockSpec((bb, t, c), lambda i: (i, 0, 0)),
            pl.BlockSpec((bb, 16, c), lambda i: (i, 0, 0)),
            pl.BlockSpec((c, c), lambda i: (0, 0)),
            pl.BlockSpec((1, c), lambda i: (0, 0)),
            pl.BlockSpec((1, c), lambda i: (0, 0)),
        ],
        out_specs=pl.BlockSpec((bb, t, c), lambda i: (i, 0, 0)),
        out_shape=jax.ShapeDtypeStruct((b, t, c), F32),
        scratch_shapes=[pltpu.VMEM((bb, t + 16, c), F32)],
        compiler_params=_cparams(("parallel",), 40),
        name="pool",
    )(u, buf16, wbd_bf, pool_scale.reshape(1, c), og.reshape(1, c))


def _alibi_slope(h):
    return 2.0 ** (-8.0 * (h + 1) / MOBA_HEADS)


def _moba_prompt_body(q_ref, k_ref, v_ref, km_ref, og_ref, o_ref, m_sc, l_sc, acc_sc, sel_sc, qst_sc, *, nblk):
    i = pl.program_id(1)
    tq = MOBA_BLOCK
    n_pairs = MOBA_HEADS // 2
    nb_rows = -(-nblk // 8) * 8
    lane = lax.broadcasted_iota(jnp.int32, (tq, LANES), 1)
    low = lane < HEAD_DIM
    row2 = lax.broadcasted_iota(jnp.int32, (tq, MOBA_BLOCK), 0)
    col2 = lax.broadcasted_iota(jnp.int32, (tq, MOBA_BLOCK), 1)
    causal = col2 <= row2
    eye_bf = jnp.where(row2 == col2, 1.0, 0.0).astype(BF16)
    blk_row = lax.broadcasted_iota(jnp.int32, (nb_rows, tq), 0)
    blk_row_f = blk_row.astype(F32)
    colrow = lax.broadcasted_iota(jnp.int32, (1, MOBA_BLOCK), 1).astype(F32)

    def pair_cols(pr):
        return slice(pr * HEAD_PAIR, (pr + 1) * HEAD_PAIR)

    def scores(pr, j):
        start = pl.multiple_of(j * MOBA_BLOCK, MOBA_BLOCK)
        kj = k_ref[pl.ds(start, MOBA_BLOCK), pair_cols(pr)]
        s = lax.dot_general(qst_sc[pr], kj, NT_DIMS, preferred_element_type=F32)
        off = colrow + ((j - i) * MOBA_BLOCK).astype(F32)
        return [s[hh * tq:(hh + 1) * tq] + _alibi_slope(2 * pr + hh) * off for hh in range(2)]

    def pv(pr, j, ps):
        start = pl.multiple_of(j * MOBA_BLOCK, MOBA_BLOCK)
        vj = v_ref[pl.ds(start, MOBA_BLOCK), pair_cols(pr)]
        return jnp.dot(jnp.concatenate(ps, axis=0).astype(BF16), vj, preferred_element_type=F32)

    for pr in range(n_pairs):
        q_pair = q_ref[:, pair_cols(pr)]
        km_pair = km_ref[0, :, pair_cols(pr)]
        q_heads = [jnp.where(low, q_pair, 0.0), jnp.where(low, 0.0, q_pair)]

        km_hi = km_pair.astype(BF16)
        km_lo = (km_pair - km_hi.astype(F32)).astype(BF16)
        for hh in range(2):
            q_hi = q_heads[hh].astype(BF16)
            q_lo = (q_heads[hh] - q_hi.astype(F32)).astype(BF16)
            gate_t = (lax.dot_general(km_hi, q_hi, NT_DIMS, preferred_element_type=F32)
                      + (lax.dot_general(km_hi, q_lo, NT_DIMS, preferred_element_type=F32)
                         + lax.dot_general(km_lo, q_hi, NT_DIMS, preferred_element_type=F32)))[:nb_rows]
            g = jnp.where(blk_row < i, gate_t, NEG_INF)
            sel_t = jnp.zeros((nb_rows, tq), F32)
            for _ in range(MOBA_TOPK):
                mx = jnp.max(g, axis=0, keepdims=True)
                idx = jnp.min(jnp.where(g == mx, blk_row_f, float(LANES)), axis=0, keepdims=True)
                pick = (blk_row_f == idx) & (mx > NEG_INF)
                sel_t = jnp.where(pick, 1.0, sel_t)
                g = jnp.where(pick, NEG_INF, g)
            sel_t = jnp.concatenate([sel_t, jnp.zeros((LANES - nb_rows, tq), F32)], axis=0).astype(BF16)
            sel_sc[2 * pr + hh] = lax.dot_general(eye_bf, sel_t, NT_DIMS, preferred_element_type=F32)

        qst_sc[pr] = (jnp.concatenate(q_heads, axis=0) * QK_SCALE).astype(BF16)

    for pr in range(n_pairs):
        s_own = scores(pr, i)
        ps = []
        for hh in range(2):
            sh = jnp.where(causal, s_own[hh], NEG_INF)
            m = jnp.max(sh, axis=-1, keepdims=True)
            p = jnp.exp(sh - m)
            m_sc[2 * pr + hh] = jnp.broadcast_to(m, (tq, LANES))
            l_sc[2 * pr + hh] = jnp.broadcast_to(jnp.sum(p, axis=-1, keepdims=True), (tq, LANES))
            ps.append(p)
        acc_sc[pr] = pv(pr, i, ps)

    def past(j, carry):
        for pr in range(n_pairs):
            s_j = scores(pr, j)
            ps, alphas = [], []
            for hh in range(2):
                h = 2 * pr + hh
                selcol = jnp.sum(jnp.where(lane == j, sel_sc[h], 0.0), axis=-1, keepdims=True)
                sh = jnp.where(selcol > 0.0, s_j[hh], NEG_INF)
                m_prev = m_sc[h]
                m_new = jnp.maximum(m_prev, jnp.max(sh, axis=-1, keepdims=True))
                alpha = jnp.exp(m_prev - m_new)
                p = jnp.exp(sh - jnp.concatenate([m_new, m_new], axis=1))
                l_sc[h] = alpha * l_sc[h] + jnp.sum(p, axis=-1, keepdims=True)
                m_sc[h] = m_new
                ps.append(p)
                alphas.append(alpha)
            acc_sc[pr] = jnp.concatenate(alphas, axis=0) * acc_sc[pr] + pv(pr, j, ps)
        return carry

    lax.fori_loop(0, i, past, 0)

    outs = []
    for pr in range(n_pairs):
        acc = acc_sc[pr]
        outs.append(jnp.where(low, acc[:tq] / l_sc[2 * pr], acc[tq:] / l_sc[2 * pr + 1]))
    o_ref[...] = _rms(jnp.concatenate(outs, axis=1)) * og_ref[...]


def _moba_prompt(q, k_bf, v_bf, kmean_pad, og, b, t):
    n, w = q.shape
    nblk = t // MOBA_BLOCK
    tq = MOBA_BLOCK
    return pl.pallas_call(
        functools.partial(_moba_prompt_body, nblk=nblk),
        grid=(b, nblk),
        in_specs=[
            pl.BlockSpec((tq, w), lambda bi, i: (bi * nblk + i, 0)),
            pl.BlockSpec((t, w), lambda bi, i: (bi, 0)),
            pl.BlockSpec((t, w), lambda bi, i: (bi, 0)),
            pl.BlockSpec((1, LANES, w), lambda bi, i: (bi, 0, 0)),
            pl.BlockSpec((1, w), lambda bi, i: (0, 0)),
        ],
        out_specs=pl.BlockSpec((tq, w), lambda bi, i: (bi * nblk + i, 0)),
        out_shape=jax.ShapeDtypeStruct((n, w), F32),
        scratch_shapes=[
            pltpu.VMEM((MOBA_HEADS, tq, LANES), F32),
            pltpu.VMEM((MOBA_HEADS, tq, LANES), F32),
            pltpu.VMEM((MOBA_HEADS // 2, 2 * tq, LANES), F32),
            pltpu.VMEM((MOBA_HEADS, tq, LANES), F32),
            pltpu.VMEM((MOBA_HEADS // 2, 2 * tq, LANES), BF16),
        ],
        compiler_params=_cparams(("parallel", "arbitrary"), 40),
        name="moba_prompt",
    )(q, k_bf, v_bf, kmean_pad, og.reshape(1, w))


SAMPLE_BLOCKS_PER_STEP = 8
PAGES_PER_BLOCK = MOBA_BLOCK // PAGE_SIZE
PAGES_PER_STEP = SAMPLE_BLOCKS_PER_STEP * PAGES_PER_BLOCK


def _moba_sample_body(pt_ref, q_ref, kn_ref, vn_ref, slope_ref, og_ref, *rest, n_past_blocks, t_new):
    kp = rest[:PAGES_PER_STEP]
    vp = rest[PAGES_PER_STEP:2 * PAGES_PER_STEP]
    o_ref, o_sc, m_sc, l_sc, kmt_sc = rest[2 * PAGES_PER_STEP:]
    c = pl.program_id(1)
    rows = MOBA_HEADS * t_new
    row_h = lax.broadcasted_iota(jnp.int32, (rows, MOBA_WIDTH), 0) // t_new
    lane_h = lax.broadcasted_iota(jnp.int32, (rows, MOBA_WIDTH), 1) // HEAD_DIM
    bd = row_h == lane_h
    q = q_ref[...]
    qbd = jnp.where(bd, jnp.concatenate([q] * MOBA_HEADS, axis=0), 0.0)
    qbd_bf = (qbd * QK_SCALE).astype(BF16)
    slope = slope_ref[...]
    slope2 = jnp.concatenate([slope, slope], axis=1)
    colf = lax.broadcasted_iota(jnp.int32, (rows, MOBA_BLOCK), 1).astype(F32)

    @pl.when(c == 0)
    def _init():
        kmt_sc[...] = jnp.zeros(kmt_sc.shape, F32)

    kt_bf = []
    for jj in range(SAMPLE_BLOCKS_PER_STEP):
        kt_pages = [kp[PAGES_PER_BLOCK * jj + p][0] for p in range(PAGES_PER_BLOCK)]
        ksum = kt_pages[0]
        for kt in kt_pages[1:]:
            ksum = ksum + kt
        kmt_sc[c, :, jj:jj + 1] = jnp.sum(ksum, axis=-1, keepdims=True) * (1.0 / MOBA_BLOCK)
        kt_bf.append(jnp.concatenate(kt_pages, axis=1).astype(BF16))
    s_all = jnp.dot(qbd_bf, jnp.concatenate(kt_bf, axis=1), preferred_element_type=F32)

    for jj in range(SAMPLE_BLOCKS_PER_STEP):
        j = c * SAMPLE_BLOCKS_PER_STEP + jj
        s = s_all[:, jj * MOBA_BLOCK:(jj + 1) * MOBA_BLOCK] + slope2 * (
            colf + ((j - n_past_blocks) * MOBA_BLOCK).astype(F32))
        m = jnp.max(s, axis=-1, keepdims=True)
        p = jnp.exp(s - m)
        vt_bf = jnp.concatenate([vp[PAGES_PER_BLOCK * jj + q][0] for q in range(PAGES_PER_BLOCK)],
                                axis=1).astype(BF16)
        o = lax.dot_general(p.astype(BF16), vt_bf, NT_DIMS, preferred_element_type=F32)
        o_sc[j] = jnp.where(bd, o, 0.0)
        m_sc[j] = jnp.broadcast_to(m, (rows, LANES))
        l_sc[j] = jnp.broadcast_to(jnp.sum(p, axis=-1, keepdims=True), (rows, LANES))

    @pl.when(c == pl.num_programs(1) - 1)
    def _finish():
        tq = lax.broadcasted_iota(jnp.int32, (rows, LANES), 0) % t_new
        kn = kn_ref[...]
        vn = vn_ref[...]
        qs = qbd * QK_SCALE
        s_own = []
        m_run = jnp.full((rows, LANES), NEG_INF, F32)
        for cc in range(t_new):
            sc = jnp.sum(qs * kn[cc:cc + 1, :], axis=-1, keepdims=True) + slope * float(cc)
            sc = jnp.where(tq >= cc, sc, NEG_INF)
            s_own.append(sc)
            m_run = jnp.maximum(m_run, sc)

        blk = lax.broadcasted_iota(jnp.int32, (rows, LANES), 1)
        blk_f = blk.astype(F32)
        gates = jnp.zeros((rows, LANES), F32)
        for cs in range(n_past_blocks // SAMPLE_BLOCKS_PER_STEP):
            g_cs = jnp.dot(qbd, kmt_sc[cs], precision=lax.Precision.HIGHEST, preferred_element_type=F32)
            g_cs = jnp.where(blk < SAMPLE_BLOCKS_PER_STEP, g_cs, 0.0)
            gates = gates + (pltpu.roll(g_cs, cs * SAMPLE_BLOCKS_PER_STEP, 1) if cs else g_cs)
        g = jnp.where(blk < n_past_blocks, gates, NEG_INF)
        sel = jnp.zeros((rows, LANES), F32)
        for _ in range(MOBA_TOPK):
            mx = jnp.max(g, axis=-1, keepdims=True)
            idx = jnp.min(jnp.where(g == mx, blk_f, float(LANES)), axis=-1, keepdims=True)
            pick = (blk_f == idx) & (mx > NEG_INF)
            sel = jnp.where(pick, 1.0, sel)
            g = jnp.where(pick, NEG_INF, g)

        m_all = jnp.full((rows, LANES), NEG_INF, F32)
        l_all = jnp.zeros((rows, LANES), F32)
        for j in range(n_past_blocks):
            m_all = jnp.where(blk == j, m_sc[j], m_all)
            l_all = jnp.where(blk == j, l_sc[j], l_all)
        m_run = jnp.maximum(m_run, jnp.max(jnp.where(sel > 0.0, m_all, NEG_INF), axis=-1, keepdims=True))
        w_all = jnp.where(sel > 0.0, jnp.exp(m_all - m_run), 0.0)

        l_run = jnp.broadcast_to(jnp.sum(w_all * l_all, axis=-1, keepdims=True), (rows, LANES))
        o_run = jnp.zeros((rows, MOBA_WIDTH), F32)
        for cc in range(t_new):
            p = jnp.exp(s_own[cc] - m_run)
            l_run = l_run + p
            o_run = o_run + jnp.concatenate([p] * (MOBA_WIDTH // LANES), axis=1) * vn[cc:cc + 1, :]
        o_run = jnp.where(bd, o_run, 0.0)
        for j in range(n_past_blocks):
            o_run = o_run + w_all[:, j:j + 1] * o_sc[j]
        o_bd = o_run / jnp.concatenate([l_run] * (MOBA_WIDTH // LANES), axis=1)
        attn = o_bd[0:t_new]
        for h in range(1, MOBA_HEADS):
            attn = attn + o_bd[h * t_new:(h + 1) * t_new]
        o_ref[...] = _rms(attn) * og_ref[...]


def _moba_sample(page_table, q, k_new, v_new, cache_kt, cache_vt, og, b, t_new):
    n_pages = page_table.shape[1]
    n_past_blocks = n_pages // PAGES_PER_BLOCK
    assert n_past_blocks <= LANES
    assert n_past_blocks % SAMPLE_BLOCKS_PER_STEP == 0
    n_steps = n_past_blocks // SAMPLE_BLOCKS_PER_STEP
    rows = MOBA_HEADS * t_new
    w = MOBA_WIDTH
    slopes = jnp.exp2(-8.0 * jnp.arange(1, MOBA_HEADS + 1, dtype=F32) / MOBA_HEADS)
    slope_rows = jnp.broadcast_to(jnp.repeat(slopes, t_new)[:, None], (rows, LANES))

    def page_spec(p):
        return pl.BlockSpec((1, w, PAGE_SIZE), lambda bi, c, pt: (pt[bi, c * PAGES_PER_STEP + p], 0, 0))

    row_spec = pl.BlockSpec((t_new, w), lambda bi, c, pt: (bi, 0))
    grid_spec = pltpu.PrefetchScalarGridSpec(
        num_scalar_prefetch=1,
        grid=(b, n_steps),
        in_specs=[row_spec, row_spec, row_spec,
                  pl.BlockSpec((rows, LANES), lambda bi, c, pt: (0, 0)),
                  pl.BlockSpec((1, w), lambda bi, c, pt: (0, 0))]
                 + [page_spec(p) for p in range(PAGES_PER_STEP)]
                 + [page_spec(p) for p in range(PAGES_PER_STEP)],
        out_specs=row_spec,
        scratch_shapes=[
            pltpu.VMEM((n_past_blocks, rows, w), F32),
            pltpu.VMEM((n_past_blocks, rows, LANES), F32),
            pltpu.VMEM((n_past_blocks, rows, LANES), F32),
            pltpu.VMEM((n_steps, w, LANES), F32),
        ],
    )
    return pl.pallas_call(
        functools.partial(_moba_sample_body, n_past_blocks=n_past_blocks, t_new=t_new),
        grid_spec=grid_spec,
        out_shape=jax.ShapeDtypeStruct((b * t_new, w), F32),
        compiler_params=_cparams(("parallel", "arbitrary"), 48),
        name="moba_sample",
    )(page_table, q, k_new, v_new, slope_rows, og.reshape(1, w),
      *([cache_kt] * PAGES_PER_STEP), *([cache_vt] * PAGES_PER_STEP))


def _mem_attn_body(q_ref, mk_ref, mv_ref, og_ref, o_ref, *, tm):
    lane = lax.broadcasted_iota(jnp.int32, (tm, LANES), 1)
    low = lane < HEAD_DIM
    outs = []
    for pr in range(MEM_HEADS // 2):
        cs = slice(pr * HEAD_PAIR, (pr + 1) * HEAD_PAIR)
        q_pair = q_ref[:, cs]
        qst = (jnp.concatenate([jnp.where(low, q_pair, 0.0), jnp.where(low, 0.0, q_pair)], axis=0)
               * QK_SCALE).astype(BF16)
        s = lax.dot_general(qst, mk_ref[0, :, cs].astype(BF16), NT_DIMS, preferred_element_type=F32)
        p = jnp.exp(s - jnp.max(s, axis=-1, keepdims=True))
        l = jnp.sum(p, axis=-1, keepdims=True)
        o = jnp.dot(p.astype(BF16), mv_ref[0, :, cs].astype(BF16), preferred_element_type=F32) / l
        outs.append(jnp.where(low, o[:tm], o[tm:]))
    o_ref[...] = _rms(jnp.concatenate(outs, axis=1)) * og_ref[...]


def _mem_attn(qm, mem_k, mem_v, og, b, t, tm):
    n, w = qm.shape
    assert t % tm == 0
    steps = t // tm
    return pl.pallas_call(
        functools.partial(_mem_attn_body, tm=tm),
        grid=(b, steps),
        in_specs=[
            pl.BlockSpec((tm, w), lambda bi, i: (bi * steps + i, 0)),
            pl.BlockSpec((1, MEM_TOKENS, w), lambda bi, i: (bi, 0, 0)),
            pl.BlockSpec((1, MEM_TOKENS, w), lambda bi, i: (bi, 0, 0)),
            pl.BlockSpec((1, w), lambda bi, i: (0, 0)),
        ],
        out_specs=pl.BlockSpec((tm, w), lambda bi, i: (bi * steps + i, 0)),
        out_shape=jax.ShapeDtypeStruct((n, w), F32),
        compiler_params=_cparams(("parallel", "arbitrary"), 32),
        name="mem_attn",
    )(qm, mem_k, mem_v, og.reshape(1, w))


ROUTER_GROUP_LANE0 = N_EXPERTS
ROUTER_GID_LANE = 64


def _outproj_router_body(op_ref, oa_ref, om_ref, x_ref, wo_ref, g2_ref, wr_ref, br_ref,
                         x1_ref, h2_ref, comb_ref, gcnt_ref, *, tm):
    a0, a1 = POOL_WIDTH, POOL_WIDTH + MOBA_WIDTH
    y = x_ref[...]
    y = y + jnp.dot(op_ref[...].astype(BF16), wo_ref[0:a0, :], preferred_element_type=F32)
    y = y + jnp.dot(oa_ref[...].astype(BF16), wo_ref[a0:a1, :], preferred_element_type=F32)
    y = y + jnp.dot(om_ref[...].astype(BF16), wo_ref[a1:, :], preferred_element_type=F32)
    x1_ref[...] = y
    h2 = _rms(y) * g2_ref[...]
    h2_hi = h2.astype(BF16)
    h2_lo = (h2 - h2_hi.astype(F32)).astype(BF16)
    h2_ref[...] = h2_hi
    hw =jnp.dot(h2_hi, wr_ref[...], preferred_element_type=F32)
    lw = jnp.dot(h2_lo, wr_ref[:, :LANES], preferred_element_type=F32)
    logits = (hw[:, :LANES] + (hw[:, LANES:] + lw)) + br_ref[...]
    lane_f = lax.broadcasted_iota(jnp.int32, (tm, LANES), 1).astype(F32)
    big = float(LANES)
    g_lo = float(ROUTER_GROUP_LANE0)
    is_g = (lane_f >= g_lo) & (lane_f < g_lo + MOE_GROUPS)
    lg = jnp.where(is_g, logits, NEG_INF)
    mg = jnp.max(lg, axis=-1, keepdims=True)
    pg_top = 1.0 / jnp.sum(jnp.exp(lg - mg), axis=-1, keepdims=True)
    gidx = jnp.min(jnp.where(lg == mg, lane_f, big), axis=-1, keepdims=True) - g_lo
    e_lo = gidx * EXPERTS_PER_GROUP
    in_grp = (lane_f >= e_lo) & (lane_f < e_lo + EXPERTS_PER_GROUP)
    le = jnp.where(in_grp, logits, NEG_INF)
    m1 = jnp.max(le, axis=-1, keepdims=True)
    se = jnp.sum(jnp.exp(le - m1), axis=-1, keepdims=True)
    i1 = jnp.min(jnp.where(le == m1, lane_f, big), axis=-1, keepdims=True)
    le2 = jnp.where(lane_f == i1, NEG_INF, le)
    m2 = jnp.max(le2, axis=-1, keepdims=True)
    i2 = jnp.min(jnp.where(le2 == m2, lane_f, big), axis=-1, keepdims=True)
    p1 = 1.0 / se
    p2 = jnp.exp(m2 - m1) / se
    den = p1 + p2
    comb_ref[...] = jnp.where(lane_f == i1, pg_top * (p1 / den),
                              jnp.where(lane_f == i2, pg_top * (p2 / den),
                                        jnp.where(lane_f == float(ROUTER_GID_LANE), gidx, 0.0)))
    gcnt_ref[0] = jnp.sum(jnp.where(lane_f == gidx, 1.0, 0.0), axis=0, keepdims=True)


def _outproj_router(o_pool, o_attn, o_mem, x, wo_bf, g2, wr, br, tm):
    n, d = x.shape
    assert n % tm == 0
    row = lambda wdt: pl.BlockSpec((tm, wdt), lambda i: (i, 0))
    full = lambda shp: pl.BlockSpec(shp, lambda i: (0, 0))
    return pl.pallas_call(
        functools.partial(_outproj_router_body, tm=tm),
        grid=(n // tm,),
        in_specs=[row(POOL_WIDTH), row(MOBA_WIDTH), row(MEM_WIDTH), row(d),
                  full((d, d)), full((1, d)), full((d, 2 * LANES)), full((1, LANES))],
        out_specs=[row(d), row(d), row(LANES), pl.BlockSpec((1, 1, LANES), lambda i: (i, 0, 0))],
        out_shape=[jax.ShapeDtypeStruct((n, d), F32), jax.ShapeDtypeStruct((n, d), BF16),
                   jax.ShapeDtypeStruct((n, LANES), F32), jax.ShapeDtypeStruct((n // tm, 1, LANES), F32)],
        compiler_params=_cparams(("parallel",), 40),
        name="outproj_router",
    )(o_pool, o_attn, o_mem, x, wo_bf, g2.reshape(1, d), wr, br)


MOE_EXPERTS_PER_STEP = 4
MOE_RUN_ALIGN = 16
MOE_SORT_PAD = LANES


def _moe_window(tm):
    return max(LANES, tm // MOE_GROUPS + 32)


def _moe_body(cnt_ref, h2_ref, comb_ref, x1_hbm, wg_ref, wu_ref, wd_ref, y_ref,
              xs_sc, cs_sc, ys_sc, p_sc, pt_sc, sem, *, tm, win):
    i = pl.program_id(0)
    s = pl.program_id(1)
    eps = MOE_EXPERTS_PER_STEP
    span = tm + MOE_SORT_PAD
    g = s // (EXPERTS_PER_GROUP // eps)
    starts, nxt = [], jnp.int32(0)
    for gg in range(MOE_GROUPS):
        starts.append(nxt)
        nxt = nxt + (cnt_ref[i * MOE_GROUPS + gg] + (MOE_RUN_ALIGN - 1)) // MOE_RUN_ALIGN * MOE_RUN_ALIGN
    x1_copy = pltpu.make_async_copy(x1_hbm.at[pl.ds(pl.multiple_of(i * tm, tm), tm)], y_ref, sem)

    @pl.when(s == 0)
    def _sort():
        x1_copy.start()
        comb = comb_ref[...]
        lane = lax.broadcasted_iota(jnp.int32, (tm, LANES), 1)
        lane_f = lane.astype(F32)
        gid = jnp.sum(jnp.where(lane == ROUTER_GID_LANE, comb, 0.0), axis=-1, keepdims=True)
        onehot = jnp.where(lane_f == gid, 1.0, 0.0)
        earlier_tok = lax.broadcasted_iota(jnp.int32, (tm, tm), 1) < lax.broadcasted_iota(jnp.int32, (tm, tm), 0)
        earlier = jnp.dot(jnp.where(earlier_tok, 1.0, 0.0).astype(BF16), onehot.astype(BF16),
                          preferred_element_type=F32)
        start_vec = jnp.zeros((tm, LANES), F32)
        for gg in range(MOE_GROUPS):
            start_vec = jnp.where(lane == gg, starts[gg].astype(F32), start_vec)
        dest = jnp.sum(onehot * (earlier + start_vec), axis=-1, keepdims=True)
        hi = jnp.floor(dest * (1.0 / 32.0))
        lo = dest - 32.0 * hi
        dest_cols = jnp.where(lane == 0, hi, jnp.where(lane == 1, lo, 0.0)).astype(BF16)
        r16 = lax.broadcasted_iota(jnp.int32, (16, LANES), 0)
        l16 = lax.broadcasted_iota(jnp.int32, (16, LANES), 1)
        pick = jnp.where((r16 == 0) & (l16 == 0), 32.0, jnp.where((r16 == 0) & (l16 == 1), 1.0, 0.0)).astype(BF16)
        dest_row = lax.dot_general(pick, dest_cols, NT_DIMS, preferred_element_type=F32)[0:1, :]
        p = jnp.where(lax.broadcasted_iota(jnp.int32, (span, tm), 0).astype(F32) == dest_row, 1.0, 0.0).astype(BF16)
        p_sc[...] = p
        pt_sc[...] = jnp.where(lax.broadcasted_iota(jnp.int32, (tm, span), 1).astype(F32) == dest,
                               1.0, 0.0).astype(BF16)
        xs_sc[0:span, :] = jnp.dot(p, h2_ref[...], preferred_element_type=F32).astype(BF16)
        xs_sc[span:, :] = jnp.zeros((win, D_MODEL), BF16)
        c1 = comb.astype(BF16)
        r1 = comb - c1.astype(F32)
        c2 = r1.astype(BF16)
        c3 = (r1 - c2.astype(F32)).astype(BF16)
        cs_sc[0:span, :] = (jnp.dot(p, c1, preferred_element_type=F32) + jnp.dot(p, c2, preferred_element_type=F32)
                            + jnp.dot(p, c3, preferred_element_type=F32))
        cs_sc[span:, :] = jnp.zeros((win, LANES), F32)
        ys_sc[...] = jnp.zeros(ys_sc.shape, F32)

    run_start = starts[0]
    for gg in range(1, MOE_GROUPS):
        run_start = jnp.where(g == gg, starts[gg], run_start)
    run_len = cnt_ref[i * MOE_GROUPS + g]
    lane_w = lax.broadcasted_iota(jnp.int32, (win, LANES), 1)

    def window(w, carry):
        r0 = pl.multiple_of(run_start + w * win, MOE_RUN_ALIGN)
        rows = xs_sc[pl.ds(r0, win), :]
        comb_rows = cs_sc[pl.ds(r0, win), :]
        acc = None
        for j in range(eps):
            e = s * eps + j
            a = jnp.dot(rows, wg_ref[j], preferred_element_type=F32)
            b = jnp.dot(rows, wu_ref[j], preferred_element_type=F32)
            ce = jnp.sum(jnp.where(lane_w == e, comb_rows, 0.0), axis=-1, keepdims=True)
            hid = ((a * jax.nn.sigmoid(a)) * b * ce).astype(BF16)
            c = jnp.dot(hid, wd_ref[j], preferred_element_type=F32)
            acc = c if acc is None else acc + c
        ys_sc[pl.ds(r0, win), :] += acc
        return carry

    lax.fori_loop(0, (run_len + win - 1) // win, window, 0)

    @pl.when(s == pl.num_programs(1) - 1)
    def _unsort():
        ys = ys_sc[0:span, :]
        moe = jnp.dot(pt_sc[...], ys.astype(BF16), preferred_element_type=F32)
        x1_copy.wait()
        y_ref[...] = y_ref[...] + moe


def _moe(tile_cnt, h2, comb, x1, wg_bf, wu_bf, wd_bf, tm):
    n, d = x1.shape
    eps = MOE_EXPERTS_PER_STEP
    assert n % tm == 0 and EXPERTS_PER_GROUP % eps == 0 and MOE_GROUPS * (MOE_RUN_ALIGN - 1) <= MOE_SORT_PAD
    win = _moe_window(tm)
    assert win % MOE_RUN_ALIGN == 0
    span = tm + MOE_SORT_PAD
    grid_spec = pltpu.PrefetchScalarGridSpec(
        num_scalar_prefetch=1,
        grid=(n // tm, N_EXPERTS // eps),
        in_specs=[
            pl.BlockSpec((tm, d), lambda i, s, cnt: (i, 0)),
            pl.BlockSpec((tm, LANES), lambda i, s, cnt: (i, 0)),
            pl.BlockSpec(memory_space=pl.ANY),
            pl.BlockSpec((eps, d, D_EXPERT), lambda i, s, cnt: (s, 0, 0)),
            pl.BlockSpec((eps, d, D_EXPERT), lambda i, s, cnt: (s, 0, 0)),
            pl.BlockSpec((eps, D_EXPERT, d), lambda i, s, cnt: (s, 0, 0)),
        ],
        out_specs=pl.BlockSpec((tm, d), lambda i, s, cnt: (i, 0)),
        scratch_shapes=[
            pltpu.VMEM((span + win, d), BF16),
            pltpu.VMEM((span + win, LANES), F32),
            pltpu.VMEM((span + win, d), F32),
            pltpu.VMEM((span, tm), BF16),
            pltpu.VMEM((tm, span), BF16),
            pltpu.SemaphoreType.DMA(()),
        ],
    )
    return pl.pallas_call(
        functools.partial(_moe_body, tm=tm, win=win),
        grid_spec=grid_spec,
        out_shape=jax.ShapeDtypeStruct((n, d), F32),
        compiler_params=_cparams(("parallel", "arbitrary"), 56),
        name="moe",
    )(tile_cnt, h2, comb, x1, wg_bf, wu_bf, wd_bf)


def _block_diag_ones(width, group):
    r = lax.broadcasted_iota(jnp.int32, (width, width), 0) // group
    c = lax.broadcasted_iota(jnp.int32, (width, width), 1) // group
    return (r == c).astype(BF16)


def _pick_tile(n, candidates):
    for c in candidates:
        if n % c == 0:
            return c
    return n


IN_SEGS = (
    (0, POOL_WIDTH, False, False, False, False),
    (POOL_WIDTH, MOBA_WIDTH, True, False, False, False),
    (POOL_WIDTH + MOBA_WIDTH, MOBA_WIDTH, True, True, True, True),
    (POOL_WIDTH + 2 * MOBA_WIDTH, MOBA_WIDTH, False, True, True, False),
    (POOL_WIDTH + 3 * MOBA_WIDTH, MEM_WIDTH, True, False, False, False),
)
IN_SEGS_SAMPLE = tuple((c0, wd, hn, False, False, False) for (c0, wd, hn, _, _, _) in IN_SEGS)
MEMKV_SEGS = ((0, MEM_WIDTH, True, False, False, False), (MEM_WIDTH, MEM_WIDTH, False, False, False, False))


def kernel(x_prompt, x_sample, mem_prompt, cache_k, cache_v, state_pool, cache_mem_k, cache_mem_v,
           page_table, norm1_gain, w_in, pool_w, pool_scale, moba_q_gain, moba_k_gain,
           mem_norm_gain, w_mem_kv, mem_q_gain, mem_k_gain, out_gain, w_out, norm2_gain,
           router_group_w, router_group_b, router_expert_w, router_expert_b, w_gate, w_up, w_down):
    bp, tp, d = x_prompt.shape
    bs, ts, _ = x_sample.shape
    n_p, n_s = bp * tp, bs * ts

    w_in_bf = w_in.astype(BF16)
    w_mem_bf = w_mem_kv.astype(BF16)
    wo_bf = w_out.astype(BF16)
    ones_bd = _block_diag_ones(MOBA_WIDTH, HEAD_DIM)
    head_gain_in = jnp.concatenate([
        jnp.ones((POOL_WIDTH,), F32), jnp.tile(moba_q_gain, MOBA_HEADS), jnp.tile(moba_k_gain, MOBA_HEADS),
        jnp.ones((MOBA_WIDTH,), F32), jnp.tile(mem_q_gain, MEM_HEADS)])
    head_gain_mem = jnp.concatenate([jnp.tile(mem_k_gain, MEM_HEADS), jnp.ones((MEM_WIDTH,), F32)])
    wbd = jnp.zeros((POOL_WIDTH, POOL_WIDTH), F32)
    for g in range(len(POOL_WINDOWS)):
        sl = slice(g * POOL_GROUP_WIDTH, (g + 1) * POOL_GROUP_WIDTH)
        wbd = wbd.at[sl, sl].set(pool_w[g])
    wbd_bf = wbd.astype(BF16)
    og_pool, og_attn, og_mem = (out_gain[:POOL_WIDTH], out_gain[POOL_WIDTH:POOL_WIDTH + MOBA_WIDTH],
                                out_gain[POOL_WIDTH + MOBA_WIDTH:])
    wr = jnp.zeros((d, LANES), F32)
    wr = wr.at[:, :N_EXPERTS].set(router_expert_w).at[:, N_EXPERTS:N_EXPERTS + MOE_GROUPS].set(router_group_w)
    br = jnp.zeros((1, LANES), F32)
    br = br.at[0, :N_EXPERTS].set(router_expert_b).at[0, N_EXPERTS:N_EXPERTS + MOE_GROUPS].set(router_group_b)
    wr_hi = wr.astype(BF16)
    wr = jnp.concatenate([wr_hi, (wr - wr_hi.astype(F32)).astype(BF16)], axis=1)
    wg = w_gate.reshape(N_EXPERTS, d, D_EXPERT).astype(BF16)
    wu = w_up.reshape(N_EXPERTS, d, D_EXPERT).astype(BF16)
    wd = w_down.reshape(N_EXPERTS, D_EXPERT, d).astype(BF16)

    def tail(o_pool, o_attn, o_mem, x2d):
        n = x2d.shape[0]
        tm_r = _pick_tile(n, (512, 256))
        x1, h2, comb, gcnt = _outproj_router(o_pool, o_attn, o_mem, x2d, wo_bf, norm2_gain, wr, br, tm_r)
        tm = _pick_tile(n, (1024, 512, 256))
        tile_cnt = gcnt[:, 0, :MOE_GROUPS].reshape(n // tm, tm // tm_r, MOE_GROUPS).sum(axis=1)
        return _moe(tile_cnt.astype(jnp.int32).reshape(-1), h2, comb, x1, wg, wu, wd, tm)

    mem_k_p, mem_v_p = _norm_proj(mem_prompt.reshape(bp * MEM_TOKENS, d), mem_norm_gain, w_mem_bf,
                                  head_gain_mem, ones_bd, MEMKV_SEGS, _pick_tile(bp * MEM_TOKENS, (512, 256)))
    xp = x_prompt.reshape(n_p, d)
    u_p, q_p, kt_p, k_p_bf, kmean_p, vt_p, v_p_bf, qm_p = _norm_proj(
        xp, norm1_gain, w_in_bf, head_gain_in, ones_bd, IN_SEGS, _pick_tile(tp, (512, 256)), rows_per_batch=tp)
    o_pool_p = _pool(u_p.reshape(bp, tp, POOL_WIDTH), jnp.zeros((bp, 16, POOL_WIDTH), F32), wbd_bf,
                     pool_scale, og_pool, 0, 1, MOBA_BLOCK).reshape(n_p, POOL_WIDTH)
    nblk = tp // MOBA_BLOCK
    kmean_pad = jnp.pad(kmean_p.reshape(bp, nblk, MOBA_WIDTH), ((0, 0), (0, LANES - nblk), (0, 0)))
    o_attn_p = _moba_prompt(q_p, k_p_bf, v_p_bf, kmean_pad, og_attn, bp, tp)
    o_mem_p = _mem_attn(qm_p, mem_k_p.reshape(bp, MEM_TOKENS, MEM_WIDTH),
                        mem_v_p.reshape(bp, MEM_TOKENS, MEM_WIDTH), og_mem, bp, tp, _pick_tile(tp, (512, 256)))
    y_p = tail(o_pool_p, o_attn_p, o_mem_p, xp)

    past_len = page_table.shape[1] * PAGE_SIZE
    xs = x_sample.reshape(n_s, d)
    u_s, q_s, k_s, v_s, qm_s = _norm_proj(xs, norm1_gain, w_in_bf, head_gain_in, ones_bd, IN_SEGS_SAMPLE, n_s)
    u_s3 = u_s.reshape(bs, ts, POOL_WIDTH)
    buf16 = jnp.concatenate([jnp.zeros((bs, 1, POOL_WIDTH), F32), state_pool], axis=1)
    o_pool_s = _pool(u_s3, buf16, wbd_bf, pool_scale, og_pool, past_len, bs, ts).reshape(n_s, POOL_WIDTH)
    o_attn_s = _moba_sample(page_table, q_s, k_s, v_s,
                            cache_k.transpose(0, 2, 3, 1).reshape(-1, MOBA_WIDTH, PAGE_SIZE),
                            cache_v.transpose(0, 2, 3, 1).reshape(-1, MOBA_WIDTH, PAGE_SIZE),
                            og_attn, bs, ts)
    o_mem_s = _mem_attn(qm_s, cache_mem_k.reshape(bs, MEM_TOKENS, MEM_WIDTH),
                        cache_mem_v.reshape(bs, MEM_TOKENS, MEM_WIDTH), og_mem, bs, ts, ts)
    y_s = tail(o_pool_s, o_attn_s, o_mem_s, xs)

    pool_prompt = u_p.reshape(bp, tp, POOL_WIDTH)[:, tp - POOL_STATE:]
    pool_sample = jnp.concatenate([state_pool, u_s3], axis=1)[:, -POOL_STATE:]
    k_p = kt_p.reshape(bp, MOBA_HEADS, HEAD_DIM, tp).transpose(0, 3, 1, 2)
    v_p = vt_p.reshape(bp, MOBA_HEADS, HEAD_DIM, tp).transpose(0, 3, 1, 2)
    return (y_p.reshape(bp, tp, d), y_s.reshape(bs, ts, d), k_p, v_p,
            pool_prompt,
            mem_k_p.reshape(bp, MEM_TOKENS, MEM_HEADS, HEAD_DIM), mem_v_p.reshape(bp, MEM_TOKENS, MEM_HEADS, HEAD_DIM),
            k_s.reshape(bs, ts, MOBA_HEADS, HEAD_DIM), v_s.reshape(bs, ts, MOBA_HEADS, HEAD_DIM),
            pool_sample)
```

```python
import functools

import jax
import jax.numpy as jnp
from jax import lax
from jax.experimental import pallas as pl
from jax.experimental.pallas import tpu as pltpu

F32 = jnp.float32
BF16 = jnp.bfloat16

D_MODEL = 1024
HEAD_DIM = 64
POOL_WIDTH = 256
POOL_WINDOWS = (2, 4, 8, 16)
POOL_GROUP_WIDTH = 64
POOL_STATE = 15
MOBA_WIDTH = 512
MOBA_HEADS = 8
MOBA_BLOCK = 256
MOBA_TOPK = 3
MEM_WIDTH = 256
MEM_HEADS = 4
MEM_TOKENS = 256
PAGE_SIZE = 128
MOE_GROUPS = 4
EXPERTS_PER_GROUP = 8
N_EXPERTS = MOE_GROUPS * EXPERTS_PER_GROUP
D_EXPERT = 256
NORM_EPS = 1e-6

LANES = 128
HEAD_PAIR = 2 * HEAD_DIM
QK_SCALE = HEAD_DIM ** -0.5
NEG_INF = float("-inf")
MIB = 1024 * 1024
NT_DIMS = (((1,), (1,)), ((), ()))


def _cparams(semantics, vmem_mib):
    return pltpu.CompilerParams(dimension_semantics=semantics, vmem_limit_bytes=vmem_mib * MIB)


def _rms(y, eps=NORM_EPS):
    return y * lax.rsqrt(jnp.mean(y * y, axis=-1, keepdims=True) + eps)


def _norm_proj_body(x_ref, g_ref, w_ref, hg_ref, ones_ref, *outs, segs, tm):
    hb = (_rms(x_ref[...]) * g_ref[...]).astype(BF16)
    oi = 0
    for (c0, wd, headnorm, transposed, want_bf16, want_blockmean) in segs:
        y = jnp.dot(hb, w_ref[:, c0:c0 + wd], preferred_element_type=F32)
        if headnorm:
            sq = (y * y).astype(BF16)
            msq = jnp.dot(sq, ones_ref[:wd, :wd], preferred_element_type=F32) * (1.0 / HEAD_DIM)
            y = (y * lax.rsqrt(msq + NORM_EPS)) * hg_ref[:, c0:c0 + wd]
        if transposed:
            outs[oi][0] = y.T
        else:
            outs[oi][...] = y
        oi += 1
        if want_bf16:
            outs[oi][...] = y.astype(BF16)
            oi += 1
        if want_blockmean:
            for bi in range(tm // MOBA_BLOCK):
                outs[oi][bi] = jnp.mean(y[bi * MOBA_BLOCK:(bi + 1) * MOBA_BLOCK], axis=0, keepdims=True)
            oi += 1


def _norm_proj(x, gain, w_bf, head_gain, ones_bd, segs, tm, rows_per_batch=None):
    n, d = x.shape
    wtot = w_bf.shape[1]
    assert n % tm == 0
    out_shape, out_specs = [], []
    for (c0, wd, headnorm, transposed, want_bf16, want_blockmean) in segs:
        if transposed:
            assert rows_per_batch % tm == 0 and n % rows_per_batch == 0
            steps = rows_per_batch // tm
            out_shape.append(jax.ShapeDtypeStruct((n // rows_per_batch, wd, rows_per_batch), F32))
            out_specs.append(pl.BlockSpec((1, wd, tm), lambda i, steps=steps: (i // steps, 0, i % steps)))
        else:
            out_shape.append(jax.ShapeDtypeStruct((n, wd), F32))
            out_specs.append(pl.BlockSpec((tm, wd), lambda i: (i, 0)))
        if want_bf16:
            out_shape.append(jax.ShapeDtypeStruct((n, wd), BF16))
            out_specs.append(pl.BlockSpec((tm, wd), lambda i: (i, 0)))
        if want_blockmean:
            assert tm % MOBA_BLOCK == 0
            nb = tm // MOBA_BLOCK
            out_shape.append(jax.ShapeDtypeStruct((n // MOBA_BLOCK, 1, wd), F32))
            out_specs.append(pl.BlockSpec((nb, 1, wd), lambda i: (i, 0, 0)))
    return pl.pallas_call(
        functools.partial(_norm_proj_body, segs=segs, tm=tm),
        grid=(n // tm,),
        in_specs=[
            pl.BlockSpec((tm, d), lambda i: (i, 0)),
            pl.BlockSpec((1, d), lambda i: (0, 0)),
            pl.BlockSpec((d, wtot), lambda i: (0, 0)),
            pl.BlockSpec((1, wtot), lambda i: (0, 0)),
            pl.BlockSpec(ones_bd.shape, lambda i: (0, 0)),
        ],
        out_specs=out_specs,
        out_shape=out_shape,
        compiler_params=_cparams(("parallel",), 48),
        name="norm_proj",
    )(x, gain.reshape(1, d), w_bf, head_gain.reshape(1, wtot), ones_bd)


def _pool_windows(win, pos0):
    r = win.shape[0] - 16
    lane = lax.broadcasted_iota(jnp.int32, (r, LANES), 1)
    pos1 = pos0 + lax.broadcasted_iota(jnp.int32, (r, LANES), 0) + 1
    low = lane < POOL_GROUP_WIDTH
    a = win[:, :LANES]
    b = win[:, LANES:]
    a2 = a + pltpu.roll(a, 1, 0)
    a4 = a2 + pltpu.roll(a2, 2, 0)
    b2 = b + pltpu.roll(b, 1, 0)
    b4 = b2 + pltpu.roll(b2, 2, 0)
    b8 = b4 + pltpu.roll(b4, 4, 0)
    b16 = b8 + pltpu.roll(b8, 8, 0)
    cnt_a = jnp.minimum(jnp.where(low, POOL_WINDOWS[0], POOL_WINDOWS[1]), pos1).astype(F32)
    cnt_b = jnp.minimum(jnp.where(low, POOL_WINDOWS[2], POOL_WINDOWS[3]), pos1).astype(F32)
    pa = jnp.where(low, a2[16:], a4[16:]) / cnt_a - a[16:]
    pb = jnp.where(low, b8[16:], b16[16:]) / cnt_b - b[16:]
    return jnp.concatenate([pa, pb], axis=1)


def _pool_finish(pooled, wbd_ref, ps_ref, og_ref):
    mixed = jnp.dot(pooled.astype(BF16), wbd_ref[...], preferred_element_type=F32) * ps_ref[...]
    return _rms(mixed) * og_ref[...]


def _pool_body(u_ref, buf_ref, wbd_ref, ps_ref, og_ref, o_ref, ext_sc, *, bb, t, r, pos0):
    for bi in range(bb):
        ext_sc[bi, 0:16, :] = buf_ref[bi]
        ext_sc[bi, 16:, :] = u_ref[bi]
    if t == r:
        pooled = [_pool_windows(ext_sc[bi], pos0) for bi in range(bb)]
        out = _pool_finish(jnp.concatenate(pooled, axis=0), wbd_ref, ps_ref, og_ref)
        for bi in range(bb):
            o_ref[bi] = out[bi * r:(bi + 1) * r]
    else:
        assert bb == 1

        def chunk(c, carry):
            base = pl.multiple_of(c * r, r)
            pooled = _pool_windows(ext_sc[0, pl.ds(base, r + 16), :], pos0 + base)
            o_ref[0, pl.ds(base, r), :] = _pool_finish(pooled, wbd_ref, ps_ref, og_ref)
            return carry

        lax.fori_loop(0, t // r, chunk, 0)


def _pool(u, buf16, wbd_bf, pool_scale, og, pos0, bb, r):
    b, t, c = u.shape
    assert b % bb == 0 and t % r == 0
    return pl.pallas_call(
        functools.partial(_pool_body, bb=bb, t=t, r=r, pos0=pos0),
        grid=(b // bb,),
        in_specs=[
            pl.BlockSpec((bb, t, c), lambda i: (i, 0, 0)),
            pl.BlockSpec((bb, 16, c), lambda i: (i, 0, 0)),
            pl.BlockSpec((c, c), lambda i: (0, 0)),
            pl.BlockSpec((1, c), lambda i: (0, 0)),
            pl.BlockSpec((1, c), lambda i: (0, 0)),
        ],
        out_specs=pl.BlockSpec((bb, t, c), lambda i: (i, 0, 0)),
        out_shape=jax.ShapeDtypeStruct((b, t, c), F32),
        scratch_shapes=[pltpu.VMEM((bb, t + 16, c), F32)],
        compiler_params=_cparams(("parallel",), 40),
        name="pool",
    )(u, buf16, wbd_bf, pool_scale.reshape(1, c), og.reshape(1, c))


def _alibi_slope(h):
    return 2.0 ** (-8.0 * (h + 1) / MOBA_HEADS)


def _moba_prompt_body(q_ref, k_ref, v_ref, km_ref, og_ref, o_ref, m_sc, l_sc, acc_sc, sel_sc, qst_sc, *, nblk):
    i = pl.program_id(1)
    tq = MOBA_BLOCK
    n_pairs = MOBA_HEADS // 2
    nb_rows = -(-nblk // 8) * 8
    lane = lax.broadcasted_iota(jnp.int32, (tq, LANES), 1)
    low = lane < HEAD_DIM
    row2 = lax.broadcasted_iota(jnp.int32, (tq, MOBA_BLOCK), 0)
    col2 = lax.broadcasted_iota(jnp.int32, (tq, MOBA_BLOCK), 1)
    causal = col2 <= row2
    eye_bf = jnp.where(row2 == col2, 1.0, 0.0).astype(BF16)
    blk_row = lax.broadcasted_iota(jnp.int32, (nb_rows, tq), 0)
    blk_row_f = blk_row.astype(F32)
    colrow = lax.broadcasted_iota(jnp.int32, (1, MOBA_BLOCK), 1).astype(F32)

    def pair_cols(pr):
        return slice(pr * HEAD_PAIR, (pr + 1) * HEAD_PAIR)

    def scores(pr, j):
        start = pl.multiple_of(j * MOBA_BLOCK, MOBA_BLOCK)
        kj = k_ref[pl.ds(start, MOBA_BLOCK), pair_cols(pr)]
        s = lax.dot_general(qst_sc[pr], kj, NT_DIMS, preferred_element_type=F32)
        off = colrow + ((j - i) * MOBA_BLOCK).astype(F32)
        return [s[hh * tq:(hh + 1) * tq] + _alibi_slope(2 * pr + hh) * off for hh in range(2)]

    def pv(pr, j, ps):
        start = pl.multiple_of(j * MOBA_BLOCK, MOBA_BLOCK)
        vj = v_ref[pl.ds(start, MOBA_BLOCK), pair_cols(pr)]
        return jnp.dot(jnp.concatenate(ps, axis=0).astype(BF16), vj, preferred_element_type=F32)

    for pr in range(n_pairs):
        q_pair = q_ref[:, pair_cols(pr)]
        km_pair = km_ref[0, :, pair_cols(pr)]
        q_heads = [jnp.where(low, q_pair, 0.0), jnp.where(low, 0.0, q_pair)]

        km_hi = km_pair.astype(BF16)
        km_lo = (km_pair - km_hi.astype(F32)).astype(BF16)
        for hh in range(2):
            q_hi = q_heads[hh].astype(BF16)
            q_lo = (q_heads[hh] - q_hi.astype(F32)).astype(BF16)
            gate_t = (lax.dot_general(km_hi, q_hi, NT_DIMS, preferred_element_type=F32)
                      + (lax.dot_general(km_hi, q_lo, NT_DIMS, preferred_element_type=F32)
                         + lax.dot_general(km_lo, q_hi, NT_DIMS, preferred_element_type=F32)))[:nb_rows]
            g = jnp.where(blk_row < i, gate_t, NEG_INF)
            sel_t = jnp.zeros((nb_rows, tq), F32)
            for _ in range(MOBA_TOPK):
                mx = jnp.max(g, axis=0, keepdims=True)
                idx = jnp.min(jnp.where(g == mx, blk_row_f, float(LANES)), axis=0, keepdims=True)
                pick = (blk_row_f == idx) & (mx > NEG_INF)
                sel_t = jnp.where(pick, 1.0, sel_t)
                g = jnp.where(pick, NEG_INF, g)
            sel_t = jnp.concatenate([sel_t, jnp.zeros((LANES - nb_rows, tq), F32)], axis=0).astype(BF16)
            sel_sc[2 * pr + hh] = lax.dot_general(eye_bf, sel_t, NT_DIMS, preferred_element_type=F32)

        qst_sc[pr] = (jnp.concatenate(q_heads, axis=0) * QK_SCALE).astype(BF16)

    for pr in range(n_pairs):
        s_own = scores(pr, i)
        ps = []
        for hh in range(2):
            sh = jnp.where(causal, s_own[hh], NEG_INF)
            m = jnp.max(sh, axis=-1, keepdims=True)
            p = jnp.exp(sh - m)
            m_sc[2 * pr + hh] = jnp.broadcast_to(m, (tq, LANES))
            l_sc[2 * pr + hh] = jnp.broadcast_to(jnp.sum(p, axis=-1, keepdims=True), (tq, LANES))
            ps.append(p)
        acc_sc[pr] = pv(pr, i, ps)

    def past(j, carry):
        for pr in range(n_pairs):
            s_j = scores(pr, j)
            ps, alphas = [], []
            for hh in range(2):
                h = 2 * pr + hh
                selcol = jnp.sum(jnp.where(lane == j, sel_sc[h], 0.0), axis=-1, keepdims=True)
                sh = jnp.where(selcol > 0.0, s_j[hh], NEG_INF)
                m_prev = m_sc[h]
                m_new = jnp.maximum(m_prev, jnp.max(sh, axis=-1, keepdims=True))
                alpha = jnp.exp(m_prev - m_new)
                p = jnp.exp(sh - jnp.concatenate([m_new, m_new], axis=1))
                l_sc[h] = alpha * l_sc[h] + jnp.sum(p, axis=-1, keepdims=True)
                m_sc[h] = m_new
                ps.append(p)
                alphas.append(alpha)
            acc_sc[pr] = jnp.concatenate(alphas, axis=0) * acc_sc[pr] + pv(pr, j, ps)
        return carry

    lax.fori_loop(0, i, past, 0)

    outs = []
    for pr in range(n_pairs):
        acc = acc_sc[pr]
        outs.append(jnp.where(low, acc[:tq] / l_sc[2 * pr], acc[tq:] / l_sc[2 * pr + 1]))
    o_ref[...] = _rms(jnp.concatenate(outs, axis=1)) * og_ref[...]


def _moba_prompt(q, k_bf, v_bf, kmean_pad, og, b, t):
    n, w = q.shape
    nblk = t // MOBA_BLOCK
    tq = MOBA_BLOCK
    return pl.pallas_call(
        functools.partial(_moba_prompt_body, nblk=nblk),
        grid=(b, nblk),
        in_specs=[
            pl.BlockSpec((tq, w), lambda bi, i: (bi * nblk + i, 0)),
            pl.BlockSpec((t, w), lambda bi, i: (bi, 0)),
            pl.BlockSpec((t, w), lambda bi, i: (bi, 0)),
            pl.BlockSpec((1, LANES, w), lambda bi, i: (bi, 0, 0)),
            pl.BlockSpec((1, w), lambda bi, i: (0, 0)),
        ],
        out_specs=pl.BlockSpec((tq, w), lambda bi, i: (bi * nblk + i, 0)),
        out_shape=jax.ShapeDtypeStruct((n, w), F32),
        scratch_shapes=[
            pltpu.VMEM((MOBA_HEADS, tq, LANES), F32),
            pltpu.VMEM((MOBA_HEADS, tq, LANES), F32),
            pltpu.VMEM((MOBA_HEADS // 2, 2 * tq, LANES), F32),
            pltpu.VMEM((MOBA_HEADS, tq, LANES), F32),
            pltpu.VMEM((MOBA_HEADS // 2, 2 * tq, LANES), BF16),
        ],
        compiler_params=_cparams(("parallel", "arbitrary"), 40),
        name="moba_prompt",
    )(q, k_bf, v_bf, kmean_pad, og.reshape(1, w))


SAMPLE_BLOCKS_PER_STEP = 8
PAGES_PER_BLOCK = MOBA_BLOCK // PAGE_SIZE
PAGES_PER_STEP = SAMPLE_BLOCKS_PER_STEP * PAGES_PER_BLOCK


def _moba_sample_body(pt_ref, q_ref, kn_ref, vn_ref, slope_ref, og_ref, *rest, n_past_blocks, t_new):
    kp = rest[:PAGES_PER_STEP]
    vp = rest[PAGES_PER_STEP:2 * PAGES_PER_STEP]
    o_ref, o_sc, m_sc, l_sc, kmt_sc = rest[2 * PAGES_PER_STEP:]
    c = pl.program_id(1)
    rows = MOBA_HEADS * t_new
    row_h = lax.broadcasted_iota(jnp.int32, (rows, MOBA_WIDTH), 0) // t_new
    lane_h = lax.broadcasted_iota(jnp.int32, (rows, MOBA_WIDTH), 1) // HEAD_DIM
    bd = row_h == lane_h
    q = q_ref[...]
    qbd = jnp.where(bd, jnp.concatenate([q] * MOBA_HEADS, axis=0), 0.0)
    qbd_bf = (qbd * QK_SCALE).astype(BF16)
    slope = slope_ref[...]
    slope2 = jnp.concatenate([slope, slope], axis=1)
    colf = lax.broadcasted_iota(jnp.int32, (rows, MOBA_BLOCK), 1).astype(F32)

    @pl.when(c == 0)
    def _init():
        kmt_sc[...] = jnp.zeros(kmt_sc.shape, F32)

    kt_bf = []
    for jj in range(SAMPLE_BLOCKS_PER_STEP):
        kt_pages = [kp[PAGES_PER_BLOCK * jj + p][0] for p in range(PAGES_PER_BLOCK)]
        ksum = kt_pages[0]
        for kt in kt_pages[1:]:
            ksum = ksum + kt
        kmt_sc[c, :, jj:jj + 1] = jnp.sum(ksum, axis=-1, keepdims=True) * (1.0 / MOBA_BLOCK)
        kt_bf.append(jnp.concatenate(kt_pages, axis=1).astype(BF16))
    s_all = jnp.dot(qbd_bf, jnp.concatenate(kt_bf, axis=1), preferred_element_type=F32)

    for jj in range(SAMPLE_BLOCKS_PER_STEP):
        j = c * SAMPLE_BLOCKS_PER_STEP + jj
        s = s_all[:, jj * MOBA_BLOCK:(jj + 1) * MOBA_BLOCK] + slope2 * (
            colf + ((j - n_past_blocks) * MOBA_BLOCK).astype(F32))
        m = jnp.max(s, axis=-1, keepdims=True)
        p = jnp.exp(s - m)
        vt_bf = jnp.concatenate([vp[PAGES_PER_BLOCK * jj + q][0] for q in range(PAGES_PER_BLOCK)],
                                axis=1).astype(BF16)
        o = lax.dot_general(p.astype(BF16), vt_bf, NT_DIMS, preferred_element_type=F32)
        o_sc[j] = jnp.where(bd, o, 0.0)
        m_sc[j] = jnp.broadcast_to(m, (rows, LANES))
        l_sc[j] = jnp.broadcast_to(jnp.sum(p, axis=-1, keepdims=True), (rows, LANES))

    @pl.when(c == pl.num_programs(1) - 1)
    def _finish():
        tq = lax.broadcasted_iota(jnp.int32, (rows, LANES), 0) % t_new
        kn = kn_ref[...]
        vn = vn_ref[...]
        qs = qbd * QK_SCALE
        s_own = []
        m_run = jnp.full((rows, LANES), NEG_INF, F32)
        for cc in range(t_new):
            sc = jnp.sum(qs * kn[cc:cc + 1, :], axis=-1, keepdims=True) + slope * float(cc)
            sc = jnp.where(tq >= cc, sc, NEG_INF)
            s_own.append(sc)
            m_run = jnp.maximum(m_run, sc)

        blk = lax.broadcasted_iota(jnp.int32, (rows, LANES), 1)
        blk_f = blk.astype(F32)
        gates = jnp.zeros((rows, LANES), F32)
        for cs in range(n_past_blocks // SAMPLE_BLOCKS_PER_STEP):
            g_cs = jnp.dot(qbd, kmt_sc[cs], precision=lax.Precision.HIGHEST, preferred_element_type=F32)
            g_cs = jnp.where(blk < SAMPLE_BLOCKS_PER_STEP, g_cs, 0.0)
            gates = gates + (pltpu.roll(g_cs, cs * SAMPLE_BLOCKS_PER_STEP, 1) if cs else g_cs)
        g = jnp.where(blk < n_past_blocks, gates, NEG_INF)
        sel = jnp.zeros((rows, LANES), F32)
        for _ in range(MOBA_TOPK):
            mx = jnp.max(g, axis=-1, keepdims=True)
            idx = jnp.min(jnp.where(g == mx, blk_f, float(LANES)), axis=-1, keepdims=True)
            pick = (blk_f == idx) & (mx > NEG_INF)
            sel = jnp.where(pick, 1.0, sel)
            g = jnp.where(pick, NEG_INF, g)

        m_all = jnp.full((rows, LANES), NEG_INF, F32)
        l_all = jnp.zeros((rows, LANES), F32)
        for j in range(n_past_blocks):
            m_all = jnp.where(blk == j, m_sc[j], m_all)
            l_all = jnp.where(blk == j, l_sc[j], l_all)
        m_run = jnp.maximum(m_run, jnp.max(jnp.where(sel > 0.0, m_all, NEG_INF), axis=-1, keepdims=True))
        w_all = jnp.where(sel > 0.0, jnp.exp(m_all - m_run), 0.0)

        l_run = jnp.broadcast_to(jnp.sum(w_all * l_all, axis=-1, keepdims=True), (rows, LANES))
        o_run = jnp.zeros((rows, MOBA_WIDTH), F32)
        for cc in range(t_new):
            p = jnp.exp(s_own[cc] - m_run)
            l_run = l_run + p
            o_run = o_run + jnp.concatenate([p] * (MOBA_WIDTH // LANES), axis=1) * vn[cc:cc + 1, :]
        o_run = jnp.where(bd, o_run, 0.0)
        for j in range(n_past_blocks):
            o_run = o_run + w_all[:, j:j + 1] * o_sc[j]
        o_bd = o_run / jnp.concatenate([l_run] * (MOBA_WIDTH // LANES), axis=1)
        attn = o_bd[0:t_new]
        for h in range(1, MOBA_HEADS):
            attn = attn + o_bd[h * t_new:(h + 1) * t_new]
        o_ref[...] = _rms(attn) * og_ref[...]


def _moba_sample(page_table, q, k_new, v_new, cache_kt, cache_vt, og, b, t_new):
    n_pages = page_table.shape[1]
    n_past_blocks = n_pages // PAGES_PER_BLOCK
    assert n_past_blocks <= LANES
    assert n_past_blocks % SAMPLE_BLOCKS_PER_STEP == 0
    n_steps = n_past_blocks // SAMPLE_BLOCKS_PER_STEP
    rows = MOBA_HEADS * t_new
    w = MOBA_WIDTH
    slopes = jnp.exp2(-8.0 * jnp.arange(1, MOBA_HEADS + 1, dtype=F32) / MOBA_HEADS)
    slope_rows = jnp.broadcast_to(jnp.repeat(slopes, t_new)[:, None], (rows, LANES))

    def page_spec(p):
        return pl.BlockSpec((1, w, PAGE_SIZE), lambda bi, c, pt: (pt[bi, c * PAGES_PER_STEP + p], 0, 0))

    row_spec = pl.BlockSpec((t_new, w), lambda bi, c, pt: (bi, 0))
    grid_spec = pltpu.PrefetchScalarGridSpec(
        num_scalar_prefetch=1,
        grid=(b, n_steps),
        in_specs=[row_spec, row_spec, row_spec,
                  pl.BlockSpec((rows, LANES), lambda bi, c, pt: (0, 0)),
                  pl.BlockSpec((1, w), lambda bi, c, pt: (0, 0))]
                 + [page_spec(p) for p in range(PAGES_PER_STEP)]
                 + [page_spec(p) for p in range(PAGES_PER_STEP)],
        out_specs=row_spec,
        scratch_shapes=[
            pltpu.VMEM((n_past_blocks, rows, w), F32),
            pltpu.VMEM((n_past_blocks, rows, LANES), F32),
            pltpu.VMEM((n_past_blocks, rows, LANES), F32),
            pltpu.VMEM((n_steps, w, LANES), F32),
        ],
    )
    return pl.pallas_call(
        functools.partial(_moba_sample_body, n_past_blocks=n_past_blocks, t_new=t_new),
        grid_spec=grid_spec,
        out_shape=jax.ShapeDtypeStruct((b * t_new, w), F32),
        compiler_params=_cparams(("parallel", "arbitrary"), 48),
        name="moba_sample",
    )(page_table, q, k_new, v_new, slope_rows, og.reshape(1, w),
      *([cache_kt] * PAGES_PER_STEP), *([cache_vt] * PAGES_PER_STEP))


def _mem_attn_body(q_ref, mk_ref, mv_ref, og_ref, o_ref, *, tm):
    lane = lax.broadcasted_iota(jnp.int32, (tm, LANES), 1)
    low = lane < HEAD_DIM
    outs = []
    for pr in range(MEM_HEADS // 2):
        cs = slice(pr * HEAD_PAIR, (pr + 1) * HEAD_PAIR)
        q_pair = q_ref[:, cs]
        qst = (jnp.concatenate([jnp.where(low, q_pair, 0.0), jnp.where(low, 0.0, q_pair)], axis=0)
               * QK_SCALE).astype(BF16)
        s = lax.dot_general(qst, mk_ref[0, :, cs].astype(BF16), NT_DIMS, preferred_element_type=F32)
        p = jnp.exp(s - jnp.max(s, axis=-1, keepdims=True))
        l = jnp.sum(p, axis=-1, keepdims=True)
        o = jnp.dot(p.astype(BF16), mv_ref[0, :, cs].astype(BF16), preferred_element_type=F32) / l
        outs.append(jnp.where(low, o[:tm], o[tm:]))
    o_ref[...] = _rms(jnp.concatenate(outs, axis=1)) * og_ref[...]


def _mem_attn(qm, mem_k, mem_v, og, b, t, tm):
    n, w = qm.shape
    assert t % tm == 0
    steps = t // tm
    return pl.pallas_call(
        functools.partial(_mem_attn_body, tm=tm),
        grid=(b, steps),
        in_specs=[
            pl.BlockSpec((tm, w), lambda bi, i: (bi * steps + i, 0)),
            pl.BlockSpec((1, MEM_TOKENS, w), lambda bi, i: (bi, 0, 0)),
            pl.BlockSpec((1, MEM_TOKENS, w), lambda bi, i: (bi, 0, 0)),
            pl.BlockSpec((1, w), lambda bi, i: (0, 0)),
        ],
        out_specs=pl.BlockSpec((tm, w), lambda bi, i: (bi * steps + i, 0)),
        out_shape=jax.ShapeDtypeStruct((n, w), F32),
        compiler_params=_cparams(("parallel", "arbitrary"), 32),
        name="mem_attn",
    )(qm, mem_k, mem_v, og.reshape(1, w))


ROUTER_GROUP_LANE0 = N_EXPERTS
ROUTER_GID_LANE = 64


def _outproj_router_body(op_ref, oa_ref, om_ref, x_ref, wo_ref, g2_ref, wr_ref, br_ref,
                         x1_ref, h2_ref, comb_ref, gcnt_ref, *, tm):
    a0, a1 = POOL_WIDTH, POOL_WIDTH + MOBA_WIDTH
    y = x_ref[...]
    y = y + jnp.dot(op_ref[...].astype(BF16), wo_ref[0:a0, :], preferred_element_type=F32)
    y = y + jnp.dot(oa_ref[...].astype(BF16), wo_ref[a0:a1, :], preferred_element_type=F32)
    y = y + jnp.dot(om_ref[...].astype(BF16), wo_ref[a1:, :], preferred_element_type=F32)
    x1_ref[...] = y
    h2 = _rms(y) * g2_ref[...]
    h2_hi = h2.astype(BF16)
    h2_lo = (h2 - h2_hi.astype(F32)).astype(BF16)
    h2_ref[...] = h2_hi
    hw =jnp.dot(h2_hi, wr_ref[...], preferred_element_type=F32)
    lw = jnp.dot(h2_lo, wr_ref[:, :LANES], preferred_element_type=F32)
    logits = (hw[:, :LANES] + (hw[:, LANES:] + lw)) + br_ref[...]
    lane_f = lax.broadcasted_iota(jnp.int32, (tm, LANES), 1).astype(F32)
    big = float(LANES)
    g_lo = float(ROUTER_GROUP_LANE0)
    is_g = (lane_f >= g_lo) & (lane_f < g_lo + MOE_GROUPS)
    lg = jnp.where(is_g, logits, NEG_INF)
    mg = jnp.max(lg, axis=-1, keepdims=True)
    pg_top = 1.0 / jnp.sum(jnp.exp(lg - mg), axis=-1, keepdims=True)
    gidx = jnp.min(jnp.where(lg == mg, lane_f, big), axis=-1, keepdims=True) - g_lo
    e_lo = gidx * EXPERTS_PER_GROUP
    in_grp = (lane_f >= e_lo) & (lane_f < e_lo + EXPERTS_PER_GROUP)
    le = jnp.where(in_grp, logits, NEG_INF)
    m1 = jnp.max(le, axis=-1, keepdims=True)
    se = jnp.sum(jnp.exp(le - m1), axis=-1, keepdims=True)
    i1 = jnp.min(jnp.where(le == m1, lane_f, big), axis=-1, keepdims=True)
    le2 = jnp.where(lane_f == i1, NEG_INF, le)
    m2 = jnp.max(le2, axis=-1, keepdims=True)
    i2 = jnp.min(jnp.where(le2 == m2, lane_f, big), axis=-1, keepdims=True)
    p1 = 1.0 / se
    p2 = jnp.exp(m2 - m1) / se
    den = p1 + p2
    comb_ref[...] = jnp.where(lane_f == i1, pg_top * (p1 / den),
                              jnp.where(lane_f == i2, pg_top * (p2 / den),
                                        jnp.where(lane_f == float(ROUTER_GID_LANE), gidx, 0.0)))
    gcnt_ref[0] = jnp.sum(jnp.where(lane_f == gidx, 1.0, 0.0), axis=0, keepdims=True)


def _outproj_router(o_pool, o_attn, o_mem, x, wo_bf, g2, wr, br, tm):
    n, d = x.shape
    assert n % tm == 0
    row = lambda wdt: pl.BlockSpec((tm, wdt), lambda i: (i, 0))
    full = lambda shp: pl.BlockSpec(shp, lambda i: (0, 0))
    return pl.pallas_call(
        functools.partial(_outproj_router_body, tm=tm),
        grid=(n // tm,),
        in_specs=[row(POOL_WIDTH), row(MOBA_WIDTH), row(MEM_WIDTH), row(d),
                  full((d, d)), full((1, d)), full((d, 2 * LANES)), full((1, LANES))],
        out_specs=[row(d), row(d), row(LANES), pl.BlockSpec((1, 1, LANES), lambda i: (i, 0, 0))],
        out_shape=[jax.ShapeDtypeStruct((n, d), F32), jax.ShapeDtypeStruct((n, d), BF16),
                   jax.ShapeDtypeStruct((n, LANES), F32), jax.ShapeDtypeStruct((n // tm, 1, LANES), F32)],
        compiler_params=_cparams(("parallel",), 40),
        name="outproj_router",
    )(o_pool, o_attn, o_mem, x, wo_bf, g2.reshape(1, d), wr, br)


MOE_EXPERTS_PER_STEP = 4
MOE_SORT_TILE = 1024
MOE_RUN_ALIGN = 16
MOE_SORT_PAD = LANES


def _moe_window(tm):
    return max(LANES, tm // MOE_GROUPS + 32)


def _moe_body(cnt_ref, h2_ref, comb_ref, x1_hbm, wg_ref, wu_ref, wd_ref, y_ref,
              xs_sc, cs_sc, ys_sc, dest_sc, sem, *, tm, half, win):
    i = pl.program_id(0)
    s = pl.program_id(1)
    eps = MOE_EXPERTS_PER_STEP
    n_half = tm // half
    span = half + MOE_SORT_PAD
    g = s // (EXPERTS_PER_GROUP // eps)

    def run_starts(a):
        starts, nxt = [], jnp.int32(0)
        for gg in range(MOE_GROUPS):
            starts.append(nxt)
            cnt = cnt_ref[(i * n_half + a) * MOE_GROUPS + gg]
            nxt = nxt + (cnt + (MOE_RUN_ALIGN - 1)) // MOE_RUN_ALIGN * MOE_RUN_ALIGN
        return starts

    x1_copy = pltpu.make_async_copy(x1_hbm.at[pl.ds(pl.multiple_of(i * tm, tm), tm)], y_ref, sem)

    @pl.when(s == 0)
    def _sort():
        x1_copy.start()
        lane = lax.broadcasted_iota(jnp.int32, (half, LANES), 1)
        lane_f = lane.astype(F32)
        earlier_tok = (lax.broadcasted_iota(jnp.int32, (half, half), 1)
                       < lax.broadcasted_iota(jnp.int32, (half, half), 0))
        earlier_bf = jnp.where(earlier_tok, 1.0, 0.0).astype(BF16)
        r16 = lax.broadcasted_iota(jnp.int32, (16, LANES), 0)
        l16 = lax.broadcasted_iota(jnp.int32, (16, LANES), 1)
        pick = jnp.where((r16 == 0) & (l16 == 0), 32.0, jnp.where((r16 == 0) & (l16 == 1), 1.0, 0.0)).astype(BF16)
        sorted_row = lax.broadcasted_iota(jnp.int32, (span, half), 0).astype(F32)
        for a in range(n_half):
            rows = slice(a * half, (a + 1) * half)
            starts = run_starts(a)
            comb = comb_ref[rows, :]
            gid = jnp.sum(jnp.where(lane == ROUTER_GID_LANE, comb, 0.0), axis=-1, keepdims=True)
            onehot = jnp.where(lane_f == gid, 1.0, 0.0)
            earlier = jnp.dot(earlier_bf, onehot.astype(BF16), preferred_element_type=F32)
            start_vec = jnp.zeros((half, LANES), F32)
            for gg in range(MOE_GROUPS):
                start_vec = jnp.where(lane == gg, starts[gg].astype(F32), start_vec)
            dest = jnp.sum(onehot * (earlier + start_vec), axis=-1, keepdims=True)
            dest_sc[a] = jnp.broadcast_to(dest, (half, LANES))
            hi = jnp.floor(dest * (1.0 / 32.0))
            lo = dest - 32.0 * hi
            dest_cols = jnp.where(lane == 0, hi, jnp.where(lane == 1, lo, 0.0)).astype(BF16)
            dest_row = lax.dot_general(pick, dest_cols, NT_DIMS, preferred_element_type=F32)[0:1, :]
            p = jnp.where(sorted_row == dest_row, 1.0, 0.0).astype(BF16)
            xs_sc[a, 0:span, :] = jnp.dot(p, h2_ref[rows, :], preferred_element_type=F32).astype(BF16)
            xs_sc[a, span:, :] = jnp.zeros((win, D_MODEL), BF16)
            c1 = comb.astype(BF16)
            r1 = comb - c1.astype(F32)
            c2 = r1.astype(BF16)
            c3 = (r1 - c2.astype(F32)).astype(BF16)
            cs_sc[a, 0:span, :] = (jnp.dot(p, c1, preferred_element_type=F32)
                                   + jnp.dot(p, c2, preferred_element_type=F32)
                                   + jnp.dot(p, c3, preferred_element_type=F32))
            cs_sc[a, span:, :] = jnp.zeros((win, LANES), F32)
            ys_sc[a] = jnp.zeros(ys_sc.shape[1:], F32)

    lane_w = lax.broadcasted_iota(jnp.int32, (win, LANES), 1)
    for a in range(n_half):
        starts = run_starts(a)
        run_start = starts[0]
        for gg in range(1, MOE_GROUPS):
            run_start = jnp.where(g == gg, starts[gg], run_start)
        run_len = cnt_ref[(i * n_half + a) * MOE_GROUPS + g]

        def window(w, carry, a=a, run_start=run_start):
            r0 = pl.multiple_of(run_start + w * win, MOE_RUN_ALIGN)
            rows = xs_sc[a, pl.ds(r0, win), :]
            comb_rows = cs_sc[a, pl.ds(r0, win), :]
            acc = None
            for j in range(eps):
                e = s * eps + j
                u = jnp.dot(rows, wg_ref[j], preferred_element_type=F32)
                v = jnp.dot(rows, wu_ref[j], preferred_element_type=F32)
                ce = jnp.sum(jnp.where(lane_w == e, comb_rows, 0.0), axis=-1, keepdims=True)
                hid = ((u * jax.nn.sigmoid(u)) * v * ce).astype(BF16)
                c = jnp.dot(hid, wd_ref[j], preferred_element_type=F32)
                acc = c if acc is None else acc + c
            ys_sc[a, pl.ds(r0, win), :] += acc
            return carry

        lax.fori_loop(0, (run_len + win - 1) // win, window, 0)

    @pl.when(s == pl.num_programs(1) - 1)
    def _unsort():
        x1_copy.wait()
        sorted_lane = lax.broadcasted_iota(jnp.int32, (half, span), 1).astype(F32)
        for a in range(n_half):
            rows = slice(a * half, (a + 1) * half)
            pt = jnp.where(sorted_lane == dest_sc[a][:, 0:1], 1.0, 0.0).astype(BF16)
            moe = jnp.dot(pt, ys_sc[a, 0:span, :].astype(BF16), preferred_element_type=F32)
            y_ref[rows, :] = y_ref[rows, :] + moe


def _moe(tile_cnt, h2, comb, x1, wg_bf, wu_bf, wd_bf, tm, half):
    n, d = x1.shape
    eps = MOE_EXPERTS_PER_STEP
    assert n % tm == 0 and tm % half == 0 and EXPERTS_PER_GROUP % eps == 0
    assert MOE_GROUPS * (MOE_RUN_ALIGN - 1) <= MOE_SORT_PAD
    win = _moe_window(half)
    assert win % MOE_RUN_ALIGN == 0
    span = half + MOE_SORT_PAD
    n_half = tm // half
    once = dict(pipeline_mode=pl.Buffered(1)) if n_half > 1 else {}
    grid_spec = pltpu.PrefetchScalarGridSpec(
        num_scalar_prefetch=1,
        grid=(n // tm, N_EXPERTS // eps),
        in_specs=[
            pl.BlockSpec((tm, d), lambda i, s, cnt: (i, 0), **once),
            pl.BlockSpec((tm, LANES), lambda i, s, cnt: (i, 0), **once),
            pl.BlockSpec(memory_space=pl.ANY),
            pl.BlockSpec((eps, d, D_EXPERT), lambda i, s, cnt: (s, 0, 0)),
            pl.BlockSpec((eps, d, D_EXPERT), lambda i, s, cnt: (s, 0, 0)),
            pl.BlockSpec((eps, D_EXPERT, d), lambda i, s, cnt: (s, 0, 0)),
        ],
        out_specs=pl.BlockSpec((tm, d), lambda i, s, cnt: (i, 0), **once),
        scratch_shapes=[
            pltpu.VMEM((n_half, span + win, d), BF16),
            pltpu.VMEM((n_half, span + win, LANES), F32),
            pltpu.VMEM((n_half, span + win, d), F32),
            pltpu.VMEM((n_half, half, LANES), F32),
            pltpu.SemaphoreType.DMA(()),
        ],
    )
    return pl.pallas_call(
        functools.partial(_moe_body, tm=tm, half=half, win=win),
        grid_spec=grid_spec,
        out_shape=jax.ShapeDtypeStruct((n, d), F32),
        compiler_params=_cparams(("parallel", "arbitrary"), 58),
        name="moe",
    )(tile_cnt, h2, comb, x1, wg_bf, wu_bf, wd_bf)


def _block_diag_ones(width, group):
    r = lax.broadcasted_iota(jnp.int32, (width, width), 0) // group
    c = lax.broadcasted_iota(jnp.int32, (width, width), 1) // group
    return (r == c).astype(BF16)


def _pick_tile(n, candidates):
    for c in candidates:
        if n % c == 0:
            return c
    return n


IN_SEGS = (
    (0, POOL_WIDTH, False, False, False, False),
    (POOL_WIDTH, MOBA_WIDTH, True, False, False, False),
    (POOL_WIDTH + MOBA_WIDTH, MOBA_WIDTH, True, True, True, True),
    (POOL_WIDTH + 2 * MOBA_WIDTH, MOBA_WIDTH, False, True, True, False),
    (POOL_WIDTH + 3 * MOBA_WIDTH, MEM_WIDTH, True, False, False, False),
)
IN_SEGS_SAMPLE = tuple((c0, wd, hn, False, False, False) for (c0, wd, hn, _, _, _) in IN_SEGS)
MEMKV_SEGS = ((0, MEM_WIDTH, True, False, False, False), (MEM_WIDTH, MEM_WIDTH, False, False, False, False))


def kernel(x_prompt, x_sample, mem_prompt, cache_k, cache_v, state_pool, cache_mem_k, cache_mem_v,
           page_table, norm1_gain, w_in, pool_w, pool_scale, moba_q_gain, moba_k_gain,
           mem_norm_gain, w_mem_kv, mem_q_gain, mem_k_gain, out_gain, w_out, norm2_gain,
           router_group_w, router_group_b, router_expert_w, router_expert_b, w_gate, w_up, w_down):
    bp, tp, d = x_prompt.shape
    bs, ts, _ = x_sample.shape
    n_p, n_s = bp * tp, bs * ts

    w_in_bf = w_in.astype(BF16)
    w_mem_bf = w_mem_kv.astype(BF16)
    wo_bf = w_out.astype(BF16)
    ones_bd = _block_diag_ones(MOBA_WIDTH, HEAD_DIM)
    head_gain_in = jnp.concatenate([
        jnp.ones((POOL_WIDTH,), F32), jnp.tile(moba_q_gain, MOBA_HEADS), jnp.tile(moba_k_gain, MOBA_HEADS),
        jnp.ones((MOBA_WIDTH,), F32), jnp.tile(mem_q_gain, MEM_HEADS)])
    head_gain_mem = jnp.concatenate([jnp.tile(mem_k_gain, MEM_HEADS), jnp.ones((MEM_WIDTH,), F32)])
    wbd = jnp.zeros((POOL_WIDTH, POOL_WIDTH), F32)
    for g in range(len(POOL_WINDOWS)):
        sl = slice(g * POOL_GROUP_WIDTH, (g + 1) * POOL_GROUP_WIDTH)
        wbd = wbd.at[sl, sl].set(pool_w[g])
    wbd_bf = wbd.astype(BF16)
    og_pool, og_attn, og_mem = (out_gain[:POOL_WIDTH], out_gain[POOL_WIDTH:POOL_WIDTH + MOBA_WIDTH],
                                out_gain[POOL_WIDTH + MOBA_WIDTH:])
    wr = jnp.zeros((d, LANES), F32)
    wr = wr.at[:, :N_EXPERTS].set(router_expert_w).at[:, N_EXPERTS:N_EXPERTS + MOE_GROUPS].set(router_group_w)
    br = jnp.zeros((1, LANES), F32)
    br = br.at[0, :N_EXPERTS].set(router_expert_b).at[0, N_EXPERTS:N_EXPERTS + MOE_GROUPS].set(router_group_b)
    wr_hi = wr.astype(BF16)
    wr = jnp.concatenate([wr_hi, (wr - wr_hi.astype(F32)).astype(BF16)], axis=1)
    wg = w_gate.reshape(N_EXPERTS, d, D_EXPERT).astype(BF16)
    wu = w_up.reshape(N_EXPERTS, d, D_EXPERT).astype(BF16)
    wd = w_down.reshape(N_EXPERTS, D_EXPERT, d).astype(BF16)

    def tail(o_pool, o_attn, o_mem, x2d):
        n = x2d.shape[0]
        tm_r = _pick_tile(n, (512, 256))
        x1, h2, comb, gcnt = _outproj_router(o_pool, o_attn, o_mem, x2d, wo_bf, norm2_gain, wr, br, tm_r)
        tm = _pick_tile(n, (2048, 1024, 512, 256))
        half = min(tm, MOE_SORT_TILE)
        tile_cnt = gcnt[:, 0, :MOE_GROUPS].reshape(n // half, half // tm_r, MOE_GROUPS).sum(axis=1)
        return _moe(tile_cnt.astype(jnp.int32).reshape(-1), h2, comb, x1, wg, wu, wd, tm, half)

    mem_k_p, mem_v_p = _norm_proj(mem_prompt.reshape(bp * MEM_TOKENS, d), mem_norm_gain, w_mem_bf,
                                  head_gain_mem, ones_bd, MEMKV_SEGS, _pick_tile(bp * MEM_TOKENS, (512, 256)))
    xp = x_prompt.reshape(n_p, d)
    u_p, q_p, kt_p, k_p_bf, kmean_p, vt_p, v_p_bf, qm_p = _norm_proj(
        xp, norm1_gain, w_in_bf, head_gain_in, ones_bd, IN_SEGS, _pick_tile(tp, (512, 256)), rows_per_batch=tp)
    o_pool_p = _pool(u_p.reshape(bp, tp, POOL_WIDTH), jnp.zeros((bp, 16, POOL_WIDTH), F32), wbd_bf,
                     pool_scale, og_pool, 0, 1, MOBA_BLOCK).reshape(n_p, POOL_WIDTH)
    nblk = tp // MOBA_BLOCK
    kmean_pad = jnp.pad(kmean_p.reshape(bp, nblk, MOBA_WIDTH), ((0, 0), (0, LANES - nblk), (0, 0)))
    o_attn_p = _moba_prompt(q_p, k_p_bf, v_p_bf, kmean_pad, og_attn, bp, tp)
    o_mem_p = _mem_attn(qm_p, mem_k_p.reshape(bp, MEM_TOKENS, MEM_WIDTH),
                        mem_v_p.reshape(bp, MEM_TOKENS, MEM_WIDTH), og_mem, bp, tp, _pick_tile(tp, (512, 256)))
    y_p = tail(o_pool_p, o_attn_p, o_mem_p, xp)

    past_len = page_table.shape[1] * PAGE_SIZE
    xs = x_sample.reshape(n_s, d)
    u_s, q_s, k_s, v_s, qm_s = _norm_proj(xs, norm1_gain, w_in_bf, head_gain_in, ones_bd, IN_SEGS_SAMPLE, n_s)
    u_s3 = u_s.reshape(bs, ts, POOL_WIDTH)
    buf16 = jnp.concatenate([jnp.zeros((bs, 1, POOL_WIDTH), F32), state_pool], axis=1)
    o_pool_s = _pool(u_s3, buf16, wbd_bf, pool_scale, og_pool, past_len, bs, ts).reshape(n_s, POOL_WIDTH)
    o_attn_s = _moba_sample(page_table, q_s, k_s, v_s,
                            cache_k.transpose(0, 2, 3, 1).reshape(-1, MOBA_WIDTH, PAGE_SIZE),
                            cache_v.transpose(0, 2, 3, 1).reshape(-1, MOBA_WIDTH, PAGE_SIZE),
                            og_attn, bs, ts)
    o_mem_s = _mem_attn(qm_s, cache_mem_k.reshape(bs, MEM_TOKENS, MEM_WIDTH),
                        cache_mem_v.reshape(bs, MEM_TOKENS, MEM_WIDTH), og_mem, bs, ts, ts)
    y_s = tail(o_pool_s, o_attn_s, o_mem_s, xs)

    pool_prompt = u_p.reshape(bp, tp, POOL_WIDTH)[:, tp - POOL_STATE:]
    pool_sample = jnp.concatenate([state_pool, u_s3], axis=1)[:, -POOL_STATE:]
    k_p = kt_p.reshape(bp, MOBA_HEADS, HEAD_DIM, tp).transpose(0, 3, 1, 2)
    v_p = vt_p.reshape(bp, MOBA_HEADS, HEAD_DIM, tp).transpose(0, 3, 1, 2)
    return (y_p.reshape(bp, tp, d), y_s.reshape(bs, ts, d), k_p, v_p,
            pool_prompt,
            mem_k_p.reshape(bp, MEM_TOKENS, MEM_HEADS, HEAD_DIM), mem_v_p.reshape(bp, MEM_TOKENS, MEM_HEADS, HEAD_DIM),
            k_s.reshape(bs, ts, MOBA_HEADS, HEAD_DIM), v_s.reshape(bs, ts, MOBA_HEADS, HEAD_DIM),
            pool_sample)
```

```python
import functools

import jax
import jax.numpy as jnp
from jax import lax
from jax.experimental import pallas as pl
from jax.experimental.pallas import tpu as pltpu

F32 = jnp.float32
BF16 = jnp.bfloat16

D_MODEL = 1024
HEAD_DIM = 64
POOL_WIDTH = 256
POOL_WINDOWS = (2, 4, 8, 16)
POOL_GROUP_WIDTH = 64
POOL_STATE = 15
MOBA_WIDTH = 512
MOBA_HEADS = 8
MOBA_BLOCK = 256
MOBA_TOPK = 3
MEM_WIDTH = 256
MEM_HEADS = 4
MEM_TOKENS = 256
PAGE_SIZE = 128
MOE_GROUPS = 4
EXPERTS_PER_GROUP = 8
N_EXPERTS = MOE_GROUPS * EXPERTS_PER_GROUP
D_EXPERT = 256
NORM_EPS = 1e-6

LANES = 128
HEAD_PAIR = 2 * HEAD_DIM
QK_SCALE = HEAD_DIM ** -0.5
NEG_INF = float("-inf")
MIB = 1024 * 1024
NT_DIMS = (((1,), (1,)), ((), ()))


def _cparams(semantics, vmem_mib):
    return pltpu.CompilerParams(dimension_semantics=semantics, vmem_limit_bytes=vmem_mib * MIB)


def _rms(y, eps=NORM_EPS):
    return y * lax.rsqrt(jnp.mean(y * y, axis=-1, keepdims=True) + eps)


def _norm_proj_body(x_ref, g_ref, w_ref, hg_ref, ones_ref, *outs, segs, tm):
    hb = (_rms(x_ref[...]) * g_ref[...]).astype(BF16)
    oi = 0
    for (c0, wd, headnorm, transposed, want_bf16, want_blockmean) in segs:
        y = jnp.dot(hb, w_ref[:, c0:c0 + wd], preferred_element_type=F32)
        if headnorm:
            sq = (y * y).astype(BF16)
            msq = jnp.dot(sq, ones_ref[:wd, :wd], preferred_element_type=F32) * (1.0 / HEAD_DIM)
            y = (y * lax.rsqrt(msq + NORM_EPS)) * hg_ref[:, c0:c0 + wd]
        if transposed:
            outs[oi][0] = y.T
        else:
            outs[oi][...] = y
        oi += 1
        if want_bf16:
            outs[oi][...] = y.astype(BF16)
            oi += 1
        if want_blockmean:
            for bi in range(tm // MOBA_BLOCK):
                outs[oi][bi] = jnp.mean(y[bi * MOBA_BLOCK:(bi + 1) * MOBA_BLOCK], axis=0, keepdims=True)
            oi += 1


def _norm_proj(x, gain, w_bf, head_gain, ones_bd, segs, tm, rows_per_batch=None):
    n, d = x.shape
    wtot = w_bf.shape[1]
    assert n % tm == 0
    out_shape, out_specs = [], []
    for (c0, wd, headnorm, transposed, want_bf16, want_blockmean) in segs:
        if transposed:
            assert rows_per_batch % tm == 0 and n % rows_per_batch == 0
            steps = rows_per_batch // tm
            out_shape.append(jax.ShapeDtypeStruct((n // rows_per_batch, wd, rows_per_batch), F32))
            out_specs.append(pl.BlockSpec((1, wd, tm), lambda i, steps=steps: (i // steps, 0, i % steps)))
        else:
            out_shape.append(jax.ShapeDtypeStruct((n, wd), F32))
            out_specs.append(pl.BlockSpec((tm, wd), lambda i: (i, 0)))
        if want_bf16:
            out_shape.append(jax.ShapeDtypeStruct((n, wd), BF16))
            out_specs.append(pl.BlockSpec((tm, wd), lambda i: (i, 0)))
        if want_blockmean:
            assert tm % MOBA_BLOCK == 0
            nb = tm // MOBA_BLOCK
            out_shape.append(jax.ShapeDtypeStruct((n // MOBA_BLOCK, 1, wd), F32))
            out_specs.append(pl.BlockSpec((nb, 1, wd), lambda i: (i, 0, 0)))
    return pl.pallas_call(
        functools.partial(_norm_proj_body, segs=segs, tm=tm),
        grid=(n // tm,),
        in_specs=[
            pl.BlockSpec((tm, d), lambda i: (i, 0)),
            pl.BlockSpec((1, d), lambda i: (0, 0)),
            pl.BlockSpec((d, wtot), lambda i: (0, 0)),
            pl.BlockSpec((1, wtot), lambda i: (0, 0)),
            pl.BlockSpec(ones_bd.shape, lambda i: (0, 0)),
        ],
        out_specs=out_specs,
        out_shape=out_shape,
        compiler_params=_cparams(("parallel",), 48),
        name="norm_proj",
    )(x, gain.reshape(1, d), w_bf, head_gain.reshape(1, wtot), ones_bd)


def _pool_windows(win, pos0):
    r = win.shape[0] - 16
    lane = lax.broadcasted_iota(jnp.int32, (r, LANES), 1)
    pos1 = pos0 + lax.broadcasted_iota(jnp.int32, (r, LANES), 0) + 1
    low = lane < POOL_GROUP_WIDTH
    a = win[:, :LANES]
    b = win[:, LANES:]
    a2 = a + pltpu.roll(a, 1, 0)
    a4 = a2 + pltpu.roll(a2, 2, 0)
    b2 = b + pltpu.roll(b, 1, 0)
    b4 = b2 + pltpu.roll(b2, 2, 0)
    b8 = b4 + pltpu.roll(b4, 4, 0)
    b16 = b8 + pltpu.roll(b8, 8, 0)
    cnt_a = jnp.minimum(jnp.where(low, POOL_WINDOWS[0], POOL_WINDOWS[1]), pos1).astype(F32)
    cnt_b = jnp.minimum(jnp.where(low, POOL_WINDOWS[2], POOL_WINDOWS[3]), pos1).astype(F32)
    pa = jnp.where(low, a2[16:], a4[16:]) / cnt_a - a[16:]
    pb = jnp.where(low, b8[16:], b16[16:]) / cnt_b - b[16:]
    return jnp.concatenate([pa, pb], axis=1)


def _pool_finish(pooled, wbd_ref, ps_ref, og_ref):
    mixed = jnp.dot(pooled.astype(BF16), wbd_ref[...], preferred_element_type=F32) * ps_ref[...]
    return _rms(mixed) * og_ref[...]


def _pool_body(u_ref, buf_ref, wbd_ref, ps_ref, og_ref, o_ref, ext_sc, *, bb, t, r, pos0):
    for bi in range(bb):
        ext_sc[bi, 0:16, :] = buf_ref[bi]
        ext_sc[bi, 16:, :] = u_ref[bi]
    if t == r:
        pooled = [_pool_windows(ext_sc[bi], pos0) for bi in range(bb)]
        out = _pool_finish(jnp.concatenate(pooled, axis=0), wbd_ref, ps_ref, og_ref)
        for bi in range(bb):
            o_ref[bi] = out[bi * r:(bi + 1) * r]
    else:
        assert bb == 1

        def chunk(c, carry):
            base = pl.multiple_of(c * r, r)
            pooled = _pool_windows(ext_sc[0, pl.ds(base, r + 16), :], pos0 + base)
            o_ref[0, pl.ds(base, r), :] = _pool_finish(pooled, wbd_ref, ps_ref, og_ref)
            return carry

        lax.fori_loop(0, t // r, chunk, 0)


def _pool(u, buf16, wbd_bf, pool_scale, og, pos0, bb, r):
    b, t, c = u.shape
    assert b % bb == 0 and t % r == 0
    return pl.pallas_call(
        functools.partial(_pool_body, bb=bb, t=t, r=r, pos0=pos0),
        grid=(b // bb,),
        in_specs=[
            pl.BlockSpec((bb, t, c), lambda i: (i, 0, 0)),
            pl.BlockSpec((bb, 16, c), lambda i: (i, 0, 0)),
            pl.BlockSpec((c, c), lambda i: (0, 0)),
            pl.BlockSpec((1, c), lambda i: (0, 0)),
            pl.BlockSpec((1, c), lambda i: (0, 0)),
        ],
        out_specs=pl.BlockSpec((bb, t, c), lambda i: (i, 0, 0)),
        out_shape=jax.ShapeDtypeStruct((b, t, c), F32),
        scratch_shapes=[pltpu.VMEM((bb, t + 16, c), F32)],
        compiler_params=_cparams(("parallel",), 40),
        name="pool",
    )(u, buf16, wbd_bf, pool_scale.reshape(1, c), og.reshape(1, c))


def _alibi_slope(h):
    return 2.0 ** (-8.0 * (h + 1) / MOBA_HEADS)


def _moba_prompt_body(q_ref, k_ref, v_ref, km_ref, og_ref, o_ref, m_sc, l_sc, acc_sc, sel_sc, qst_sc, *, nblk):
    i = pl.program_id(1)
    tq = MOBA_BLOCK
    n_pairs = MOBA_HEADS // 2
    nb_rows = -(-nblk // 8) * 8
    lane = lax.broadcasted_iota(jnp.int32, (tq, LANES), 1)
    low = lane < HEAD_DIM
    row2 = lax.broadcasted_iota(jnp.int32, (tq, MOBA_BLOCK), 0)
    col2 = lax.broadcasted_iota(jnp.int32, (tq, MOBA_BLOCK), 1)
    causal = col2 <= row2
    eye_bf = jnp.where(row2 == col2, 1.0, 0.0).astype(BF16)
    blk_row = lax.broadcasted_iota(jnp.int32, (nb_rows, tq), 0)
    blk_row_f = blk_row.astype(F32)
    colrow = lax.broadcasted_iota(jnp.int32, (1, MOBA_BLOCK), 1).astype(F32)

    def pair_cols(pr):
        return slice(pr * HEAD_PAIR, (pr + 1) * HEAD_PAIR)

    def scores(pr, j):
        start = pl.multiple_of(j * MOBA_BLOCK, MOBA_BLOCK)
        kj = k_ref[pl.ds(start, MOBA_BLOCK), pair_cols(pr)]
        s = lax.dot_general(qst_sc[pr], kj, NT_DIMS, preferred_element_type=F32)
        off = colrow + ((j - i) * MOBA_BLOCK).astype(F32)
        return [s[hh * tq:(hh + 1) * tq] + _alibi_slope(2 * pr + hh) * off for hh in range(2)]

    def pv(pr, j, ps):
        start = pl.multiple_of(j * MOBA_BLOCK, MOBA_BLOCK)
        vj = v_ref[pl.ds(start, MOBA_BLOCK), pair_cols(pr)]
        return jnp.dot(jnp.concatenate(ps, axis=0).astype(BF16), vj, preferred_element_type=F32)

    for pr in range(n_pairs):
        q_pair = q_ref[:, pair_cols(pr)]
        km_pair = km_ref[0, :, pair_cols(pr)]
        q_heads = [jnp.where(low, q_pair, 0.0), jnp.where(low, 0.0, q_pair)]

        km_hi = km_pair.astype(BF16)
        km_lo = (km_pair - km_hi.astype(F32)).astype(BF16)
        for hh in range(2):
            q_hi = q_heads[hh].astype(BF16)
            q_lo = (q_heads[hh] - q_hi.astype(F32)).astype(BF16)
            gate_t = (lax.dot_general(km_hi, q_hi, NT_DIMS, preferred_element_type=F32)
                      + (lax.dot_general(km_hi, q_lo, NT_DIMS, preferred_element_type=F32)
                         + lax.dot_general(km_lo, q_hi, NT_DIMS, preferred_element_type=F32)))[:nb_rows]
            g = jnp.where(blk_row < i, gate_t, NEG_INF)
            sel_t = jnp.zeros((nb_rows, tq), F32)
            for _ in range(MOBA_TOPK):
                mx = jnp.max(g, axis=0, keepdims=True)
                idx = jnp.min(jnp.where(g == mx, blk_row_f, float(LANES)), axis=0, keepdims=True)
                pick = (blk_row_f == idx) & (mx > NEG_INF)
                sel_t = jnp.where(pick, 1.0, sel_t)
                g = jnp.where(pick, NEG_INF, g)
            sel_t = jnp.concatenate([sel_t, jnp.zeros((LANES - nb_rows, tq), F32)], axis=0).astype(BF16)
            sel_sc[2 * pr + hh] = lax.dot_general(eye_bf, sel_t, NT_DIMS, preferred_element_type=F32)

        qst_sc[pr] = (jnp.concatenate(q_heads, axis=0) * QK_SCALE).astype(BF16)

    for pr in range(n_pairs):
        s_own = scores(pr, i)
        ps = []
        for hh in range(2):
            sh = jnp.where(causal, s_own[hh], NEG_INF)
            m = jnp.max(sh, axis=-1, keepdims=True)
            p = jnp.exp(sh - m)
            m_sc[2 * pr + hh] = jnp.broadcast_to(m, (tq, LANES))
            l_sc[2 * pr + hh] = jnp.broadcast_to(jnp.sum(p, axis=-1, keepdims=True), (tq, LANES))
            ps.append(p)
        acc_sc[pr] = pv(pr, i, ps)

    def past(j, carry):
        for pr in range(n_pairs):
            s_j = scores(pr, j)
            ps, alphas = [], []
            for hh in range(2):
                h = 2 * pr + hh
                selcol = jnp.sum(jnp.where(lane == j, sel_sc[h], 0.0), axis=-1, keepdims=True)
                sh = jnp.where(selcol > 0.0, s_j[hh], NEG_INF)
                m_prev = m_sc[h]
                m_new = jnp.maximum(m_prev, jnp.max(sh, axis=-1, keepdims=True))
                alpha = jnp.exp(m_prev - m_new)
                p = jnp.exp(sh - jnp.concatenate([m_new, m_new], axis=1))
                l_sc[h] = alpha * l_sc[h] + jnp.sum(p, axis=-1, keepdims=True)
                m_sc[h] = m_new
                ps.append(p)
                alphas.append(alpha)
            acc_sc[pr] = jnp.concatenate(alphas, axis=0) * acc_sc[pr] + pv(pr, j, ps)
        return carry

    span2 = 2 * MOBA_BLOCK
    colrow2 = lax.broadcasted_iota(jnp.int32, (1, span2), 1).astype(F32)

    def past2(jj, carry):
        j0 = 2 * jj
        start = pl.multiple_of(j0 * MOBA_BLOCK, MOBA_BLOCK)
        off = colrow2 + ((j0 - i) * MOBA_BLOCK).astype(F32)
        for pr in range(n_pairs):
            s2 = lax.dot_general(qst_sc[pr], k_ref[pl.ds(start, span2), pair_cols(pr)], NT_DIMS,
                                 preferred_element_type=F32)
            ps, alphas = [], []
            for hh in range(2):
                h = 2 * pr + hh
                sh = s2[hh * tq:(hh + 1) * tq] + _alibi_slope(h) * off
                halves = []
                for bb in range(2):
                    selcol = jnp.sum(jnp.where(lane == j0 + bb, sel_sc[h], 0.0), axis=-1, keepdims=True)
                    halves.append(jnp.where(selcol > 0.0, sh[:, bb * MOBA_BLOCK:(bb + 1) * MOBA_BLOCK], NEG_INF))
                sh = jnp.concatenate(halves, axis=1)
                m_prev = m_sc[h]
                m_new = jnp.maximum(m_prev, jnp.max(sh, axis=-1, keepdims=True))
                alpha = jnp.exp(m_prev - m_new)
                p = jnp.exp(sh - jnp.concatenate([m_new] * (span2 // LANES), axis=1))
                l_sc[h] = alpha * l_sc[h] + jnp.sum(p, axis=-1, keepdims=True)
                m_sc[h] = m_new
                ps.append(p)
                alphas.append(alpha)
            pv2 = jnp.dot(jnp.concatenate(ps, axis=0).astype(BF16), v_ref[pl.ds(start, span2), pair_cols(pr)],
                          preferred_element_type=F32)
            acc_sc[pr] = jnp.concatenate(alphas, axis=0) * acc_sc[pr] + pv2
        return carry

    lax.fori_loop(0, i // 2, past2, 0)

    @pl.when(i % 2 == 1)
    def _odd_block():
        past(i - 1, 0)

    outs = []
    for pr in range(n_pairs):
        acc = acc_sc[pr]
        outs.append(jnp.where(low, acc[:tq] / l_sc[2 * pr], acc[tq:] / l_sc[2 * pr + 1]))
    o_ref[...] = _rms(jnp.concatenate(outs, axis=1)) * og_ref[...]


def _moba_prompt(q, k_bf, v_bf, kmean_pad, og, b, t):
    n, w = q.shape
    nblk = t // MOBA_BLOCK
    tq = MOBA_BLOCK
    return pl.pallas_call(
        functools.partial(_moba_prompt_body, nblk=nblk),
        grid=(b, nblk),
        in_specs=[
            pl.BlockSpec((tq, w), lambda bi, i: (bi * nblk + i, 0)),
            pl.BlockSpec((t, w), lambda bi, i: (bi, 0)),
            pl.BlockSpec((t, w), lambda bi, i: (bi, 0)),
            pl.BlockSpec((1, LANES, w), lambda bi, i: (bi, 0, 0)),
            pl.BlockSpec((1, w), lambda bi, i: (0, 0)),
        ],
        out_specs=pl.BlockSpec((tq, w), lambda bi, i: (bi * nblk + i, 0)),
        out_shape=jax.ShapeDtypeStruct((n, w), F32),
        scratch_shapes=[
            pltpu.VMEM((MOBA_HEADS, tq, LANES), F32),
            pltpu.VMEM((MOBA_HEADS, tq, LANES), F32),
            pltpu.VMEM((MOBA_HEADS // 2, 2 * tq, LANES), F32),
            pltpu.VMEM((MOBA_HEADS, tq, LANES), F32),
            pltpu.VMEM((MOBA_HEADS // 2, 2 * tq, LANES), BF16),
        ],
        compiler_params=_cparams(("parallel", "arbitrary"), 40),
        name="moba_prompt",
    )(q, k_bf, v_bf, kmean_pad, og.reshape(1, w))


SAMPLE_BLOCKS_PER_STEP = 8
PAGES_PER_BLOCK = MOBA_BLOCK // PAGE_SIZE
PAGES_PER_STEP = SAMPLE_BLOCKS_PER_STEP * PAGES_PER_BLOCK


def _moba_sample_body(pt_ref, q_ref, kn_ref, vn_ref, slope_ref, og_ref, *rest, n_past_blocks, t_new):
    kp = rest[:PAGES_PER_STEP]
    vp = rest[PAGES_PER_STEP:2 * PAGES_PER_STEP]
    o_ref, o_sc, m_sc, l_sc, kmt_sc = rest[2 * PAGES_PER_STEP:]
    c = pl.program_id(1)
    rows = MOBA_HEADS * t_new
    row_h = lax.broadcasted_iota(jnp.int32, (rows, MOBA_WIDTH), 0) // t_new
    lane_h = lax.broadcasted_iota(jnp.int32, (rows, MOBA_WIDTH), 1) // HEAD_DIM
    bd = row_h == lane_h
    q = q_ref[...]
    qbd = jnp.where(bd, jnp.concatenate([q] * MOBA_HEADS, axis=0), 0.0)
    qbd_bf = (qbd * QK_SCALE).astype(BF16)
    slope = slope_ref[...]
    slope2 = jnp.concatenate([slope, slope], axis=1)
    colf = lax.broadcasted_iota(jnp.int32, (rows, MOBA_BLOCK), 1).astype(F32)

    @pl.when(c == 0)
    def _init():
        kmt_sc[...] = jnp.zeros(kmt_sc.shape, F32)

    kt_bf = []
    for jj in range(SAMPLE_BLOCKS_PER_STEP):
        kt_pages = [kp[PAGES_PER_BLOCK * jj + p][0] for p in range(PAGES_PER_BLOCK)]
        ksum = kt_pages[0]
        for kt in kt_pages[1:]:
            ksum = ksum + kt
        kmt_sc[c, :, jj:jj + 1] = jnp.sum(ksum, axis=-1, keepdims=True) * (1.0 / MOBA_BLOCK)
        kt_bf.append(jnp.concatenate(kt_pages, axis=1).astype(BF16))
    s_all = jnp.dot(qbd_bf, jnp.concatenate(kt_bf, axis=1), preferred_element_type=F32)

    for jj in range(SAMPLE_BLOCKS_PER_STEP):
        j = c * SAMPLE_BLOCKS_PER_STEP + jj
        s = s_all[:, jj * MOBA_BLOCK:(jj + 1) * MOBA_BLOCK] + slope2 * (
            colf + ((j - n_past_blocks) * MOBA_BLOCK).astype(F32))
        m = jnp.max(s, axis=-1, keepdims=True)
        p = jnp.exp(s - m)
        vt_bf = jnp.concatenate([vp[PAGES_PER_BLOCK * jj + q][0] for q in range(PAGES_PER_BLOCK)],
                                axis=1).astype(BF16)
        o = lax.dot_general(p.astype(BF16), vt_bf, NT_DIMS, preferred_element_type=F32)
        o_sc[j] = jnp.where(bd, o, 0.0)
        m_sc[j] = jnp.broadcast_to(m, (rows, LANES))
        l_sc[j] = jnp.broadcast_to(jnp.sum(p, axis=-1, keepdims=True), (rows, LANES))

    @pl.when(c == pl.num_programs(1) - 1)
    def _finish():
        tq = lax.broadcasted_iota(jnp.int32, (rows, LANES), 0) % t_new
        kn = kn_ref[...]
        vn = vn_ref[...]
        qs = qbd * QK_SCALE
        s_own = []
        m_run = jnp.full((rows, LANES), NEG_INF, F32)
        for cc in range(t_new):
            sc = jnp.sum(qs * kn[cc:cc + 1, :], axis=-1, keepdims=True) + slope * float(cc)
            sc = jnp.where(tq >= cc, sc, NEG_INF)
            s_own.append(sc)
            m_run = jnp.maximum(m_run, sc)

        blk = lax.broadcasted_iota(jnp.int32, (rows, LANES), 1)
        blk_f = blk.astype(F32)
        gates = jnp.zeros((rows, LANES), F32)
        for cs in range(n_past_blocks // SAMPLE_BLOCKS_PER_STEP):
            g_cs = jnp.dot(qbd, kmt_sc[cs], precision=lax.Precision.HIGHEST, preferred_element_type=F32)
            g_cs = jnp.where(blk < SAMPLE_BLOCKS_PER_STEP, g_cs, 0.0)
            gates = gates + (pltpu.roll(g_cs, cs * SAMPLE_BLOCKS_PER_STEP, 1) if cs else g_cs)
        g = jnp.where(blk < n_past_blocks, gates, NEG_INF)
        sel = jnp.zeros((rows, LANES), F32)
        for _ in range(MOBA_TOPK):
            mx = jnp.max(g, axis=-1, keepdims=True)
            idx = jnp.min(jnp.where(g == mx, blk_f, float(LANES)), axis=-1, keepdims=True)
            pick = (blk_f == idx) & (mx > NEG_INF)
            sel = jnp.where(pick, 1.0, sel)
            g = jnp.where(pick, NEG_INF, g)

        m_all = jnp.full((rows, LANES), NEG_INF, F32)
        l_all = jnp.zeros((rows, LANES), F32)
        for j in range(n_past_blocks):
            m_all = jnp.where(blk == j, m_sc[j], m_all)
            l_all = jnp.where(blk == j, l_sc[j], l_all)
        m_run = jnp.maximum(m_run, jnp.max(jnp.where(sel > 0.0, m_all, NEG_INF), axis=-1, keepdims=True))
        w_all = jnp.where(sel > 0.0, jnp.exp(m_all - m_run), 0.0)

        l_run = jnp.broadcast_to(jnp.sum(w_all * l_all, axis=-1, keepdims=True), (rows, LANES))
        o_run = jnp.zeros((rows, MOBA_WIDTH), F32)
        for cc in range(t_new):
            p = jnp.exp(s_own[cc] - m_run)
            l_run = l_run + p
            o_run = o_run + jnp.concatenate([p] * (MOBA_WIDTH // LANES), axis=1) * vn[cc:cc + 1, :]
        o_run = jnp.where(bd, o_run, 0.0)
        for j in range(n_past_blocks):
            o_run = o_run + w_all[:, j:j + 1] * o_sc[j]
        o_bd = o_run / jnp.concatenate([l_run] * (MOBA_WIDTH // LANES), axis=1)
        attn = o_bd[0:t_new]
        for h in range(1, MOBA_HEADS):
            attn = attn + o_bd[h * t_new:(h + 1) * t_new]
        o_ref[...] = _rms(attn) * og_ref[...]


def _moba_sample(page_table, q, k_new, v_new, cache_kt, cache_vt, og, b, t_new):
    n_pages = page_table.shape[1]
    n_past_blocks = n_pages // PAGES_PER_BLOCK
    assert n_past_blocks <= LANES
    assert n_past_blocks % SAMPLE_BLOCKS_PER_STEP == 0
    n_steps = n_past_blocks // SAMPLE_BLOCKS_PER_STEP
    rows = MOBA_HEADS * t_new
    w = MOBA_WIDTH
    slopes = jnp.exp2(-8.0 * jnp.arange(1, MOBA_HEADS + 1, dtype=F32) / MOBA_HEADS)
    slope_rows = jnp.broadcast_to(jnp.repeat(slopes, t_new)[:, None], (rows, LANES))

    def page_spec(p):
        return pl.BlockSpec((1, w, PAGE_SIZE), lambda bi, c, pt: (pt[bi, c * PAGES_PER_STEP + p], 0, 0))

    row_spec = pl.BlockSpec((t_new, w), lambda bi, c, pt: (bi, 0))
    grid_spec = pltpu.PrefetchScalarGridSpec(
        num_scalar_prefetch=1,
        grid=(b, n_steps),
        in_specs=[row_spec, row_spec, row_spec,
                  pl.BlockSpec((rows, LANES), lambda bi, c, pt: (0, 0)),
                  pl.BlockSpec((1, w), lambda bi, c, pt: (0, 0))]
                 + [page_spec(p) for p in range(PAGES_PER_STEP)]
                 + [page_spec(p) for p in range(PAGES_PER_STEP)],
        out_specs=row_spec,
        scratch_shapes=[
            pltpu.VMEM((n_past_blocks, rows, w), F32),
            pltpu.VMEM((n_past_blocks, rows, LANES), F32),
            pltpu.VMEM((n_past_blocks, rows, LANES), F32),
            pltpu.VMEM((n_steps, w, LANES), F32),
        ],
    )
    return pl.pallas_call(
        functools.partial(_moba_sample_body, n_past_blocks=n_past_blocks, t_new=t_new),
        grid_spec=grid_spec,
        out_shape=jax.ShapeDtypeStruct((b * t_new, w), F32),
        compiler_params=_cparams(("parallel", "arbitrary"), 48),
        name="moba_sample",
    )(page_table, q, k_new, v_new, slope_rows, og.reshape(1, w),
      *([cache_kt] * PAGES_PER_STEP), *([cache_vt] * PAGES_PER_STEP))


def _mem_attn_body(q_ref, mk_ref, mv_ref, og_ref, o_ref, *, tm):
    lane = lax.broadcasted_iota(jnp.int32, (tm, LANES), 1)
    low = lane < HEAD_DIM
    outs = []
    for pr in range(MEM_HEADS // 2):
        cs = slice(pr * HEAD_PAIR, (pr + 1) * HEAD_PAIR)
        q_pair = q_ref[:, cs]
        qst = (jnp.concatenate([jnp.where(low, q_pair, 0.0), jnp.where(low, 0.0, q_pair)], axis=0)
               * QK_SCALE).astype(BF16)
        s = lax.dot_general(qst, mk_ref[0, :, cs].astype(BF16), NT_DIMS, preferred_element_type=F32)
        p = jnp.exp(s - jnp.max(s, axis=-1, keepdims=True))
        l = jnp.sum(p, axis=-1, keepdims=True)
        o = jnp.dot(p.astype(BF16), mv_ref[0, :, cs].astype(BF16), preferred_element_type=F32) / l
        outs.append(jnp.where(low, o[:tm], o[tm:]))
    o_ref[...] = _rms(jnp.concatenate(outs, axis=1)) * og_ref[...]


def _mem_attn(qm, mem_k, mem_v, og, b, t, tm):
    n, w = qm.shape
    assert t % tm == 0
    steps = t // tm
    return pl.pallas_call(
        functools.partial(_mem_attn_body, tm=tm),
        grid=(b, steps),
        in_specs=[
            pl.BlockSpec((tm, w), lambda bi, i: (bi * steps + i, 0)),
            pl.BlockSpec((1, MEM_TOKENS, w), lambda bi, i: (bi, 0, 0)),
            pl.BlockSpec((1, MEM_TOKENS, w), lambda bi, i: (bi, 0, 0)),
            pl.BlockSpec((1, w), lambda bi, i: (0, 0)),
        ],
        out_specs=pl.BlockSpec((tm, w), lambda bi, i: (bi * steps + i, 0)),
        out_shape=jax.ShapeDtypeStruct((n, w), F32),
        compiler_params=_cparams(("parallel", "arbitrary"), 32),
        name="mem_attn",
    )(qm, mem_k, mem_v, og.reshape(1, w))


ROUTER_GROUP_LANE0 = N_EXPERTS
ROUTER_GID_LANE = 64


def _outproj_router_body(op_ref, oa_ref, om_ref, x_ref, wo_ref, g2_ref, wr_ref, br_ref,
                         x1_ref, h2_ref, comb_ref, gcnt_ref, *, tm):
    a0, a1 = POOL_WIDTH, POOL_WIDTH + MOBA_WIDTH
    y = x_ref[...]
    y = y + jnp.dot(op_ref[...].astype(BF16), wo_ref[0:a0, :], preferred_element_type=F32)
    y = y + jnp.dot(oa_ref[...].astype(BF16), wo_ref[a0:a1, :], preferred_element_type=F32)
    y = y + jnp.dot(om_ref[...].astype(BF16), wo_ref[a1:, :], preferred_element_type=F32)
    x1_ref[...] = y
    h2 = _rms(y) * g2_ref[...]
    h2_hi = h2.astype(BF16)
    h2_lo = (h2 - h2_hi.astype(F32)).astype(BF16)
    h2_ref[...] = h2_hi
    hw =jnp.dot(h2_hi, wr_ref[...], preferred_element_type=F32)
    lw = jnp.dot(h2_lo, wr_ref[:, :LANES], preferred_element_type=F32)
    logits = (hw[:, :LANES] + (hw[:, LANES:] + lw)) + br_ref[...]
    lane_f = lax.broadcasted_iota(jnp.int32, (tm, LANES), 1).astype(F32)
    big = float(LANES)
    g_lo = float(ROUTER_GROUP_LANE0)
    is_g = (lane_f >= g_lo) & (lane_f < g_lo + MOE_GROUPS)
    lg = jnp.where(is_g, logits, NEG_INF)
    mg = jnp.max(lg, axis=-1, keepdims=True)
    pg_top = 1.0 / jnp.sum(jnp.exp(lg - mg), axis=-1, keepdims=True)
    gidx = jnp.min(jnp.where(lg == mg, lane_f, big), axis=-1, keepdims=True) - g_lo
    e_lo = gidx * EXPERTS_PER_GROUP
    in_grp = (lane_f >= e_lo) & (lane_f < e_lo + EXPERTS_PER_GROUP)
    le = jnp.where(in_grp, logits, NEG_INF)
    m1 = jnp.max(le, axis=-1, keepdims=True)
    se = jnp.sum(jnp.exp(le - m1), axis=-1, keepdims=True)
    i1 = jnp.min(jnp.where(le == m1, lane_f, big), axis=-1, keepdims=True)
    le2 = jnp.where(lane_f == i1, NEG_INF, le)
    m2 = jnp.max(le2, axis=-1, keepdims=True)
    i2 = jnp.min(jnp.where(le2 == m2, lane_f, big), axis=-1, keepdims=True)
    p1 = 1.0 / se
    p2 = jnp.exp(m2 - m1) / se
    den = p1 + p2
    comb_ref[...] = jnp.where(lane_f == i1, pg_top * (p1 / den),
                              jnp.where(lane_f == i2, pg_top * (p2 / den),
                                        jnp.where(lane_f == float(ROUTER_GID_LANE), gidx, 0.0)))
    gcnt_ref[0] = jnp.sum(jnp.where(lane_f == gidx, 1.0, 0.0), axis=0, keepdims=True)


def _outproj_router(o_pool, o_attn, o_mem, x, wo_bf, g2, wr, br, tm):
    n, d = x.shape
    assert n % tm == 0
    row = lambda wdt: pl.BlockSpec((tm, wdt), lambda i: (i, 0))
    full = lambda shp: pl.BlockSpec(shp, lambda i: (0, 0))
    return pl.pallas_call(
        functools.partial(_outproj_router_body, tm=tm),
        grid=(n // tm,),
        in_specs=[row(POOL_WIDTH), row(MOBA_WIDTH), row(MEM_WIDTH), row(d),
                  full((d, d)), full((1, d)), full((d, 2 * LANES)), full((1, LANES))],
        out_specs=[row(d), row(d), row(LANES), pl.BlockSpec((1, 1, LANES), lambda i: (i, 0, 0))],
        out_shape=[jax.ShapeDtypeStruct((n, d), F32), jax.ShapeDtypeStruct((n, d), BF16),
                   jax.ShapeDtypeStruct((n, LANES), F32), jax.ShapeDtypeStruct((n // tm, 1, LANES), F32)],
        compiler_params=_cparams(("parallel",), 40),
        name="outproj_router",
    )(o_pool, o_attn, o_mem, x, wo_bf, g2.reshape(1, d), wr, br)


MOE_EXPERTS_PER_STEP = 4
MOE_RUN_ALIGN = 16
MOE_SORT_PAD = LANES


def _moe_window(tm):
    return max(LANES, tm // MOE_GROUPS + 32)


def _moe_body(cnt_ref, h2_ref, comb_ref, x1_hbm, wg_ref, wu_ref, wd_ref, y_ref,
              xs_sc, cs_sc, ys_sc, p_sc, pt_sc, sem, *, tm, win):
    i = pl.program_id(0)
    s = pl.program_id(1)
    eps = MOE_EXPERTS_PER_STEP
    span = tm + MOE_SORT_PAD
    g = s // (EXPERTS_PER_GROUP // eps)
    starts, nxt = [], jnp.int32(0)
    for gg in range(MOE_GROUPS):
        starts.append(nxt)
        nxt = nxt + (cnt_ref[i * MOE_GROUPS + gg] + (MOE_RUN_ALIGN - 1)) // MOE_RUN_ALIGN * MOE_RUN_ALIGN
    x1_copy = pltpu.make_async_copy(x1_hbm.at[pl.ds(pl.multiple_of(i * tm, tm), tm)], y_ref, sem)

    @pl.when(s == 0)
    def _sort():
        x1_copy.start()
        comb = comb_ref[...]
        lane = lax.broadcasted_iota(jnp.int32, (tm, LANES), 1)
        lane_f = lane.astype(F32)
        gid = jnp.sum(jnp.where(lane == ROUTER_GID_LANE, comb, 0.0), axis=-1, keepdims=True)
        onehot = jnp.where(lane_f == gid, 1.0, 0.0)
        earlier_tok = lax.broadcasted_iota(jnp.int32, (tm, tm), 1) < lax.broadcasted_iota(jnp.int32, (tm, tm), 0)
        earlier = jnp.dot(jnp.where(earlier_tok, 1.0, 0.0).astype(BF16), onehot.astype(BF16),
                          preferred_element_type=F32)
        start_vec = jnp.zeros((tm, LANES), F32)
        for gg in range(MOE_GROUPS):
            start_vec = jnp.where(lane == gg, starts[gg].astype(F32), start_vec)
        dest = jnp.sum(onehot * (earlier + start_vec), axis=-1, keepdims=True)
        hi = jnp.floor(dest * (1.0 / 32.0))
        lo = dest - 32.0 * hi
        dest_cols = jnp.where(lane == 0, hi, jnp.where(lane == 1, lo, 0.0)).astype(BF16)
        r16 = lax.broadcasted_iota(jnp.int32, (16, LANES), 0)
        l16 = lax.broadcasted_iota(jnp.int32, (16, LANES), 1)
        pick = jnp.where((r16 == 0) & (l16 == 0), 32.0, jnp.where((r16 == 0) & (l16 == 1), 1.0, 0.0)).astype(BF16)
        dest_row = lax.dot_general(pick, dest_cols, NT_DIMS, preferred_element_type=F32)[0:1, :]
        p = jnp.where(lax.broadcasted_iota(jnp.int32, (span, tm), 0).astype(F32) == dest_row, 1.0, 0.0).astype(BF16)
        p_sc[...] = p
        pt_sc[...] = jnp.where(lax.broadcasted_iota(jnp.int32, (tm, span), 1).astype(F32) == dest,
                               1.0, 0.0).astype(BF16)
        xs_sc[0:span, :] = jnp.dot(p, h2_ref[...], preferred_element_type=F32).astype(BF16)
        xs_sc[span:, :] = jnp.zeros((win, D_MODEL), BF16)
        c1 = comb.astype(BF16)
        r1 = comb - c1.astype(F32)
        c2 = r1.astype(BF16)
        c3 = (r1 - c2.astype(F32)).astype(BF16)
        cs_sc[0:span, :] = (jnp.dot(p, c1, preferred_element_type=F32) + jnp.dot(p, c2, preferred_element_type=F32)
                            + jnp.dot(p, c3, preferred_element_type=F32))
        cs_sc[span:, :] = jnp.zeros((win, LANES), F32)
        ys_sc[...] = jnp.zeros(ys_sc.shape, F32)

    run_start = starts[0]
    for gg in range(1, MOE_GROUPS):
        run_start = jnp.where(g == gg, starts[gg], run_start)
    run_len = cnt_ref[i * MOE_GROUPS + g]
    lane_w = lax.broadcasted_iota(jnp.int32, (win, LANES), 1)

    def window(w, carry):
        r0 = pl.multiple_of(run_start + w * win, MOE_RUN_ALIGN)
        rows = xs_sc[pl.ds(r0, win), :]
        comb_rows = cs_sc[pl.ds(r0, win), :]
        acc = None
        for j in range(eps):
            e = s * eps + j
            a = jnp.dot(rows, wg_ref[j], preferred_element_type=F32)
            b = jnp.dot(rows, wu_ref[j], preferred_element_type=F32)
            ce = jnp.sum(jnp.where(lane_w == e, comb_rows, 0.0), axis=-1, keepdims=True)
            hid = ((a * jax.nn.sigmoid(a)) * b * ce).astype(BF16)
            c = jnp.dot(hid, wd_ref[j], preferred_element_type=F32)
            acc = c if acc is None else acc + c
        ys_sc[pl.ds(r0, win), :] += acc
        return carry

    lax.fori_loop(0, (run_len + win - 1) // win, window, 0)

    @pl.when(s == pl.num_programs(1) - 1)
    def _unsort():
        ys = ys_sc[0:span, :]
        moe = jnp.dot(pt_sc[...], ys.astype(BF16), preferred_element_type=F32)
        x1_copy.wait()
        y_ref[...] = y_ref[...] + moe


def _moe(tile_cnt, h2, comb, x1, wg_bf, wu_bf, wd_bf, tm):
    n, d = x1.shape
    eps = MOE_EXPERTS_PER_STEP
    assert n % tm == 0 and EXPERTS_PER_GROUP % eps == 0 and MOE_GROUPS * (MOE_RUN_ALIGN - 1) <= MOE_SORT_PAD
    win = _moe_window(tm)
    assert win % MOE_RUN_ALIGN == 0
    span = tm + MOE_SORT_PAD
    grid_spec = pltpu.PrefetchScalarGridSpec(
        num_scalar_prefetch=1,
        grid=(n // tm, N_EXPERTS // eps),
        in_specs=[
            pl.BlockSpec((tm, d), lambda i, s, cnt: (i, 0)),
            pl.BlockSpec((tm, LANES), lambda i, s, cnt: (i, 0)),
            pl.BlockSpec(memory_space=pl.ANY),
            pl.BlockSpec((eps, d, D_EXPERT), lambda i, s, cnt: (s, 0, 0)),
            pl.BlockSpec((eps, d, D_EXPERT), lambda i, s, cnt: (s, 0, 0)),
            pl.BlockSpec((eps, D_EXPERT, d), lambda i, s, cnt: (s, 0, 0)),
        ],
        out_specs=pl.BlockSpec((tm, d), lambda i, s, cnt: (i, 0)),
        scratch_shapes=[
            pltpu.VMEM((span + win, d), BF16),
            pltpu.VMEM((span + win, LANES), F32),
            pltpu.VMEM((span + win, d), F32),
            pltpu.VMEM((span, tm), BF16),
            pltpu.VMEM((tm, span), BF16),
            pltpu.SemaphoreType.DMA(()),
        ],
    )
    return pl.pallas_call(
        functools.partial(_moe_body, tm=tm, win=win),
        grid_spec=grid_spec,
        out_shape=jax.ShapeDtypeStruct((n, d), F32),
        compiler_params=_cparams(("parallel", "arbitrary"), 56),
        name="moe",
    )(tile_cnt, h2, comb, x1, wg_bf, wu_bf, wd_bf)


def _block_diag_ones(width, group):
    r = lax.broadcasted_iota(jnp.int32, (width, width), 0) // group
    c = lax.broadcasted_iota(jnp.int32, (width, width), 1) // group
    return (r == c).astype(BF16)


def _pick_tile(n, candidates):
    for c in candidates:
        if n % c == 0:
            return c
    return n


IN_SEGS = (
    (0, POOL_WIDTH, False, False, False, False),
    (POOL_WIDTH, MOBA_WIDTH, True, False, False, False),
    (POOL_WIDTH + MOBA_WIDTH, MOBA_WIDTH, True, True, True, True),
    (POOL_WIDTH + 2 * MOBA_WIDTH, MOBA_WIDTH, False, True, True, False),
    (POOL_WIDTH + 3 * MOBA_WIDTH, MEM_WIDTH, True, False, False, False),
)
IN_SEGS_SAMPLE = tuple((c0, wd, hn, False, False, False) for (c0, wd, hn, _, _, _) in IN_SEGS)
MEMKV_SEGS = ((0, MEM_WIDTH, True, False, False, False), (MEM_WIDTH, MEM_WIDTH, False, False, False, False))


def kernel(x_prompt, x_sample, mem_prompt, cache_k, cache_v, state_pool, cache_mem_k, cache_mem_v,
           page_table, norm1_gain, w_in, pool_w, pool_scale, moba_q_gain, moba_k_gain,
           mem_norm_gain, w_mem_kv, mem_q_gain, mem_k_gain, out_gain, w_out, norm2_gain,
           router_group_w, router_group_b, router_expert_w, router_expert_b, w_gate, w_up, w_down):
    bp, tp, d = x_prompt.shape
    bs, ts, _ = x_sample.shape
    n_p, n_s = bp * tp, bs * ts

    w_in_bf = w_in.astype(BF16)
    w_mem_bf = w_mem_kv.astype(BF16)
    wo_bf = w_out.astype(BF16)
    ones_bd = _block_diag_ones(MOBA_WIDTH, HEAD_DIM)
    head_gain_in = jnp.concatenate([
        jnp.ones((POOL_WIDTH,), F32), jnp.tile(moba_q_gain, MOBA_HEADS), jnp.tile(moba_k_gain, MOBA_HEADS),
        jnp.ones((MOBA_WIDTH,), F32), jnp.tile(mem_q_gain, MEM_HEADS)])
    head_gain_mem = jnp.concatenate([jnp.tile(mem_k_gain, MEM_HEADS), jnp.ones((MEM_WIDTH,), F32)])
    wbd = jnp.zeros((POOL_WIDTH, POOL_WIDTH), F32)
    for g in range(len(POOL_WINDOWS)):
        sl = slice(g * POOL_GROUP_WIDTH, (g + 1) * POOL_GROUP_WIDTH)
        wbd = wbd.at[sl, sl].set(pool_w[g])
    wbd_bf = wbd.astype(BF16)
    og_pool, og_attn, og_mem = (out_gain[:POOL_WIDTH], out_gain[POOL_WIDTH:POOL_WIDTH + MOBA_WIDTH],
                                out_gain[POOL_WIDTH + MOBA_WIDTH:])
    wr = jnp.zeros((d, LANES), F32)
    wr = wr.at[:, :N_EXPERTS].set(router_expert_w).at[:, N_EXPERTS:N_EXPERTS + MOE_GROUPS].set(router_group_w)
    br = jnp.zeros((1, LANES), F32)
    br = br.at[0, :N_EXPERTS].set(router_expert_b).at[0, N_EXPERTS:N_EXPERTS + MOE_GROUPS].set(router_group_b)
    wr_hi = wr.astype(BF16)
    wr = jnp.concatenate([wr_hi, (wr - wr_hi.astype(F32)).astype(BF16)], axis=1)
    wg = w_gate.reshape(N_EXPERTS, d, D_EXPERT).astype(BF16)
    wu = w_up.reshape(N_EXPERTS, d, D_EXPERT).astype(BF16)
    wd = w_down.reshape(N_EXPERTS, D_EXPERT, d).astype(BF16)

    def tail(o_pool, o_attn, o_mem, x2d):
        n = x2d.shape[0]
        tm_r = _pick_tile(n, (512, 256))
        x1, h2, comb, gcnt = _outproj_router(o_pool, o_attn, o_mem, x2d, wo_bf, norm2_gain, wr, br, tm_r)
        tm = _pick_tile(n, (1024, 512, 256))
        tile_cnt = gcnt[:, 0, :MOE_GROUPS].reshape(n // tm, tm // tm_r, MOE_GROUPS).sum(axis=1)
        return _moe(tile_cnt.astype(jnp.int32).reshape(-1), h2, comb, x1, wg, wu, wd, tm)

    mem_k_p, mem_v_p = _norm_proj(mem_prompt.reshape(bp * MEM_TOKENS, d), mem_norm_gain, w_mem_bf,
                                  head_gain_mem, ones_bd, MEMKV_SEGS, _pick_tile(bp * MEM_TOKENS, (512, 256)))
    xp = x_prompt.reshape(n_p, d)
    u_p, q_p, kt_p, k_p_bf, kmean_p, vt_p, v_p_bf, qm_p = _norm_proj(
        xp, norm1_gain, w_in_bf, head_gain_in, ones_bd, IN_SEGS, _pick_tile(tp, (512, 256)), rows_per_batch=tp)
    o_pool_p = _pool(u_p.reshape(bp, tp, POOL_WIDTH), jnp.zeros((bp, 16, POOL_WIDTH), F32), wbd_bf,
                     pool_scale, og_pool, 0, 1, MOBA_BLOCK).reshape(n_p, POOL_WIDTH)
    nblk = tp // MOBA_BLOCK
    kmean_pad = jnp.pad(kmean_p.reshape(bp, nblk, MOBA_WIDTH), ((0, 0), (0, LANES - nblk), (0, 0)))
    o_attn_p = _moba_prompt(q_p, k_p_bf, v_p_bf, kmean_pad, og_attn, bp, tp)
    o_mem_p = _mem_attn(qm_p, mem_k_p.reshape(bp, MEM_TOKENS, MEM_WIDTH),
                        mem_v_p.reshape(bp, MEM_TOKENS, MEM_WIDTH), og_mem, bp, tp, _pick_tile(tp, (512, 256)))
    y_p = tail(o_pool_p, o_attn_p, o_mem_p, xp)

    past_len = page_table.shape[1] * PAGE_SIZE
    xs = x_sample.reshape(n_s, d)
    u_s, q_s, k_s, v_s, qm_s = _norm_proj(xs, norm1_gain, w_in_bf, head_gain_in, ones_bd, IN_SEGS_SAMPLE, n_s)
    u_s3 = u_s.reshape(bs, ts, POOL_WIDTH)
    buf16 = jnp.concatenate([jnp.zeros((bs, 1, POOL_WIDTH), F32), state_pool], axis=1)
    o_pool_s = _pool(u_s3, buf16, wbd_bf, pool_scale, og_pool, past_len, bs, ts).reshape(n_s, POOL_WIDTH)
    o_attn_s = _moba_sample(page_table, q_s, k_s, v_s,
                            cache_k.transpose(0, 2, 3, 1).reshape(-1, MOBA_WIDTH, PAGE_SIZE),
                            cache_v.transpose(0, 2, 3, 1).reshape(-1, MOBA_WIDTH, PAGE_SIZE),
                            og_attn, bs, ts)
    o_mem_s = _mem_attn(qm_s, cache_mem_k.reshape(bs, MEM_TOKENS, MEM_WIDTH),
                        cache_mem_v.reshape(bs, MEM_TOKENS, MEM_WIDTH), og_mem, bs, ts, ts)
    y_s = tail(o_pool_s, o_attn_s, o_mem_s, xs)

    pool_prompt = u_p.reshape(bp, tp, POOL_WIDTH)[:, tp - POOL_STATE:]
    pool_sample = jnp.concatenate([state_pool, u_s3], axis=1)[:, -POOL_STATE:]
    k_p = kt_p.reshape(bp, MOBA_HEADS, HEAD_DIM, tp).transpose(0, 3, 1, 2)
    v_p = vt_p.reshape(bp, MOBA_HEADS, HEAD_DIM, tp).transpose(0, 3, 1, 2)
    return (y_p.reshape(bp, tp, d), y_s.reshape(bs, ts, d), k_p, v_p,
            pool_prompt,
            mem_k_p.reshape(bp, MEM_TOKENS, MEM_HEADS, HEAD_DIM), mem_v_p.reshape(bp, MEM_TOKENS, MEM_HEADS, HEAD_DIM),
            k_s.reshape(bs, ts, MOBA_HEADS, HEAD_DIM), v_s.reshape(bs, ts, MOBA_HEADS, HEAD_DIM),
            pool_sample)
```

```python
import functools

import jax
import jax.numpy as jnp
from jax import lax
from jax.experimental import pallas as pl
from jax.experimental.pallas import tpu as pltpu

F32 = jnp.float32
BF16 = jnp.bfloat16

D_MODEL = 1024
HEAD_DIM = 64
POOL_WIDTH = 256
POOL_WINDOWS = (2, 4, 8, 16)
POOL_GROUP_WIDTH = 64
POOL_STATE = 15
MOBA_WIDTH = 512
MOBA_HEADS = 8
MOBA_BLOCK = 256
MOBA_TOPK = 3
MEM_WIDTH = 256
MEM_HEADS = 4
MEM_TOKENS = 256
PAGE_SIZE = 128
MOE_GROUPS = 4
EXPERTS_PER_GROUP = 8
N_EXPERTS = MOE_GROUPS * EXPERTS_PER_GROUP
D_EXPERT = 256
NORM_EPS = 1e-6

LANES = 128
HEAD_PAIR = 2 * HEAD_DIM
QK_SCALE = HEAD_DIM ** -0.5
NEG_INF = float("-inf")
MIB = 1024 * 1024
NT_DIMS = (((1,), (1,)), ((), ()))


def _cparams(semantics, vmem_mib):
    return pltpu.CompilerParams(dimension_semantics=semantics, vmem_limit_bytes=vmem_mib * MIB)


def _rms(y, eps=NORM_EPS):
    return y * lax.rsqrt(jnp.mean(y * y, axis=-1, keepdims=True) + eps)


def _norm_proj_body(x_ref, g_ref, w_ref, hg_ref, ones_ref, *outs, segs, tm):
    hb = (_rms(x_ref[...]) * g_ref[...]).astype(BF16)
    oi = 0
    for (c0, wd, headnorm, transposed, want_bf16, want_blockmean) in segs:
        y = jnp.dot(hb, w_ref[:, c0:c0 + wd], preferred_element_type=F32)
        if headnorm:
            sq = (y * y).astype(BF16)
            msq = jnp.dot(sq, ones_ref[:wd, :wd], preferred_element_type=F32) * (1.0 / HEAD_DIM)
            y = (y * lax.rsqrt(msq + NORM_EPS)) * hg_ref[:, c0:c0 + wd]
        if transposed:
            outs[oi][0] = y.T
        else:
            outs[oi][...] = y
        oi += 1
        if want_bf16:
            outs[oi][...] = y.astype(BF16)
            oi += 1
        if want_blockmean:
            for bi in range(tm // MOBA_BLOCK):
                outs[oi][bi] = jnp.mean(y[bi * MOBA_BLOCK:(bi + 1) * MOBA_BLOCK], axis=0, keepdims=True)
            oi += 1


def _norm_proj(x, gain, w_bf, head_gain, ones_bd, segs, tm, rows_per_batch=None):
    n, d = x.shape
    wtot = w_bf.shape[1]
    assert n % tm == 0
    out_shape, out_specs = [], []
    for (c0, wd, headnorm, transposed, want_bf16, want_blockmean) in segs:
        if transposed:
            assert rows_per_batch % tm == 0 and n % rows_per_batch == 0
            steps = rows_per_batch // tm
            out_shape.append(jax.ShapeDtypeStruct((n // rows_per_batch, wd, rows_per_batch), F32))
            out_specs.append(pl.BlockSpec((1, wd, tm), lambda i, steps=steps: (i // steps, 0, i % steps)))
        else:
            out_shape.append(jax.ShapeDtypeStruct((n, wd), F32))
            out_specs.append(pl.BlockSpec((tm, wd), lambda i: (i, 0)))
        if want_bf16:
            out_shape.append(jax.ShapeDtypeStruct((n, wd), BF16))
            out_specs.append(pl.BlockSpec((tm, wd), lambda i: (i, 0)))
        if want_blockmean:
            assert tm % MOBA_BLOCK == 0
            nb = tm // MOBA_BLOCK
            out_shape.append(jax.ShapeDtypeStruct((n // MOBA_BLOCK, 1, wd), F32))
            out_specs.append(pl.BlockSpec((nb, 1, wd), lambda i: (i, 0, 0)))
    return pl.pallas_call(
        functools.partial(_norm_proj_body, segs=segs, tm=tm),
        grid=(n // tm,),
        in_specs=[
            pl.BlockSpec((tm, d), lambda i: (i, 0)),
            pl.BlockSpec((1, d), lambda i: (0, 0)),
            pl.BlockSpec((d, wtot), lambda i: (0, 0)),
            pl.BlockSpec((1, wtot), lambda i: (0, 0)),
            pl.BlockSpec(ones_bd.shape, lambda i: (0, 0)),
        ],
        out_specs=out_specs,
        out_shape=out_shape,
        compiler_params=_cparams(("parallel",), 48),
        name="norm_proj",
    )(x, gain.reshape(1, d), w_bf, head_gain.reshape(1, wtot), ones_bd)


def _pool_windows(win, pos0):
    r = win.shape[0] - 16
    lane = lax.broadcasted_iota(jnp.int32, (r, LANES), 1)
    pos1 = pos0 + lax.broadcasted_iota(jnp.int32, (r, LANES), 0) + 1
    low = lane < POOL_GROUP_WIDTH
    a = win[:, :LANES]
    b = win[:, LANES:]
    a2 = a + pltpu.roll(a, 1, 0)
    a4 = a2 + pltpu.roll(a2, 2, 0)
    b2 = b + pltpu.roll(b, 1, 0)
    b4 = b2 + pltpu.roll(b2, 2, 0)
    b8 = b4 + pltpu.roll(b4, 4, 0)
    b16 = b8 + pltpu.roll(b8, 8, 0)
    cnt_a = jnp.minimum(jnp.where(low, POOL_WINDOWS[0], POOL_WINDOWS[1]), pos1).astype(F32)
    cnt_b = jnp.minimum(jnp.where(low, POOL_WINDOWS[2], POOL_WINDOWS[3]), pos1).astype(F32)
    pa = jnp.where(low, a2[16:], a4[16:]) / cnt_a - a[16:]
    pb = jnp.where(low, b8[16:], b16[16:]) / cnt_b - b[16:]
    return jnp.concatenate([pa, pb], axis=1)


def _pool_finish(pooled, wbd_ref, ps_ref, og_ref):
    mixed = jnp.dot(pooled.astype(BF16), wbd_ref[...], preferred_element_type=F32) * ps_ref[...]
    return _rms(mixed) * og_ref[...]


def _pool_body(u_ref, buf_ref, wbd_ref, ps_ref, og_ref, o_ref, ext_sc, *, bb, t, r, pos0):
    for bi in range(bb):
        ext_sc[bi, 0:16, :] = buf_ref[bi]
        ext_sc[bi, 16:, :] = u_ref[bi]
    if t == r:
        pooled = [_pool_windows(ext_sc[bi], pos0) for bi in range(bb)]
        out = _pool_finish(jnp.concatenate(pooled, axis=0), wbd_ref, ps_ref, og_ref)
        for bi in range(bb):
            o_ref[bi] = out[bi * r:(bi + 1) * r]
    else:
        assert bb == 1

        def chunk(c, carry):
            base = pl.multiple_of(c * r, r)
            pooled = _pool_windows(ext_sc[0, pl.ds(base, r + 16), :], pos0 + base)
            o_ref[0, pl.ds(base, r), :] = _pool_finish(pooled, wbd_ref, ps_ref, og_ref)
            return carry

        lax.fori_loop(0, t // r, chunk, 0)


def _pool(u, buf16, wbd_bf, pool_scale, og, pos0, bb, r):
    b, t, c = u.shape
    assert b % bb == 0 and t % r == 0
    return pl.pallas_call(
        functools.partial(_pool_body, bb=bb, t=t, r=r, pos0=pos0),
        grid=(b // bb,),
        in_specs=[
            pl.BlockSpec((bb, t, c), lambda i: (i, 0, 0)),
            pl.BlockSpec((bb, 16, c), lambda i: (i, 0, 0)),
            pl.BlockSpec((c, c), lambda i: (0, 0)),
            pl.BlockSpec((1, c), lambda i: (0, 0)),
            pl.BlockSpec((1, c), lambda i: (0, 0)),
        ],
        out_specs=pl.BlockSpec((bb, t, c), lambda i: (i, 0, 0)),
        out_shape=jax.ShapeDtypeStruct((b, t, c), F32),
        scratch_shapes=[pltpu.VMEM((bb, t + 16, c), F32)],
        compiler_params=_cparams(("parallel",), 40),
        name="pool",
    )(u, buf16, wbd_bf, pool_scale.reshape(1, c), og.reshape(1, c))


def _alibi_slope(h):
    return 2.0 ** (-8.0 * (h + 1) / MOBA_HEADS)


def _moba_prompt_body(q_ref, k_ref, v_ref, km_ref, og_ref, o_ref, m_sc, l_sc, acc_sc, sel_sc, qst_sc, *, nblk):
    i = pl.program_id(1)
    tq = MOBA_BLOCK
    n_pairs = MOBA_HEADS // 2
    nb_rows = -(-nblk // 8) * 8
    lane = lax.broadcasted_iota(jnp.int32, (tq, LANES), 1)
    low = lane < HEAD_DIM
    row2 = lax.broadcasted_iota(jnp.int32, (tq, MOBA_BLOCK), 0)
    col2 = lax.broadcasted_iota(jnp.int32, (tq, MOBA_BLOCK), 1)
    causal = col2 <= row2
    eye_bf = jnp.where(row2 == col2, 1.0, 0.0).astype(BF16)
    colrow = lax.broadcasted_iota(jnp.int32, (1, MOBA_BLOCK), 1).astype(F32)

    def pair_cols(pr):
        return slice(pr * HEAD_PAIR, (pr + 1) * HEAD_PAIR)

    def scores(pr, j):
        start = pl.multiple_of(j * MOBA_BLOCK, MOBA_BLOCK)
        kj = k_ref[pl.ds(start, MOBA_BLOCK), pair_cols(pr)]
        s = lax.dot_general(qst_sc[pr], kj, NT_DIMS, preferred_element_type=F32)
        off = colrow + ((j - i) * MOBA_BLOCK).astype(F32)
        return [s[hh * tq:(hh + 1) * tq] + _alibi_slope(2 * pr + hh) * off for hh in range(2)]

    def pv(pr, j, ps):
        start = pl.multiple_of(j * MOBA_BLOCK, MOBA_BLOCK)
        vj = v_ref[pl.ds(start, MOBA_BLOCK), pair_cols(pr)]
        return jnp.dot(jnp.concatenate(ps, axis=0).astype(BF16), vj, preferred_element_type=F32)

    gates_t = []
    for pr in range(n_pairs):
        q_pair = q_ref[:, pair_cols(pr)]
        km_pair = km_ref[0, :, pair_cols(pr)]
        q_heads = [jnp.where(low, q_pair, 0.0), jnp.where(low, 0.0, q_pair)]
        qst_sc[pr] = (jnp.concatenate(q_heads, axis=0) * QK_SCALE).astype(BF16)
        km_hi = km_pair.astype(BF16)
        km_lo = (km_pair - km_hi.astype(F32)).astype(BF16)
        for hh in range(2):
            q_hi = q_heads[hh].astype(BF16)
            q_lo = (q_heads[hh] - q_hi.astype(F32)).astype(BF16)
            gates_t.append((lax.dot_general(km_hi, q_hi, NT_DIMS, preferred_element_type=F32)
                            + (lax.dot_general(km_hi, q_lo, NT_DIMS, preferred_element_type=F32)
                               + lax.dot_general(km_lo, q_hi, NT_DIMS, preferred_element_type=F32)))[:nb_rows])
    blk_all = lax.broadcasted_iota(jnp.int32, (nb_rows, MOBA_HEADS * tq), 0)
    blk_all_f = blk_all.astype(F32)
    g = jnp.where(blk_all < i, jnp.concatenate(gates_t, axis=1), NEG_INF)
    sel_t = jnp.zeros((nb_rows, MOBA_HEADS * tq), F32)
    for _ in range(MOBA_TOPK):
        mx = jnp.max(g, axis=0, keepdims=True)
        idx = jnp.min(jnp.where(g == mx, blk_all_f, float(LANES)), axis=0, keepdims=True)
        pick = (blk_all_f == idx) & (mx > NEG_INF)
        sel_t = jnp.where(pick, 1.0, sel_t)
        g = jnp.where(pick, NEG_INF, g)
    sel_t = jnp.concatenate([sel_t, jnp.zeros((LANES - nb_rows, MOBA_HEADS * tq), F32)], axis=0).astype(BF16)
    for h in range(MOBA_HEADS):
        sel_sc[h] = lax.dot_general(eye_bf, sel_t[:, h * tq:(h + 1) * tq], NT_DIMS, preferred_element_type=F32)

    for pr in range(n_pairs):
        s_own = scores(pr, i)
        ps = []
        for hh in range(2):
            sh = jnp.where(causal, s_own[hh], NEG_INF)
            m = jnp.max(sh, axis=-1, keepdims=True)
            p = jnp.exp(sh - m)
            m_sc[2 * pr + hh] = jnp.broadcast_to(m, (tq, LANES))
            l_sc[2 * pr + hh] = jnp.broadcast_to(jnp.sum(p, axis=-1, keepdims=True), (tq, LANES))
            ps.append(p)
        acc_sc[pr] = pv(pr, i, ps)

    def past(j, carry):
        for pr in range(n_pairs):
            s_j = scores(pr, j)
            ps, alphas = [], []
            for hh in range(2):
                h = 2 * pr + hh
                selcol = jnp.sum(jnp.where(lane == j, sel_sc[h], 0.0), axis=-1, keepdims=True)
                sh = jnp.where(selcol > 0.0, s_j[hh], NEG_INF)
                m_prev = m_sc[h]
                m_new = jnp.maximum(m_prev, jnp.max(sh, axis=-1, keepdims=True))
                alpha = jnp.exp(m_prev - m_new)
                p = jnp.exp(sh - jnp.concatenate([m_new, m_new], axis=1))
                l_sc[h] = alpha * l_sc[h] + jnp.sum(p, axis=-1, keepdims=True)
                m_sc[h] = m_new
                ps.append(p)
                alphas.append(alpha)
            acc_sc[pr] = jnp.concatenate(alphas, axis=0) * acc_sc[pr] + pv(pr, j, ps)
        return carry

    span2 = 2 * MOBA_BLOCK
    colrow2 = lax.broadcasted_iota(jnp.int32, (1, span2), 1).astype(F32)

    def past2(jj, carry):
        j0 = 2 * jj
        start = pl.multiple_of(j0 * MOBA_BLOCK, MOBA_BLOCK)
        off = colrow2 + ((j0 - i) * MOBA_BLOCK).astype(F32)
        for pr in range(n_pairs):
            s2 = lax.dot_general(qst_sc[pr], k_ref[pl.ds(start, span2), pair_cols(pr)], NT_DIMS,
                                 preferred_element_type=F32)
            ps, alphas = [], []
            for hh in range(2):
                h = 2 * pr + hh
                sh = s2[hh * tq:(hh + 1) * tq] + _alibi_slope(h) * off
                halves = []
                for bb in range(2):
                    selcol = jnp.sum(jnp.where(lane == j0 + bb, sel_sc[h], 0.0), axis=-1, keepdims=True)
                    halves.append(jnp.where(selcol > 0.0, sh[:, bb * MOBA_BLOCK:(bb + 1) * MOBA_BLOCK], NEG_INF))
                sh = jnp.concatenate(halves, axis=1)
                m_prev = m_sc[h]
                m_new = jnp.maximum(m_prev, jnp.max(sh, axis=-1, keepdims=True))
                alpha = jnp.exp(m_prev - m_new)
                p = jnp.exp(sh - jnp.concatenate([m_new] * (span2 // LANES), axis=1))
                l_sc[h] = alpha * l_sc[h] + jnp.sum(p, axis=-1, keepdims=True)
                m_sc[h] = m_new
                ps.append(p)
                alphas.append(alpha)
            pv2 = jnp.dot(jnp.concatenate(ps, axis=0).astype(BF16), v_ref[pl.ds(start, span2), pair_cols(pr)],
                          preferred_element_type=F32)
            acc_sc[pr] = jnp.concatenate(alphas, axis=0) * acc_sc[pr] + pv2
        return carry

    lax.fori_loop(0, i // 2, past2, 0)

    @pl.when(i % 2 == 1)
    def _odd_block():
        past(i - 1, 0)

    outs = []
    for pr in range(n_pairs):
        acc = acc_sc[pr]
        outs.append(jnp.where(low, acc[:tq] / l_sc[2 * pr], acc[tq:] / l_sc[2 * pr + 1]))
    o_ref[...] = _rms(jnp.concatenate(outs, axis=1)) * og_ref[...]


def _moba_prompt(q, k_bf, v_bf, kmean_pad, og, b, t):
    n, w = q.shape
    nblk = t // MOBA_BLOCK
    tq = MOBA_BLOCK
    return pl.pallas_call(
        functools.partial(_moba_prompt_body, nblk=nblk),
        grid=(b, nblk),
        in_specs=[
            pl.BlockSpec((tq, w), lambda bi, i: (bi * nblk + i, 0)),
            pl.BlockSpec((t, w), lambda bi, i: (bi, 0)),
            pl.BlockSpec((t, w), lambda bi, i: (bi, 0)),
            pl.BlockSpec((1, LANES, w), lambda bi, i: (bi, 0, 0)),
            pl.BlockSpec((1, w), lambda bi, i: (0, 0)),
        ],
        out_specs=pl.BlockSpec((tq, w), lambda bi, i: (bi * nblk + i, 0)),
        out_shape=jax.ShapeDtypeStruct((n, w), F32),
        scratch_shapes=[
            pltpu.VMEM((MOBA_HEADS, tq, LANES), F32),
            pltpu.VMEM((MOBA_HEADS, tq, LANES), F32),
            pltpu.VMEM((MOBA_HEADS // 2, 2 * tq, LANES), F32),
            pltpu.VMEM((MOBA_HEADS, tq, LANES), F32),
            pltpu.VMEM((MOBA_HEADS // 2, 2 * tq, LANES), BF16),
        ],
        compiler_params=_cparams(("parallel", "arbitrary"), 40),
        name="moba_prompt",
    )(q, k_bf, v_bf, kmean_pad, og.reshape(1, w))


SAMPLE_BLOCKS_PER_STEP = 8
PAGES_PER_BLOCK = MOBA_BLOCK // PAGE_SIZE
PAGES_PER_STEP = SAMPLE_BLOCKS_PER_STEP * PAGES_PER_BLOCK


def _moba_sample_body(pt_ref, q_ref, kn_ref, vn_ref, slope_ref, og_ref, *rest, n_past_blocks, t_new):
    kp = rest[:PAGES_PER_STEP]
    vp = rest[PAGES_PER_STEP:2 * PAGES_PER_STEP]
    o_ref, o_sc, m_sc, l_sc, kmt_sc = rest[2 * PAGES_PER_STEP:]
    c = pl.program_id(1)
    rows = MOBA_HEADS * t_new
    row_h = lax.broadcasted_iota(jnp.int32, (rows, MOBA_WIDTH), 0) // t_new
    lane_h = lax.broadcasted_iota(jnp.int32, (rows, MOBA_WIDTH), 1) // HEAD_DIM
    bd = row_h == lane_h
    q = q_ref[...]
    qbd = jnp.where(bd, jnp.concatenate([q] * MOBA_HEADS, axis=0), 0.0)
    qbd_bf = (qbd * QK_SCALE).astype(BF16)
    slope = slope_ref[...]
    slope2 = jnp.concatenate([slope, slope], axis=1)
    colf = lax.broadcasted_iota(jnp.int32, (rows, MOBA_BLOCK), 1).astype(F32)

    @pl.when(c == 0)
    def _init():
        kmt_sc[...] = jnp.zeros(kmt_sc.shape, F32)

    kt_bf = []
    for jj in range(SAMPLE_BLOCKS_PER_STEP):
        kt_pages = [kp[PAGES_PER_BLOCK * jj + p][0] for p in range(PAGES_PER_BLOCK)]
        ksum = kt_pages[0]
        for kt in kt_pages[1:]:
            ksum = ksum + kt
        kmt_sc[c, :, jj:jj + 1] = jnp.sum(ksum, axis=-1, keepdims=True) * (1.0 / MOBA_BLOCK)
        kt_bf.append(jnp.concatenate(kt_pages, axis=1).astype(BF16))
    s_all = jnp.dot(qbd_bf, jnp.concatenate(kt_bf, axis=1), preferred_element_type=F32)

    for jj in range(SAMPLE_BLOCKS_PER_STEP):
        j = c * SAMPLE_BLOCKS_PER_STEP + jj
        s = s_all[:, jj * MOBA_BLOCK:(jj + 1) * MOBA_BLOCK] + slope2 * (
            colf + ((j - n_past_blocks) * MOBA_BLOCK).astype(F32))
        m = jnp.max(s, axis=-1, keepdims=True)
        p = jnp.exp(s - m)
        vt_bf = jnp.concatenate([vp[PAGES_PER_BLOCK * jj + q][0] for q in range(PAGES_PER_BLOCK)],
                                axis=1).astype(BF16)
        o = lax.dot_general(p.astype(BF16), vt_bf, NT_DIMS, preferred_element_type=F32)
        o_sc[j] = jnp.where(bd, o, 0.0)
        m_sc[j] = jnp.broadcast_to(m, (rows, LANES))
        l_sc[j] = jnp.broadcast_to(jnp.sum(p, axis=-1, keepdims=True), (rows, LANES))

    @pl.when(c == pl.num_programs(1) - 1)
    def _finish():
        tq = lax.broadcasted_iota(jnp.int32, (rows, LANES), 0) % t_new
        kn = kn_ref[...]
        vn = vn_ref[...]
        qs = qbd * QK_SCALE
        s_own = []
        m_run = jnp.full((rows, LANES), NEG_INF, F32)
        for cc in range(t_new):
            sc = jnp.sum(qs * kn[cc:cc + 1, :], axis=-1, keepdims=True) + slope * float(cc)
            sc = jnp.where(tq >= cc, sc, NEG_INF)
            s_own.append(sc)
            m_run = jnp.maximum(m_run, sc)

        blk = lax.broadcasted_iota(jnp.int32, (rows, LANES), 1)
        blk_f = blk.astype(F32)
        gates = jnp.zeros((rows, LANES), F32)
        for cs in range(n_past_blocks // SAMPLE_BLOCKS_PER_STEP):
            g_cs = jnp.dot(qbd, kmt_sc[cs], precision=lax.Precision.HIGHEST, preferred_element_type=F32)
            g_cs = jnp.where(blk < SAMPLE_BLOCKS_PER_STEP, g_cs, 0.0)
            gates = gates + (pltpu.roll(g_cs, cs * SAMPLE_BLOCKS_PER_STEP, 1) if cs else g_cs)
        g = jnp.where(blk < n_past_blocks, gates, NEG_INF)
        sel = jnp.zeros((rows, LANES), F32)
        for _ in range(MOBA_TOPK):
            mx = jnp.max(g, axis=-1, keepdims=True)
            idx = jnp.min(jnp.where(g == mx, blk_f, float(LANES)), axis=-1, keepdims=True)
            pick = (blk_f == idx) & (mx > NEG_INF)
            sel = jnp.where(pick, 1.0, sel)
            g = jnp.where(pick, NEG_INF, g)

        m_all = jnp.full((rows, LANES), NEG_INF, F32)
        l_all = jnp.zeros((rows, LANES), F32)
        for j in range(n_past_blocks):
            m_all = jnp.where(blk == j, m_sc[j], m_all)
            l_all = jnp.where(blk == j, l_sc[j], l_all)
        m_run = jnp.maximum(m_run, jnp.max(jnp.where(sel > 0.0, m_all, NEG_INF), axis=-1, keepdims=True))
        w_all = jnp.where(sel > 0.0, jnp.exp(m_all - m_run), 0.0)

        l_run = jnp.broadcast_to(jnp.sum(w_all * l_all, axis=-1, keepdims=True), (rows, LANES))
        o_run = jnp.zeros((rows, MOBA_WIDTH), F32)
        for cc in range(t_new):
            p = jnp.exp(s_own[cc] - m_run)
            l_run = l_run + p
            o_run = o_run + jnp.concatenate([p] * (MOBA_WIDTH // LANES), axis=1) * vn[cc:cc + 1, :]
        o_run = jnp.where(bd, o_run, 0.0)
        for j in range(n_past_blocks):
            o_run = o_run + w_all[:, j:j + 1] * o_sc[j]
        o_bd = o_run / jnp.concatenate([l_run] * (MOBA_WIDTH // LANES), axis=1)
        attn = o_bd[0:t_new]
        for h in range(1, MOBA_HEADS):
            attn = attn + o_bd[h * t_new:(h + 1) * t_new]
        o_ref[...] = _rms(attn) * og_ref[...]


def _moba_sample(page_table, q, k_new, v_new, cache_kt, cache_vt, og, b, t_new):
    n_pages = page_table.shape[1]
    n_past_blocks = n_pages // PAGES_PER_BLOCK
    assert n_past_blocks <= LANES
    assert n_past_blocks % SAMPLE_BLOCKS_PER_STEP == 0
    n_steps = n_past_blocks // SAMPLE_BLOCKS_PER_STEP
    rows = MOBA_HEADS * t_new
    w = MOBA_WIDTH
    slopes = jnp.exp2(-8.0 * jnp.arange(1, MOBA_HEADS + 1, dtype=F32) / MOBA_HEADS)
    slope_rows = jnp.broadcast_to(jnp.repeat(slopes, t_new)[:, None], (rows, LANES))

    def page_spec(p):
        return pl.BlockSpec((1, w, PAGE_SIZE), lambda bi, c, pt: (pt[bi, c * PAGES_PER_STEP + p], 0, 0))

    row_spec = pl.BlockSpec((t_new, w), lambda bi, c, pt: (bi, 0))
    grid_spec = pltpu.PrefetchScalarGridSpec(
        num_scalar_prefetch=1,
        grid=(b, n_steps),
        in_specs=[row_spec, row_spec, row_spec,
                  pl.BlockSpec((rows, LANES), lambda bi, c, pt: (0, 0)),
                  pl.BlockSpec((1, w), lambda bi, c, pt: (0, 0))]
                 + [page_spec(p) for p in range(PAGES_PER_STEP)]
                 + [page_spec(p) for p in range(PAGES_PER_STEP)],
        out_specs=row_spec,
        scratch_shapes=[
            pltpu.VMEM((n_past_blocks, rows, w), F32),
            pltpu.VMEM((n_past_blocks, rows, LANES), F32),
            pltpu.VMEM((n_past_blocks, rows, LANES), F32),
            pltpu.VMEM((n_steps, w, LANES), F32),
        ],
    )
    return pl.pallas_call(
        functools.partial(_moba_sample_body, n_past_blocks=n_past_blocks, t_new=t_new),
        grid_spec=grid_spec,
        out_shape=jax.ShapeDtypeStruct((b * t_new, w), F32),
        compiler_params=_cparams(("parallel", "arbitrary"), 48),
        name="moba_sample",
    )(page_table, q, k_new, v_new, slope_rows, og.reshape(1, w),
      *([cache_kt] * PAGES_PER_STEP), *([cache_vt] * PAGES_PER_STEP))


def _mem_attn_body(q_ref, mk_ref, mv_ref, og_ref, o_ref, *, tm):
    lane = lax.broadcasted_iota(jnp.int32, (tm, LANES), 1)
    low = lane < HEAD_DIM
    outs = []
    for pr in range(MEM_HEADS // 2):
        cs = slice(pr * HEAD_PAIR, (pr + 1) * HEAD_PAIR)
        q_pair = q_ref[:, cs]
        qst = (jnp.concatenate([jnp.where(low, q_pair, 0.0), jnp.where(low, 0.0, q_pair)], axis=0)
               * QK_SCALE).astype(BF16)
        s = lax.dot_general(qst, mk_ref[0, :, cs].astype(BF16), NT_DIMS, preferred_element_type=F32)
        p = jnp.exp(s - jnp.max(s, axis=-1, keepdims=True))
        l = jnp.sum(p, axis=-1, keepdims=True)
        o = jnp.dot(p.astype(BF16), mv_ref[0, :, cs].astype(BF16), preferred_element_type=F32) / l
        outs.append(jnp.where(low, o[:tm], o[tm:]))
    o_ref[...] = _rms(jnp.concatenate(outs, axis=1)) * og_ref[...]


def _mem_attn(qm, mem_k, mem_v, og, b, t, tm):
    n, w = qm.shape
    assert t % tm == 0
    steps = t // tm
    return pl.pallas_call(
        functools.partial(_mem_attn_body, tm=tm),
        grid=(b, steps),
        in_specs=[
            pl.BlockSpec((tm, w), lambda bi, i: (bi * steps + i, 0)),
            pl.BlockSpec((1, MEM_TOKENS, w), lambda bi, i: (bi, 0, 0)),
            pl.BlockSpec((1, MEM_TOKENS, w), lambda bi, i: (bi, 0, 0)),
            pl.BlockSpec((1, w), lambda bi, i: (0, 0)),
        ],
        out_specs=pl.BlockSpec((tm, w), lambda bi, i: (bi * steps + i, 0)),
        out_shape=jax.ShapeDtypeStruct((n, w), F32),
        compiler_params=_cparams(("parallel", "arbitrary"), 32),
        name="mem_attn",
    )(qm, mem_k, mem_v, og.reshape(1, w))


ROUTER_GROUP_LANE0 = N_EXPERTS
ROUTER_GID_LANE = 64


def _outproj_router_body(op_ref, oa_ref, om_ref, x_ref, wo_ref, g2_ref, wr_ref, br_ref,
                         x1_ref, h2_ref, comb_ref, gcnt_ref, *, tm):
    a0, a1 = POOL_WIDTH, POOL_WIDTH + MOBA_WIDTH
    y = x_ref[...]
    y = y + jnp.dot(op_ref[...].astype(BF16), wo_ref[0:a0, :], preferred_element_type=F32)
    y = y + jnp.dot(oa_ref[...].astype(BF16), wo_ref[a0:a1, :], preferred_element_type=F32)
    y = y + jnp.dot(om_ref[...].astype(BF16), wo_ref[a1:, :], preferred_element_type=F32)
    x1_ref[...] = y
    h2 = _rms(y) * g2_ref[...]
    h2_hi = h2.astype(BF16)
    h2_lo = (h2 - h2_hi.astype(F32)).astype(BF16)
    h2_ref[...] = h2_hi
    hw =jnp.dot(h2_hi, wr_ref[...], preferred_element_type=F32)
    lw = jnp.dot(h2_lo, wr_ref[:, :LANES], preferred_element_type=F32)
    logits = (hw[:, :LANES] + (hw[:, LANES:] + lw)) + br_ref[...]
    lane_f = lax.broadcasted_iota(jnp.int32, (tm, LANES), 1).astype(F32)
    big = float(LANES)
    g_lo = float(ROUTER_GROUP_LANE0)
    is_g = (lane_f >= g_lo) & (lane_f < g_lo + MOE_GROUPS)
    lg = jnp.where(is_g, logits, NEG_INF)
    mg = jnp.max(lg, axis=-1, keepdims=True)
    pg_top = 1.0 / jnp.sum(jnp.exp(lg - mg), axis=-1, keepdims=True)
    gidx = jnp.min(jnp.where(lg == mg, lane_f, big), axis=-1, keepdims=True) - g_lo
    e_lo = gidx * EXPERTS_PER_GROUP
    in_grp = (lane_f >= e_lo) & (lane_f < e_lo + EXPERTS_PER_GROUP)
    le = jnp.where(in_grp, logits, NEG_INF)
    m1 = jnp.max(le, axis=-1, keepdims=True)
    se = jnp.sum(jnp.exp(le - m1), axis=-1, keepdims=True)
    i1 = jnp.min(jnp.where(le == m1, lane_f, big), axis=-1, keepdims=True)
    le2 = jnp.where(lane_f == i1, NEG_INF, le)
    m2 = jnp.max(le2, axis=-1, keepdims=True)
    i2 = jnp.min(jnp.where(le2 == m2, lane_f, big), axis=-1, keepdims=True)
    p1 = 1.0 / se
    p2 = jnp.exp(m2 - m1) / se
    den = p1 + p2
    comb_ref[...] = jnp.where(lane_f == i1, pg_top * (p1 / den),
                              jnp.where(lane_f == i2, pg_top * (p2 / den),
                                        jnp.where(lane_f == float(ROUTER_GID_LANE), gidx, 0.0)))
    gcnt_ref[0] = jnp.sum(jnp.where(lane_f == gidx, 1.0, 0.0), axis=0, keepdims=True)


def _outproj_router(o_pool, o_attn, o_mem, x, wo_bf, g2, wr, br, tm):
    n, d = x.shape
    assert n % tm == 0
    row = lambda wdt: pl.BlockSpec((tm, wdt), lambda i: (i, 0))
    full = lambda shp: pl.BlockSpec(shp, lambda i: (0, 0))
    return pl.pallas_call(
        functools.partial(_outproj_router_body, tm=tm),
        grid=(n // tm,),
        in_specs=[row(POOL_WIDTH), row(MOBA_WIDTH), row(MEM_WIDTH), row(d),
                  full((d, d)), full((1, d)), full((d, 2 * LANES)), full((1, LANES))],
        out_specs=[row(d), row(d), row(LANES), pl.BlockSpec((1, 1, LANES), lambda i: (i, 0, 0))],
        out_shape=[jax.ShapeDtypeStruct((n, d), F32), jax.ShapeDtypeStruct((n, d), BF16),
                   jax.ShapeDtypeStruct((n, LANES), F32), jax.ShapeDtypeStruct((n // tm, 1, LANES), F32)],
        compiler_params=_cparams(("parallel",), 40),
        name="outproj_router",
    )(o_pool, o_attn, o_mem, x, wo_bf, g2.reshape(1, d), wr, br)


MOE_EXPERTS_PER_STEP = 4
MOE_RUN_ALIGN = 16
MOE_SORT_PAD = LANES


def _moe_window(tm):
    return max(LANES, tm // MOE_GROUPS + 32)


def _moe_body(cnt_ref, h2_ref, comb_ref, x1_hbm, wg_ref, wu_ref, wd_ref, y_ref,
              xs_sc, cs_sc, ys_sc, p_sc, pt_sc, sem, *, tm, win):
    i = pl.program_id(0)
    s = pl.program_id(1)
    eps = MOE_EXPERTS_PER_STEP
    span = tm + MOE_SORT_PAD
    g = s // (EXPERTS_PER_GROUP // eps)
    starts, nxt = [], jnp.int32(0)
    for gg in range(MOE_GROUPS):
        starts.append(nxt)
        nxt = nxt + (cnt_ref[i * MOE_GROUPS + gg] + (MOE_RUN_ALIGN - 1)) // MOE_RUN_ALIGN * MOE_RUN_ALIGN
    x1_copy = pltpu.make_async_copy(x1_hbm.at[pl.ds(pl.multiple_of(i * tm, tm), tm)], y_ref, sem)

    @pl.when(s == 0)
    def _sort():
        x1_copy.start()
        comb = comb_ref[...]
        lane = lax.broadcasted_iota(jnp.int32, (tm, LANES), 1)
        lane_f = lane.astype(F32)
        gid = jnp.sum(jnp.where(lane == ROUTER_GID_LANE, comb, 0.0), axis=-1, keepdims=True)
        onehot = jnp.where(lane_f == gid, 1.0, 0.0)
        earlier_tok = lax.broadcasted_iota(jnp.int32, (tm, tm), 1) < lax.broadcasted_iota(jnp.int32, (tm, tm), 0)
        earlier = jnp.dot(jnp.where(earlier_tok, 1.0, 0.0).astype(BF16), onehot.astype(BF16),
                          preferred_element_type=F32)
        start_vec = jnp.zeros((tm, LANES), F32)
        for gg in range(MOE_GROUPS):
            start_vec = jnp.where(lane == gg, starts[gg].astype(F32), start_vec)
        dest = jnp.sum(onehot * (earlier + start_vec), axis=-1, keepdims=True)
        hi = jnp.floor(dest * (1.0 / 32.0))
        lo = dest - 32.0 * hi
        dest_cols = jnp.where(lane == 0, hi, jnp.where(lane == 1, lo, 0.0)).astype(BF16)
        r16 = lax.broadcasted_iota(jnp.int32, (16, LANES), 0)
        l16 = lax.broadcasted_iota(jnp.int32, (16, LANES), 1)
        pick = jnp.where((r16 == 0) & (l16 == 0), 32.0, jnp.where((r16 == 0) & (l16 == 1), 1.0, 0.0)).astype(BF16)
        dest_row = lax.dot_general(pick, dest_cols, NT_DIMS, preferred_element_type=F32)[0:1, :]
        p = jnp.where(lax.broadcasted_iota(jnp.int32, (span, tm), 0).astype(F32) == dest_row, 1.0, 0.0).astype(BF16)
        p_sc[...] = p
        pt_sc[...] = jnp.where(lax.broadcasted_iota(jnp.int32, (tm, span), 1).astype(F32) == dest,
                               1.0, 0.0).astype(BF16)
        xs_sc[0:span, :] = jnp.dot(p, h2_ref[...], preferred_element_type=F32).astype(BF16)
        xs_sc[span:, :] = jnp.zeros((win, D_MODEL), BF16)
        c1 = comb.astype(BF16)
        r1 = comb - c1.astype(F32)
        c2 = r1.astype(BF16)
        c3 = (r1 - c2.astype(F32)).astype(BF16)
        cs_sc[0:span, :] = (jnp.dot(p, c1, preferred_element_type=F32) + jnp.dot(p, c2, preferred_element_type=F32)
                            + jnp.dot(p, c3, preferred_element_type=F32))
        cs_sc[span:, :] = jnp.zeros((win, LANES), F32)
        ys_sc[...] = jnp.zeros(ys_sc.shape, F32)

    run_start = starts[0]
    for gg in range(1, MOE_GROUPS):
        run_start = jnp.where(g == gg, starts[gg], run_start)
    run_len = cnt_ref[i * MOE_GROUPS + g]
    lane_w = lax.broadcasted_iota(jnp.int32, (win, LANES), 1)

    def window(w, carry):
        r0 = pl.multiple_of(run_start + w * win, MOE_RUN_ALIGN)
        rows = xs_sc[pl.ds(r0, win), :]
        comb_rows = cs_sc[pl.ds(r0, win), :]
        acc = None
        for j in range(eps):
            e = s * eps + j
            a = jnp.dot(rows, wg_ref[j], preferred_element_type=F32)
            b = jnp.dot(rows, wu_ref[j], preferred_element_type=F32)
            ce = jnp.sum(jnp.where(lane_w == e, comb_rows, 0.0), axis=-1, keepdims=True)
            hid = ((a * jax.nn.sigmoid(a)) * b * ce).astype(BF16)
            c = jnp.dot(hid, wd_ref[j], preferred_element_type=F32)
            acc = c if acc is None else acc + c
        ys_sc[pl.ds(r0, win), :] += acc
        return carry

    lax.fori_loop(0, (run_len + win - 1) // win, window, 0)

    @pl.when(s == pl.num_programs(1) - 1)
    def _unsort():
        ys = ys_sc[0:span, :]
        moe = jnp.dot(pt_sc[...], ys.astype(BF16), preferred_element_type=F32)
        x1_copy.wait()
        y_ref[...] = y_ref[...] + moe


def _moe(tile_cnt, h2, comb, x1, wg_bf, wu_bf, wd_bf, tm):
    n, d = x1.shape
    eps = MOE_EXPERTS_PER_STEP
    assert n % tm == 0 and EXPERTS_PER_GROUP % eps == 0 and MOE_GROUPS * (MOE_RUN_ALIGN - 1) <= MOE_SORT_PAD
    win = _moe_window(tm)
    assert win % MOE_RUN_ALIGN == 0
    span = tm + MOE_SORT_PAD
    grid_spec = pltpu.PrefetchScalarGridSpec(
        num_scalar_prefetch=1,
        grid=(n // tm, N_EXPERTS // eps),
        in_specs=[
            pl.BlockSpec((tm, d), lambda i, s, cnt: (i, 0)),
            pl.BlockSpec((tm, LANES), lambda i, s, cnt: (i, 0)),
            pl.BlockSpec(memory_space=pl.ANY),
            pl.BlockSpec((eps, d, D_EXPERT), lambda i, s, cnt: (s, 0, 0)),
            pl.BlockSpec((eps, d, D_EXPERT), lambda i, s, cnt: (s, 0, 0)),
            pl.BlockSpec((eps, D_EXPERT, d), lambda i, s, cnt: (s, 0, 0)),
        ],
        out_specs=pl.BlockSpec((tm, d), lambda i, s, cnt: (i, 0)),
        scratch_shapes=[
            pltpu.VMEM((span + win, d), BF16),
            pltpu.VMEM((span + win, LANES), F32),
            pltpu.VMEM((span + win, d), F32),
            pltpu.VMEM((span, tm), BF16),
            pltpu.VMEM((tm, span), BF16),
            pltpu.SemaphoreType.DMA(()),
        ],
    )
    return pl.pallas_call(
        functools.partial(_moe_body, tm=tm, win=win),
        grid_spec=grid_spec,
        out_shape=jax.ShapeDtypeStruct((n, d), F32),
        compiler_params=_cparams(("parallel", "arbitrary"), 56),
        name="moe",
    )(tile_cnt, h2, comb, x1, wg_bf, wu_bf, wd_bf)


def _block_diag_ones(width, group):
    r = lax.broadcasted_iota(jnp.int32, (width, width), 0) // group
    c = lax.broadcasted_iota(jnp.int32, (width, width), 1) // group
    return (r == c).astype(BF16)


def _pick_tile(n, candidates):
    for c in candidates:
        if n % c == 0:
            return c
    return n


IN_SEGS = (
    (0, POOL_WIDTH, False, False, False, False),
    (POOL_WIDTH, MOBA_WIDTH, True, False, False, False),
    (POOL_WIDTH + MOBA_WIDTH, MOBA_WIDTH, True, True, True, True),
    (POOL_WIDTH + 2 * MOBA_WIDTH, MOBA_WIDTH, False, True, True, False),
    (POOL_WIDTH + 3 * MOBA_WIDTH, MEM_WIDTH, True, False, False, False),
)
IN_SEGS_SAMPLE = tuple((c0, wd, hn, False, False, False) for (c0, wd, hn, _, _, _) in IN_SEGS)
MEMKV_SEGS = ((0, MEM_WIDTH, True, False, False, False), (MEM_WIDTH, MEM_WIDTH, False, False, False, False))


def kernel(x_prompt, x_sample, mem_prompt, cache_k, cache_v, state_pool, cache_mem_k, cache_mem_v,
           page_table, norm1_gain, w_in, pool_w, pool_scale, moba_q_gain, moba_k_gain,
           mem_norm_gain, w_mem_kv, mem_q_gain, mem_k_gain, out_gain, w_out, norm2_gain,
           router_group_w, router_group_b, router_expert_w, router_expert_b, w_gate, w_up, w_down):
    bp, tp, d = x_prompt.shape
    bs, ts, _ = x_sample.shape
    n_p, n_s = bp * tp, bs * ts

    w_in_bf = w_in.astype(BF16)
    w_mem_bf = w_mem_kv.astype(BF16)
    wo_bf = w_out.astype(BF16)
    ones_bd = _block_diag_ones(MOBA_WIDTH, HEAD_DIM)
    head_gain_in = jnp.concatenate([
        jnp.ones((POOL_WIDTH,), F32), jnp.tile(moba_q_gain, MOBA_HEADS), jnp.tile(moba_k_gain, MOBA_HEADS),
        jnp.ones((MOBA_WIDTH,), F32), jnp.tile(mem_q_gain, MEM_HEADS)])
    head_gain_mem = jnp.concatenate([jnp.tile(mem_k_gain, MEM_HEADS), jnp.ones((MEM_WIDTH,), F32)])
    wbd = jnp.zeros((POOL_WIDTH, POOL_WIDTH), F32)
    for g in range(len(POOL_WINDOWS)):
        sl = slice(g * POOL_GROUP_WIDTH, (g + 1) * POOL_GROUP_WIDTH)
        wbd = wbd.at[sl, sl].set(pool_w[g])
    wbd_bf = wbd.astype(BF16)
    og_pool, og_attn, og_mem = (out_gain[:POOL_WIDTH], out_gain[POOL_WIDTH:POOL_WIDTH + MOBA_WIDTH],
                                out_gain[POOL_WIDTH + MOBA_WIDTH:])
    wr = jnp.zeros((d, LANES), F32)
    wr = wr.at[:, :N_EXPERTS].set(router_expert_w).at[:, N_EXPERTS:N_EXPERTS + MOE_GROUPS].set(router_group_w)
    br = jnp.zeros((1, LANES), F32)
    br = br.at[0, :N_EXPERTS].set(router_expert_b).at[0, N_EXPERTS:N_EXPERTS + MOE_GROUPS].set(router_group_b)
    wr_hi = wr.astype(BF16)
    wr = jnp.concatenate([wr_hi, (wr - wr_hi.astype(F32)).astype(BF16)], axis=1)
    wg = w_gate.reshape(N_EXPERTS, d, D_EXPERT).astype(BF16)
    wu = w_up.reshape(N_EXPERTS, d, D_EXPERT).astype(BF16)
    wd = w_down.reshape(N_EXPERTS, D_EXPERT, d).astype(BF16)

    def tail(o_pool, o_attn, o_mem, x2d):
        n = x2d.shape[0]
        tm_r = _pick_tile(n, (512, 256))
        x1, h2, comb, gcnt = _outproj_router(o_pool, o_attn, o_mem, x2d, wo_bf, norm2_gain, wr, br, tm_r)
        tm = _pick_tile(n, (1024, 512, 256))
        tile_cnt = gcnt[:, 0, :MOE_GROUPS].reshape(n // tm, tm // tm_r, MOE_GROUPS).sum(axis=1)
        return _moe(tile_cnt.astype(jnp.int32).reshape(-1), h2, comb, x1, wg, wu, wd, tm)

    mem_k_p, mem_v_p = _norm_proj(mem_prompt.reshape(bp * MEM_TOKENS, d), mem_norm_gain, w_mem_bf,
                                  head_gain_mem, ones_bd, MEMKV_SEGS, _pick_tile(bp * MEM_TOKENS, (512, 256)))
    xp = x_prompt.reshape(n_p, d)
    u_p, q_p, kt_p, k_p_bf, kmean_p, vt_p, v_p_bf, qm_p = _norm_proj(
        xp, norm1_gain, w_in_bf, head_gain_in, ones_bd, IN_SEGS, _pick_tile(tp, (512, 256)), rows_per_batch=tp)
    o_pool_p = _pool(u_p.reshape(bp, tp, POOL_WIDTH), jnp.zeros((bp, 16, POOL_WIDTH), F32), wbd_bf,
                     pool_scale, og_pool, 0, 1, MOBA_BLOCK).reshape(n_p, POOL_WIDTH)
    nblk = tp // MOBA_BLOCK
    kmean_pad = jnp.pad(kmean_p.reshape(bp, nblk, MOBA_WIDTH), ((0, 0), (0, LANES - nblk), (0, 0)))
    o_attn_p = _moba_prompt(q_p, k_p_bf, v_p_bf, kmean_pad, og_attn, bp, tp)
    o_mem_p = _mem_attn(qm_p, mem_k_p.reshape(bp, MEM_TOKENS, MEM_WIDTH),
                        mem_v_p.reshape(bp, MEM_TOKENS, MEM_WIDTH), og_mem, bp, tp, _pick_tile(tp, (512, 256)))
    y_p = tail(o_pool_p, o_attn_p, o_mem_p, xp)

    past_len = page_table.shape[1] * PAGE_SIZE
    xs = x_sample.reshape(n_s, d)
    u_s, q_s, k_s, v_s, qm_s = _norm_proj(xs, norm1_gain, w_in_bf, head_gain_in, ones_bd, IN_SEGS_SAMPLE, n_s)
    u_s3 = u_s.reshape(bs, ts, POOL_WIDTH)
    buf16 = jnp.concatenate([jnp.zeros((bs, 1, POOL_WIDTH), F32), state_pool], axis=1)
    o_pool_s = _pool(u_s3, buf16, wbd_bf, pool_scale, og_pool, past_len, bs, ts).reshape(n_s, POOL_WIDTH)
    o_attn_s = _moba_sample(page_table, q_s, k_s, v_s,
                            cache_k.transpose(0, 2, 3, 1).reshape(-1, MOBA_WIDTH, PAGE_SIZE),
                            cache_v.transpose(0, 2, 3, 1).reshape(-1, MOBA_WIDTH, PAGE_SIZE),
                            og_attn, bs, ts)
    o_mem_s = _mem_attn(qm_s, cache_mem_k.reshape(bs, MEM_TOKENS, MEM_WIDTH),
                        cache_mem_v.reshape(bs, MEM_TOKENS, MEM_WIDTH), og_mem, bs, ts, ts)
    y_s = tail(o_pool_s, o_attn_s, o_mem_s, xs)

    pool_prompt = u_p.reshape(bp, tp, POOL_WIDTH)[:, tp - POOL_STATE:]
    pool_sample = jnp.concatenate([state_pool, u_s3], axis=1)[:, -POOL_STATE:]
    k_p = kt_p.reshape(bp, MOBA_HEADS, HEAD_DIM, tp).transpose(0, 3, 1, 2)
    v_p = vt_p.reshape(bp, MOBA_HEADS, HEAD_DIM, tp).transpose(0, 3, 1, 2)
    return (y_p.reshape(bp, tp, d), y_s.reshape(bs, ts, d), k_p, v_p,
            pool_prompt,
            mem_k_p.reshape(bp, MEM_TOKENS, MEM_HEADS, HEAD_DIM), mem_v_p.reshape(bp, MEM_TOKENS, MEM_HEADS, HEAD_DIM),
            k_s.reshape(bs, ts, MOBA_HEADS, HEAD_DIM), v_s.reshape(bs, ts, MOBA_HEADS, HEAD_DIM),
            pool_sample)
```

```python
import functools

import jax
import jax.numpy as jnp
from jax import lax
from jax.experimental import pallas as pl
from jax.experimental.pallas import tpu as pltpu

F32 = jnp.float32
BF16 = jnp.bfloat16

D_MODEL = 1024
HEAD_DIM = 64
POOL_WIDTH = 256
POOL_WINDOWS = (2, 4, 8, 16)
POOL_GROUP_WIDTH = 64
POOL_STATE = 15
MOBA_WIDTH = 512
MOBA_HEADS = 8
MOBA_BLOCK = 256
MOBA_TOPK = 3
MEM_WIDTH = 256
MEM_HEADS = 4
MEM_TOKENS = 256
PAGE_SIZE = 128
MOE_GROUPS = 4
EXPERTS_PER_GROUP = 8
N_EXPERTS = MOE_GROUPS * EXPERTS_PER_GROUP
D_EXPERT = 256
NORM_EPS = 1e-6

LANES = 128
HEAD_PAIR = 2 * HEAD_DIM
QK_SCALE = HEAD_DIM ** -0.5
NEG_INF = float("-inf")
MIB = 1024 * 1024
NT_DIMS = (((1,), (1,)), ((), ()))


def _cparams(semantics, vmem_mib):
    return pltpu.CompilerParams(dimension_semantics=semantics, vmem_limit_bytes=vmem_mib * MIB)


def _rms(y, eps=NORM_EPS):
    return y * lax.rsqrt(jnp.mean(y * y, axis=-1, keepdims=True) + eps)


def _norm_proj_body(x_ref, g_ref, w_ref, hg_ref, ones_ref, *outs, segs, tm):
    hb = (_rms(x_ref[...]) * g_ref[...]).astype(BF16)
    oi = 0
    for (c0, wd, headnorm, transposed, want_bf16, want_blockmean) in segs:
        y = jnp.dot(hb, w_ref[:, c0:c0 + wd], preferred_element_type=F32)
        if headnorm:
            sq = (y * y).astype(BF16)
            msq = jnp.dot(sq, ones_ref[:wd, :wd], preferred_element_type=F32) * (1.0 / HEAD_DIM)
            y = (y * lax.rsqrt(msq + NORM_EPS)) * hg_ref[:, c0:c0 + wd]
        if transposed:
            outs[oi][0] = y.T
        else:
            outs[oi][...] = y
        oi += 1
        if want_bf16:
            outs[oi][...] = y.astype(BF16)
            oi += 1
        if want_blockmean:
            for bi in range(tm // MOBA_BLOCK):
                outs[oi][bi] = jnp.mean(y[bi * MOBA_BLOCK:(bi + 1) * MOBA_BLOCK], axis=0, keepdims=True)
            oi += 1


def _norm_proj(x, gain, w_bf, head_gain, ones_bd, segs, tm, rows_per_batch=None):
    n, d = x.shape
    wtot = w_bf.shape[1]
    assert n % tm == 0
    out_shape, out_specs = [], []
    for (c0, wd, headnorm, transposed, want_bf16, want_blockmean) in segs:
        if transposed:
            assert rows_per_batch % tm == 0 and n % rows_per_batch == 0
            steps = rows_per_batch // tm
            out_shape.append(jax.ShapeDtypeStruct((n // rows_per_batch, wd, rows_per_batch), F32))
            out_specs.append(pl.BlockSpec((1, wd, tm), lambda i, steps=steps: (i // steps, 0, i % steps)))
        else:
            out_shape.append(jax.ShapeDtypeStruct((n, wd), F32))
            out_specs.append(pl.BlockSpec((tm, wd), lambda i: (i, 0)))
        if want_bf16:
            out_shape.append(jax.ShapeDtypeStruct((n, wd), BF16))
            out_specs.append(pl.BlockSpec((tm, wd), lambda i: (i, 0)))
        if want_blockmean:
            assert tm % MOBA_BLOCK == 0
            nb = tm // MOBA_BLOCK
            out_shape.append(jax.ShapeDtypeStruct((n // MOBA_BLOCK, 1, wd), F32))
            out_specs.append(pl.BlockSpec((nb, 1, wd), lambda i: (i, 0, 0)))
    return pl.pallas_call(
        functools.partial(_norm_proj_body, segs=segs, tm=tm),
        grid=(n // tm,),
        in_specs=[
            pl.BlockSpec((tm, d), lambda i: (i, 0)),
            pl.BlockSpec((1, d), lambda i: (0, 0)),
            pl.BlockSpec((d, wtot), lambda i: (0, 0)),
            pl.BlockSpec((1, wtot), lambda i: (0, 0)),
            pl.BlockSpec(ones_bd.shape, lambda i: (0, 0)),
        ],
        out_specs=out_specs,
        out_shape=out_shape,
        compiler_params=_cparams(("parallel",), 48),
        name="norm_proj",
    )(x, gain.reshape(1, d), w_bf, head_gain.reshape(1, wtot), ones_bd)


def _pool_windows(win, pos0):
    r = win.shape[0] - 16
    lane = lax.broadcasted_iota(jnp.int32, (r, LANES), 1)
    pos1 = pos0 + lax.broadcasted_iota(jnp.int32, (r, LANES), 0) + 1
    low = lane < POOL_GROUP_WIDTH
    a = win[:, :LANES]
    b = win[:, LANES:]
    a2 = a + pltpu.roll(a, 1, 0)
    a4 = a2 + pltpu.roll(a2, 2, 0)
    b2 = b + pltpu.roll(b, 1, 0)
    b4 = b2 + pltpu.roll(b2, 2, 0)
    b8 = b4 + pltpu.roll(b4, 4, 0)
    b16 = b8 + pltpu.roll(b8, 8, 0)
    cnt_a = jnp.minimum(jnp.where(low, POOL_WINDOWS[0], POOL_WINDOWS[1]), pos1).astype(F32)
    cnt_b = jnp.minimum(jnp.where(low, POOL_WINDOWS[2], POOL_WINDOWS[3]), pos1).astype(F32)
    pa = jnp.where(low, a2[16:], a4[16:]) / cnt_a - a[16:]
    pb = jnp.where(low, b8[16:], b16[16:]) / cnt_b - b[16:]
    return jnp.concatenate([pa, pb], axis=1)


def _pool_finish(pooled, wbd_ref, ps_ref, og_ref):
    mixed = jnp.dot(pooled.astype(BF16), wbd_ref[...], preferred_element_type=F32) * ps_ref[...]
    return _rms(mixed) * og_ref[...]


def _pool_body(u_ref, buf_ref, wbd_ref, ps_ref, og_ref, o_ref, ext_sc, *, bb, t, r, pos0):
    for bi in range(bb):
        ext_sc[bi, 0:16, :] = buf_ref[bi]
        ext_sc[bi, 16:, :] = u_ref[bi]
    if t == r:
        pooled = [_pool_windows(ext_sc[bi], pos0) for bi in range(bb)]
        out = _pool_finish(jnp.concatenate(pooled, axis=0), wbd_ref, ps_ref, og_ref)
        for bi in range(bb):
            o_ref[bi] = out[bi * r:(bi + 1) * r]
    else:
        assert bb == 1

        def chunk(c, carry):
            base = pl.multiple_of(c * r, r)
            pooled = _pool_windows(ext_sc[0, pl.ds(base, r + 16), :], pos0 + base)
            o_ref[0, pl.ds(base, r), :] = _pool_finish(pooled, wbd_ref, ps_ref, og_ref)
            return carry

        lax.fori_loop(0, t // r, chunk, 0)


def _pool(u, buf16, wbd_bf, pool_scale, og, pos0, bb, r):
    b, t, c = u.shape
    assert b % bb == 0 and t % r == 0
    return pl.pallas_call(
        functools.partial(_pool_body, bb=bb, t=t, r=r, pos0=pos0),
        grid=(b // bb,),
        in_specs=[
            pl.BlockSpec((bb, t, c), lambda i: (i, 0, 0)),
            pl.BlockSpec((bb, 16, c), lambda i: (i, 0, 0)),
            pl.BlockSpec((c, c), lambda i: (0, 0)),
            pl.BlockSpec((1, c), lambda i: (0, 0)),
            pl.BlockSpec((1, c), lambda i: (0, 0)),
        ],
        out_specs=pl.BlockSpec((bb, t, c), lambda i: (i, 0, 0)),
        out_shape=jax.ShapeDtypeStruct((b, t, c), F32),
        scratch_shapes=[pltpu.VMEM((bb, t + 16, c), F32)],
        compiler_params=_cparams(("parallel",), 40),
        name="pool",
    )(u, buf16, wbd_bf, pool_scale.reshape(1, c), og.reshape(1, c))


def _alibi_slope(h):
    return 2.0 ** (-8.0 * (h + 1) / MOBA_HEADS)


def _moba_prompt_body(q_ref, k_ref, v_ref, km_ref, og_ref, o_ref, m_sc, l_sc, acc_sc, sel_sc, qst_sc, *, nblk):
    i = pl.program_id(1)
    tq = MOBA_BLOCK
    n_pairs = MOBA_HEADS // 2
    nb_rows = -(-nblk // 8) * 8
    lane = lax.broadcasted_iota(jnp.int32, (tq, LANES), 1)
    low = lane < HEAD_DIM
    row2 = lax.broadcasted_iota(jnp.int32, (tq, MOBA_BLOCK), 0)
    col2 = lax.broadcasted_iota(jnp.int32, (tq, MOBA_BLOCK), 1)
    causal = col2 <= row2
    eye_bf = jnp.where(row2 == col2, 1.0, 0.0).astype(BF16)
    colrow = lax.broadcasted_iota(jnp.int32, (1, MOBA_BLOCK), 1).astype(F32)

    def pair_cols(pr):
        return slice(pr * HEAD_PAIR, (pr + 1) * HEAD_PAIR)

    def scores(pr, j):
        start = pl.multiple_of(j * MOBA_BLOCK, MOBA_BLOCK)
        kj = k_ref[pl.ds(start, MOBA_BLOCK), pair_cols(pr)]
        s = lax.dot_general(qst_sc[pr], kj, NT_DIMS, preferred_element_type=F32)
        off = colrow + ((j - i) * MOBA_BLOCK).astype(F32)
        return [s[hh * tq:(hh + 1) * tq] + _alibi_slope(2 * pr + hh) * off for hh in range(2)]

    def pv(pr, j, ps):
        start = pl.multiple_of(j * MOBA_BLOCK, MOBA_BLOCK)
        vj = v_ref[pl.ds(start, MOBA_BLOCK), pair_cols(pr)]
        return jnp.dot(jnp.concatenate(ps, axis=0).astype(BF16), vj, preferred_element_type=F32)

    gates_t = []
    for pr in range(n_pairs):
        q_pair = q_ref[:, pair_cols(pr)]
        km_pair = km_ref[0, :, pair_cols(pr)]
        q_heads = [jnp.where(low, q_pair, 0.0), jnp.where(low, 0.0, q_pair)]
        qst_sc[pr] = (jnp.concatenate(q_heads, axis=0) * QK_SCALE).astype(BF16)
        km_hi = km_pair.astype(BF16)
        km_lo = (km_pair - km_hi.astype(F32)).astype(BF16)
        for hh in range(2):
            q_hi = q_heads[hh].astype(BF16)
            q_lo = (q_heads[hh] - q_hi.astype(F32)).astype(BF16)
            gates_t.append((lax.dot_general(km_hi, q_hi, NT_DIMS, preferred_element_type=F32)
                            + (lax.dot_general(km_hi, q_lo, NT_DIMS, preferred_element_type=F32)
                               + lax.dot_general(km_lo, q_hi, NT_DIMS, preferred_element_type=F32)))[:nb_rows])
    blk_all = lax.broadcasted_iota(jnp.int32, (nb_rows, MOBA_HEADS * tq), 0)
    blk_all_f = blk_all.astype(F32)
    g = jnp.where(blk_all < i, jnp.concatenate(gates_t, axis=1), NEG_INF)
    sel_t = jnp.zeros((nb_rows, MOBA_HEADS * tq), F32)
    for _ in range(MOBA_TOPK):
        mx = jnp.max(g, axis=0, keepdims=True)
        idx = jnp.min(jnp.where(g == mx, blk_all_f, float(LANES)), axis=0, keepdims=True)
        pick = (blk_all_f == idx) & (mx > NEG_INF)
        sel_t = jnp.where(pick, 1.0, sel_t)
        g = jnp.where(pick, NEG_INF, g)
    sel_t = jnp.concatenate([sel_t, jnp.zeros((LANES - nb_rows, MOBA_HEADS * tq), F32)], axis=0).astype(BF16)
    for h in range(MOBA_HEADS):
        sel_sc[h] = lax.dot_general(eye_bf, sel_t[:, h * tq:(h + 1) * tq], NT_DIMS, preferred_element_type=F32)

    for pr in range(n_pairs):
        s_own = scores(pr, i)
        ps = []
        for hh in range(2):
            sh = jnp.where(causal, s_own[hh], NEG_INF)
            m = jnp.max(sh, axis=-1, keepdims=True)
            p = jnp.exp(sh - m)
            m_sc[2 * pr + hh] = jnp.broadcast_to(m, (tq, LANES))
            l_sc[2 * pr + hh] = jnp.broadcast_to(jnp.sum(p, axis=-1, keepdims=True), (tq, LANES))
            ps.append(p)
        acc_sc[pr] = pv(pr, i, ps)

    def past(j, carry):
        for pr in range(n_pairs):
            s_j = scores(pr, j)
            ps, alphas = [], []
            for hh in range(2):
                h = 2 * pr + hh
                selcol = jnp.sum(jnp.where(lane == j, sel_sc[h], 0.0), axis=-1, keepdims=True)
                sh = jnp.where(selcol > 0.0, s_j[hh], NEG_INF)
                m_prev = m_sc[h]
                m_new = jnp.maximum(m_prev, jnp.max(sh, axis=-1, keepdims=True))
                alpha = jnp.exp(m_prev - m_new)
                p = jnp.exp(sh - jnp.concatenate([m_new, m_new], axis=1))
                l_sc[h] = alpha * l_sc[h] + jnp.sum(p, axis=-1, keepdims=True)
                m_sc[h] = m_new
                ps.append(p)
                alphas.append(alpha)
            acc_sc[pr] = jnp.concatenate(alphas, axis=0) * acc_sc[pr] + pv(pr, j, ps)
        return carry

    span2 = 2 * MOBA_BLOCK
    colrow2 = lax.broadcasted_iota(jnp.int32, (1, span2), 1).astype(F32)

    def past2(jj, carry):
        j0 = 2 * jj
        start = pl.multiple_of(j0 * MOBA_BLOCK, MOBA_BLOCK)
        off = colrow2 + ((j0 - i) * MOBA_BLOCK).astype(F32)
        for pr in range(n_pairs):
            s2 = lax.dot_general(qst_sc[pr], k_ref[pl.ds(start, span2), pair_cols(pr)], NT_DIMS,
                                 preferred_element_type=F32)
            ps, alphas = [], []
            for hh in range(2):
                h = 2 * pr + hh
                sh = s2[hh * tq:(hh + 1) * tq] + _alibi_slope(h) * off
                halves = []
                for bb in range(2):
                    selcol = jnp.sum(jnp.where(lane == j0 + bb, sel_sc[h], 0.0), axis=-1, keepdims=True)
                    halves.append(jnp.where(selcol > 0.0, sh[:, bb * MOBA_BLOCK:(bb + 1) * MOBA_BLOCK], NEG_INF))
                sh = jnp.concatenate(halves, axis=1)
                m_prev = m_sc[h]
                m_new = jnp.maximum(m_prev, jnp.max(sh, axis=-1, keepdims=True))
                alpha = jnp.exp(m_prev - m_new)
                p = jnp.exp(sh - jnp.concatenate([m_new] * (span2 // LANES), axis=1))
                l_sc[h] = alpha * l_sc[h] + jnp.sum(p, axis=-1, keepdims=True)
                m_sc[h] = m_new
                ps.append(p)
                alphas.append(alpha)
            pv2 = jnp.dot(jnp.concatenate(ps, axis=0).astype(BF16), v_ref[pl.ds(start, span2), pair_cols(pr)],
                          preferred_element_type=F32)
            acc_sc[pr] = jnp.concatenate(alphas, axis=0) * acc_sc[pr] + pv2
        return carry

    lax.fori_loop(0, i // 2, past2, 0)

    @pl.when(i % 2 == 1)
    def _odd_block():
        past(i - 1, 0)

    outs = []
    for pr in range(n_pairs):
        acc = acc_sc[pr]
        outs.append(jnp.where(low, acc[:tq] / l_sc[2 * pr], acc[tq:] / l_sc[2 * pr + 1]))
    o_ref[...] = _rms(jnp.concatenate(outs, axis=1)) * og_ref[...]


def _moba_prompt(q, k_bf, v_bf, kmean_pad, og, b, t):
    n, w = q.shape
    nblk = t // MOBA_BLOCK
    tq = MOBA_BLOCK
    return pl.pallas_call(
        functools.partial(_moba_prompt_body, nblk=nblk),
        grid=(b, nblk),
        in_specs=[
            pl.BlockSpec((tq, w), lambda bi, i: (bi * nblk + i, 0)),
            pl.BlockSpec((t, w), lambda bi, i: (bi, 0)),
            pl.BlockSpec((t, w), lambda bi, i: (bi, 0)),
            pl.BlockSpec((1, LANES, w), lambda bi, i: (bi, 0, 0)),
            pl.BlockSpec((1, w), lambda bi, i: (0, 0)),
        ],
        out_specs=pl.BlockSpec((tq, w), lambda bi, i: (bi * nblk + i, 0)),
        out_shape=jax.ShapeDtypeStruct((n, w), F32),
        scratch_shapes=[
            pltpu.VMEM((MOBA_HEADS, tq, LANES), F32),
            pltpu.VMEM((MOBA_HEADS, tq, LANES), F32),
            pltpu.VMEM((MOBA_HEADS // 2, 2 * tq, LANES), F32),
            pltpu.VMEM((MOBA_HEADS, tq, LANES), F32),
            pltpu.VMEM((MOBA_HEADS // 2, 2 * tq, LANES), BF16),
        ],
        compiler_params=_cparams(("parallel", "arbitrary"), 40),
        name="moba_prompt",
    )(q, k_bf, v_bf, kmean_pad, og.reshape(1, w))


SAMPLE_BLOCKS_PER_STEP = 8
PAGES_PER_BLOCK = MOBA_BLOCK // PAGE_SIZE
PAGES_PER_STEP = SAMPLE_BLOCKS_PER_STEP * PAGES_PER_BLOCK


def _moba_sample_body(pt_ref, q_ref, kn_ref, vn_ref, slope_ref, og_ref, *rest, n_past_blocks, t_new):
    kp = rest[:PAGES_PER_STEP]
    vp = rest[PAGES_PER_STEP:2 * PAGES_PER_STEP]
    o_ref, o_sc, m_sc, l_sc, kmt_sc = rest[2 * PAGES_PER_STEP:]
    c = pl.program_id(1)
    rows = MOBA_HEADS * t_new
    row_h = lax.broadcasted_iota(jnp.int32, (rows, MOBA_WIDTH), 0) // t_new
    lane_h = lax.broadcasted_iota(jnp.int32, (rows, MOBA_WIDTH), 1) // HEAD_DIM
    bd = row_h == lane_h
    q = q_ref[...]
    qbd = jnp.where(bd, jnp.concatenate([q] * MOBA_HEADS, axis=0), 0.0)
    qbd_bf = (qbd * QK_SCALE).astype(BF16)
    slope = slope_ref[...]
    slope2 = jnp.concatenate([slope, slope], axis=1)
    colf = lax.broadcasted_iota(jnp.int32, (rows, MOBA_BLOCK), 1).astype(F32)

    @pl.when(c == 0)
    def _init():
        kmt_sc[...] = jnp.zeros(kmt_sc.shape, F32)

    kt_bf = []
    for jj in range(SAMPLE_BLOCKS_PER_STEP):
        kt_pages = [kp[PAGES_PER_BLOCK * jj + p][0] for p in range(PAGES_PER_BLOCK)]
        ksum = kt_pages[0]
        for kt in kt_pages[1:]:
            ksum = ksum + kt
        kmt_sc[c, :, jj:jj + 1] = jnp.sum(ksum, axis=-1, keepdims=True) * (1.0 / MOBA_BLOCK)
        kt_bf.append(jnp.concatenate(kt_pages, axis=1).astype(BF16))
    s_all = jnp.dot(qbd_bf, jnp.concatenate(kt_bf, axis=1), preferred_element_type=F32)

    for jj in range(SAMPLE_BLOCKS_PER_STEP):
        j = c * SAMPLE_BLOCKS_PER_STEP + jj
        s = s_all[:, jj * MOBA_BLOCK:(jj + 1) * MOBA_BLOCK] + slope2 * (
            colf + ((j - n_past_blocks) * MOBA_BLOCK).astype(F32))
        m = jnp.max(s, axis=-1, keepdims=True)
        p = jnp.exp(s - m)
        vt_bf = jnp.concatenate([vp[PAGES_PER_BLOCK * jj + q][0] for q in range(PAGES_PER_BLOCK)],
                                axis=1).astype(BF16)
        o = lax.dot_general(p.astype(BF16), vt_bf, NT_DIMS, preferred_element_type=F32)
        o_sc[j] = jnp.where(bd, o, 0.0)
        m_sc[j] = jnp.broadcast_to(m, (rows, LANES))
        l_sc[j] = jnp.broadcast_to(jnp.sum(p, axis=-1, keepdims=True), (rows, LANES))

    @pl.when(c == pl.num_programs(1) - 1)
    def _finish():
        tq = lax.broadcasted_iota(jnp.int32, (rows, LANES), 0) % t_new
        kn = kn_ref[...]
        vn = vn_ref[...]
        qs = qbd * QK_SCALE
        s_own = []
        m_run = jnp.full((rows, LANES), NEG_INF, F32)
        for cc in range(t_new):
            sc = jnp.sum(qs * kn[cc:cc + 1, :], axis=-1, keepdims=True) + slope * float(cc)
            sc = jnp.where(tq >= cc, sc, NEG_INF)
            s_own.append(sc)
            m_run = jnp.maximum(m_run, sc)

        blk = lax.broadcasted_iota(jnp.int32, (rows, LANES), 1)
        blk_f = blk.astype(F32)
        gates = jnp.zeros((rows, LANES), F32)
        for cs in range(n_past_blocks // SAMPLE_BLOCKS_PER_STEP):
            g_cs = jnp.dot(qbd, kmt_sc[cs], precision=lax.Precision.HIGHEST, preferred_element_type=F32)
            g_cs = jnp.where(blk < SAMPLE_BLOCKS_PER_STEP, g_cs, 0.0)
            gates = gates + (pltpu.roll(g_cs, cs * SAMPLE_BLOCKS_PER_STEP, 1) if cs else g_cs)
        g = jnp.where(blk < n_past_blocks, gates, NEG_INF)
        sel = jnp.zeros((rows, LANES), F32)
        for _ in range(MOBA_TOPK):
            mx = jnp.max(g, axis=-1, keepdims=True)
            idx = jnp.min(jnp.where(g == mx, blk_f, float(LANES)), axis=-1, keepdims=True)
            pick = (blk_f == idx) & (mx > NEG_INF)
            sel = jnp.where(pick, 1.0, sel)
            g = jnp.where(pick, NEG_INF, g)

        m_all = jnp.full((rows, LANES), NEG_INF, F32)
        l_all = jnp.zeros((rows, LANES), F32)
        for j in range(n_past_blocks):
            m_all = jnp.where(blk == j, m_sc[j], m_all)
            l_all = jnp.where(blk == j, l_sc[j], l_all)
        m_run = jnp.maximum(m_run, jnp.max(jnp.where(sel > 0.0, m_all, NEG_INF), axis=-1, keepdims=True))
        w_all = jnp.where(sel > 0.0, jnp.exp(m_all - m_run), 0.0)

        l_run = jnp.broadcast_to(jnp.sum(w_all * l_all, axis=-1, keepdims=True), (rows, LANES))
        o_run = jnp.zeros((rows, MOBA_WIDTH), F32)
        for cc in range(t_new):
            p = jnp.exp(s_own[cc] - m_run)
            l_run = l_run + p
            o_run = o_run + jnp.concatenate([p] * (MOBA_WIDTH // LANES), axis=1) * vn[cc:cc + 1, :]
        o_run = jnp.where(bd, o_run, 0.0)
        for j in range(n_past_blocks):
            o_run = o_run + w_all[:, j:j + 1] * o_sc[j]
        o_bd = o_run / jnp.concatenate([l_run] * (MOBA_WIDTH // LANES), axis=1)
        attn = o_bd[0:t_new]
        for h in range(1, MOBA_HEADS):
            attn = attn + o_bd[h * t_new:(h + 1) * t_new]
        o_ref[...] = _rms(attn) * og_ref[...]


def _moba_sample(page_table, q, k_new, v_new, cache_kt, cache_vt, og, b, t_new):
    n_pages = page_table.shape[1]
    n_past_blocks = n_pages // PAGES_PER_BLOCK
    assert n_past_blocks <= LANES
    assert n_past_blocks % SAMPLE_BLOCKS_PER_STEP == 0
    n_steps = n_past_blocks // SAMPLE_BLOCKS_PER_STEP
    rows = MOBA_HEADS * t_new
    w = MOBA_WIDTH
    slopes = jnp.exp2(-8.0 * jnp.arange(1, MOBA_HEADS + 1, dtype=F32) / MOBA_HEADS)
    slope_rows = jnp.broadcast_to(jnp.repeat(slopes, t_new)[:, None], (rows, LANES))

    def page_spec(p):
        return pl.BlockSpec((1, w, PAGE_SIZE), lambda bi, c, pt: (pt[bi, c * PAGES_PER_STEP + p], 0, 0))

    row_spec = pl.BlockSpec((t_new, w), lambda bi, c, pt: (bi, 0))
    grid_spec = pltpu.PrefetchScalarGridSpec(
        num_scalar_prefetch=1,
        grid=(b, n_steps),
        in_specs=[row_spec, row_spec, row_spec,
                  pl.BlockSpec((rows, LANES), lambda bi, c, pt: (0, 0)),
                  pl.BlockSpec((1, w), lambda bi, c, pt: (0, 0))]
                 + [page_spec(p) for p in range(PAGES_PER_STEP)]
                 + [page_spec(p) for p in range(PAGES_PER_STEP)],
        out_specs=row_spec,
        scratch_shapes=[
            pltpu.VMEM((n_past_blocks, rows, w), F32),
            pltpu.VMEM((n_past_blocks, rows, LANES), F32),
            pltpu.VMEM((n_past_blocks, rows, LANES), F32),
            pltpu.VMEM((n_steps, w, LANES), F32),
        ],
    )
    return pl.pallas_call(
        functools.partial(_moba_sample_body, n_past_blocks=n_past_blocks, t_new=t_new),
        grid_spec=grid_spec,
        out_shape=jax.ShapeDtypeStruct((b * t_new, w), F32),
        compiler_params=_cparams(("parallel", "arbitrary"), 48),
        name="moba_sample",
    )(page_table, q, k_new, v_new, slope_rows, og.reshape(1, w),
      *([cache_kt] * PAGES_PER_STEP), *([cache_vt] * PAGES_PER_STEP))


def _mem_attn_body(q_ref, mk_ref, mv_ref, og_ref, o_ref, *, tm):
    lane = lax.broadcasted_iota(jnp.int32, (tm, LANES), 1)
    low = lane < HEAD_DIM
    outs = []
    for pr in range(MEM_HEADS // 2):
        cs = slice(pr * HEAD_PAIR, (pr + 1) * HEAD_PAIR)
        q_pair = q_ref[:, cs]
        qst = (jnp.concatenate([jnp.where(low, q_pair, 0.0), jnp.where(low, 0.0, q_pair)], axis=0)
               * QK_SCALE).astype(BF16)
        s = lax.dot_general(qst, mk_ref[0, :, cs].astype(BF16), NT_DIMS, preferred_element_type=F32)
        p = jnp.exp(s - jnp.max(s, axis=-1, keepdims=True))
        l = jnp.sum(p, axis=-1, keepdims=True)
        o = jnp.dot(p.astype(BF16), mv_ref[0, :, cs].astype(BF16), preferred_element_type=F32) / l
        outs.append(jnp.where(low, o[:tm], o[tm:]))
    o_ref[...] = _rms(jnp.concatenate(outs, axis=1)) * og_ref[...]


def _mem_attn(qm, mem_k, mem_v, og, b, t, tm):
    n, w = qm.shape
    assert t % tm == 0
    steps = t // tm
    return pl.pallas_call(
        functools.partial(_mem_attn_body, tm=tm),
        grid=(b, steps),
        in_specs=[
            pl.BlockSpec((tm, w), lambda bi, i: (bi * steps + i, 0)),
            pl.BlockSpec((1, MEM_TOKENS, w), lambda bi, i: (bi, 0, 0)),
            pl.BlockSpec((1, MEM_TOKENS, w), lambda bi, i: (bi, 0, 0)),
            pl.BlockSpec((1, w), lambda bi, i: (0, 0)),
        ],
        out_specs=pl.BlockSpec((tm, w), lambda bi, i: (bi * steps + i, 0)),
        out_shape=jax.ShapeDtypeStruct((n, w), F32),
        compiler_params=_cparams(("parallel", "arbitrary"), 32),
        name="mem_attn",
    )(qm, mem_k, mem_v, og.reshape(1, w))


ROUTER_GROUP_LANE0 = N_EXPERTS
ROUTER_GID_LANE = 64


def _outproj_router_body(op_ref, oa_ref, om_ref, x_ref, wo_ref, g2_ref, wr_ref, br_ref,
                         x1_ref, h2_ref, comb_ref, gcnt_ref, *, tm):
    a0, a1 = POOL_WIDTH, POOL_WIDTH + MOBA_WIDTH
    y = x_ref[...]
    y = y + jnp.dot(op_ref[...].astype(BF16), wo_ref[0:a0, :], preferred_element_type=F32)
    y = y + jnp.dot(oa_ref[...].astype(BF16), wo_ref[a0:a1, :], preferred_element_type=F32)
    y = y + jnp.dot(om_ref[...].astype(BF16), wo_ref[a1:, :], preferred_element_type=F32)
    x1_ref[...] = y
    h2 = _rms(y) * g2_ref[...]
    h2_hi = h2.astype(BF16)
    h2_lo = (h2 - h2_hi.astype(F32)).astype(BF16)
    h2_ref[...] = h2_hi
    hw =jnp.dot(h2_hi, wr_ref[...], preferred_element_type=F32)
    lw = jnp.dot(h2_lo, wr_ref[:, :LANES], preferred_element_type=F32)
    logits = (hw[:, :LANES] + (hw[:, LANES:] + lw)) + br_ref[...]
    lane_f = lax.broadcasted_iota(jnp.int32, (tm, LANES), 1).astype(F32)
    big = float(LANES)
    g_lo = float(ROUTER_GROUP_LANE0)
    is_g = (lane_f >= g_lo) & (lane_f < g_lo + MOE_GROUPS)
    lg = jnp.where(is_g, logits, NEG_INF)
    mg = jnp.max(lg, axis=-1, keepdims=True)
    pg_top = 1.0 / jnp.sum(jnp.exp(lg - mg), axis=-1, keepdims=True)
    gidx = jnp.min(jnp.where(lg == mg, lane_f, big), axis=-1, keepdims=True) - g_lo
    e_lo = gidx * EXPERTS_PER_GROUP
    in_grp = (lane_f >= e_lo) & (lane_f < e_lo + EXPERTS_PER_GROUP)
    le = jnp.where(in_grp, logits, NEG_INF)
    m1 = jnp.max(le, axis=-1, keepdims=True)
    se = jnp.sum(jnp.exp(le - m1), axis=-1, keepdims=True)
    i1 = jnp.min(jnp.where(le == m1, lane_f, big), axis=-1, keepdims=True)
    le2 = jnp.where(lane_f == i1, NEG_INF, le)
    m2 = jnp.max(le2, axis=-1, keepdims=True)
    i2 = jnp.min(jnp.where(le2 == m2, lane_f, big), axis=-1, keepdims=True)
    p1 = 1.0 / se
    p2 = jnp.exp(m2 - m1) / se
    den = p1 + p2
    comb_ref[...] = jnp.where(lane_f == i1, pg_top * (p1 / den),
                              jnp.where(lane_f == i2, pg_top * (p2 / den),
                                        jnp.where(lane_f == float(ROUTER_GID_LANE), gidx, 0.0)))
    gcnt_ref[0] = jnp.sum(jnp.where(lane_f == gidx, 1.0, 0.0), axis=0, keepdims=True)


def _outproj_router(o_pool, o_attn, o_mem, x, wo_bf, g2, wr, br, tm):
    n, d = x.shape
    assert n % tm == 0
    row = lambda wdt: pl.BlockSpec((tm, wdt), lambda i: (i, 0))
    full = lambda shp: pl.BlockSpec(shp, lambda i: (0, 0))
    return pl.pallas_call(
        functools.partial(_outproj_router_body, tm=tm),
        grid=(n // tm,),
        in_specs=[row(POOL_WIDTH), row(MOBA_WIDTH), row(MEM_WIDTH), row(d),
                  full((d, d)), full((1, d)), full((d, 2 * LANES)), full((1, LANES))],
        out_specs=[row(d), row(d), row(LANES), pl.BlockSpec((1, 1, LANES), lambda i: (i, 0, 0))],
        out_shape=[jax.ShapeDtypeStruct((n, d), F32), jax.ShapeDtypeStruct((n, d), BF16),
                   jax.ShapeDtypeStruct((n, LANES), F32), jax.ShapeDtypeStruct((n // tm, 1, LANES), F32)],
        compiler_params=_cparams(("parallel",), 40),
        name="outproj_router",
    )(o_pool, o_attn, o_mem, x, wo_bf, g2.reshape(1, d), wr, br)


MOE_EXPERTS_PER_STEP = 8
MOE_RUN_ALIGN = 16
MOE_SORT_PAD = LANES


def _moe_window(tm):
    return max(LANES, tm // MOE_GROUPS + 32)


def _moe_body(cnt_ref, h2_ref, comb_ref, x1_hbm, wg_ref, wu_ref, wd_ref, y_ref,
              xs_sc, cs_sc, ys_sc, p_sc, pt_sc, sem, *, tm, win):
    i = pl.program_id(0)
    s = pl.program_id(1)
    eps = MOE_EXPERTS_PER_STEP
    span = tm + MOE_SORT_PAD
    g = s // (EXPERTS_PER_GROUP // eps)
    starts, nxt = [], jnp.int32(0)
    for gg in range(MOE_GROUPS):
        starts.append(nxt)
        nxt = nxt + (cnt_ref[i * MOE_GROUPS + gg] + (MOE_RUN_ALIGN - 1)) // MOE_RUN_ALIGN * MOE_RUN_ALIGN
    x1_copy = pltpu.make_async_copy(x1_hbm.at[pl.ds(pl.multiple_of(i * tm, tm), tm)], y_ref, sem)

    @pl.when(s == 0)
    def _sort():
        x1_copy.start()
        comb = comb_ref[...]
        lane = lax.broadcasted_iota(jnp.int32, (tm, LANES), 1)
        lane_f = lane.astype(F32)
        gid = jnp.sum(jnp.where(lane == ROUTER_GID_LANE, comb, 0.0), axis=-1, keepdims=True)
        onehot = jnp.where(lane_f == gid, 1.0, 0.0)
        earlier_tok = lax.broadcasted_iota(jnp.int32, (tm, tm), 1) < lax.broadcasted_iota(jnp.int32, (tm, tm), 0)
        earlier = jnp.dot(jnp.where(earlier_tok, 1.0, 0.0).astype(BF16), onehot.astype(BF16),
                          preferred_element_type=F32)
        start_vec = jnp.zeros((tm, LANES), F32)
        for gg in range(MOE_GROUPS):
            start_vec = jnp.where(lane == gg, starts[gg].astype(F32), start_vec)
        dest = jnp.sum(onehot * (earlier + start_vec), axis=-1, keepdims=True)
        hi = jnp.floor(dest * (1.0 / 32.0))
        lo = dest - 32.0 * hi
        dest_cols = jnp.where(lane == 0, hi, jnp.where(lane == 1, lo, 0.0)).astype(BF16)
        r16 = lax.broadcasted_iota(jnp.int32, (16, LANES), 0)
        l16 = lax.broadcasted_iota(jnp.int32, (16, LANES), 1)
        pick = jnp.where((r16 == 0) & (l16 == 0), 32.0, jnp.where((r16 == 0) & (l16 == 1), 1.0, 0.0)).astype(BF16)
        dest_row = lax.dot_general(pick, dest_cols, NT_DIMS, preferred_element_type=F32)[0:1, :]
        p = jnp.where(lax.broadcasted_iota(jnp.int32, (span, tm), 0).astype(F32) == dest_row, 1.0, 0.0).astype(BF16)
        p_sc[...] = p
        pt_sc[...] = jnp.where(lax.broadcasted_iota(jnp.int32, (tm, span), 1).astype(F32) == dest,
                               1.0, 0.0).astype(BF16)
        xs_sc[0:span, :] = jnp.dot(p, h2_ref[...], preferred_element_type=F32).astype(BF16)
        xs_sc[span:, :] = jnp.zeros((win, D_MODEL), BF16)
        c1 = comb.astype(BF16)
        r1 = comb - c1.astype(F32)
        c2 = r1.astype(BF16)
        c3 = (r1 - c2.astype(F32)).astype(BF16)
        cs_sc[0:span, :] = (jnp.dot(p, c1, preferred_element_type=F32) + jnp.dot(p, c2, preferred_element_type=F32)
                            + jnp.dot(p, c3, preferred_element_type=F32))
        cs_sc[span:, :] = jnp.zeros((win, LANES), F32)
        ys_sc[...] = jnp.zeros(ys_sc.shape, F32)

    run_start = starts[0]
    for gg in range(1, MOE_GROUPS):
        run_start = jnp.where(g == gg, starts[gg], run_start)
    run_len = cnt_ref[i * MOE_GROUPS + g]
    lane_w = lax.broadcasted_iota(jnp.int32, (win, LANES), 1)

    def window(w, carry):
        r0 = pl.multiple_of(run_start + w * win, MOE_RUN_ALIGN)
        rows = xs_sc[pl.ds(r0, win), :]
        comb_rows = cs_sc[pl.ds(r0, win), :]
        acc = None
        for j in range(eps):
            e = s * eps + j
            a = jnp.dot(rows, wg_ref[j], preferred_element_type=F32)
            b = jnp.dot(rows, wu_ref[j], preferred_element_type=F32)
            ce = jnp.sum(jnp.where(lane_w == e, comb_rows, 0.0), axis=-1, keepdims=True)
            hid = ((a * jax.nn.sigmoid(a)) * b * ce).astype(BF16)
            c = jnp.dot(hid, wd_ref[j], preferred_element_type=F32)
            acc = c if acc is None else acc + c
        ys_sc[pl.ds(r0, win), :] += acc
        return carry

    lax.fori_loop(0, (run_len + win - 1) // win, window, 0)

    @pl.when(s == pl.num_programs(1) - 1)
    def _unsort():
        ys = ys_sc[0:span, :]
        moe = jnp.dot(pt_sc[...], ys.astype(BF16), preferred_element_type=F32)
        x1_copy.wait()
        y_ref[...] = y_ref[...] + moe


def _moe(tile_cnt, h2, comb, x1, wg_bf, wu_bf, wd_bf, tm):
    n, d = x1.shape
    eps = MOE_EXPERTS_PER_STEP
    assert n % tm == 0 and EXPERTS_PER_GROUP % eps == 0 and MOE_GROUPS * (MOE_RUN_ALIGN - 1) <= MOE_SORT_PAD
    win = _moe_window(tm)
    assert win % MOE_RUN_ALIGN == 0
    span = tm + MOE_SORT_PAD
    grid_spec = pltpu.PrefetchScalarGridSpec(
        num_scalar_prefetch=1,
        grid=(n // tm, N_EXPERTS // eps),
        in_specs=[
            pl.BlockSpec((tm, d), lambda i, s, cnt: (i, 0)),
            pl.BlockSpec((tm, LANES), lambda i, s, cnt: (i, 0)),
            pl.BlockSpec(memory_space=pl.ANY),
            pl.BlockSpec((eps, d, D_EXPERT), lambda i, s, cnt: (s, 0, 0)),
            pl.BlockSpec((eps, d, D_EXPERT), lambda i, s, cnt: (s, 0, 0)),
            pl.BlockSpec((eps, D_EXPERT, d), lambda i, s, cnt: (s, 0, 0)),
        ],
        out_specs=pl.BlockSpec((tm, d), lambda i, s, cnt: (i, 0)),
        scratch_shapes=[
            pltpu.VMEM((span + win, d), BF16),
            pltpu.VMEM((span + win, LANES), F32),
            pltpu.VMEM((span + win, d), F32),
            pltpu.VMEM((span, tm), BF16),
            pltpu.VMEM((tm, span), BF16),
            pltpu.SemaphoreType.DMA(()),
        ],
    )
    return pl.pallas_call(
        functools.partial(_moe_body, tm=tm, win=win),
        grid_spec=grid_spec,
        out_shape=jax.ShapeDtypeStruct((n, d), F32),
        compiler_params=_cparams(("parallel", "arbitrary"), 56),
        name="moe",
    )(tile_cnt, h2, comb, x1, wg_bf, wu_bf, wd_bf)


def _block_diag_ones(width, group):
    r = lax.broadcasted_iota(jnp.int32, (width, width), 0) // group
    c = lax.broadcasted_iota(jnp.int32, (width, width), 1) // group
    return (r == c).astype(BF16)


def _pick_tile(n, candidates):
    for c in candidates:
        if n % c == 0:
            return c
    return n


IN_SEGS = (
    (0, POOL_WIDTH, False, False, False, False),
    (POOL_WIDTH, MOBA_WIDTH, True, False, False, False),
    (POOL_WIDTH + MOBA_WIDTH, MOBA_WIDTH, True, True, True, True),
    (POOL_WIDTH + 2 * MOBA_WIDTH, MOBA_WIDTH, False, True, True, False),
    (POOL_WIDTH + 3 * MOBA_WIDTH, MEM_WIDTH, True, False, False, False),
)
IN_SEGS_SAMPLE = tuple((c0, wd, hn, False, False, False) for (c0, wd, hn, _, _, _) in IN_SEGS)
MEMKV_SEGS = ((0, MEM_WIDTH, True, False, False, False), (MEM_WIDTH, MEM_WIDTH, False, False, False, False))


def kernel(x_prompt, x_sample, mem_prompt, cache_k, cache_v, state_pool, cache_mem_k, cache_mem_v,
           page_table, norm1_gain, w_in, pool_w, pool_scale, moba_q_gain, moba_k_gain,
           mem_norm_gain, w_mem_kv, mem_q_gain, mem_k_gain, out_gain, w_out, norm2_gain,
           router_group_w, router_group_b, router_expert_w, router_expert_b, w_gate, w_up, w_down):
    bp, tp, d = x_prompt.shape
    bs, ts, _ = x_sample.shape
    n_p, n_s = bp * tp, bs * ts

    w_in_bf = w_in.astype(BF16)
    w_mem_bf = w_mem_kv.astype(BF16)
    wo_bf = w_out.astype(BF16)
    ones_bd = _block_diag_ones(MOBA_WIDTH, HEAD_DIM)
    head_gain_in = jnp.concatenate([
        jnp.ones((POOL_WIDTH,), F32), jnp.tile(moba_q_gain, MOBA_HEADS), jnp.tile(moba_k_gain, MOBA_HEADS),
        jnp.ones((MOBA_WIDTH,), F32), jnp.tile(mem_q_gain, MEM_HEADS)])
    head_gain_mem = jnp.concatenate([jnp.tile(mem_k_gain, MEM_HEADS), jnp.ones((MEM_WIDTH,), F32)])
    wbd = jnp.zeros((POOL_WIDTH, POOL_WIDTH), F32)
    for g in range(len(POOL_WINDOWS)):
        sl = slice(g * POOL_GROUP_WIDTH, (g + 1) * POOL_GROUP_WIDTH)
        wbd = wbd.at[sl, sl].set(pool_w[g])
    wbd_bf = wbd.astype(BF16)
    og_pool, og_attn, og_mem = (out_gain[:POOL_WIDTH], out_gain[POOL_WIDTH:POOL_WIDTH + MOBA_WIDTH],
                                out_gain[POOL_WIDTH + MOBA_WIDTH:])
    wr = jnp.zeros((d, LANES), F32)
    wr = wr.at[:, :N_EXPERTS].set(router_expert_w).at[:, N_EXPERTS:N_EXPERTS + MOE_GROUPS].set(router_group_w)
    br = jnp.zeros((1, LANES), F32)
    br = br.at[0, :N_EXPERTS].set(router_expert_b).at[0, N_EXPERTS:N_EXPERTS + MOE_GROUPS].set(router_group_b)
    wr_hi = wr.astype(BF16)
    wr = jnp.concatenate([wr_hi, (wr - wr_hi.astype(F32)).astype(BF16)], axis=1)
    wg = w_gate.reshape(N_EXPERTS, d, D_EXPERT).astype(BF16)
    wu = w_up.reshape(N_EXPERTS, d, D_EXPERT).astype(BF16)
    wd = w_down.reshape(N_EXPERTS, D_EXPERT, d).astype(BF16)

    def tail(o_pool, o_attn, o_mem, x2d):
        n = x2d.shape[0]
        tm_r = _pick_tile(n, (512, 256))
        x1, h2, comb, gcnt = _outproj_router(o_pool, o_attn, o_mem, x2d, wo_bf, norm2_gain, wr, br, tm_r)
        tm = _pick_tile(n, (1024, 512, 256))
        tile_cnt = gcnt[:, 0, :MOE_GROUPS].reshape(n // tm, tm // tm_r, MOE_GROUPS).sum(axis=1)
        return _moe(tile_cnt.astype(jnp.int32).reshape(-1), h2, comb, x1, wg, wu, wd, tm)

    mem_k_p, mem_v_p = _norm_proj(mem_prompt.reshape(bp * MEM_TOKENS, d), mem_norm_gain, w_mem_bf,
                                  head_gain_mem, ones_bd, MEMKV_SEGS, _pick_tile(bp * MEM_TOKENS, (512, 256)))
    xp = x_prompt.reshape(n_p, d)
    u_p, q_p, kt_p, k_p_bf, kmean_p, vt_p, v_p_bf, qm_p = _norm_proj(
        xp, norm1_gain, w_in_bf, head_gain_in, ones_bd, IN_SEGS, _pick_tile(tp, (512, 256)), rows_per_batch=tp)
    o_pool_p = _pool(u_p.reshape(bp, tp, POOL_WIDTH), jnp.zeros((bp, 16, POOL_WIDTH), F32), wbd_bf,
                     pool_scale, og_pool, 0, 1, MOBA_BLOCK).reshape(n_p, POOL_WIDTH)
    nblk = tp // MOBA_BLOCK
    kmean_pad = jnp.pad(kmean_p.reshape(bp, nblk, MOBA_WIDTH), ((0, 0), (0, LANES - nblk), (0, 0)))
    o_attn_p = _moba_prompt(q_p, k_p_bf, v_p_bf, kmean_pad, og_attn, bp, tp)
    o_mem_p = _mem_attn(qm_p, mem_k_p.reshape(bp, MEM_TOKENS, MEM_WIDTH),
                        mem_v_p.reshape(bp, MEM_TOKENS, MEM_WIDTH), og_mem, bp, tp, _pick_tile(tp, (512, 256)))
    y_p = tail(o_pool_p, o_attn_p, o_mem_p, xp)

    past_len = page_table.shape[1] * PAGE_SIZE
    xs = x_sample.reshape(n_s, d)
    u_s, q_s, k_s, v_s, qm_s = _norm_proj(xs, norm1_gain, w_in_bf, head_gain_in, ones_bd, IN_SEGS_SAMPLE, n_s)
    u_s3 = u_s.reshape(bs, ts, POOL_WIDTH)
    buf16 = jnp.concatenate([jnp.zeros((bs, 1, POOL_WIDTH), F32), state_pool], axis=1)
    o_pool_s = _pool(u_s3, buf16, wbd_bf, pool_scale, og_pool, past_len, bs, ts).reshape(n_s, POOL_WIDTH)
    o_attn_s = _moba_sample(page_table, q_s, k_s, v_s,
                            cache_k.transpose(0, 2, 3, 1).reshape(-1, MOBA_WIDTH, PAGE_SIZE),
                            cache_v.transpose(0, 2, 3, 1).reshape(-1, MOBA_WIDTH, PAGE_SIZE),
                            og_attn, bs, ts)
    o_mem_s = _mem_attn(qm_s, cache_mem_k.reshape(bs, MEM_TOKENS, MEM_WIDTH),
                        cache_mem_v.reshape(bs, MEM_TOKENS, MEM_WIDTH), og_mem, bs, ts, ts)
    y_s = tail(o_pool_s, o_attn_s, o_mem_s, xs)

    pool_prompt = u_p.reshape(bp, tp, POOL_WIDTH)[:, tp - POOL_STATE:]
    pool_sample = jnp.concatenate([state_pool, u_s3], axis=1)[:, -POOL_STATE:]
    k_p = kt_p.reshape(bp, MOBA_HEADS, HEAD_DIM, tp).transpose(0, 3, 1, 2)
    v_p = vt_p.reshape(bp, MOBA_HEADS, HEAD_DIM, tp).transpose(0, 3, 1, 2)
    return (y_p.reshape(bp, tp, d), y_s.reshape(bs, ts, d), k_p, v_p,
            pool_prompt,
            mem_k_p.reshape(bp, MEM_TOKENS, MEM_HEADS, HEAD_DIM), mem_v_p.reshape(bp, MEM_TOKENS, MEM_HEADS, HEAD_DIM),
            k_s.reshape(bs, ts, MOBA_HEADS, HEAD_DIM), v_s.reshape(bs, ts, MOBA_HEADS, HEAD_DIM),
            pool_sample)
```

```python
import functools

import jax
import jax.numpy as jnp
from jax import lax
from jax.experimental import pallas as pl
from jax.experimental.pallas import tpu as pltpu

F32 = jnp.float32
BF16 = jnp.bfloat16

D_MODEL = 1024
HEAD_DIM = 64
POOL_WIDTH = 256
POOL_WINDOWS = (2, 4, 8, 16)
POOL_GROUP_WIDTH = 64
POOL_STATE = 15
MOBA_WIDTH = 512
MOBA_HEADS = 8
MOBA_BLOCK = 256
MOBA_TOPK = 3
MEM_WIDTH = 256
MEM_HEADS = 4
MEM_TOKENS = 256
PAGE_SIZE = 128
MOE_GROUPS = 4
EXPERTS_PER_GROUP = 8
N_EXPERTS = MOE_GROUPS * EXPERTS_PER_GROUP
D_EXPERT = 256
NORM_EPS = 1e-6

LANES = 128
HEAD_PAIR = 2 * HEAD_DIM
QK_SCALE = HEAD_DIM ** -0.5
NEG_INF = float("-inf")
MIB = 1024 * 1024
NT_DIMS = (((1,), (1,)), ((), ()))


def _cparams(semantics, vmem_mib):
    return pltpu.CompilerParams(dimension_semantics=semantics, vmem_limit_bytes=vmem_mib * MIB)


def _rms(y, eps=NORM_EPS):
    return y * lax.rsqrt(jnp.mean(y * y, axis=-1, keepdims=True) + eps)


def _norm_proj_body(x_ref, g_ref, w_ref, hg_ref, ones_ref, *outs, segs, tm):
    hb = (_rms(x_ref[...]) * g_ref[...]).astype(BF16)
    oi = 0
    for (c0, wd, headnorm, transposed, want_bf16, want_blockmean) in segs:
        y = jnp.dot(hb, w_ref[:, c0:c0 + wd], preferred_element_type=F32)
        if headnorm:
            sq = (y * y).astype(BF16)
            msq = jnp.dot(sq, ones_ref[:wd, :wd], preferred_element_type=F32) * (1.0 / HEAD_DIM)
            y = (y * lax.rsqrt(msq + NORM_EPS)) * hg_ref[:, c0:c0 + wd]
        if transposed:
            outs[oi][0] = y.T
        else:
            outs[oi][...] = y
        oi += 1
        if want_bf16:
            outs[oi][...] = y.astype(BF16)
            oi += 1
        if want_blockmean:
            for bi in range(tm // MOBA_BLOCK):
                outs[oi][bi] = jnp.mean(y[bi * MOBA_BLOCK:(bi + 1) * MOBA_BLOCK], axis=0, keepdims=True)
            oi += 1


def _norm_proj(x, gain, w_bf, head_gain, ones_bd, segs, tm, rows_per_batch=None):
    n, d = x.shape
    wtot = w_bf.shape[1]
    assert n % tm == 0
    out_shape, out_specs = [], []
    for (c0, wd, headnorm, transposed, want_bf16, want_blockmean) in segs:
        if transposed:
            assert rows_per_batch % tm == 0 and n % rows_per_batch == 0
            steps = rows_per_batch // tm
            out_shape.append(jax.ShapeDtypeStruct((n // rows_per_batch, wd, rows_per_batch), F32))
            out_specs.append(pl.BlockSpec((1, wd, tm), lambda i, steps=steps: (i // steps, 0, i % steps)))
        else:
            out_shape.append(jax.ShapeDtypeStruct((n, wd), F32))
            out_specs.append(pl.BlockSpec((tm, wd), lambda i: (i, 0)))
        if want_bf16:
            out_shape.append(jax.ShapeDtypeStruct((n, wd), BF16))
            out_specs.append(pl.BlockSpec((tm, wd), lambda i: (i, 0)))
        if want_blockmean:
            assert tm % MOBA_BLOCK == 0
            nb = tm // MOBA_BLOCK
            out_shape.append(jax.ShapeDtypeStruct((n // MOBA_BLOCK, 1, wd), F32))
            out_specs.append(pl.BlockSpec((nb, 1, wd), lambda i: (i, 0, 0)))
    return pl.pallas_call(
        functools.partial(_norm_proj_body, segs=segs, tm=tm),
        grid=(n // tm,),
        in_specs=[
            pl.BlockSpec((tm, d), lambda i: (i, 0)),
            pl.BlockSpec((1, d), lambda i: (0, 0)),
            pl.BlockSpec((d, wtot), lambda i: (0, 0)),
            pl.BlockSpec((1, wtot), lambda i: (0, 0)),
            pl.BlockSpec(ones_bd.shape, lambda i: (0, 0)),
        ],
        out_specs=out_specs,
        out_shape=out_shape,
        compiler_params=_cparams(("parallel",), 48),
        name="norm_proj",
    )(x, gain.reshape(1, d), w_bf, head_gain.reshape(1, wtot), ones_bd)


def _pool_windows(win, pos0):
    r = win.shape[0] - 16
    lane = lax.broadcasted_iota(jnp.int32, (r, LANES), 1)
    pos1 = pos0 + lax.broadcasted_iota(jnp.int32, (r, LANES), 0) + 1
    low = lane < POOL_GROUP_WIDTH
    a = win[:, :LANES]
    b = win[:, LANES:]
    a2 = a + pltpu.roll(a, 1, 0)
    a4 = a2 + pltpu.roll(a2, 2, 0)
    b2 = b + pltpu.roll(b, 1, 0)
    b4 = b2 + pltpu.roll(b2, 2, 0)
    b8 = b4 + pltpu.roll(b4, 4, 0)
    b16 = b8 + pltpu.roll(b8, 8, 0)
    cnt_a = jnp.minimum(jnp.where(low, POOL_WINDOWS[0], POOL_WINDOWS[1]), pos1).astype(F32)
    cnt_b = jnp.minimum(jnp.where(low, POOL_WINDOWS[2], POOL_WINDOWS[3]), pos1).astype(F32)
    pa = jnp.where(low, a2[16:], a4[16:]) / cnt_a - a[16:]
    pb = jnp.where(low, b8[16:], b16[16:]) / cnt_b - b[16:]
    return jnp.concatenate([pa, pb], axis=1)


def _pool_finish(pooled, wbd_ref, ps_ref, og_ref):
    mixed = jnp.dot(pooled.astype(BF16), wbd_ref[...], preferred_element_type=F32) * ps_ref[...]
    return _rms(mixed) * og_ref[...]


def _pool_body(u_ref, buf_ref, wbd_ref, ps_ref, og_ref, o_ref, ext_sc, *, bb, t, r, pos0):
    for bi in range(bb):
        ext_sc[bi, 0:16, :] = buf_ref[bi]
        ext_sc[bi, 16:, :] = u_ref[bi]
    if t == r:
        pooled = [_pool_windows(ext_sc[bi], pos0) for bi in range(bb)]
        out = _pool_finish(jnp.concatenate(pooled, axis=0), wbd_ref, ps_ref, og_ref)
        for bi in range(bb):
            o_ref[bi] = out[bi * r:(bi + 1) * r]
    else:
        assert bb == 1

        def chunk(c, carry):
            base = pl.multiple_of(c * r, r)
            pooled = _pool_windows(ext_sc[0, pl.ds(base, r + 16), :], pos0 + base)
            o_ref[0, pl.ds(base, r), :] = _pool_finish(pooled, wbd_ref, ps_ref, og_ref)
            return carry

        lax.fori_loop(0, t // r, chunk, 0)


def _pool(u, buf16, wbd_bf, pool_scale, og, pos0, bb, r):
    b, t, c = u.shape
    assert b % bb == 0 and t % r == 0
    return pl.pallas_call(
        functools.partial(_pool_body, bb=bb, t=t, r=r, pos0=pos0),
        grid=(b // bb,),
        in_specs=[
            pl.BlockSpec((bb, t, c), lambda i: (i, 0, 0)),
            pl.BlockSpec((bb, 16, c), lambda i: (i, 0, 0)),
            pl.BlockSpec((c, c), lambda i: (0, 0)),
            pl.BlockSpec((1, c), lambda i: (0, 0)),
            pl.BlockSpec((1, c), lambda i: (0, 0)),
        ],
        out_specs=pl.BlockSpec((bb, t, c), lambda i: (i, 0, 0)),
        out_shape=jax.ShapeDtypeStruct((b, t, c), F32),
        scratch_shapes=[pltpu.VMEM((bb, t + 16, c), F32)],
        compiler_params=_cparams(("parallel",), 40),
        name="pool",
    )(u, buf16, wbd_bf, pool_scale.reshape(1, c), og.reshape(1, c))


def _alibi_slope(h):
    return 2.0 ** (-8.0 * (h + 1) / MOBA_HEADS)


def _moba_prompt_body(q_ref, k_ref, v_ref, km_ref, og_ref, o_ref, m_sc, l_sc, acc_sc, sel_sc, qst_sc, *, nblk):
    i = pl.program_id(1)
    tq = MOBA_BLOCK
    n_pairs = MOBA_HEADS // 2
    nb_rows = -(-nblk // 8) * 8
    lane = lax.broadcasted_iota(jnp.int32, (tq, LANES), 1)
    low = lane < HEAD_DIM
    row2 = lax.broadcasted_iota(jnp.int32, (tq, MOBA_BLOCK), 0)
    col2 = lax.broadcasted_iota(jnp.int32, (tq, MOBA_BLOCK), 1)
    causal = col2 <= row2
    eye_bf = jnp.where(row2 == col2, 1.0, 0.0).astype(BF16)
    colrow = lax.broadcasted_iota(jnp.int32, (1, MOBA_BLOCK), 1).astype(F32)

    def pair_cols(pr):
        return slice(pr * HEAD_PAIR, (pr + 1) * HEAD_PAIR)

    def scores(pr, j):
        start = pl.multiple_of(j * MOBA_BLOCK, MOBA_BLOCK)
        kj = k_ref[pl.ds(start, MOBA_BLOCK), pair_cols(pr)]
        s = lax.dot_general(qst_sc[pr], kj, NT_DIMS, preferred_element_type=F32)
        off = colrow + ((j - i) * MOBA_BLOCK).astype(F32)
        return [s[hh * tq:(hh + 1) * tq] + _alibi_slope(2 * pr + hh) * off for hh in range(2)]

    def pv(pr, j, ps):
        start = pl.multiple_of(j * MOBA_BLOCK, MOBA_BLOCK)
        vj = v_ref[pl.ds(start, MOBA_BLOCK), pair_cols(pr)]
        return jnp.dot(jnp.concatenate(ps, axis=0).astype(BF16), vj, preferred_element_type=F32)

    gates_t = []
    for pr in range(n_pairs):
        q_pair = q_ref[:, pair_cols(pr)]
        km_pair = km_ref[0, :, pair_cols(pr)]
        q_heads = [jnp.where(low, q_pair, 0.0), jnp.where(low, 0.0, q_pair)]
        qst_sc[pr] = (jnp.concatenate(q_heads, axis=0) * QK_SCALE).astype(BF16)
        km_hi = km_pair.astype(BF16)
        km_lo = (km_pair - km_hi.astype(F32)).astype(BF16)
        for hh in range(2):
            q_hi = q_heads[hh].astype(BF16)
            q_lo = (q_heads[hh] - q_hi.astype(F32)).astype(BF16)
            gates_t.append((lax.dot_general(km_hi, q_hi, NT_DIMS, preferred_element_type=F32)
                            + (lax.dot_general(km_hi, q_lo, NT_DIMS, preferred_element_type=F32)
                               + lax.dot_general(km_lo, q_hi, NT_DIMS, preferred_element_type=F32)))[:nb_rows])
    blk_all = lax.broadcasted_iota(jnp.int32, (nb_rows, MOBA_HEADS * tq), 0)
    blk_all_f = blk_all.astype(F32)
    g = jnp.where(blk_all < i, jnp.concatenate(gates_t, axis=1), NEG_INF)
    sel_t = jnp.zeros((nb_rows, MOBA_HEADS * tq), F32)
    for _ in range(MOBA_TOPK):
        mx = jnp.max(g, axis=0, keepdims=True)
        idx = jnp.min(jnp.where(g == mx, blk_all_f, float(LANES)), axis=0, keepdims=True)
        pick = (blk_all_f == idx) & (mx > NEG_INF)
        sel_t = jnp.where(pick, 1.0, sel_t)
        g = jnp.where(pick, NEG_INF, g)
    sel_t = jnp.concatenate([sel_t, jnp.zeros((LANES - nb_rows, MOBA_HEADS * tq), F32)], axis=0).astype(BF16)
    for h in range(MOBA_HEADS):
        sel_sc[h] = lax.dot_general(eye_bf, sel_t[:, h * tq:(h + 1) * tq], NT_DIMS, preferred_element_type=F32)

    for pr in range(n_pairs):
        s_own = scores(pr, i)
        ps = []
        for hh in range(2):
            sh = jnp.where(causal, s_own[hh], NEG_INF)
            m = jnp.max(sh, axis=-1, keepdims=True)
            p = jnp.exp(sh - m)
            m_sc[2 * pr + hh] = jnp.broadcast_to(m, (tq, LANES))
            l_sc[2 * pr + hh] = jnp.broadcast_to(jnp.sum(p, axis=-1, keepdims=True), (tq, LANES))
            ps.append(p)
        acc_sc[pr] = pv(pr, i, ps)

    def past(j, carry):
        for pr in range(n_pairs):
            s_j = scores(pr, j)
            ps, alphas = [], []
            for hh in range(2):
                h = 2 * pr + hh
                selcol = jnp.sum(jnp.where(lane == j, sel_sc[h], 0.0), axis=-1, keepdims=True)
                sh = jnp.where(selcol > 0.0, s_j[hh], NEG_INF)
                m_prev = m_sc[h]
                m_new = jnp.maximum(m_prev, jnp.max(sh, axis=-1, keepdims=True))
                alpha = jnp.exp(m_prev - m_new)
                p = jnp.exp(sh - jnp.concatenate([m_new, m_new], axis=1))
                l_sc[h] = alpha * l_sc[h] + jnp.sum(p, axis=-1, keepdims=True)
                m_sc[h] = m_new
                ps.append(p)
                alphas.append(alpha)
            acc_sc[pr] = jnp.concatenate(alphas, axis=0) * acc_sc[pr] + pv(pr, j, ps)
        return carry

    span2 = 2 * MOBA_BLOCK
    colrow2 = lax.broadcasted_iota(jnp.int32, (1, span2), 1).astype(F32)

    def past2(jj, carry):
        j0 = 2 * jj
        start = pl.multiple_of(j0 * MOBA_BLOCK, MOBA_BLOCK)
        off = colrow2 + ((j0 - i) * MOBA_BLOCK).astype(F32)
        for pr in range(n_pairs):
            s2 = lax.dot_general(qst_sc[pr], k_ref[pl.ds(start, span2), pair_cols(pr)], NT_DIMS,
                                 preferred_element_type=F32)
            ps, alphas = [], []
            for hh in range(2):
                h = 2 * pr + hh
                sh = s2[hh * tq:(hh + 1) * tq] + _alibi_slope(h) * off
                halves = []
                for bb in range(2):
                    selcol = jnp.sum(jnp.where(lane == j0 + bb, sel_sc[h], 0.0), axis=-1, keepdims=True)
                    halves.append(jnp.where(selcol > 0.0, sh[:, bb * MOBA_BLOCK:(bb + 1) * MOBA_BLOCK], NEG_INF))
                sh = jnp.concatenate(halves, axis=1)
                m_prev = m_sc[h]
                m_new = jnp.maximum(m_prev, jnp.max(sh, axis=-1, keepdims=True))
                alpha = jnp.exp(m_prev - m_new)
                p = jnp.exp(sh - jnp.concatenate([m_new] * (span2 // LANES), axis=1))
                l_sc[h] = alpha * l_sc[h] + jnp.sum(p, axis=-1, keepdims=True)
                m_sc[h] = m_new
                ps.append(p)
                alphas.append(alpha)
            pv2 = jnp.dot(jnp.concatenate(ps, axis=0).astype(BF16), v_ref[pl.ds(start, span2), pair_cols(pr)],
                          preferred_element_type=F32)
            acc_sc[pr] = jnp.concatenate(alphas, axis=0) * acc_sc[pr] + pv2
        return carry

    lax.fori_loop(0, i // 2, past2, 0)

    @pl.when(i % 2 == 1)
    def _odd_block():
        past(i - 1, 0)

    outs = []
    for pr in range(n_pairs):
        acc = acc_sc[pr]
        outs.append(jnp.where(low, acc[:tq] / l_sc[2 * pr], acc[tq:] / l_sc[2 * pr + 1]))
    o_ref[...] = _rms(jnp.concatenate(outs, axis=1)) * og_ref[...]


def _moba_prompt(q, k_bf, v_bf, kmean_pad, og, b, t):
    n, w = q.shape
    nblk = t // MOBA_BLOCK
    tq = MOBA_BLOCK
    return pl.pallas_call(
        functools.partial(_moba_prompt_body, nblk=nblk),
        grid=(b, nblk),
        in_specs=[
            pl.BlockSpec((tq, w), lambda bi, i: (bi * nblk + i, 0)),
            pl.BlockSpec((t, w), lambda bi, i: (bi, 0)),
            pl.BlockSpec((t, w), lambda bi, i: (bi, 0)),
            pl.BlockSpec((1, LANES, w), lambda bi, i: (bi, 0, 0)),
            pl.BlockSpec((1, w), lambda bi, i: (0, 0)),
        ],
        out_specs=pl.BlockSpec((tq, w), lambda bi, i: (bi * nblk + i, 0)),
        out_shape=jax.ShapeDtypeStruct((n, w), F32),
        scratch_shapes=[
            pltpu.VMEM((MOBA_HEADS, tq, LANES), F32),
            pltpu.VMEM((MOBA_HEADS, tq, LANES), F32),
            pltpu.VMEM((MOBA_HEADS // 2, 2 * tq, LANES), F32),
            pltpu.VMEM((MOBA_HEADS, tq, LANES), F32),
            pltpu.VMEM((MOBA_HEADS // 2, 2 * tq, LANES), BF16),
        ],
        compiler_params=_cparams(("parallel", "arbitrary"), 40),
        name="moba_prompt",
    )(q, k_bf, v_bf, kmean_pad, og.reshape(1, w))


SAMPLE_BLOCKS_PER_STEP = 8
PAGES_PER_BLOCK = MOBA_BLOCK // PAGE_SIZE
PAGES_PER_STEP = SAMPLE_BLOCKS_PER_STEP * PAGES_PER_BLOCK


def _moba_sample_body(pt_ref, q_ref, kn_ref, vn_ref, slope_ref, og_ref, *rest, n_past_blocks, t_new):
    kp = rest[:PAGES_PER_STEP]
    vp = rest[PAGES_PER_STEP:2 * PAGES_PER_STEP]
    o_ref, o_sc, m_sc, l_sc, kmt_sc = rest[2 * PAGES_PER_STEP:]
    c = pl.program_id(1)
    rows = MOBA_HEADS * t_new
    row_h = lax.broadcasted_iota(jnp.int32, (rows, MOBA_WIDTH), 0) // t_new
    lane_h = lax.broadcasted_iota(jnp.int32, (rows, MOBA_WIDTH), 1) // HEAD_DIM
    bd = row_h == lane_h
    q = q_ref[...]
    qbd = jnp.where(bd, jnp.concatenate([q] * MOBA_HEADS, axis=0), 0.0)
    qbd_bf = (qbd * QK_SCALE).astype(BF16)
    slope = slope_ref[...]
    slope2 = jnp.concatenate([slope, slope], axis=1)
    colf = lax.broadcasted_iota(jnp.int32, (rows, MOBA_BLOCK), 1).astype(F32)

    @pl.when(c == 0)
    def _init():
        kmt_sc[...] = jnp.zeros(kmt_sc.shape, F32)

    kt_bf = []
    for jj in range(SAMPLE_BLOCKS_PER_STEP):
        kt_pages = [kp[PAGES_PER_BLOCK * jj + p][0] for p in range(PAGES_PER_BLOCK)]
        ksum = kt_pages[0]
        for kt in kt_pages[1:]:
            ksum = ksum + kt
        kmt_sc[c, :, jj:jj + 1] = jnp.sum(ksum, axis=-1, keepdims=True) * (1.0 / MOBA_BLOCK)
        kt_bf.append(jnp.concatenate(kt_pages, axis=1).astype(BF16))
    s_all = jnp.dot(qbd_bf, jnp.concatenate(kt_bf, axis=1), preferred_element_type=F32)

    for jj in range(SAMPLE_BLOCKS_PER_STEP):
        j = c * SAMPLE_BLOCKS_PER_STEP + jj
        s = s_all[:, jj * MOBA_BLOCK:(jj + 1) * MOBA_BLOCK] + slope2 * (
            colf + ((j - n_past_blocks) * MOBA_BLOCK).astype(F32))
        m = jnp.max(s, axis=-1, keepdims=True)
        p = jnp.exp(s - m)
        vt_bf = jnp.concatenate([vp[PAGES_PER_BLOCK * jj + q][0] for q in range(PAGES_PER_BLOCK)],
                                axis=1).astype(BF16)
        o = lax.dot_general(p.astype(BF16), vt_bf, NT_DIMS, preferred_element_type=F32)
        o_sc[j] = jnp.where(bd, o, 0.0)
        m_sc[j] = jnp.broadcast_to(m, (rows, LANES))
        l_sc[j] = jnp.broadcast_to(jnp.sum(p, axis=-1, keepdims=True), (rows, LANES))

    @pl.when(c == pl.num_programs(1) - 1)
    def _finish():
        tq = lax.broadcasted_iota(jnp.int32, (rows, LANES), 0) % t_new
        kn = kn_ref[...]
        vn = vn_ref[...]
        qs = qbd * QK_SCALE
        s_own = []
        m_run = jnp.full((rows, LANES), NEG_INF, F32)
        for cc in range(t_new):
            sc = jnp.sum(qs * kn[cc:cc + 1, :], axis=-1, keepdims=True) + slope * float(cc)
            sc = jnp.where(tq >= cc, sc, NEG_INF)
            s_own.append(sc)
            m_run = jnp.maximum(m_run, sc)

        blk = lax.broadcasted_iota(jnp.int32, (rows, LANES), 1)
        blk_f = blk.astype(F32)
        gates = jnp.zeros((rows, LANES), F32)
        for cs in range(n_past_blocks // SAMPLE_BLOCKS_PER_STEP):
            g_cs = jnp.dot(qbd, kmt_sc[cs], precision=lax.Precision.HIGHEST, preferred_element_type=F32)
            g_cs = jnp.where(blk < SAMPLE_BLOCKS_PER_STEP, g_cs, 0.0)
            gates = gates + (pltpu.roll(g_cs, cs * SAMPLE_BLOCKS_PER_STEP, 1) if cs else g_cs)
        g = jnp.where(blk < n_past_blocks, gates, NEG_INF)
        sel = jnp.zeros((rows, LANES), F32)
        for _ in range(MOBA_TOPK):
            mx = jnp.max(g, axis=-1, keepdims=True)
            idx = jnp.min(jnp.where(g == mx, blk_f, float(LANES)), axis=-1, keepdims=True)
            pick = (blk_f == idx) & (mx > NEG_INF)
            sel = jnp.where(pick, 1.0, sel)
            g = jnp.where(pick, NEG_INF, g)

        m_all = jnp.full((rows, LANES), NEG_INF, F32)
        l_all = jnp.zeros((rows, LANES), F32)
        for j in range(n_past_blocks):
            m_all = jnp.where(blk == j, m_sc[j], m_all)
            l_all = jnp.where(blk == j, l_sc[j], l_all)
        m_run = jnp.maximum(m_run, jnp.max(jnp.where(sel > 0.0, m_all, NEG_INF), axis=-1, keepdims=True))
        w_all = jnp.where(sel > 0.0, jnp.exp(m_all - m_run), 0.0)

        l_run = jnp.broadcast_to(jnp.sum(w_all * l_all, axis=-1, keepdims=True), (rows, LANES))
        o_run = jnp.zeros((rows, MOBA_WIDTH), F32)
        for cc in range(t_new):
            p = jnp.exp(s_own[cc] - m_run)
            l_run = l_run + p
            o_run = o_run + jnp.concatenate([p] * (MOBA_WIDTH // LANES), axis=1) * vn[cc:cc + 1, :]
        o_run = jnp.where(bd, o_run, 0.0)
        for j in range(n_past_blocks):
            o_run = o_run + w_all[:, j:j + 1] * o_sc[j]
        o_bd = o_run / jnp.concatenate([l_run] * (MOBA_WIDTH // LANES), axis=1)
        attn = o_bd[0:t_new]
        for h in range(1, MOBA_HEADS):
            attn = attn + o_bd[h * t_new:(h + 1) * t_new]
        o_ref[...] = _rms(attn) * og_ref[...]


def _moba_sample(page_table, q, k_new, v_new, cache_kt, cache_vt, og, b, t_new):
    n_pages = page_table.shape[1]
    n_past_blocks = n_pages // PAGES_PER_BLOCK
    assert n_past_blocks <= LANES
    assert n_past_blocks % SAMPLE_BLOCKS_PER_STEP == 0
    n_steps = n_past_blocks // SAMPLE_BLOCKS_PER_STEP
    rows = MOBA_HEADS * t_new
    w = MOBA_WIDTH
    slopes = jnp.exp2(-8.0 * jnp.arange(1, MOBA_HEADS + 1, dtype=F32) / MOBA_HEADS)
    slope_rows = jnp.broadcast_to(jnp.repeat(slopes, t_new)[:, None], (rows, LANES))

    def page_spec(p):
        return pl.BlockSpec((1, w, PAGE_SIZE), lambda bi, c, pt: (pt[bi, c * PAGES_PER_STEP + p], 0, 0))

    row_spec = pl.BlockSpec((t_new, w), lambda bi, c, pt: (bi, 0))
    grid_spec = pltpu.PrefetchScalarGridSpec(
        num_scalar_prefetch=1,
        grid=(b, n_steps),
        in_specs=[row_spec, row_spec, row_spec,
                  pl.BlockSpec((rows, LANES), lambda bi, c, pt: (0, 0)),
                  pl.BlockSpec((1, w), lambda bi, c, pt: (0, 0))]
                 + [page_spec(p) for p in range(PAGES_PER_STEP)]
                 + [page_spec(p) for p in range(PAGES_PER_STEP)],
        out_specs=row_spec,
        scratch_shapes=[
            pltpu.VMEM((n_past_blocks, rows, w), F32),
            pltpu.VMEM((n_past_blocks, rows, LANES), F32),
            pltpu.VMEM((n_past_blocks, rows, LANES), F32),
            pltpu.VMEM((n_steps, w, LANES), F32),
        ],
    )
    return pl.pallas_call(
        functools.partial(_moba_sample_body, n_past_blocks=n_past_blocks, t_new=t_new),
        grid_spec=grid_spec,
        out_shape=jax.ShapeDtypeStruct((b * t_new, w), F32),
        compiler_params=_cparams(("parallel", "arbitrary"), 48),
        name="moba_sample",
    )(page_table, q, k_new, v_new, slope_rows, og.reshape(1, w),
      *([cache_kt] * PAGES_PER_STEP), *([cache_vt] * PAGES_PER_STEP))


def _mem_attn_body(q_ref, mk_ref, mv_ref, og_ref, o_ref, *, tm):
    lane = lax.broadcasted_iota(jnp.int32, (tm, LANES), 1)
    low = lane < HEAD_DIM
    outs = []
    for pr in range(MEM_HEADS // 2):
        cs = slice(pr * HEAD_PAIR, (pr + 1) * HEAD_PAIR)
        q_pair = q_ref[:, cs]
        qst = (jnp.concatenate([jnp.where(low, q_pair, 0.0), jnp.where(low, 0.0, q_pair)], axis=0)
               * QK_SCALE).astype(BF16)
        s = lax.dot_general(qst, mk_ref[0, :, cs].astype(BF16), NT_DIMS, preferred_element_type=F32)
        p = jnp.exp(s - jnp.max(s, axis=-1, keepdims=True))
        l = jnp.sum(p, axis=-1, keepdims=True)
        o = jnp.dot(p.astype(BF16), mv_ref[0, :, cs].astype(BF16), preferred_element_type=F32) / l
        outs.append(jnp.where(low, o[:tm], o[tm:]))
    o_ref[...] = _rms(jnp.concatenate(outs, axis=1)) * og_ref[...]


def _mem_attn(qm, mem_k, mem_v, og, b, t, tm):
    n, w = qm.shape
    assert t % tm == 0
    steps = t // tm
    return pl.pallas_call(
        functools.partial(_mem_attn_body, tm=tm),
        grid=(b, steps),
        in_specs=[
            pl.BlockSpec((tm, w), lambda bi, i: (bi * steps + i, 0)),
            pl.BlockSpec((1, MEM_TOKENS, w), lambda bi, i: (bi, 0, 0)),
            pl.BlockSpec((1, MEM_TOKENS, w), lambda bi, i: (bi, 0, 0)),
            pl.BlockSpec((1, w), lambda bi, i: (0, 0)),
        ],
        out_specs=pl.BlockSpec((tm, w), lambda bi, i: (bi * steps + i, 0)),
        out_shape=jax.ShapeDtypeStruct((n, w), F32),
        compiler_params=_cparams(("parallel", "arbitrary"), 32),
        name="mem_attn",
    )(qm, mem_k, mem_v, og.reshape(1, w))


ROUTER_GROUP_LANE0 = N_EXPERTS
ROUTER_GID_LANE = 64


def _outproj_router_body(op_ref, oa_ref, om_ref, x_ref, wo_ref, g2_ref, wr_ref, br_ref,
                         x1_ref, h2_ref, comb_ref, gcnt_ref, *, tm):
    a0, a1 = POOL_WIDTH, POOL_WIDTH + MOBA_WIDTH
    y = x_ref[...]
    y = y + jnp.dot(op_ref[...].astype(BF16), wo_ref[0:a0, :], preferred_element_type=F32)
    y = y + jnp.dot(oa_ref[...].astype(BF16), wo_ref[a0:a1, :], preferred_element_type=F32)
    y = y + jnp.dot(om_ref[...].astype(BF16), wo_ref[a1:, :], preferred_element_type=F32)
    x1_ref[...] = y
    h2 = _rms(y) * g2_ref[...]
    h2_hi = h2.astype(BF16)
    h2_lo = (h2 - h2_hi.astype(F32)).astype(BF16)
    h2_ref[...] = h2_hi
    hw =jnp.dot(h2_hi, wr_ref[...], preferred_element_type=F32)
    lw = jnp.dot(h2_lo, wr_ref[:, :LANES], preferred_element_type=F32)
    logits = (hw[:, :LANES] + (hw[:, LANES:] + lw)) + br_ref[...]
    lane_f = lax.broadcasted_iota(jnp.int32, (tm, LANES), 1).astype(F32)
    big = float(LANES)
    g_lo = float(ROUTER_GROUP_LANE0)
    is_g = (lane_f >= g_lo) & (lane_f < g_lo + MOE_GROUPS)
    lg = jnp.where(is_g, logits, NEG_INF)
    mg = jnp.max(lg, axis=-1, keepdims=True)
    pg_top = 1.0 / jnp.sum(jnp.exp(lg - mg), axis=-1, keepdims=True)
    gidx = jnp.min(jnp.where(lg == mg, lane_f, big), axis=-1, keepdims=True) - g_lo
    e_lo = gidx * EXPERTS_PER_GROUP
    in_grp = (lane_f >= e_lo) & (lane_f < e_lo + EXPERTS_PER_GROUP)
    le = jnp.where(in_grp, logits, NEG_INF)
    m1 = jnp.max(le, axis=-1, keepdims=True)
    se = jnp.sum(jnp.exp(le - m1), axis=-1, keepdims=True)
    i1 = jnp.min(jnp.where(le == m1, lane_f, big), axis=-1, keepdims=True)
    le2 = jnp.where(lane_f == i1, NEG_INF, le)
    m2 = jnp.max(le2, axis=-1, keepdims=True)
    i2 = jnp.min(jnp.where(le2 == m2, lane_f, big), axis=-1, keepdims=True)
    p1 = 1.0 / se
    p2 = jnp.exp(m2 - m1) / se
    den = p1 + p2
    comb_ref[...] = jnp.where(lane_f == i1, pg_top * (p1 / den),
                              jnp.where(lane_f == i2, pg_top * (p2 / den),
                                        jnp.where(lane_f == float(ROUTER_GID_LANE), gidx, 0.0)))
    gcnt_ref[0] = jnp.sum(jnp.where(lane_f == gidx, 1.0, 0.0), axis=0, keepdims=True)


def _outproj_router(o_pool, o_attn, o_mem, x, wo_bf, g2, wr, br, tm):
    n, d = x.shape
    assert n % tm == 0
    row = lambda wdt: pl.BlockSpec((tm, wdt), lambda i: (i, 0))
    full = lambda shp: pl.BlockSpec(shp, lambda i: (0, 0))
    return pl.pallas_call(
        functools.partial(_outproj_router_body, tm=tm),
        grid=(n // tm,),
        in_specs=[row(POOL_WIDTH), row(MOBA_WIDTH), row(MEM_WIDTH), row(d),
                  full((d, d)), full((1, d)), full((d, 2 * LANES)), full((1, LANES))],
        out_specs=[row(d), row(d), row(LANES), pl.BlockSpec((1, 1, LANES), lambda i: (i, 0, 0))],
        out_shape=[jax.ShapeDtypeStruct((n, d), F32), jax.ShapeDtypeStruct((n, d), BF16),
                   jax.ShapeDtypeStruct((n, LANES), F32), jax.ShapeDtypeStruct((n // tm, 1, LANES), F32)],
        compiler_params=_cparams(("parallel",), 40),
        name="outproj_router",
    )(o_pool, o_attn, o_mem, x, wo_bf, g2.reshape(1, d), wr, br)


MOE_EXPERTS_PER_STEP = 8
MOE_RUN_ALIGN = 16
MOE_SORT_PAD = LANES


def _moe_window(tm):
    return max(LANES, tm // MOE_GROUPS + 32)


def _moe_body(cnt_ref, h2_ref, comb_ref, x1_hbm, wg_ref, wu_ref, wd_ref, y_ref,
              xs_sc, cs_sc, ys_sc, p_sc, pt_sc, sem, *, tm, win):
    i = pl.program_id(0)
    s = pl.program_id(1)
    eps = MOE_EXPERTS_PER_STEP
    span = tm + MOE_SORT_PAD
    g = s // (EXPERTS_PER_GROUP // eps)
    starts, nxt = [], jnp.int32(0)
    for gg in range(MOE_GROUPS):
        starts.append(nxt)
        nxt = nxt + (cnt_ref[i * MOE_GROUPS + gg] + (MOE_RUN_ALIGN - 1)) // MOE_RUN_ALIGN * MOE_RUN_ALIGN
    x1_copy = pltpu.make_async_copy(x1_hbm.at[pl.ds(pl.multiple_of(i * tm, tm), tm)], y_ref, sem)

    @pl.when(s == 0)
    def _sort():
        x1_copy.start()
        comb = comb_ref[...]
        lane = lax.broadcasted_iota(jnp.int32, (tm, LANES), 1)
        lane_f = lane.astype(F32)
        gid = jnp.sum(jnp.where(lane == ROUTER_GID_LANE, comb, 0.0), axis=-1, keepdims=True)
        onehot = jnp.where(lane_f == gid, 1.0, 0.0)
        earlier_tok = lax.broadcasted_iota(jnp.int32, (tm, tm), 1) < lax.broadcasted_iota(jnp.int32, (tm, tm), 0)
        earlier = jnp.dot(jnp.where(earlier_tok, 1.0, 0.0).astype(BF16), onehot.astype(BF16),
                          preferred_element_type=F32)
        start_vec = jnp.zeros((tm, LANES), F32)
        for gg in range(MOE_GROUPS):
            start_vec = jnp.where(lane == gg, starts[gg].astype(F32), start_vec)
        dest = jnp.sum(onehot * (earlier + start_vec), axis=-1, keepdims=True)
        hi = jnp.floor(dest * (1.0 / 32.0))
        lo = dest - 32.0 * hi
        dest_cols = jnp.where(lane == 0, hi, jnp.where(lane == 1, lo, 0.0)).astype(BF16)
        r16 = lax.broadcasted_iota(jnp.int32, (16, LANES), 0)
        l16 = lax.broadcasted_iota(jnp.int32, (16, LANES), 1)
        pick = jnp.where((r16 == 0) & (l16 == 0), 32.0, jnp.where((r16 == 0) & (l16 == 1), 1.0, 0.0)).astype(BF16)
        dest_row = lax.dot_general(pick, dest_cols, NT_DIMS, preferred_element_type=F32)[0:1, :]
        p = jnp.where(lax.broadcasted_iota(jnp.int32, (span, tm), 0).astype(F32) == dest_row, 1.0, 0.0).astype(BF16)
        p_sc[...] = p
        pt_sc[...] = jnp.where(lax.broadcasted_iota(jnp.int32, (tm, span), 1).astype(F32) == dest,
                               1.0, 0.0).astype(BF16)
        xs_sc[0:span, :] = jnp.dot(p, h2_ref[...], preferred_element_type=F32).astype(BF16)
        xs_sc[span:, :] = jnp.zeros((win, D_MODEL), BF16)
        c1 = comb.astype(BF16)
        r1 = comb - c1.astype(F32)
        c2 = r1.astype(BF16)
        c3 = (r1 - c2.astype(F32)).astype(BF16)
        cs_sc[0:span, :] = (jnp.dot(p, c1, preferred_element_type=F32) + jnp.dot(p, c2, preferred_element_type=F32)
                            + jnp.dot(p, c3, preferred_element_type=F32))
        cs_sc[span:, :] = jnp.zeros((win, LANES), F32)
        ys_sc[...] = jnp.zeros(ys_sc.shape, F32)

    run_start = starts[0]
    for gg in range(1, MOE_GROUPS):
        run_start = jnp.where(g == gg, starts[gg], run_start)
    run_len = cnt_ref[i * MOE_GROUPS + g]
    lane_w = lax.broadcasted_iota(jnp.int32, (win, LANES), 1)

    def window(w, carry):
        r0 = pl.multiple_of(run_start + w * win, MOE_RUN_ALIGN)
        rows = xs_sc[pl.ds(r0, win), :]
        comb_rows = cs_sc[pl.ds(r0, win), :]
        acc = None
        for j in range(eps):
            e = s * eps + j
            a = jnp.dot(rows, wg_ref[j], preferred_element_type=F32)
            b = jnp.dot(rows, wu_ref[j], preferred_element_type=F32)
            ce = jnp.sum(jnp.where(lane_w == e, comb_rows, 0.0), axis=-1, keepdims=True)
            hid = ((a * jax.nn.sigmoid(a)) * b * ce).astype(BF16)
            c = jnp.dot(hid, wd_ref[j], preferred_element_type=F32)
            acc = c if acc is None else acc + c
        ys_sc[pl.ds(r0, win), :] += acc
        return carry

    lax.fori_loop(0, (run_len + win - 1) // win, window, 0)

    @pl.when(s == pl.num_programs(1) - 1)
    def _unsort():
        ys = ys_sc[0:span, :]
        moe = jnp.dot(pt_sc[...], ys.astype(BF16), preferred_element_type=F32)
        x1_copy.wait()
        y_ref[...] = y_ref[...] + moe


def _moe(tile_cnt, h2, comb, x1, wg_bf, wu_bf, wd_bf, tm):
    n, d = x1.shape
    eps = MOE_EXPERTS_PER_STEP
    assert n % tm == 0 and EXPERTS_PER_GROUP % eps == 0 and MOE_GROUPS * (MOE_RUN_ALIGN - 1) <= MOE_SORT_PAD
    win = _moe_window(tm)
    assert win % MOE_RUN_ALIGN == 0
    span = tm + MOE_SORT_PAD
    grid_spec = pltpu.PrefetchScalarGridSpec(
        num_scalar_prefetch=1,
        grid=(n // tm, N_EXPERTS // eps),
        in_specs=[
            pl.BlockSpec((tm, d), lambda i, s, cnt: (i, 0)),
            pl.BlockSpec((tm, LANES), lambda i, s, cnt: (i, 0)),
            pl.BlockSpec(memory_space=pl.ANY),
            pl.BlockSpec((eps, d, D_EXPERT), lambda i, s, cnt: (s, 0, 0)),
            pl.BlockSpec((eps, d, D_EXPERT), lambda i, s, cnt: (s, 0, 0)),
            pl.BlockSpec((eps, D_EXPERT, d), lambda i, s, cnt: (s, 0, 0)),
        ],
        out_specs=pl.BlockSpec((tm, d), lambda i, s, cnt: (i, 0)),
        scratch_shapes=[
            pltpu.VMEM((span + win, d), BF16),
            pltpu.VMEM((span + win, LANES), F32),
            pltpu.VMEM((span + win, d), F32),
            pltpu.VMEM((span, tm), BF16),
            pltpu.VMEM((tm, span), BF16),
            pltpu.SemaphoreType.DMA(()),
        ],
    )
    return pl.pallas_call(
        functools.partial(_moe_body, tm=tm, win=win),
        grid_spec=grid_spec,
        out_shape=jax.ShapeDtypeStruct((n, d), F32),
        compiler_params=_cparams(("parallel", "arbitrary"), 56),
        name="moe",
    )(tile_cnt, h2, comb, x1, wg_bf, wu_bf, wd_bf)


def _block_diag_ones(width, group):
    r = lax.broadcasted_iota(jnp.int32, (width, width), 0) // group
    c = lax.broadcasted_iota(jnp.int32, (width, width), 1) // group
    return (r == c).astype(BF16)


def _pick_tile(n, candidates):
    for c in candidates:
        if n % c == 0:
            return c
    return n


IN_SEGS = (
    (0, POOL_WIDTH, False, False, False, False),
    (POOL_WIDTH, MOBA_WIDTH, True, False, False, False),
    (POOL_WIDTH + MOBA_WIDTH, MOBA_WIDTH, True, True, True, True),
    (POOL_WIDTH + 2 * MOBA_WIDTH, MOBA_WIDTH, False, True, True, False),
    (POOL_WIDTH + 3 * MOBA_WIDTH, MEM_WIDTH, True, False, False, False),
)
IN_SEGS_SAMPLE = tuple((c0, wd, hn, False, False, False) for (c0, wd, hn, _, _, _) in IN_SEGS)
MEMKV_SEGS = ((0, MEM_WIDTH, True, False, False, False), (MEM_WIDTH, MEM_WIDTH, False, False, False, False))


def kernel(x_prompt, x_sample, mem_prompt, cache_k, cache_v, state_pool, cache_mem_k, cache_mem_v,
           page_table, norm1_gain, w_in, pool_w, pool_scale, moba_q_gain, moba_k_gain,
           mem_norm_gain, w_mem_kv, mem_q_gain, mem_k_gain, out_gain, w_out, norm2_gain,
           router_group_w, router_group_b, router_expert_w, router_expert_b, w_gate, w_up, w_down):
    bp, tp, d = x_prompt.shape
    bs, ts, _ = x_sample.shape
    n_p, n_s = bp * tp, bs * ts

    w_in_bf = w_in.astype(BF16)
    w_mem_bf = w_mem_kv.astype(BF16)
    wo_bf = w_out.astype(BF16)
    ones_bd = _block_diag_ones(MOBA_WIDTH, HEAD_DIM)
    head_gain_in = jnp.concatenate([
        jnp.ones((POOL_WIDTH,), F32), jnp.tile(moba_q_gain, MOBA_HEADS), jnp.tile(moba_k_gain, MOBA_HEADS),
        jnp.ones((MOBA_WIDTH,), F32), jnp.tile(mem_q_gain, MEM_HEADS)])
    head_gain_mem = jnp.concatenate([jnp.tile(mem_k_gain, MEM_HEADS), jnp.ones((MEM_WIDTH,), F32)])
    wbd = jnp.zeros((POOL_WIDTH, POOL_WIDTH), F32)
    for g in range(len(POOL_WINDOWS)):
        sl = slice(g * POOL_GROUP_WIDTH, (g + 1) * POOL_GROUP_WIDTH)
        wbd = wbd.at[sl, sl].set(pool_w[g])
    wbd_bf = wbd.astype(BF16)
    og_pool, og_attn, og_mem = (out_gain[:POOL_WIDTH], out_gain[POOL_WIDTH:POOL_WIDTH + MOBA_WIDTH],
                                out_gain[POOL_WIDTH + MOBA_WIDTH:])
    wr = jnp.zeros((d, LANES), F32)
    wr = wr.at[:, :N_EXPERTS].set(router_expert_w).at[:, N_EXPERTS:N_EXPERTS + MOE_GROUPS].set(router_group_w)
    br = jnp.zeros((1, LANES), F32)
    br = br.at[0, :N_EXPERTS].set(router_expert_b).at[0, N_EXPERTS:N_EXPERTS + MOE_GROUPS].set(router_group_b)
    wr_hi = wr.astype(BF16)
    wr = jnp.concatenate([wr_hi, (wr - wr_hi.astype(F32)).astype(BF16)], axis=1)
    wg = w_gate.reshape(N_EXPERTS, d, D_EXPERT).astype(BF16)
    wu = w_up.reshape(N_EXPERTS, d, D_EXPERT).astype(BF16)
    wd = w_down.reshape(N_EXPERTS, D_EXPERT, d).astype(BF16)

    def tail(o_pool, o_attn, o_mem, x2d):
        n = x2d.shape[0]
        tm_r = _pick_tile(n, (1024, 512, 256))
        x1, h2, comb, gcnt = _outproj_router(o_pool, o_attn, o_mem, x2d, wo_bf, norm2_gain, wr, br, tm_r)
        tm = _pick_tile(n, (1024, 512, 256))
        tile_cnt = gcnt[:, 0, :MOE_GROUPS].reshape(n // tm, tm // tm_r, MOE_GROUPS).sum(axis=1)
        return _moe(tile_cnt.astype(jnp.int32).reshape(-1), h2, comb, x1, wg, wu, wd, tm)

    mem_k_p, mem_v_p = _norm_proj(mem_prompt.reshape(bp * MEM_TOKENS, d), mem_norm_gain, w_mem_bf,
                                  head_gain_mem, ones_bd, MEMKV_SEGS, _pick_tile(bp * MEM_TOKENS, (512, 256)))
    xp = x_prompt.reshape(n_p, d)
    u_p, q_p, kt_p, k_p_bf, kmean_p, vt_p, v_p_bf, qm_p = _norm_proj(
        xp, norm1_gain, w_in_bf, head_gain_in, ones_bd, IN_SEGS, _pick_tile(tp, (1024, 512, 256)), rows_per_batch=tp)
    o_pool_p = _pool(u_p.reshape(bp, tp, POOL_WIDTH), jnp.zeros((bp, 16, POOL_WIDTH), F32), wbd_bf,
                     pool_scale, og_pool, 0, 1, MOBA_BLOCK).reshape(n_p, POOL_WIDTH)
    nblk = tp // MOBA_BLOCK
    kmean_pad = jnp.pad(kmean_p.reshape(bp, nblk, MOBA_WIDTH), ((0, 0), (0, LANES - nblk), (0, 0)))
    o_attn_p = _moba_prompt(q_p, k_p_bf, v_p_bf, kmean_pad, og_attn, bp, tp)
    o_mem_p = _mem_attn(qm_p, mem_k_p.reshape(bp, MEM_TOKENS, MEM_WIDTH),
                        mem_v_p.reshape(bp, MEM_TOKENS, MEM_WIDTH), og_mem, bp, tp, _pick_tile(tp, (512, 256)))
    y_p = tail(o_pool_p, o_attn_p, o_mem_p, xp)

    past_len = page_table.shape[1] * PAGE_SIZE
    xs = x_sample.reshape(n_s, d)
    u_s, q_s, k_s, v_s, qm_s = _norm_proj(xs, norm1_gain, w_in_bf, head_gain_in, ones_bd, IN_SEGS_SAMPLE, n_s)
    u_s3 = u_s.reshape(bs, ts, POOL_WIDTH)
    buf16 = jnp.concatenate([jnp.zeros((bs, 1, POOL_WIDTH), F32), state_pool], axis=1)
    o_pool_s = _pool(u_s3, buf16, wbd_bf, pool_scale, og_pool, past_len, bs, ts).reshape(n_s, POOL_WIDTH)
    o_attn_s = _moba_sample(page_table, q_s, k_s, v_s,
                            cache_k.transpose(0, 2, 3, 1).reshape(-1, MOBA_WIDTH, PAGE_SIZE),
                            cache_v.transpose(0, 2, 3, 1).reshape(-1, MOBA_WIDTH, PAGE_SIZE),
                            og_attn, bs, ts)
    o_mem_s = _mem_attn(qm_s, cache_mem_k.reshape(bs, MEM_TOKENS, MEM_WIDTH),
                        cache_mem_v.reshape(bs, MEM_TOKENS, MEM_WIDTH), og_mem, bs, ts, ts)
    y_s = tail(o_pool_s, o_attn_s, o_mem_s, xs)

    pool_prompt = u_p.reshape(bp, tp, POOL_WIDTH)[:, tp - POOL_STATE:]
    pool_sample = jnp.concatenate([state_pool, u_s3], axis=1)[:, -POOL_STATE:]
    k_p = kt_p.reshape(bp, MOBA_HEADS, HEAD_DIM, tp).transpose(0, 3, 1, 2)
    v_p = vt_p.reshape(bp, MOBA_HEADS, HEAD_DIM, tp).transpose(0, 3, 1, 2)
    return (y_p.reshape(bp, tp, d), y_s.reshape(bs, ts, d), k_p, v_p,
            pool_prompt,
            mem_k_p.reshape(bp, MEM_TOKENS, MEM_HEADS, HEAD_DIM), mem_v_p.reshape(bp, MEM_TOKENS, MEM_HEADS, HEAD_DIM),
            k_s.reshape(bs, ts, MOBA_HEADS, HEAD_DIM), v_s.reshape(bs, ts, MOBA_HEADS, HEAD_DIM),
            pool_sample)
```
